```python
import math
import jax, jax.numpy as jnp
from jax import lax
import numpy as np

D_MODEL = 1024
BATCH = 2
SEQ = 8192
DEPTH = 1

HEAD_DIM = 64
A_Q_HEADS = D_MODEL // HEAD_DIM
A_KV_HEADS = A_Q_HEADS // 4
A_HALF_WINDOW = 128
B_GROUPS = ((128, 1), (512, 4), (2048, 16))
B_HEADS_PER_GROUP = 4
B_N_HEADS = B_HEADS_PER_GROUP * len(B_GROUPS)
TOTAL_BIAS_HEADS = A_Q_HEADS + B_N_HEADS
N_BUCKETS = 32
MAX_DISTANCE = 1024
N_EXPERTS = 32
TOP_K = 4
D_FF = D_MODEL
SWIGLU_LIMIT = 7.0
SWIGLU_ALPHA = 1.702
EPS = 1e-5
NEG_INF = -1e30

A_Q_W = A_Q_HEADS * HEAD_DIM
A_KV_W = A_KV_HEADS * HEAD_DIM
B_QKV_W = B_N_HEADS * HEAD_DIM
B_OUT_W = B_HEADS_PER_GROUP * HEAD_DIM
IN_WIDTHS = (A_Q_W, A_KV_W, A_KV_W, B_QKV_W, B_QKV_W, B_QKV_W, D_MODEL, D_MODEL)
IN_WIDTH = sum(IN_WIDTHS)

kernel_name = "hybrid_gated_window_dilated_attn_moe"


def rmsnorm(x, g):
    xf = x.astype(jnp.float32)
    y = xf * lax.rsqrt(jnp.mean(xf * xf, axis=-1, keepdims=True) + EPS)
    return (y * g.astype(jnp.float32)).astype(x.dtype)


def t5_buckets(rel):
    half = N_BUCKETS // 2
    max_exact = half // 2
    ret = np.where(rel > 0, half, 0)
    n = np.abs(rel)
    large = max_exact + (np.log(np.maximum(n, 1) / max_exact)
                         / np.log(MAX_DISTANCE / max_exact) * (half - max_exact)).astype(np.int32)
    large = np.minimum(large, half - 1)
    return (ret + np.where(n < max_exact, n, large)).astype(np.int32)


def banded_attention(q, k, v, half_window, dilation, bias_table, sink, valid_len):
    n, L, hkv, g, dh = q.shape
    W = half_window
    nb = L // W
    qb = q.reshape(n, nb, W, hkv, g, dh)

    def band(t):
        tp = jnp.pad(t, ((0, 0), (W, W), (0, 0), (0, 0))).reshape(n, nb + 2, W, hkv, dh)
        return jnp.concatenate([tp[:, :-2], tp[:, 1:-1], tp[:, 2:]], axis=2)

    kb, vb = band(k), band(v)
    off = np.arange(3 * W)[None, :] - W - np.arange(W)[:, None]
    kpos = np.arange(nb)[:, None] * W + np.arange(3 * W)[None, :] - W
    mask = (np.abs(off) <= W)[None] & ((kpos >= 0) & (kpos < valid_len))[:, None, :]
    bias = bias_table[t5_buckets(off * dilation)].astype(jnp.float32)
    bias = jnp.transpose(bias, (2, 0, 1)).reshape(hkv, g, 1, W, 3 * W)

    s = jnp.einsum('nbqhgd,nbkhd->nhgbqk', qb, kb).astype(jnp.float32) * (dh ** -0.5) + bias
    s = jnp.where(mask, s, NEG_INF)
    m = jnp.max(s, axis=-1, keepdims=True)
    if sink is not None:
        sk = sink.astype(jnp.float32).reshape(hkv, g, 1, 1, 1)
        m = jnp.maximum(m, sk)
    p = jnp.exp(s - m)
    denom = jnp.sum(p, axis=-1, keepdims=True)
    if sink is not None:
        denom = denom + jnp.exp(sk - m)
    out = jnp.einsum('nhgbqk,nbkhd->nbqhgd', p / denom, vb.astype(jnp.float32)).astype(q.dtype)
    lse = (m + jnp.log(denom))[..., 0]
    return out.reshape(n, L, hkv, g, dh), lse.reshape(n, hkv, g, L)


def windowed_gqa(q, k, v, sinks, rel_bias):
    b, s = q.shape[:2]
    qg = q.reshape(b, s, A_KV_HEADS, A_Q_HEADS // A_KV_HEADS, HEAD_DIM)
    out, _ = banded_attention(qg, k, v, A_HALF_WINDOW, 1, rel_bias[:, :A_Q_HEADS], sinks, s)
    return out.reshape(b, s, A_Q_W)


def dilated_attention(q, k, v, rel_bias):
    b, s = q.shape[:2]
    H = B_HEADS_PER_GROUP
    outs, lses = [], []
    for gi, (window, dil) in enumerate(B_GROUPS):
        hw = window // (2 * dil)
        L = s // dil
        Lp = -(-L // hw) * hw

        def to_sub(t):
            t = t.reshape(b, L, dil, H, HEAD_DIM).transpose(0, 2, 1, 3, 4).reshape(b * dil, L, H, HEAD_DIM)
            return jnp.pad(t, ((0, 0), (0, Lp - L), (0, 0), (0, 0)))

        c0 = A_Q_HEADS + gi * H
        o, l = banded_attention(to_sub(q[:, :, gi])[:, :, :, None], to_sub(k[:, :, gi]), to_sub(v[:, :, gi]),
                                hw, dil, rel_bias[:, c0:c0 + H], None, L)
        o = o[:, :L, :, 0].reshape(b, dil, L, H, HEAD_DIM).transpose(0, 2, 1, 3, 4).reshape(b, s, H, HEAD_DIM)
        l = l[:, :, 0, :L].reshape(b, dil, H, L).transpose(0, 3, 1, 2).reshape(b, s, H)
        outs.append(o)
        lses.append(l)
    w = jax.nn.softmax(jnp.stack(lses, axis=0), axis=0)
    y = jnp.sum(w[..., None] * jnp.stack(outs, axis=0).astype(jnp.float32), axis=0)
    return y.astype(q.dtype).reshape(b, s, B_OUT_W)


def moe(h, router_w, router_b, w_gate, b_gate, w_up, b_up, w_down, b_down):
    b, s, d = h.shape
    hf = h.reshape(b * s, d)
    logits = (hf @ router_w + router_b).astype(jnp.float32)
    top_val, top_idx = lax.top_k(logits, TOP_K)
    probs = jax.nn.softmax(top_val, axis=-1)
    combine = jnp.sum(jax.nn.one_hot(top_idx, N_EXPERTS, dtype=jnp.float32) * probs[..., None], axis=1)
    y = jnp.zeros((b * s, d), jnp.float32)
    for e in range(N_EXPERTS):
        gate = jnp.minimum(hf @ w_gate[e] + b_gate[e], SWIGLU_LIMIT)
        up = jnp.clip(hf @ w_up[e] + b_up[e], -SWIGLU_LIMIT, SWIGLU_LIMIT)
        act = gate * jax.nn.sigmoid(SWIGLU_ALPHA * gate) * (up + 1.0)
        y = y + combine[:, e:e + 1] * (act @ w_down[e] + b_down[e])
    return y.astype(h.dtype).reshape(b, s, d)


def setup_inputs(seed: int = 0) -> dict:
    key = jax.random.key(seed)
    ks = jax.random.split(key, 20)
    nrm = jax.random.normal
    f32 = jnp.float32
    Ly = DEPTH
    return {
        "x": nrm(ks[0], (BATCH, SEQ, D_MODEL), f32),
        "norm_mix": 1.0 + 0.02 * nrm(ks[1], (Ly, D_MODEL), f32),
        "w_in": nrm(ks[2], (Ly, D_MODEL, IN_WIDTH), f32) * D_MODEL ** -0.5,
        "b_in": 0.02 * nrm(ks[3], (Ly, IN_WIDTH), f32),
        "sinks": 0.5 * nrm(ks[4], (Ly, A_Q_HEADS), f32),
        "rel_bias": 0.2 * nrm(ks[5], (N_BUCKETS, TOTAL_BIAS_HEADS), f32),
        "w_branch_a": nrm(ks[6], (Ly, A_Q_W, D_MODEL), f32) * A_Q_W ** -0.5,
        "w_branch_b": nrm(ks[7], (Ly, B_OUT_W, D_MODEL), f32) * B_OUT_W ** -0.5,
        "w_out": nrm(ks[8], (Ly, D_MODEL, D_MODEL), f32) * D_MODEL ** -0.5,
        "norm_ffn": 1.0 + 0.02 * nrm(ks[9], (Ly, D_MODEL), f32),
        "router_w": nrm(ks[10], (Ly, D_MODEL, N_EXPERTS), f32) * D_MODEL ** -0.5,
        "router_b": 0.01 * nrm(ks[11], (Ly, N_EXPERTS), f32),
        "w_gate": nrm(ks[12], (Ly, N_EXPERTS, D_MODEL, D_FF), f32) * D_MODEL ** -0.5,
        "b_gate": 0.01 * nrm(ks[13], (Ly, N_EXPERTS, D_FF), f32),
        "w_up": nrm(ks[14], (Ly, N_EXPERTS, D_MODEL, D_FF), f32) * D_MODEL ** -0.5,
        "b_up": 0.01 * nrm(ks[15], (Ly, N_EXPERTS, D_FF), f32),
        "w_down": nrm(ks[16], (Ly, N_EXPERTS, D_FF, D_MODEL), f32) * D_FF ** -0.5,
        "b_down": 0.01 * nrm(ks[17], (Ly, N_EXPERTS, D_MODEL), f32),
        "norm_final": 1.0 + 0.02 * nrm(ks[18], (D_MODEL,), f32),
    }


def reference(x, norm_mix, w_in, b_in, sinks, rel_bias, w_branch_a, w_branch_b, w_out,
              norm_ffn, router_w, router_b, w_gate, b_gate, w_up, b_up, w_down, b_down, norm_final):
    b, s, _ = x.shape
    split_points = [int(c) for c in np.cumsum(IN_WIDTHS)[:-1]]
    for layer in range(DEPTH):
        h = rmsnorm(x, norm_mix[layer])
        proj = h @ w_in[layer] + b_in[layer]
        qa, ka, va, qb, kb, vb, ga, gb = jnp.split(proj, split_points, axis=-1)
        ya = windowed_gqa(qa.reshape(b, s, A_Q_HEADS, HEAD_DIM),
                          ka.reshape(b, s, A_KV_HEADS, HEAD_DIM),
                          va.reshape(b, s, A_KV_HEADS, HEAD_DIM),
                          sinks[layer], rel_bias)
        grp = (b, s, len(B_GROUPS), B_HEADS_PER_GROUP, HEAD_DIM)
        yb = dilated_attention(qb.reshape(grp), kb.reshape(grp), vb.reshape(grp), rel_bias)
        merged = (jax.nn.sigmoid(ga) * (ya @ w_branch_a[layer])
                  + jax.nn.sigmoid(gb) * (yb @ w_branch_b[layer]))
        x = x + merged @ w_out[layer]
        h2 = rmsnorm(x, norm_ffn[layer])
        x = x + moe(h2, router_w[layer], router_b[layer], w_gate[layer], b_gate[layer],
                    w_up[layer], b_up[layer], w_down[layer], b_down[layer])
    return rmsnorm(x, norm_final)
```

```python
import functools

import numpy as np
import jax
import jax.numpy as jnp
from jax import lax
from jax.experimental import pallas as pl
from jax.experimental.pallas import tpu as pltpu

F32 = jnp.float32
BF16 = jnp.bfloat16

HEAD_DIM = 64
A_KV_HEADS = 4
A_GROUP = 4
A_HALF_WINDOW = 128
B_GROUPS = ((128, 1), (512, 4), (2048, 16))
B_HEADS = 4
N_BUCKETS = 32
MAX_DISTANCE = 1024
N_EXPERTS = 32
TOP_K = 4
SWIGLU_LIMIT = 7.0
SWIGLU_ALPHA = 1.702
EPS = 1e-5
NEG_INF = -1e30

V7X_VMEM_LIMIT_BYTES = 56 * 1024 * 1024
LANES = 128

TM_PROJ = 512
TQ_ATTN = 128
TM_POST = 512
TM_MOE = 1024


def _t5_buckets(rel):
    half = N_BUCKETS // 2
    max_exact = half // 2
    ret = np.where(rel > 0, half, 0)
    n = np.abs(rel)
    large = max_exact + (np.log(np.maximum(n, 1) / max_exact)
                         / np.log(MAX_DISTANCE / max_exact) * (half - max_exact)).astype(np.int32)
    large = np.minimum(large, half - 1)
    return (ret + np.where(n < max_exact, n, large)).astype(np.int32)


def _rms(x, g):
    return x * lax.rsqrt(jnp.mean(x * x, axis=-1, keepdims=True) + EPS) * g


def _inproj_kernel(x_ref, g_ref, w_ref, b_ref, *out_refs, segs):
    h = _rms(x_ref[...], g_ref[...]).astype(BF16)
    for ref, (c0, width, kind) in zip(out_refs, segs):
        acc = jnp.dot(h, w_ref[:, c0:c0 + width], preferred_element_type=F32) + b_ref[:, c0:c0 + width]
        if kind == "q":
            acc = acc * (HEAD_DIM ** -0.5)
        elif kind == "gate":
            acc = jax.nn.sigmoid(acc)
        ref[...] = acc.astype(ref.dtype)


def _inproj(x2d, g, w_bf16, b, widths, kinds):
    t, d = x2d.shape
    n = w_bf16.shape[1]
    starts = np.concatenate([[0], np.cumsum(widths)[:-1]])
    segs = tuple((int(c0), int(w), k) for c0, w, k in zip(starts, widths, kinds))
    return pl.pallas_call(
        functools.partial(_inproj_kernel, segs=segs),
        grid=(t // TM_PROJ,),
        in_specs=[
            pl.BlockSpec((TM_PROJ, d), lambda i: (i, 0)),
            pl.BlockSpec((1, d), lambda i: (0, 0)),
            pl.BlockSpec((d, n), lambda i: (0, 0), pipeline_mode=pl.Buffered(1)),
            pl.BlockSpec((1, n), lambda i: (0, 0)),
        ],
        out_specs=[pl.BlockSpec((TM_PROJ, w), lambda i: (i, 0)) for w in widths],
        out_shape=[jax.ShapeDtypeStruct((t, w), BF16) for w in widths],
        compiler_params=pltpu.CompilerParams(
            dimension_semantics=("parallel",), vmem_limit_bytes=V7X_VMEM_LIMIT_BYTES),
        name="inproj",
    )(x2d, g, w_bf16, b)


def _band_bias(table, hw, tq, dil):
    tk = tq + 2 * hw
    off = np.arange(tk)[None, :] - hw - np.arange(tq)[:, None]
    band = np.abs(off) <= hw
    col = np.arange(tk)[None, :]
    masks = np.stack([band & (col >= hw), band, band & (col < hw + tq)])
    bias = jnp.transpose(table[_t5_buckets(off * dil)].astype(F32), (2, 0, 1))
    return jnp.where(masks[:, None], bias[None], NEG_INF)


def _band_attn_kernel(*refs, n_kv, grp, has_sink, want_lse):
    q_ref, kp, kc, kn, vp, vc, vn, bias_ref = refs[:8]
    rest = refs[8:]
    sink_ref = None
    if has_sink:
        sink_ref, rest = rest[0], rest[1:]
    o_ref = rest[0]
    lse_ref = rest[1] if want_lse else None

    k = jnp.concatenate([kp[0], kc[0], kn[0]], axis=0)
    v = jnp.concatenate([vp[0], vc[0], vn[0]], axis=0)
    for h in range(n_kv):
        k_h = k[:, h * HEAD_DIM:(h + 1) * HEAD_DIM]
        v_h = v[:, h * HEAD_DIM:(h + 1) * HEAD_DIM]
        for g in range(grp):
            hq = h * grp + g
            c = hq * HEAD_DIM
            q = q_ref[0, :, c:c + HEAD_DIM]
            s = lax.dot_general(q, k_h, (((1,), (1,)), ((), ())), preferred_element_type=F32)
            s = s + bias_ref[0, hq]
            m = jnp.max(s, axis=-1, keepdims=True)
            if has_sink:
                sk = sink_ref[hq]
                m = jnp.maximum(m, sk)
            p = jnp.exp(s - m)
            l = jnp.sum(p, axis=-1, keepdims=True)
            if has_sink:
                l = l + jnp.exp(sk - m)
            o = jnp.dot(p.astype(BF16), v_h, preferred_element_type=F32) / l
            o_ref[0, :, c:c + HEAD_DIM] = o.astype(o_ref.dtype)
            if want_lse:
                lse_ref[0, :, c:c + HEAD_DIM] = jnp.broadcast_to(m + jnp.log(l), o.shape)


def _band_attn(q, k, v, bias3, sinks, *, n_kv, grp, hw, tq, q_cols, kv_cols, reps, want_lse):
    b, l, _ = q.shape
    hq = n_kv * grp
    qw, kw = hq * HEAD_DIM, n_kv * HEAD_DIM
    nt = l // tq
    ratio = tq // hw
    nhw = l // hw
    tk = tq + 2 * hw

    def var(j):
        return jnp.where(j == 0, 0, jnp.where(j == nt - 1, 2, 1))

    in_specs = [
        pl.BlockSpec((1, tq, qw), lambda bi, r, j: (bi, j, q_cols(r))),
        pl.BlockSpec((1, hw, kw), lambda bi, r, j: (bi, jnp.maximum(j * ratio - 1, 0), kv_cols(r))),
        pl.BlockSpec((1, tq, kw), lambda bi, r, j: (bi, j, kv_cols(r))),
        pl.BlockSpec((1, hw, kw), lambda bi, r, j: (bi, jnp.minimum((j + 1) * ratio, nhw - 1), kv_cols(r))),
        pl.BlockSpec((1, hw, kw), lambda bi, r, j: (bi, jnp.maximum(j * ratio - 1, 0), kv_cols(r))),
        pl.BlockSpec((1, tq, kw), lambda bi, r, j: (bi, j, kv_cols(r))),
        pl.BlockSpec((1, hw, kw), lambda bi, r, j: (bi, jnp.minimum((j + 1) * ratio, nhw - 1), kv_cols(r))),
        pl.BlockSpec((1, hq, tq, tk), lambda bi, r, j: (var(j), 0, 0, 0)),
    ]
    args = [q, k, k, k, v, v, v, bias3]
    if sinks is not None:
        in_specs.append(pl.BlockSpec(memory_space=pltpu.SMEM))
        args.append(sinks)
    out_specs = [pl.BlockSpec((1, tq, qw), lambda bi, r, j: (bi, j, r))]
    out_shape = [jax.ShapeDtypeStruct((b, l, reps * qw), BF16)]
    if want_lse:
        out_specs.append(pl.BlockSpec((1, tq, qw), lambda bi, r, j: (bi, j, r)))
        out_shape.append(jax.ShapeDtypeStruct((b, l, reps * qw), F32))
    return pl.pallas_call(
        functools.partial(_band_attn_kernel, n_kv=n_kv, grp=grp,
                          has_sink=sinks is not None, want_lse=want_lse),
        grid=(b, reps, nt),
        in_specs=in_specs,
        out_specs=out_specs,
        out_shape=out_shape,
        compiler_params=pltpu.CompilerParams(
            dimension_semantics=("parallel", "parallel", "parallel"),
            vmem_limit_bytes=V7X_VMEM_LIMIT_BYTES),
        name="band_attn_a" if sinks is not None else "band_attn_b",
    )(*args)


def _post_kernel(x_ref, ya_ref, o1, o2, o3, l1, l2, l3, ga_ref, gb_ref, wa_ref, wb_ref, wo_ref,
                 gf_ref, rwt_ref, rb_ref, x1_ref, h2_ref, comb_ref):
    ls = [l1[...], l2[...], l3[...]]
    mx = jnp.maximum(jnp.maximum(ls[0], ls[1]), ls[2])
    es = [jnp.exp(l - mx) for l in ls]
    den = es[0] + es[1] + es[2]
    yb = (es[0] * o1[...].astype(F32) + es[1] * o2[...].astype(F32) + es[2] * o3[...].astype(F32)) / den
    za = jnp.dot(ya_ref[...], wa_ref[...], preferred_element_type=F32)
    zb = jnp.dot(yb.astype(BF16), wb_ref[...], preferred_element_type=F32)
    merged = ga_ref[...].astype(F32) * za + gb_ref[...].astype(F32) * zb
    x1 = x_ref[...] + jnp.dot(merged.astype(BF16), wo_ref[...], preferred_element_type=F32)
    x1_ref[...] = x1
    h2 = _rms(x1, gf_ref[...])
    h2_ref[...] = h2.astype(BF16)

    logits = lax.dot_general(rwt_ref[...], h2, (((1,), (1,)), ((), ())),
                             precision=lax.Precision.HIGHEST, preferred_element_type=F32) + rb_ref[...]
    n_e, tm = logits.shape
    iota_e = lax.broadcasted_iota(jnp.int32, (n_e, tm), 0)
    work = logits
    vals, hots = [], []
    for _ in range(TOP_K):
        mk = jnp.max(work, axis=0, keepdims=True)
        ik = jnp.min(jnp.where(work == mk, iota_e, n_e), axis=0, keepdims=True)
        hot = iota_e == ik
        vals.append(mk)
        hots.append(hot)
        work = jnp.where(hot, -jnp.inf, work)
    exps = [jnp.exp(vk - vals[0]) for vk in vals]
    tot = exps[0] + exps[1] + exps[2] + exps[3]
    comb = jnp.zeros((n_e, tm), F32)
    for hot, ek in zip(hots, exps):
        comb = comb + jnp.where(hot, ek / tot, 0.0)
    comb_pad = jnp.concatenate([comb, jnp.zeros((LANES - n_e, tm), F32)], axis=0)
    comb_ref[...] = comb_pad.T


def _post(x2d, ya, os_, ls_, ga, gb, wa, wb, wo, gf, rwt, rb):
    t, d = x2d.shape
    bw = os_[0].shape[1]
    tm = TM_POST
    row = lambda w: pl.BlockSpec((tm, w), lambda i: (i, 0))
    full = lambda a: pl.BlockSpec(a.shape, lambda i: (0,) * a.ndim)
    return pl.pallas_call(
        _post_kernel,
        grid=(t // tm,),
        in_specs=[row(d), row(d), row(bw), row(bw), row(bw), row(bw), row(bw), row(bw), row(d), row(d),
                  full(wa), full(wb), full(wo), full(gf), full(rwt), full(rb)],
        out_specs=[row(d), row(d), row(LANES)],
        out_shape=[jax.ShapeDtypeStruct((t, d), F32), jax.ShapeDtypeStruct((t, d), BF16),
                   jax.ShapeDtypeStruct((t, LANES), F32)],
        compiler_params=pltpu.CompilerParams(
            dimension_semantics=("parallel",), vmem_limit_bytes=V7X_VMEM_LIMIT_BYTES),
        name="post_attn",
    )(x2d, ya, *os_, *ls_, ga, gb, wa, wb, wo, gf, rwt, rb)


def _moe_kernel(h_ref, comb_ref, wg_ref, bg_ref, wu_ref, bu_ref, wd_ref, bd_ref, x1_ref, gn_ref,
                o_ref, acc_ref, *, final_norm):
    e = pl.program_id(1)

    @pl.when(e == 0)
    def _():
        acc_ref[...] = jnp.zeros_like(acc_ref)

    h = h_ref[...]
    gate = jnp.minimum(jnp.dot(h, wg_ref[0], preferred_element_type=F32) + bg_ref[0], SWIGLU_LIMIT)
    up = jnp.clip(jnp.dot(h, wu_ref[0], preferred_element_type=F32) + bu_ref[0], -SWIGLU_LIMIT, SWIGLU_LIMIT)
    act = gate * jax.nn.sigmoid(SWIGLU_ALPHA * gate) * (up + 1.0)
    y = jnp.dot(act.astype(BF16), wd_ref[0], preferred_element_type=F32) + bd_ref[0]
    comb = comb_ref[...]
    lane = lax.broadcasted_iota(jnp.int32, comb.shape, 1)
    w = jnp.sum(jnp.where(lane == e, comb, 0.0), axis=1, keepdims=True)
    acc_ref[...] += w * y

    @pl.when(e == pl.num_programs(1) - 1)
    def _():
        xo = x1_ref[...] + acc_ref[...]
        o_ref[...] = _rms(xo, gn_ref[...]) if final_norm else xo


def _moe(h2, comb, wg, bg, wu, bu, wd, bd, x1, gn, final_norm):
    t, d = h2.shape
    n_e, _, dff = wg.shape
    tm = TM_MOE
    tok = lambda w: pl.BlockSpec((tm, w), lambda i, e: (i, 0))
    return pl.pallas_call(
        functools.partial(_moe_kernel, final_norm=final_norm),
        grid=(t // tm, n_e),
        in_specs=[tok(d), tok(LANES),
                  pl.BlockSpec((1, d, dff), lambda i, e: (e, 0, 0)),
                  pl.BlockSpec((1, 1, dff), lambda i, e: (e, 0, 0)),
                  pl.BlockSpec((1, d, dff), lambda i, e: (e, 0, 0)),
                  pl.BlockSpec((1, 1, dff), lambda i, e: (e, 0, 0)),
                  pl.BlockSpec((1, dff, d), lambda i, e: (e, 0, 0)),
                  pl.BlockSpec((1, 1, d), lambda i, e: (e, 0, 0)),
                  tok(d),
                  pl.BlockSpec((1, d), lambda i, e: (0, 0))],
        out_specs=tok(d),
        out_shape=jax.ShapeDtypeStruct((t, d), F32),
        scratch_shapes=[pltpu.VMEM((tm, d), F32)],
        compiler_params=pltpu.CompilerParams(
            dimension_semantics=("parallel", "arbitrary"), vmem_limit_bytes=V7X_VMEM_LIMIT_BYTES),
        name="moe_dense",
    )(h2, comb, wg, bg, wu, bu, wd, bd, x1, gn)


def kernel(x, norm_mix, w_in, b_in, sinks, rel_bias, w_branch_a, w_branch_b, w_out, norm_ffn,
           router_w, router_b, w_gate, b_gate, w_up, b_up, w_down, b_down, norm_final):
    b, s, d = x.shape
    t = b * s
    depth = w_in.shape[0]
    a_q_w = A_KV_HEADS * A_GROUP * HEAD_DIM
    a_kv_w = A_KV_HEADS * HEAD_DIM
    b_w = B_HEADS * HEAD_DIM
    n_grp = len(B_GROUPS)
    widths = (a_q_w, a_kv_w, a_kv_w, n_grp * b_w, n_grp * b_w, n_grp * b_w, d, d)
    kinds = ("q", "k", "v", "q", "k", "v", "gate", "gate")
    n_a = A_KV_HEADS * A_GROUP

    bias_a = _band_bias(rel_bias[:, :n_a], A_HALF_WINDOW, TQ_ATTN, 1)
    bias_b = [_band_bias(rel_bias[:, n_a + gi * B_HEADS:n_a + (gi + 1) * B_HEADS],
                         win // (2 * dil), TQ_ATTN, dil) for gi, (win, dil) in enumerate(B_GROUPS)]

    x2d = x.reshape(t, d)
    for layer in range(depth):
        qa, ka, va, qb, kb, vb, ga, gb = _inproj(
            x2d, norm_mix[layer][None], w_in[layer].astype(BF16), b_in[layer][None], widths, kinds)

        (ya,) = _band_attn(qa.reshape(b, s, a_q_w), ka.reshape(b, s, a_kv_w), va.reshape(b, s, a_kv_w),
                           bias_a, sinks[layer], n_kv=A_KV_HEADS, grp=A_GROUP, hw=A_HALF_WINDOW,
                           tq=TQ_ATTN, q_cols=lambda r: 0, kv_cols=lambda r: 0, reps=1, want_lse=False)
        os_, ls_ = [], []
        for gi, (win, dil) in enumerate(B_GROUPS):
            sub = lambda a: a.reshape(b, s // dil, dil * n_grp * b_w)
            sel = lambda r, gi=gi: r * n_grp + gi
            o, lse = _band_attn(sub(qb), sub(kb), sub(vb), bias_b[gi], None, n_kv=B_HEADS, grp=1,
                                hw=win // (2 * dil), tq=TQ_ATTN, q_cols=sel, kv_cols=sel, reps=dil,
                                want_lse=True)
            os_.append(o.reshape(t, b_w))
            ls_.append(lse.reshape(t, b_w))

        x1, h2, comb = _post(
            x2d, ya.reshape(t, a_q_w), os_, ls_, ga, gb,
            w_branch_a[layer].astype(BF16), w_branch_b[layer].astype(BF16), w_out[layer].astype(BF16),
            norm_ffn[layer][None], router_w[layer].T, router_b[layer][:, None])

        final = layer == depth - 1
        x2d = _moe(h2, comb, w_gate[layer].astype(BF16), b_gate[layer][:, None], w_up[layer].astype(BF16),
                   b_up[layer][:, None], w_down[layer].astype(BF16), b_down[layer][:, None], x1,
                   norm_final[None], final)
    return x2d.reshape(b, s, d)
```

```python
import functools

import numpy as np
import jax
import jax.numpy as jnp
from jax import lax
from jax.experimental import pallas as pl
from jax.experimental.pallas import tpu as pltpu

F32 = jnp.float32
BF16 = jnp.bfloat16
I32 = jnp.int32

HEAD_DIM = 64
A_KV_HEADS = 4
A_GROUP = 4
A_HALF_WINDOW = 128
B_GROUPS = ((128, 1), (512, 4), (2048, 16))
B_HEADS = 4
N_BUCKETS = 32
MAX_DISTANCE = 1024
N_EXPERTS = 32
TOP_K = 4
SWIGLU_LIMIT = 7.0
SWIGLU_ALPHA = 1.702
EPS = 1e-5
NEG_INF = -1e30

V7X_VMEM_LIMIT_BYTES = 56 * 1024 * 1024
LANES = 128
BF16_SUBLANES = 16

TM_PROJ = 512
TQ_ATTN = 128
TM_POST = 512
TM_EXPERT = 512

CHUNK = BF16_SUBLANES
CHUNKS_PER_TILE = TM_EXPERT // CHUNK
SLOTS = -(-(TM_POST * TOP_K + N_EXPERTS * (CHUNK - 1)) // LANES) * LANES
SLACK_ROWS = SLOTS - TM_POST * TOP_K
assert (2 * TM_EXPERT) % SLACK_ROWS == 0


def _t5_buckets(rel):
    half = N_BUCKETS // 2
    max_exact = half // 2
    ret = np.where(rel > 0, half, 0)
    n = np.abs(rel)
    large = max_exact + (np.log(np.maximum(n, 1) / max_exact)
                         / np.log(MAX_DISTANCE / max_exact) * (half - max_exact)).astype(np.int32)
    large = np.minimum(large, half - 1)
    return (ret + np.where(n < max_exact, n, large)).astype(np.int32)


def _rms(x, g):
    return x * lax.rsqrt(jnp.mean(x * x, axis=-1, keepdims=True) + EPS) * g


def _inproj_kernel(x_ref, g_ref, w_ref, b_ref, *out_refs, segs):
    h = _rms(x_ref[...], g_ref[...]).astype(BF16)
    for ref, (c0, width, kind) in zip(out_refs, segs):
        acc = jnp.dot(h, w_ref[:, c0:c0 + width], preferred_element_type=F32) + b_ref[:, c0:c0 + width]
        if kind == "q":
            acc = acc * (HEAD_DIM ** -0.5)
        elif kind == "gate":
            acc = jax.nn.sigmoid(acc)
        ref[...] = acc.astype(ref.dtype)


def _inproj(x2d, g, w_bf16, b, widths, kinds):
    t, d = x2d.shape
    n = w_bf16.shape[1]
    starts = np.concatenate([[0], np.cumsum(widths)[:-1]])
    segs = tuple((int(c0), int(w), k) for c0, w, k in zip(starts, widths, kinds))
    return pl.pallas_call(
        functools.partial(_inproj_kernel, segs=segs),
        grid=(t // TM_PROJ,),
        in_specs=[
            pl.BlockSpec((TM_PROJ, d), lambda i: (i, 0)),
            pl.BlockSpec((1, d), lambda i: (0, 0)),
            pl.BlockSpec((d, n), lambda i: (0, 0), pipeline_mode=pl.Buffered(1)),
            pl.BlockSpec((1, n), lambda i: (0, 0)),
        ],
        out_specs=[pl.BlockSpec((TM_PROJ, w), lambda i: (i, 0)) for w in widths],
        out_shape=[jax.ShapeDtypeStruct((t, w), BF16) for w in widths],
        compiler_params=pltpu.CompilerParams(
            dimension_semantics=("parallel",), vmem_limit_bytes=V7X_VMEM_LIMIT_BYTES),
        name="inproj",
    )(x2d, g, w_bf16, b)


def _band_bias(table, hw, tq, dil):
    tk = tq + 2 * hw
    off = np.arange(tk)[None, :] - hw - np.arange(tq)[:, None]
    band = np.abs(off) <= hw
    col = np.arange(tk)[None, :]
    masks = np.stack([band & (col >= hw), band, band & (col < hw + tq)])
    onehot = (_t5_buckets(off * dil)[..., None] == np.arange(N_BUCKETS)).astype(np.float32)
    bias = jnp.einsum("qkn,nh->hqk", jnp.asarray(onehot), table.astype(F32),
                      precision=lax.Precision.HIGHEST)
    return jnp.where(masks[:, None], bias[None], NEG_INF)


def _band_attn_kernel(*refs, n_kv, grp, has_sink, want_lse):
    q_ref, kp, kc, kn, vp, vc, vn, bias_ref = refs[:8]
    rest = refs[8:]
    sink_ref = None
    if has_sink:
        sink_ref, rest = rest[0], rest[1:]
    o_ref = rest[0]
    lse_ref = rest[1] if want_lse else None

    k = jnp.concatenate([kp[0], kc[0], kn[0]], axis=0)
    v = jnp.concatenate([vp[0], vc[0], vn[0]], axis=0)
    for h in range(n_kv):
        k_h = k[:, h * HEAD_DIM:(h + 1) * HEAD_DIM]
        v_h = v[:, h * HEAD_DIM:(h + 1) * HEAD_DIM]
        for g in range(grp):
            hq = h * grp + g
            c = hq * HEAD_DIM
            q = q_ref[0, :, c:c + HEAD_DIM]
            s = lax.dot_general(q, k_h, (((1,), (1,)), ((), ())), preferred_element_type=F32)
            s = s + bias_ref[0, hq]
            m = jnp.max(s, axis=-1, keepdims=True)
            if has_sink:
                sk = sink_ref[hq]
                m = jnp.maximum(m, sk)
            p = jnp.exp(s - m)
            l = jnp.sum(p, axis=-1, keepdims=True)
            if has_sink:
                l = l + jnp.exp(sk - m)
            o = jnp.dot(p.astype(BF16), v_h, preferred_element_type=F32) / l
            o_ref[0, :, c:c + HEAD_DIM] = o.astype(o_ref.dtype)
            if want_lse:
                lse_ref[0, :, c:c + HEAD_DIM] = jnp.broadcast_to(m + jnp.log(l), o.shape)


def _band_attn(q, k, v, bias3, sinks, *, n_kv, grp, hw, tq, q_cols, kv_cols, reps, want_lse):
    b, l, _ = q.shape
    hq = n_kv * grp
    qw, kw = hq * HEAD_DIM, n_kv * HEAD_DIM
    nt = l // tq
    ratio = tq // hw
    nhw = l // hw
    tk = tq + 2 * hw

    def var(j):
        return jnp.where(j == 0, 0, jnp.where(j == nt - 1, 2, 1))

    in_specs = [
        pl.BlockSpec((1, tq, qw), lambda bi, r, j: (bi, j, q_cols(r))),
        pl.BlockSpec((1, hw, kw), lambda bi, r, j: (bi, jnp.maximum(j * ratio - 1, 0), kv_cols(r))),
        pl.BlockSpec((1, tq, kw), lambda bi, r, j: (bi, j, kv_cols(r))),
        pl.BlockSpec((1, hw, kw), lambda bi, r, j: (bi, jnp.minimum((j + 1) * ratio, nhw - 1), kv_cols(r))),
        pl.BlockSpec((1, hw, kw), lambda bi, r, j: (bi, jnp.maximum(j * ratio - 1, 0), kv_cols(r))),
        pl.BlockSpec((1, tq, kw), lambda bi, r, j: (bi, j, kv_cols(r))),
        pl.BlockSpec((1, hw, kw), lambda bi, r, j: (bi, jnp.minimum((j + 1) * ratio, nhw - 1), kv_cols(r))),
        pl.BlockSpec((1, hq, tq, tk), lambda bi, r, j: (var(j), 0, 0, 0)),
    ]
    args = [q, k, k, k, v, v, v, bias3]
    if sinks is not None:
        in_specs.append(pl.BlockSpec(memory_space=pltpu.SMEM))
        args.append(sinks)
    out_specs = [pl.BlockSpec((1, tq, qw), lambda bi, r, j: (bi, j, r))]
    out_shape = [jax.ShapeDtypeStruct((b, l, reps * qw), BF16)]
    if want_lse:
        out_specs.append(pl.BlockSpec((1, tq, qw), lambda bi, r, j: (bi, j, r)))
        out_shape.append(jax.ShapeDtypeStruct((b, l, reps * qw), F32))
    return pl.pallas_call(
        functools.partial(_band_attn_kernel, n_kv=n_kv, grp=grp,
                          has_sink=sinks is not None, want_lse=want_lse),
        grid=(b, reps, nt),
        in_specs=in_specs,
        out_specs=out_specs,
        out_shape=out_shape,
        compiler_params=pltpu.CompilerParams(
            dimension_semantics=("parallel", "parallel", "parallel"),
            vmem_limit_bytes=V7X_VMEM_LIMIT_BYTES),
        name="band_attn_a" if sinks is not None else "band_attn_b",
    )(*args)


def _post_kernel(x_ref, ya_ref, o1, o2, o3, l1, l2, l3, ga_ref, gb_ref, wa_ref, wb_ref, wo_ref,
                 gf_ref, rwt_ref, rb_ref, tri_ref, x1_ref, xg_ref, meta_ref, segs_ref):
    ls = [l1[...], l2[...], l3[...]]
    mx = jnp.maximum(jnp.maximum(ls[0], ls[1]), ls[2])
    es = [jnp.exp(l - mx) for l in ls]
    den = es[0] + es[1] + es[2]
    yb = (es[0] * o1[...].astype(F32) + es[1] * o2[...].astype(F32) + es[2] * o3[...].astype(F32)) / den
    za = jnp.dot(ya_ref[...], wa_ref[...], preferred_element_type=F32)
    zb = jnp.dot(yb.astype(BF16), wb_ref[...], preferred_element_type=F32)
    merged = ga_ref[...].astype(F32) * za + gb_ref[...].astype(F32) * zb
    x1 = x_ref[...] + jnp.dot(merged.astype(BF16), wo_ref[...], preferred_element_type=F32)
    x1_ref[...] = x1
    h2 = _rms(x1, gf_ref[...])

    logits = lax.dot_general(rwt_ref[...], h2, (((1,), (1,)), ((), ())),
                             precision=lax.Precision.HIGHEST, preferred_element_type=F32) + rb_ref[...]
    n_e, tm = logits.shape
    iota_e = lax.broadcasted_iota(I32, (n_e, tm), 0)
    work = logits
    vals, hots = [], []
    for _ in range(TOP_K):
        mk = jnp.max(work, axis=0, keepdims=True)
        ik = jnp.min(jnp.where(work == mk, iota_e, n_e), axis=0, keepdims=True)
        hot = iota_e == ik
        vals.append(mk)
        hots.append(hot)
        work = jnp.where(hot, -jnp.inf, work)
    exps = [jnp.exp(vk - vals[0]) for vk in vals]
    tot = exps[0] + exps[1] + exps[2] + exps[3]
    probs = [ek / tot for ek in exps]

    sel = jnp.zeros((n_e, tm), F32)
    for hot in hots:
        sel = sel + jnp.where(hot, 1.0, 0.0)
    cnt = jnp.sum(sel, axis=1, keepdims=True)
    pcnt = jnp.floor((cnt + (CHUNK - 1)) / CHUNK) * CHUNK
    r_i = lax.broadcasted_iota(I32, (n_e, n_e), 0)
    c_i = lax.broadcasted_iota(I32, (n_e, n_e), 1)
    pcnt_row = jnp.sum(jnp.where(r_i == c_i, pcnt, 0.0), axis=0, keepdims=True)
    seg_off = jnp.sum(jnp.where(c_i < r_i, pcnt_row, 0.0), axis=1, keepdims=True)
    used = jnp.sum(pcnt, axis=0, keepdims=True)
    before = jnp.dot(sel.astype(BF16), tri_ref[...], preferred_element_type=F32)
    slot_of = seg_off + before
    slots = [jnp.sum(jnp.where(hot, slot_of, 0.0), axis=0, keepdims=True) for hot in hots]

    iota_s = lax.broadcasted_iota(I32, (SLOTS, tm), 0)
    perm = jnp.zeros((SLOTS, tm), F32)
    for sk in slots:
        perm = perm + jnp.where(iota_s == sk.astype(I32), 1.0, 0.0)
    xg_ref[...] = jnp.dot(perm.astype(BF16), h2.astype(BF16), preferred_element_type=F32).astype(BF16)

    rows = slots + probs
    meta_t = jnp.concatenate(rows + [jnp.zeros((LANES - len(rows), tm), F32)], axis=0)
    meta_ref[...] = meta_t.T
    lane = lax.broadcasted_iota(I32, (n_e, LANES), 1)
    segs_ref[...] = jnp.where(lane == 0, pcnt, jnp.where(lane == 1, seg_off, jnp.where(lane == 2, used, 0.0)))


def _post(x2d, ya, os_, ls_, ga, gb, wa, wb, wo, gf, rwt, rb):
    t, d = x2d.shape
    bw = os_[0].shape[1]
    tm = TM_POST
    nt = t // tm
    tri = jnp.asarray(np.triu(np.ones((tm, tm), np.float32), k=1), dtype=BF16)
    row = lambda w: pl.BlockSpec((tm, w), lambda i: (i, 0))
    full = lambda a: pl.BlockSpec(a.shape, lambda i: (0,) * a.ndim)
    return pl.pallas_call(
        _post_kernel,
        grid=(nt,),
        in_specs=[row(d), row(d), row(bw), row(bw), row(bw), row(bw), row(bw), row(bw), row(d), row(d),
                  full(wa), full(wb), full(wo), full(gf), full(rwt), full(rb), full(tri)],
        out_specs=[row(d), pl.BlockSpec((SLOTS, d), lambda i: (i, 0)), row(LANES),
                   pl.BlockSpec((N_EXPERTS, LANES), lambda i: (i, 0))],
        out_shape=[jax.ShapeDtypeStruct((t, d), F32), jax.ShapeDtypeStruct((nt * SLOTS, d), BF16),
                   jax.ShapeDtypeStruct((t, LANES), F32), jax.ShapeDtypeStruct((nt * N_EXPERTS, LANES), F32)],
        compiler_params=pltpu.CompilerParams(
            dimension_semantics=("parallel",), vmem_limit_bytes=V7X_VMEM_LIMIT_BYTES),
        name="post_attn",
    )(x2d, ya, *os_, *ls_, ga, gb, wa, wb, wo, gf, rwt, rb, tri)


def _expert_tiles(pcnt, seg_off, n_tiles):
    nt, n_e = pcnt.shape
    cpt = CHUNKS_PER_TILE
    nch = (pcnt // CHUNK).T
    cum = jnp.cumsum(nch, axis=1)
    total = cum[:, -1]
    tiles_e = (total + cpt - 1) // cpt
    tile_end = jnp.cumsum(tiles_e)
    n_active = tile_end[-1]
    i = jnp.arange(n_tiles, dtype=I32)
    last = jnp.minimum(i, n_active - 1)
    te = jnp.sum((last[:, None] >= tile_end[None, :]).astype(I32), axis=1)
    hot_e = (te[:, None] == jnp.arange(n_e, dtype=I32)[None, :])
    pick = lambda tab: jnp.sum(jnp.where(hot_e[:, :, None], tab[None], 0), axis=1)
    tile_start = jnp.sum(jnp.where(hot_e, (tile_end - tiles_e)[None, :], 0), axis=1)
    total_t = jnp.sum(jnp.where(hot_e, total[None, :], 0), axis=1)
    q = (last - tile_start)[:, None] * cpt + jnp.arange(cpt, dtype=I32)[None, :]
    valid = (q < total_t[:, None]) & (i < n_active)[:, None]
    cum_t, nch_t = pick(cum), pick(nch)
    chunk0_t = pick((seg_off.T + jnp.arange(nt, dtype=I32)[None, :] * SLOTS) // CHUNK)
    jj = jnp.sum((q[:, :, None] >= cum_t[:, None, :]).astype(I32), axis=2)
    hot_j = jj[:, :, None] == jnp.arange(nt, dtype=I32)[None, None, :]
    first = jnp.sum(jnp.where(hot_j, (cum_t - nch_t)[:, None, :], 0), axis=2)
    base = jnp.sum(jnp.where(hot_j, chunk0_t[:, None, :], 0), axis=2)
    src = jnp.where(valid, base + q - first, 0)
    trash = nt * SLOTS // CHUNK + (i % 2)[:, None] * cpt + jnp.arange(cpt, dtype=I32)[None, :]
    dst = jnp.where(valid, src, trash)
    return te.astype(I32), n_active.astype(I32)[None], src.reshape(-1).astype(I32), dst.reshape(-1).astype(I32)


def _expert_kernel(te_ref, na_ref, cs_ref, cd_ref, xg_hbm, wg_ref, bg_ref, wu_ref, bu_ref, wd_ref, bd_ref,
                   yg_hbm, xbuf, ybuf, wbf, zbuf, sem_in, sem_out, sem_zero, *, n_token_tiles):
    i = pl.program_id(0)
    n_active = na_ref[0]
    slot = i % 2
    cpt = CHUNKS_PER_TILE

    zero_starts = [j * SLOTS + TM_POST * TOP_K for j in range(n_token_tiles)]
    zero_starts += [n_token_tiles * SLOTS + j * SLACK_ROWS for j in range(2 * TM_EXPERT // SLACK_ROWS)]

    def zero_copy(n):
        return pltpu.make_async_copy(zbuf, yg_hbm.at[pl.ds(zero_starts[n], SLACK_ROWS), :], sem_zero.at[n])

    @pl.when(i == 0)
    def _():
        zbuf[...] = jnp.zeros_like(zbuf)
        for n in range(len(zero_starts)):
            zero_copy(n).start()

    def in_copy(c, sl, chunk):
        row = pl.multiple_of(chunk * CHUNK, CHUNK)
        return pltpu.make_async_copy(xg_hbm.at[pl.ds(row, CHUNK), :],
                                     xbuf.at[sl, pl.ds(c * CHUNK, CHUNK), :], sem_in.at[sl, c])

    def out_copy(c, sl, chunk):
        row = pl.multiple_of(chunk * CHUNK, CHUNK)
        return pltpu.make_async_copy(ybuf.at[sl, pl.ds(c * CHUNK, CHUNK), :],
                                     yg_hbm.at[pl.ds(row, CHUNK), :], sem_out.at[sl, c])

    def start_gather(tile, sl):
        for c in range(cpt):
            in_copy(c, sl, cs_ref[tile * cpt + c]).start()

    def wait_scatter(sl):
        for c in range(cpt):
            out_copy(c, sl, 0).wait()

    @pl.when(i == 0)
    def _():
        start_gather(0, 0)

    @pl.when(i + 1 < n_active)
    def _():
        start_gather(i + 1, 1 - slot)

    @pl.when(i < n_active)
    def _():
        for c in range(cpt):
            in_copy(c, slot, 0).wait()

        @pl.when((i == 0) | (te_ref[i] != te_ref[jnp.maximum(i - 1, 0)]))
        def _():
            wbf[0] = wg_ref[0].astype(BF16)
            wbf[1] = wu_ref[0].astype(BF16)
            wbf[2] = wd_ref[0].astype(BF16)

        @pl.when(i >= 2)
        def _():
            wait_scatter(slot)

        x = xbuf[slot]
        gate = jnp.minimum(jnp.dot(x, wbf[0], preferred_element_type=F32) + bg_ref[0], SWIGLU_LIMIT)
        up = jnp.clip(jnp.dot(x, wbf[1], preferred_element_type=F32) + bu_ref[0], -SWIGLU_LIMIT, SWIGLU_LIMIT)
        act = gate * jax.nn.sigmoid(SWIGLU_ALPHA * gate) * (up + 1.0)
        y = jnp.dot(act.astype(BF16), wbf[2], preferred_element_type=F32) + bd_ref[0]
        ybuf[slot] = y.astype(BF16)

        @pl.when(i == 0)
        def _():
            for n in range(len(zero_starts)):
                zero_copy(n).wait()

        for c in range(cpt):
            out_copy(c, slot, cd_ref[i * cpt + c]).start()

        @pl.when(i == n_active - 1)
        def _():
            wait_scatter(slot)

            @pl.when(i >= 1)
            def _():
                wait_scatter(1 - slot)


def _experts(xg, te, n_active, src, dst, wg, bg, wu, bu, wd, bd):
    rows, d = xg.shape
    n_e, _, dff = wg.shape
    n_tiles = te.shape[0]
    n_token_tiles = rows // SLOTS
    n_zero = n_token_tiles + 2 * TM_EXPERT // SLACK_ROWS
    w_spec = lambda a, b_: pl.BlockSpec((1, a, b_), lambda i, te_, *_: (te_[i], 0, 0))
    return pl.pallas_call(
        functools.partial(_expert_kernel, n_token_tiles=n_token_tiles),
        grid_spec=pltpu.PrefetchScalarGridSpec(
            num_scalar_prefetch=4,
            grid=(n_tiles,),
            in_specs=[pl.BlockSpec(memory_space=pl.ANY),
                      w_spec(d, dff), w_spec(1, dff), w_spec(d, dff), w_spec(1, dff),
                      w_spec(dff, d), w_spec(1, d)],
            out_specs=pl.BlockSpec(memory_space=pl.ANY),
            scratch_shapes=[pltpu.VMEM((2, TM_EXPERT, d), BF16), pltpu.VMEM((2, TM_EXPERT, d), BF16),
                            pltpu.VMEM((3, d, dff), BF16), pltpu.VMEM((SLACK_ROWS, d), BF16),
                            pltpu.SemaphoreType.DMA((2, CHUNKS_PER_TILE)),
                            pltpu.SemaphoreType.DMA((2, CHUNKS_PER_TILE)),
                            pltpu.SemaphoreType.DMA((n_zero,))]),
        out_shape=jax.ShapeDtypeStruct((rows + 2 * TM_EXPERT, d), BF16),
        compiler_params=pltpu.CompilerParams(
            dimension_semantics=("arbitrary",), vmem_limit_bytes=V7X_VMEM_LIMIT_BYTES),
        name="experts",
    )(te, n_active, src, dst, xg, wg, bg, wu, bu, wd, bd)


def _combine_kernel(x1_ref, yg_ref, meta_ref, segs_ref, gn_ref, o_ref, *, final_norm):
    meta = meta_ref[...]
    tm = meta.shape[0]
    iota_s = lax.broadcasted_iota(I32, (tm, SLOTS), 1)
    wperm = jnp.zeros((tm, SLOTS), F32)
    for k in range(TOP_K):
        wperm = wperm + jnp.where(iota_s == meta[:, k:k + 1].astype(I32), meta[:, TOP_K + k:TOP_K + k + 1], 0.0)
    used = segs_ref[0:1, 2:3].astype(I32)
    row = lax.broadcasted_iota(I32, (SLOTS, 1), 0)
    yg = jnp.where(row < used, yg_ref[...], jnp.zeros((), BF16))
    xo = x1_ref[...] + jnp.dot(wperm.astype(BF16), yg, preferred_element_type=F32)
    o_ref[...] = _rms(xo, gn_ref[...]) if final_norm else xo


def _combine(x1, yg, meta, segs, gn, final_norm):
    t, d = x1.shape
    tm = TM_POST
    return pl.pallas_call(
        functools.partial(_combine_kernel, final_norm=final_norm),
        grid=(t // tm,),
        in_specs=[pl.BlockSpec((tm, d), lambda i: (i, 0)),
                  pl.BlockSpec((SLOTS, d), lambda i: (i, 0)),
                  pl.BlockSpec((tm, LANES), lambda i: (i, 0)),
                  pl.BlockSpec((N_EXPERTS, LANES), lambda i: (i, 0)),
                  pl.BlockSpec((1, d), lambda i: (0, 0))],
        out_specs=pl.BlockSpec((tm, d), lambda i: (i, 0)),
        out_shape=jax.ShapeDtypeStruct((t, d), F32),
        compiler_params=pltpu.CompilerParams(
            dimension_semantics=("parallel",), vmem_limit_bytes=V7X_VMEM_LIMIT_BYTES),
        name="combine",
    )(x1, yg, meta, segs, gn)


def kernel(x, norm_mix, w_in, b_in, sinks, rel_bias, w_branch_a, w_branch_b, w_out, norm_ffn,
           router_w, router_b, w_gate, b_gate, w_up, b_up, w_down, b_down, norm_final):
    b, s, d = x.shape
    t = b * s
    depth = w_in.shape[0]
    a_q_w = A_KV_HEADS * A_GROUP * HEAD_DIM
    a_kv_w = A_KV_HEADS * HEAD_DIM
    b_w = B_HEADS * HEAD_DIM
    n_grp = len(B_GROUPS)
    widths = (a_q_w, a_kv_w, a_kv_w, n_grp * b_w, n_grp * b_w, n_grp * b_w, d, d)
    kinds = ("q", "k", "v", "q", "k", "v", "gate", "gate")
    n_a = A_KV_HEADS * A_GROUP
    nt = t // TM_POST
    max_chunks = nt * ((TM_POST * TOP_K + N_EXPERTS * (CHUNK - 1)) // CHUNK)
    n_tiles = -(-(max_chunks + N_EXPERTS * (CHUNKS_PER_TILE - 1)) // CHUNKS_PER_TILE)

    bias_a = _band_bias(rel_bias[:, :n_a], A_HALF_WINDOW, TQ_ATTN, 1)
    bias_b = [_band_bias(rel_bias[:, n_a + gi * B_HEADS:n_a + (gi + 1) * B_HEADS],
                         win // (2 * dil), TQ_ATTN, dil) for gi, (win, dil) in enumerate(B_GROUPS)]

    x2d = x.reshape(t, d)
    for layer in range(depth):
        qa, ka, va, qb, kb, vb, ga, gb = _inproj(
            x2d, norm_mix[layer][None], w_in[layer].astype(BF16), b_in[layer][None], widths, kinds)

        (ya,) = _band_attn(qa.reshape(b, s, a_q_w), ka.reshape(b, s, a_kv_w), va.reshape(b, s, a_kv_w),
                           bias_a, sinks[layer], n_kv=A_KV_HEADS, grp=A_GROUP, hw=A_HALF_WINDOW,
                           tq=TQ_ATTN, q_cols=lambda r: 0, kv_cols=lambda r: 0, reps=1, want_lse=False)
        os_, ls_ = [], []
        for gi, (win, dil) in enumerate(B_GROUPS):
            sub = lambda a: a.reshape(b, s // dil, dil * n_grp * b_w)
            sel = lambda r, gi=gi: r * n_grp + gi
            o, lse = _band_attn(sub(qb), sub(kb), sub(vb), bias_b[gi], None, n_kv=B_HEADS, grp=1,
                                hw=win // (2 * dil), tq=TQ_ATTN, q_cols=sel, kv_cols=sel, reps=dil,
                                want_lse=True)
            os_.append(o.reshape(t, b_w))
            ls_.append(lse.reshape(t, b_w))

        x1, xg, meta, segs = _post(
            x2d, ya.reshape(t, a_q_w), os_, ls_, ga, gb,
            w_branch_a[layer].astype(BF16), w_branch_b[layer].astype(BF16), w_out[layer].astype(BF16),
            norm_ffn[layer][None], router_w[layer].T, router_b[layer][:, None])

        segs3 = segs.reshape(nt, N_EXPERTS, LANES)
        te, n_active, src, dst = _expert_tiles(segs3[:, :, 0].astype(I32), segs3[:, :, 1].astype(I32), n_tiles)
        yg = _experts(xg, te, n_active, src, dst, w_gate[layer], b_gate[layer][:, None], w_up[layer],
                      b_up[layer][:, None], w_down[layer], b_down[layer][:, None])
        x2d = _combine(x1, yg, meta, segs, norm_final[None], layer == depth - 1)
    return x2d.reshape(b, s, d)
```

```python
import functools

import numpy as np
import jax
import jax.numpy as jnp
from jax import lax
from jax.experimental import pallas as pl
from jax.experimental.pallas import tpu as pltpu

F32 = jnp.float32
BF16 = jnp.bfloat16
I32 = jnp.int32

HEAD_DIM = 64
A_KV_HEADS = 4
A_GROUP = 4
A_HALF_WINDOW = 128
B_GROUPS = ((128, 1), (512, 4), (2048, 16))
B_HEADS = 4
N_BUCKETS = 32
MAX_DISTANCE = 1024
N_EXPERTS = 32
TOP_K = 4
SWIGLU_LIMIT = 7.0
SWIGLU_ALPHA = 1.702
EPS = 1e-5
NEG_INF = -1e30
LOG2E = 1.4426950408889634

V7X_VMEM_LIMIT_BYTES = 56 * 1024 * 1024
LANES = 128
BF16_SUBLANES = 16

TM_PROJ = 512
TQ_ATTN = 128
BLOCK_ATTN_A = 256
BLOCK_ATTN_B = 512
TM_POST = 512
TM_EXPERT = 512

CHUNK = BF16_SUBLANES
CHUNKS_PER_TILE = TM_EXPERT // CHUNK
SLOTS = -(-(TM_POST * TOP_K + N_EXPERTS * (CHUNK - 1)) // LANES) * LANES
SLACK_ROWS = SLOTS - TM_POST * TOP_K
assert (2 * TM_EXPERT) % SLACK_ROWS == 0


def _t5_buckets(rel):
    half = N_BUCKETS // 2
    max_exact = half // 2
    ret = np.where(rel > 0, half, 0)
    n = np.abs(rel)
    large = max_exact + (np.log(np.maximum(n, 1) / max_exact)
                         / np.log(MAX_DISTANCE / max_exact) * (half - max_exact)).astype(np.int32)
    large = np.minimum(large, half - 1)
    return (ret + np.where(n < max_exact, n, large)).astype(np.int32)


def _rms(x, g):
    return x * lax.rsqrt(jnp.mean(x * x, axis=-1, keepdims=True) + EPS) * g


def _inproj_kernel(x_ref, g_ref, w_ref, b_ref, *refs, segs):
    out_refs, scr = refs[:-1], refs[-1]
    h = _rms(x_ref[...], g_ref[...]).astype(BF16)
    for ref, (c0, width, kind, dil) in zip(out_refs, segs):
        acc = jnp.dot(h, w_ref[:, c0:c0 + width], preferred_element_type=F32) + b_ref[:, c0:c0 + width]
        if kind == "q":
            acc = acc * (HEAD_DIM ** -0.5 * LOG2E)
        elif kind == "gate":
            acc = jax.nn.sigmoid(acc)
        if dil == 1:
            ref[...] = acc.astype(ref.dtype)
        else:
            n = acc.shape[0] // dil
            for c in range(width // LANES):
                scr[c] = acc[:, c * LANES:(c + 1) * LANES]
            for r in range(dil):
                for c in range(width // LANES):
                    col = r * width + c * LANES
                    ref[:, col:col + LANES] = scr[c, pl.ds(r, n, stride=dil), :].astype(ref.dtype)


def _inproj(x2d, g, w_bf16, b, segs):
    t, d = x2d.shape
    n = w_bf16.shape[1]
    max_w = max(w for _, w, _, dil in segs if dil > 1)
    return pl.pallas_call(
        functools.partial(_inproj_kernel, segs=segs),
        grid=(t // TM_PROJ,),
        in_specs=[
            pl.BlockSpec((TM_PROJ, d), lambda i: (i, 0)),
            pl.BlockSpec((1, d), lambda i: (0, 0)),
            pl.BlockSpec((d, n), lambda i: (0, 0), pipeline_mode=pl.Buffered(1)),
            pl.BlockSpec((1, n), lambda i: (0, 0)),
        ],
        out_specs=[pl.BlockSpec((TM_PROJ // dil, dil * w), lambda i: (i, 0)) for _, w, _, dil in segs],
        out_shape=[jax.ShapeDtypeStruct((t // dil, dil * w), BF16) for _, w, _, dil in segs],
        scratch_shapes=[pltpu.VMEM((max_w // LANES, TM_PROJ, LANES), F32)],
        compiler_params=pltpu.CompilerParams(
            dimension_semantics=("parallel",), vmem_limit_bytes=V7X_VMEM_LIMIT_BYTES),
        name="inproj",
    )(x2d, g, w_bf16, b)


def _band_bias(table, hw, tq, dil):
    tk = tq + 2 * hw
    off = np.arange(tk)[None, :] - hw - np.arange(tq)[:, None]
    band = np.abs(off) <= hw
    col = np.arange(tk)[None, :]
    masks = np.stack([band & (col >= hw), band, band & (col < hw + tq)])
    onehot = (_t5_buckets(off * dil)[..., None] == np.arange(N_BUCKETS)).astype(np.float32)
    bias = jnp.einsum("qkn,nh->hqk", jnp.asarray(onehot), table.astype(F32),
                      precision=lax.Precision.HIGHEST)
    return jnp.where(masks[:, None], bias[None] * LOG2E, NEG_INF)


def _band_attn_kernel(*refs, n_kv, grp, tq, hw, has_sink, want_lse):
    q_ref, kp, kc, kn, vp, vc, vn, bias_ref = refs[:8]
    rest = refs[8:]
    sink_ref = None
    if has_sink:
        sink_ref, rest = rest[0], rest[1:]
    o_ref = rest[0]
    lse_ref = rest[1] if want_lse else None

    j, nt = pl.program_id(2), pl.num_programs(2)
    k = jnp.concatenate([kp[0], kc[0], kn[0]], axis=0)
    v = jnp.concatenate([vp[0], vc[0], vn[0]], axis=0)
    n_sub = q_ref.shape[1] // tq
    tk = tq + 2 * hw
    ones = jnp.ones((tk, HEAD_DIM), BF16)
    row = lax.broadcasted_iota(I32, (grp * tq, 1), 0)
    for sub in range(n_sub):
        var = 1
        if sub == 0:
            var = jnp.where(j == 0, 0, var)
        if sub == n_sub - 1:
            var = jnp.where(j == nt - 1, 2, var)
        r0 = sub * tq
        for h in range(n_kv):
            k_h = k[r0:r0 + tk, h * HEAD_DIM:(h + 1) * HEAD_DIM]
            v_h = jnp.concatenate([v[r0:r0 + tk, h * HEAD_DIM:(h + 1) * HEAD_DIM], ones], axis=1)
            c0 = h * grp * HEAD_DIM
            q = jnp.concatenate([q_ref[0, r0:r0 + tq, c0 + g * HEAD_DIM:c0 + (g + 1) * HEAD_DIM]
                                 for g in range(grp)], axis=0)
            s = lax.dot_general(q, k_h, (((1,), (1,)), ((), ())), preferred_element_type=F32)
            s = s + bias_ref[var, h]
            m = jnp.max(s, axis=-1, keepdims=True)
            if has_sink:
                sk = jnp.full((grp * tq, 1), sink_ref[h * grp] * LOG2E, F32)
                for g in range(1, grp):
                    sk = jnp.where(row >= g * tq, sink_ref[h * grp + g] * LOG2E, sk)
                m = jnp.maximum(m, sk)
            p = jnp.exp2((s - m).astype(BF16))
            ol = jnp.dot(p, v_h, preferred_element_type=F32)
            l = ol[:, HEAD_DIM:HEAD_DIM + 1]
            if has_sink:
                l = l + jnp.exp2(sk - m)
            o = ol[:, :HEAD_DIM] / l
            if want_lse:
                lse = jnp.broadcast_to(m + jnp.log2(l), o.shape)
            for g in range(grp):
                c = c0 + g * HEAD_DIM
                o_ref[0, r0:r0 + tq, c:c + HEAD_DIM] = o[g * tq:(g + 1) * tq].astype(o_ref.dtype)
                if want_lse:
                    lse_ref[0, r0:r0 + tq, c:c + HEAD_DIM] = lse[g * tq:(g + 1) * tq]


def _band_attn_heads_kernel(q_ref, kp, kc, kn, vp, vc, vn, bias_ref, o_ref, lse_ref, *, n_heads, tq, hw):
    j, nt = pl.program_id(2), pl.num_programs(2)
    k = jnp.concatenate([kp[0], kc[0], kn[0]], axis=0)
    v = jnp.concatenate([vp[0], vc[0], vn[0]], axis=0)
    n_sub = q_ref.shape[1] // tq
    tk = tq + 2 * hw
    width = n_heads * HEAD_DIM
    lane_head = lax.broadcasted_iota(I32, (tq, width), 1) // HEAD_DIM
    ones = jnp.ones((tk, LANES), BF16)
    for sub in range(n_sub):
        var = 1
        if sub == 0:
            var = jnp.where(j == 0, 0, var)
        if sub == n_sub - 1:
            var = jnp.where(j == nt - 1, 2, var)
        r0 = sub * tq
        q = q_ref[0, r0:r0 + tq, :]
        q_bd = jnp.concatenate([jnp.where(lane_head == h, q, jnp.zeros_like(q)) for h in range(n_heads)], axis=0)
        s = lax.dot_general(q_bd, k[r0:r0 + tk], (((1,), (1,)), ((), ())), preferred_element_type=F32)
        s = s + bias_ref[var, 0]
        m = jnp.max(s, axis=-1, keepdims=True)
        p = jnp.exp2((s - m).astype(BF16))
        o_full = jnp.dot(p, v[r0:r0 + tk], preferred_element_type=F32)
        l = jnp.dot(p, ones, preferred_element_type=F32)[:, :1]
        o_sel = jnp.zeros((tq, width), F32)
        m_sel = jnp.zeros((tq, width), F32)
        l_sel = jnp.ones((tq, width), F32)
        for h in range(n_heads):
            hit = lane_head == h
            o_sel = jnp.where(hit, o_full[h * tq:(h + 1) * tq], o_sel)
            m_sel = jnp.where(hit, m[h * tq:(h + 1) * tq], m_sel)
            l_sel = jnp.where(hit, l[h * tq:(h + 1) * tq], l_sel)
        o_ref[0, r0:r0 + tq, :] = (o_sel / l_sel).astype(o_ref.dtype)
        lse_ref[0, r0:r0 + tq, :] = m_sel + jnp.log2(l_sel)


def _band_attn(q, k, v, bias3, sinks, *, n_kv, grp, hw, tq, block, reps, want_lse):
    b, l, _ = q.shape
    hq = n_kv * grp
    qw, kw = hq * HEAD_DIM, n_kv * HEAD_DIM
    nt = l // block
    ratio = block // hw
    nhw = l // hw
    tk = tq + 2 * hw
    assert nt * (block // tq) >= 2

    prev = lambda bi, r, j: (bi, jnp.maximum(j * ratio - 1, 0), r)
    cur = lambda bi, r, j: (bi, j, r)
    nxt = lambda bi, r, j: (bi, jnp.minimum((j + 1) * ratio, nhw - 1), r)
    in_specs = [
        pl.BlockSpec((1, block, qw), cur),
        pl.BlockSpec((1, hw, kw), prev), pl.BlockSpec((1, block, kw), cur), pl.BlockSpec((1, hw, kw), nxt),
        pl.BlockSpec((1, hw, kw), prev), pl.BlockSpec((1, block, kw), cur), pl.BlockSpec((1, hw, kw), nxt),
    ]
    stacked = grp == 1
    bias_shape = (3, 1, hq * tq, tk) if stacked else (3, n_kv, grp * tq, tk)
    in_specs.append(pl.BlockSpec(bias_shape, lambda bi, r, j: (0, 0, 0, 0), pipeline_mode=pl.Buffered(1)))
    args = [q, k, k, k, v, v, v, bias3.reshape(bias_shape)]
    if stacked:
        assert sinks is None and want_lse
        body = functools.partial(_band_attn_heads_kernel, n_heads=n_kv, tq=tq, hw=hw)
    else:
        body = functools.partial(_band_attn_kernel, n_kv=n_kv, grp=grp, tq=tq, hw=hw,
                                 has_sink=sinks is not None, want_lse=want_lse)
    if sinks is not None:
        in_specs.append(pl.BlockSpec(memory_space=pltpu.SMEM))
        args.append(sinks)
    out_specs = [pl.BlockSpec((1, block, qw), cur)]
    out_shape = [jax.ShapeDtypeStruct((b, l, reps * qw), BF16)]
    if want_lse:
        out_specs.append(pl.BlockSpec((1, block, qw), cur))
        out_shape.append(jax.ShapeDtypeStruct((b, l, reps * qw), F32))
    return pl.pallas_call(
        body,
        grid=(b, reps, nt),
        in_specs=in_specs,
        out_specs=out_specs,
        out_shape=out_shape,
        compiler_params=pltpu.CompilerParams(
            dimension_semantics=("parallel", "parallel", "parallel"),
            vmem_limit_bytes=V7X_VMEM_LIMIT_BYTES),
        name="band_attn_a" if sinks is not None else "band_attn_b",
    )(*args)


def _token_order(ref, scr, dil):
    if dil == 1:
        return ref[...].astype(F32)
    n = ref.shape[0]
    width = ref.shape[1] // dil
    for r in range(dil):
        for c in range(width // LANES):
            col = r * width + c * LANES
            scr[c, pl.ds(r, n, stride=dil), :] = ref[:, col:col + LANES].astype(F32)
    return jnp.concatenate([scr[c] for c in range(width // LANES)], axis=1)


def _post_kernel(x_ref, ya_ref, o1, o2, o3, l1, l2, l3, ga_ref, gb_ref, wa_ref, wb_ref, wo_ref,
                 gf_ref, rwt_ref, rb_ref, tri_ref, x1_ref, xg_ref, meta_ref, segs_ref, *scrs, dils):
    os_ = [_token_order(r, scrs[2 * i], dil) for i, (r, dil) in enumerate(zip((o1, o2, o3), dils))]
    ls = [_token_order(r, scrs[2 * i + 1], dil) for i, (r, dil) in enumerate(zip((l1, l2, l3), dils))]
    mx = jnp.maximum(jnp.maximum(ls[0], ls[1]), ls[2])
    es = [jnp.exp2(l - mx) for l in ls]
    den = es[0] + es[1] + es[2]
    yb = (es[0] * os_[0] + es[1] * os_[1] + es[2] * os_[2]) / den
    za = jnp.dot(ya_ref[...], wa_ref[...], preferred_element_type=F32)
    zb = jnp.dot(yb.astype(BF16), wb_ref[...], preferred_element_type=F32)
    merged = ga_ref[...].astype(F32) * za + gb_ref[...].astype(F32) * zb
    x1 = x_ref[...] + jnp.dot(merged.astype(BF16), wo_ref[...], preferred_element_type=F32)
    x1_ref[...] = x1
    h2 = _rms(x1, gf_ref[...])

    logits = lax.dot_general(rwt_ref[...], h2, (((1,), (1,)), ((), ())),
                             precision=lax.Precision.HIGHEST, preferred_element_type=F32) + rb_ref[...]
    n_e, tm = logits.shape
    iota_e = lax.broadcasted_iota(I32, (n_e, tm), 0)
    work = logits
    vals, hots = [], []
    for _ in range(TOP_K):
        mk = jnp.max(work, axis=0, keepdims=True)
        ik = jnp.min(jnp.where(work == mk, iota_e, n_e), axis=0, keepdims=True)
        hot = iota_e == ik
        vals.append(mk)
        hots.append(hot)
        work = jnp.where(hot, -jnp.inf, work)
    exps = [jnp.exp(vk - vals[0]) for vk in vals]
    tot = exps[0] + exps[1] + exps[2] + exps[3]
    probs = [ek / tot for ek in exps]

    sel = jnp.zeros((n_e, tm), F32)
    for hot in hots:
        sel = sel + jnp.where(hot, 1.0, 0.0)
    cnt = jnp.sum(sel, axis=1, keepdims=True)
    pcnt = jnp.floor((cnt + (CHUNK - 1)) / CHUNK) * CHUNK
    r_i = lax.broadcasted_iota(I32, (n_e, n_e), 0)
    c_i = lax.broadcasted_iota(I32, (n_e, n_e), 1)
    pcnt_row = jnp.sum(jnp.where(r_i == c_i, pcnt, 0.0), axis=0, keepdims=True)
    seg_off = jnp.sum(jnp.where(c_i < r_i, pcnt_row, 0.0), axis=1, keepdims=True)
    used = jnp.sum(pcnt, axis=0, keepdims=True)
    before = jnp.dot(sel.astype(BF16), tri_ref[...], preferred_element_type=F32)
    slot_of = seg_off + before
    slots = [jnp.sum(jnp.where(hot, slot_of, 0.0), axis=0, keepdims=True) for hot in hots]

    iota_s = lax.broadcasted_iota(I32, (SLOTS, tm), 0)
    perm = jnp.zeros((SLOTS, tm), F32)
    for sk in slots:
        perm = perm + jnp.where(iota_s == sk.astype(I32), 1.0, 0.0)
    xg_ref[...] = jnp.dot(perm.astype(BF16), h2.astype(BF16), preferred_element_type=F32).astype(BF16)

    rows = slots + probs
    meta_t = jnp.concatenate(rows + [jnp.zeros((LANES - len(rows), tm), F32)], axis=0)
    meta_ref[...] = meta_t.T
    lane = lax.broadcasted_iota(I32, (n_e, LANES), 1)
    segs_ref[...] = jnp.where(lane == 0, pcnt, jnp.where(lane == 1, seg_off, jnp.where(lane == 2, used, 0.0)))


def _post(x2d, ya, os_, ls_, dils, ga, gb, wa, wb, wo, gf, rwt, rb):
    t, d = x2d.shape
    bw = os_[0].shape[1] // dils[0]
    tm = TM_POST
    nt = t // tm
    tri = jnp.asarray(np.triu(np.ones((tm, tm), np.float32), k=1), dtype=BF16)
    row = lambda w, dil=1: pl.BlockSpec((tm // dil, dil * w), lambda i: (i, 0))
    full = lambda a: pl.BlockSpec(a.shape, lambda i: (0,) * a.ndim)
    return pl.pallas_call(
        functools.partial(_post_kernel, dils=dils),
        grid=(nt,),
        in_specs=[row(d), row(d)] + [row(bw, dil) for dil in dils] + [row(bw, dil) for dil in dils]
                 + [row(d), row(d), full(wa), full(wb), full(wo), full(gf), full(rwt), full(rb), full(tri)],
        scratch_shapes=[pltpu.VMEM((bw // LANES, tm, LANES), F32) for _ in range(2 * len(dils))],
        out_specs=[row(d), pl.BlockSpec((SLOTS, d), lambda i: (i, 0)), row(LANES),
                   pl.BlockSpec((N_EXPERTS, LANES), lambda i: (i, 0))],
        out_shape=[jax.ShapeDtypeStruct((t, d), F32), jax.ShapeDtypeStruct((nt * SLOTS, d), BF16),
                   jax.ShapeDtypeStruct((t, LANES), F32), jax.ShapeDtypeStruct((nt * N_EXPERTS, LANES), F32)],
        compiler_params=pltpu.CompilerParams(
            dimension_semantics=("parallel",), vmem_limit_bytes=V7X_VMEM_LIMIT_BYTES),
        name="post_attn",
    )(x2d, ya, *os_, *ls_, ga, gb, wa, wb, wo, gf, rwt, rb, tri)


def _expert_tiles(pcnt, seg_off, n_tiles):
    nt, n_e = pcnt.shape
    cpt = CHUNKS_PER_TILE
    nch = (pcnt // CHUNK).T
    cum = jnp.cumsum(nch, axis=1)
    total = cum[:, -1]
    tiles_e = (total + cpt - 1) // cpt
    tile_end = jnp.cumsum(tiles_e)
    n_active = tile_end[-1]
    i = jnp.arange(n_tiles, dtype=I32)
    last = jnp.minimum(i, n_active - 1)
    te = jnp.sum((last[:, None] >= tile_end[None, :]).astype(I32), axis=1)
    hot_e = (te[:, None] == jnp.arange(n_e, dtype=I32)[None, :])
    pick = lambda tab: jnp.sum(jnp.where(hot_e[:, :, None], tab[None], 0), axis=1)
    tile_start = jnp.sum(jnp.where(hot_e, (tile_end - tiles_e)[None, :], 0), axis=1)
    total_t = jnp.sum(jnp.where(hot_e, total[None, :], 0), axis=1)
    q = (last - tile_start)[:, None] * cpt + jnp.arange(cpt, dtype=I32)[None, :]
    valid = (q < total_t[:, None]) & (i < n_active)[:, None]
    cum_t, nch_t = pick(cum), pick(nch)
    chunk0_t = pick((seg_off.T + jnp.arange(nt, dtype=I32)[None, :] * SLOTS) // CHUNK)
    jj = jnp.sum((q[:, :, None] >= cum_t[:, None, :]).astype(I32), axis=2)
    hot_j = jj[:, :, None] == jnp.arange(nt, dtype=I32)[None, None, :]
    first = jnp.sum(jnp.where(hot_j, (cum_t - nch_t)[:, None, :], 0), axis=2)
    base = jnp.sum(jnp.where(hot_j, chunk0_t[:, None, :], 0), axis=2)
    src = jnp.where(valid, base + q - first, 0)
    trash = nt * SLOTS // CHUNK + (i % 2)[:, None] * cpt + jnp.arange(cpt, dtype=I32)[None, :]
    dst = jnp.where(valid, src, trash)
    return te.astype(I32), n_active.astype(I32)[None], src.reshape(-1).astype(I32), dst.reshape(-1).astype(I32)


def _expert_kernel(te_ref, na_ref, cs_ref, cd_ref, xg_hbm, wg_ref, bg_ref, wu_ref, bu_ref, wd_ref, bd_ref,
                   yg_hbm, xbuf, ybuf, wbf, zbuf, sem_in, sem_out, sem_zero, *, n_token_tiles):
    i = pl.program_id(0)
    n_active = na_ref[0]
    slot = i % 2
    cpt = CHUNKS_PER_TILE

    zero_starts = [j * SLOTS + TM_POST * TOP_K for j in range(n_token_tiles)]
    zero_starts += [n_token_tiles * SLOTS + j * SLACK_ROWS for j in range(2 * TM_EXPERT // SLACK_ROWS)]

    def zero_copy(n):
        return pltpu.make_async_copy(zbuf, yg_hbm.at[pl.ds(zero_starts[n], SLACK_ROWS), :], sem_zero.at[n])

    @pl.when(i == 0)
    def _():
        zbuf[...] = jnp.zeros_like(zbuf)
        for n in range(len(zero_starts)):
            zero_copy(n).start()

    def in_copy(c, sl, chunk):
        row = pl.multiple_of(chunk * CHUNK, CHUNK)
        return pltpu.make_async_copy(xg_hbm.at[pl.ds(row, CHUNK), :],
                                     xbuf.at[sl, pl.ds(c * CHUNK, CHUNK), :], sem_in.at[sl, c])

    def out_copy(c, sl, chunk):
        row = pl.multiple_of(chunk * CHUNK, CHUNK)
        return pltpu.make_async_copy(ybuf.at[sl, pl.ds(c * CHUNK, CHUNK), :],
                                     yg_hbm.at[pl.ds(row, CHUNK), :], sem_out.at[sl, c])

    def start_gather(tile, sl):
        for c in range(cpt):
            in_copy(c, sl, cs_ref[tile * cpt + c]).start()

    def wait_scatter(sl):
        for c in range(cpt):
            out_copy(c, sl, 0).wait()

    @pl.when(i == 0)
    def _():
        start_gather(0, 0)

    @pl.when(i + 1 < n_active)
    def _():
        start_gather(i + 1, 1 - slot)

    @pl.when(i < n_active)
    def _():
        for c in range(cpt):
            in_copy(c, slot, 0).wait()

        @pl.when((i == 0) | (te_ref[i] != te_ref[jnp.maximum(i - 1, 0)]))
        def _():
            wbf[0] = wg_ref[0].astype(BF16)
            wbf[1] = wu_ref[0].astype(BF16)
            wbf[2] = wd_ref[0].astype(BF16)

        @pl.when(i >= 2)
        def _():
            wait_scatter(slot)

        x = xbuf[slot]
        gate = jnp.minimum(jnp.dot(x, wbf[0], preferred_element_type=F32) + bg_ref[0], SWIGLU_LIMIT)
        up = jnp.clip(jnp.dot(x, wbf[1], preferred_element_type=F32) + bu_ref[0], -SWIGLU_LIMIT, SWIGLU_LIMIT)
        act = gate * jax.nn.sigmoid(SWIGLU_ALPHA * gate) * (up + 1.0)
        y = jnp.dot(act.astype(BF16), wbf[2], preferred_element_type=F32) + bd_ref[0]
        ybuf[slot] = y.astype(BF16)

        @pl.when(i == 0)
        def _():
            for n in range(len(zero_starts)):
                zero_copy(n).wait()

        for c in range(cpt):
            out_copy(c, slot, cd_ref[i * cpt + c]).start()

        @pl.when(i == n_active - 1)
        def _():
            wait_scatter(slot)

            @pl.when(i >= 1)
            def _():
                wait_scatter(1 - slot)


def _experts(xg, te, n_active, src, dst, wg, bg, wu, bu, wd, bd):
    rows, d = xg.shape
    n_e, _, dff = wg.shape
    n_tiles = te.shape[0]
    n_token_tiles = rows // SLOTS
    n_zero = n_token_tiles + 2 * TM_EXPERT // SLACK_ROWS
    w_spec = lambda a, b_: pl.BlockSpec((1, a, b_), lambda i, te_, *_: (te_[i], 0, 0))
    return pl.pallas_call(
        functools.partial(_expert_kernel, n_token_tiles=n_token_tiles),
        grid_spec=pltpu.PrefetchScalarGridSpec(
            num_scalar_prefetch=4,
            grid=(n_tiles,),
            in_specs=[pl.BlockSpec(memory_space=pl.ANY),
                      w_spec(d, dff), w_spec(1, dff), w_spec(d, dff), w_spec(1, dff),
                      w_spec(dff, d), w_spec(1, d)],
            out_specs=pl.BlockSpec(memory_space=pl.ANY),
            scratch_shapes=[pltpu.VMEM((2, TM_EXPERT, d), BF16), pltpu.VMEM((2, TM_EXPERT, d), BF16),
                            pltpu.VMEM((3, d, dff), BF16), pltpu.VMEM((SLACK_ROWS, d), BF16),
                            pltpu.SemaphoreType.DMA((2, CHUNKS_PER_TILE)),
                            pltpu.SemaphoreType.DMA((2, CHUNKS_PER_TILE)),
                            pltpu.SemaphoreType.DMA((n_zero,))]),
        out_shape=jax.ShapeDtypeStruct((rows + 2 * TM_EXPERT, d), BF16),
        compiler_params=pltpu.CompilerParams(
            dimension_semantics=("arbitrary",), vmem_limit_bytes=V7X_VMEM_LIMIT_BYTES),
        name="experts",
    )(te, n_active, src, dst, xg, wg, bg, wu, bu, wd, bd)


def _combine_kernel(x1_ref, yg_ref, meta_ref, segs_ref, gn_ref, o_ref, *, final_norm):
    meta = meta_ref[...]
    tm = meta.shape[0]
    iota_s = lax.broadcasted_iota(I32, (tm, SLOTS), 1)
    wperm = jnp.zeros((tm, SLOTS), F32)
    for k in range(TOP_K):
        wperm = wperm + jnp.where(iota_s == meta[:, k:k + 1].astype(I32), meta[:, TOP_K + k:TOP_K + k + 1], 0.0)
    used = segs_ref[0:1, 2:3].astype(I32)
    row = lax.broadcasted_iota(I32, (SLOTS, 1), 0)
    yg = jnp.where(row < used, yg_ref[...], jnp.zeros((), BF16))
    xo = x1_ref[...] + jnp.dot(wperm.astype(BF16), yg, preferred_element_type=F32)
    o_ref[...] = _rms(xo, gn_ref[...]) if final_norm else xo


def _combine(x1, yg, meta, segs, gn, final_norm):
    t, d = x1.shape
    tm = TM_POST
    return pl.pallas_call(
        functools.partial(_combine_kernel, final_norm=final_norm),
        grid=(t // tm,),
        in_specs=[pl.BlockSpec((tm, d), lambda i: (i, 0)),
                  pl.BlockSpec((SLOTS, d), lambda i: (i, 0)),
                  pl.BlockSpec((tm, LANES), lambda i: (i, 0)),
                  pl.BlockSpec((N_EXPERTS, LANES), lambda i: (i, 0)),
                  pl.BlockSpec((1, d), lambda i: (0, 0))],
        out_specs=pl.BlockSpec((tm, d), lambda i: (i, 0)),
        out_shape=jax.ShapeDtypeStruct((t, d), F32),
        compiler_params=pltpu.CompilerParams(
            dimension_semantics=("parallel",), vmem_limit_bytes=V7X_VMEM_LIMIT_BYTES),
        name="combine",
    )(x1, yg, meta, segs, gn)


def kernel(x, norm_mix, w_in, b_in, sinks, rel_bias, w_branch_a, w_branch_b, w_out, norm_ffn,
           router_w, router_b, w_gate, b_gate, w_up, b_up, w_down, b_down, norm_final):
    b, s, d = x.shape
    t = b * s
    depth = w_in.shape[0]
    a_q_w = A_KV_HEADS * A_GROUP * HEAD_DIM
    a_kv_w = A_KV_HEADS * HEAD_DIM
    b_w = B_HEADS * HEAD_DIM
    n_grp = len(B_GROUPS)
    dils = tuple(dil for _, dil in B_GROUPS)
    segs, col = [], 0
    for kind, width in (("q", a_q_w), ("k", a_kv_w), ("v", a_kv_w)):
        segs.append((col, width, kind, 1))
        col += width
    for kind in ("q", "k", "v"):
        for dil in dils:
            segs.append((col, b_w, kind, dil))
            col += b_w
    for _ in range(2):
        segs.append((col, d, "gate", 1))
        col += d
    segs = tuple(segs)
    n_a = A_KV_HEADS * A_GROUP
    nt = t // TM_POST
    max_chunks = nt * ((TM_POST * TOP_K + N_EXPERTS * (CHUNK - 1)) // CHUNK)
    n_tiles = -(-(max_chunks + N_EXPERTS * (CHUNKS_PER_TILE - 1)) // CHUNKS_PER_TILE)

    bias_a = _band_bias(rel_bias[:, :n_a], A_HALF_WINDOW, TQ_ATTN, 1)
    bias_b = [_band_bias(rel_bias[:, n_a + gi * B_HEADS:n_a + (gi + 1) * B_HEADS],
                         win // (2 * dil), TQ_ATTN, dil) for gi, (win, dil) in enumerate(B_GROUPS)]

    x2d = x.reshape(t, d)
    for layer in range(depth):
        proj = _inproj(x2d, norm_mix[layer][None], w_in[layer].astype(BF16), b_in[layer][None], segs)
        qa, ka, va = proj[:3]
        qb, kb, vb = proj[3:3 + n_grp], proj[3 + n_grp:3 + 2 * n_grp], proj[3 + 2 * n_grp:3 + 3 * n_grp]
        ga, gb = proj[-2:]

        (ya,) = _band_attn(qa.reshape(b, s, a_q_w), ka.reshape(b, s, a_kv_w), va.reshape(b, s, a_kv_w),
                           bias_a, sinks[layer], n_kv=A_KV_HEADS, grp=A_GROUP, hw=A_HALF_WINDOW,
                           tq=TQ_ATTN, block=BLOCK_ATTN_A, reps=1, want_lse=False)
        os_, ls_ = [], []
        for gi, (win, dil) in enumerate(B_GROUPS):
            sub = lambda a: a.reshape(b, s // dil, dil * b_w)
            o, lse = _band_attn(sub(qb[gi]), sub(kb[gi]), sub(vb[gi]), bias_b[gi], None, n_kv=B_HEADS, grp=1,
                                hw=win // (2 * dil), tq=TQ_ATTN, block=BLOCK_ATTN_B, reps=dil,
                                want_lse=True)
            os_.append(o.reshape(t // dil, dil * b_w))
            ls_.append(lse.reshape(t // dil, dil * b_w))

        x1, xg, meta, segs_out = _post(
            x2d, ya.reshape(t, a_q_w), os_, ls_, dils, ga, gb,
            w_branch_a[layer].astype(BF16), w_branch_b[layer].astype(BF16), w_out[layer].astype(BF16),
            norm_ffn[layer][None], router_w[layer].T, router_b[layer][:, None])

        segs3 = segs_out.reshape(nt, N_EXPERTS, LANES)
        te, n_active, src, dst = _expert_tiles(segs3[:, :, 0].astype(I32), segs3[:, :, 1].astype(I32), n_tiles)
        yg = _experts(xg, te, n_active, src, dst, w_gate[layer], b_gate[layer][:, None], w_up[layer],
                      b_up[layer][:, None], w_down[layer], b_down[layer][:, None])
        x2d = _combine(x1, yg, meta, segs_out, norm_final[None], layer == depth - 1)
    return x2d.reshape(b, s, d)
```

```python
import functools

import numpy as np
import jax
import jax.numpy as jnp
from jax import lax
from jax.experimental import pallas as pl
from jax.experimental.pallas import tpu as pltpu

F32 = jnp.float32
BF16 = jnp.bfloat16
I32 = jnp.int32

HEAD_DIM = 64
A_KV_HEADS = 4
A_GROUP = 4
A_HALF_WINDOW = 128
B_GROUPS = ((128, 1), (512, 4), (2048, 16))
B_HEADS = 4
N_BUCKETS = 32
MAX_DISTANCE = 1024
N_EXPERTS = 32
TOP_K = 4
SWIGLU_LIMIT = 7.0
SWIGLU_ALPHA = 1.702
EPS = 1e-5
NEG_INF = -1e30
LOG2E = 1.4426950408889634

V7X_VMEM_LIMIT_BYTES = 56 * 1024 * 1024
LANES = 128
BF16_SUBLANES = 16

TM_PROJ = 512
TQ_ATTN = 128
BLOCK_ATTN_A = 256
BLOCK_ATTN_B = 512
TM_POST = 512
TM_EXPERT = 512

CHUNK = BF16_SUBLANES
CHUNKS_PER_TILE = TM_EXPERT // CHUNK
SLOTS = -(-(TM_POST * TOP_K + N_EXPERTS * (CHUNK - 1)) // LANES) * LANES
SLACK_ROWS = SLOTS - TM_POST * TOP_K
assert (2 * TM_EXPERT) % SLACK_ROWS == 0


def _t5_buckets(rel):
    half = N_BUCKETS // 2
    max_exact = half // 2
    ret = np.where(rel > 0, half, 0)
    n = np.abs(rel)
    large = max_exact + (np.log(np.maximum(n, 1) / max_exact)
                         / np.log(MAX_DISTANCE / max_exact) * (half - max_exact)).astype(np.int32)
    large = np.minimum(large, half - 1)
    return (ret + np.where(n < max_exact, n, large)).astype(np.int32)


def _rms(x, g):
    return x * lax.rsqrt(jnp.mean(x * x, axis=-1, keepdims=True) + EPS) * g


def _inproj_kernel(x_ref, g_ref, w_ref, b_ref, *refs, segs):
    out_refs, scr = refs[:-1], refs[-1]
    h = _rms(x_ref[...], g_ref[...]).astype(BF16)
    for ref, (c0, width, kind, dil) in zip(out_refs, segs):
        acc = jnp.dot(h, w_ref[:, c0:c0 + width], preferred_element_type=F32) + b_ref[:, c0:c0 + width]
        if kind == "q":
            acc = acc * (HEAD_DIM ** -0.5 * LOG2E)
        elif kind == "gate":
            acc = jax.nn.sigmoid(acc)
        if dil == 1:
            ref[...] = acc.astype(ref.dtype)
        else:
            n = acc.shape[0] // dil
            for c in range(width // LANES):
                scr[c] = acc[:, c * LANES:(c + 1) * LANES]
            for r in range(dil):
                for c in range(width // LANES):
                    col = r * width + c * LANES
                    ref[:, col:col + LANES] = scr[c, pl.ds(r, n, stride=dil), :].astype(ref.dtype)


def _inproj(x2d, g, w_bf16, b, segs):
    t, d = x2d.shape
    n = w_bf16.shape[1]
    max_w = max(w for _, w, _, dil in segs if dil > 1)
    return pl.pallas_call(
        functools.partial(_inproj_kernel, segs=segs),
        grid=(t // TM_PROJ,),
        in_specs=[
            pl.BlockSpec((TM_PROJ, d), lambda i: (i, 0)),
            pl.BlockSpec((1, d), lambda i: (0, 0)),
            pl.BlockSpec((d, n), lambda i: (0, 0), pipeline_mode=pl.Buffered(1)),
            pl.BlockSpec((1, n), lambda i: (0, 0)),
        ],
        out_specs=[pl.BlockSpec((TM_PROJ // dil, dil * w), lambda i: (i, 0)) for _, w, _, dil in segs],
        out_shape=[jax.ShapeDtypeStruct((t // dil, dil * w), BF16) for _, w, _, dil in segs],
        scratch_shapes=[pltpu.VMEM((max_w // LANES, TM_PROJ, LANES), F32)],
        compiler_params=pltpu.CompilerParams(
            dimension_semantics=("parallel",), vmem_limit_bytes=V7X_VMEM_LIMIT_BYTES),
        name="inproj",
    )(x2d, g, w_bf16, b)


def _band_bias(table, hw, tq, dil):
    tk = tq + 2 * hw
    off = np.arange(tk)[None, :] - hw - np.arange(tq)[:, None]
    band = np.abs(off) <= hw
    col = np.arange(tk)[None, :]
    masks = np.stack([band & (col >= hw), band, band & (col < hw + tq)])
    onehot = (_t5_buckets(off * dil)[..., None] == np.arange(N_BUCKETS)).astype(np.float32)
    bias = jnp.einsum("qkn,nh->hqk", jnp.asarray(onehot), table.astype(F32),
                      precision=lax.Precision.HIGHEST)
    return jnp.where(masks[:, None], bias[None] * LOG2E, NEG_INF)


def _band_attn_kernel(*refs, n_kv, grp, tq, hw, has_sink, want_lse):
    q_ref, kp, kc, kn, vp, vc, vn, bias_ref = refs[:8]
    rest = refs[8:]
    sink_ref = None
    if has_sink:
        sink_ref, rest = rest[0], rest[1:]
    o_ref = rest[0]
    lse_ref = rest[1] if want_lse else None

    j, nt = pl.program_id(2), pl.num_programs(2)
    k = jnp.concatenate([kp[0], kc[0], kn[0]], axis=0)
    v = jnp.concatenate([vp[0], vc[0], vn[0]], axis=0)
    n_sub = q_ref.shape[1] // tq
    tk = tq + 2 * hw
    ones = jnp.ones((tk, HEAD_DIM), BF16)
    row = lax.broadcasted_iota(I32, (grp * tq, 1), 0)
    for sub in range(n_sub):
        var = 1
        if sub == 0:
            var = jnp.where(j == 0, 0, var)
        if sub == n_sub - 1:
            var = jnp.where(j == nt - 1, 2, var)
        r0 = sub * tq
        for h in range(n_kv):
            k_h = k[r0:r0 + tk, h * HEAD_DIM:(h + 1) * HEAD_DIM]
            v_h = jnp.concatenate([v[r0:r0 + tk, h * HEAD_DIM:(h + 1) * HEAD_DIM], ones], axis=1)
            c0 = h * grp * HEAD_DIM
            q = jnp.concatenate([q_ref[0, r0:r0 + tq, c0 + g * HEAD_DIM:c0 + (g + 1) * HEAD_DIM]
                                 for g in range(grp)], axis=0)
            s = lax.dot_general(q, k_h, (((1,), (1,)), ((), ())), preferred_element_type=F32)
            s = s + bias_ref[var, h]
            m = jnp.max(s, axis=-1, keepdims=True)
            if has_sink:
                sk = jnp.full((grp * tq, 1), sink_ref[h * grp] * LOG2E, F32)
                for g in range(1, grp):
                    sk = jnp.where(row >= g * tq, sink_ref[h * grp + g] * LOG2E, sk)
                m = jnp.maximum(m, sk)
            p = jnp.exp2((s - m).astype(BF16))
            ol = jnp.dot(p, v_h, preferred_element_type=F32)
            l = ol[:, HEAD_DIM:HEAD_DIM + 1]
            if has_sink:
                l = l + jnp.exp2(sk - m)
            o = ol[:, :HEAD_DIM] / l
            if want_lse:
                lse = jnp.broadcast_to(m + jnp.log2(l), o.shape)
            for g in range(grp):
                c = c0 + g * HEAD_DIM
                o_ref[0, r0:r0 + tq, c:c + HEAD_DIM] = o[g * tq:(g + 1) * tq].astype(o_ref.dtype)
                if want_lse:
                    lse_ref[0, r0:r0 + tq, c:c + HEAD_DIM] = lse[g * tq:(g + 1) * tq]


def _band_attn_heads_kernel(q_ref, kp, kc, kn, vp, vc, vn, bias_ref, o_ref, lse_ref, *, n_heads, tq, hw):
    j, nt = pl.program_id(2), pl.num_programs(2)
    k = jnp.concatenate([kp[0], kc[0], kn[0]], axis=0)
    v = jnp.concatenate([vp[0], vc[0], vn[0]], axis=0)
    n_sub = q_ref.shape[1] // tq
    tk = tq + 2 * hw
    width = n_heads * HEAD_DIM
    lane_head = lax.broadcasted_iota(I32, (tq, width), 1) // HEAD_DIM
    ones = jnp.ones((tk, LANES), BF16)
    for sub in range(n_sub):
        var = 1
        if sub == 0:
            var = jnp.where(j == 0, 0, var)
        if sub == n_sub - 1:
            var = jnp.where(j == nt - 1, 2, var)
        r0 = sub * tq
        q = q_ref[0, r0:r0 + tq, :]
        q_bd = jnp.concatenate([jnp.where(lane_head == h, q, jnp.zeros_like(q)) for h in range(n_heads)], axis=0)
        s = lax.dot_general(q_bd, k[r0:r0 + tk], (((1,), (1,)), ((), ())), preferred_element_type=F32)
        s = s + bias_ref[var, 0]
        m = jnp.max(s, axis=-1, keepdims=True)
        p = jnp.exp2((s - m).astype(BF16))
        o_full = jnp.dot(p, v[r0:r0 + tk], preferred_element_type=F32)
        l = jnp.dot(p, ones, preferred_element_type=F32)[:, :1]
        o_sel = jnp.zeros((tq, width), F32)
        m_sel = jnp.zeros((tq, width), F32)
        l_sel = jnp.ones((tq, width), F32)
        for h in range(n_heads):
            hit = lane_head == h
            o_sel = jnp.where(hit, o_full[h * tq:(h + 1) * tq], o_sel)
            m_sel = jnp.where(hit, m[h * tq:(h + 1) * tq], m_sel)
            l_sel = jnp.where(hit, l[h * tq:(h + 1) * tq], l_sel)
        o_ref[0, r0:r0 + tq, :] = (o_sel / l_sel).astype(o_ref.dtype)
        lse_ref[0, r0:r0 + tq, :] = m_sel + jnp.log2(l_sel)


def _band_attn(q, k, v, bias3, sinks, *, n_kv, grp, hw, tq, block, reps, want_lse):
    b, l, _ = q.shape
    hq = n_kv * grp
    qw, kw = hq * HEAD_DIM, n_kv * HEAD_DIM
    nt = l // block
    ratio = block // hw
    nhw = l // hw
    tk = tq + 2 * hw
    assert nt * (block // tq) >= 2

    prev = lambda bi, r, j: (bi, jnp.maximum(j * ratio - 1, 0), r)
    cur = lambda bi, r, j: (bi, j, r)
    nxt = lambda bi, r, j: (bi, jnp.minimum((j + 1) * ratio, nhw - 1), r)
    in_specs = [
        pl.BlockSpec((1, block, qw), cur),
        pl.BlockSpec((1, hw, kw), prev), pl.BlockSpec((1, block, kw), cur), pl.BlockSpec((1, hw, kw), nxt),
        pl.BlockSpec((1, hw, kw), prev), pl.BlockSpec((1, block, kw), cur), pl.BlockSpec((1, hw, kw), nxt),
    ]
    stacked = grp == 1
    bias_shape = (3, 1, hq * tq, tk) if stacked else (3, n_kv, grp * tq, tk)
    in_specs.append(pl.BlockSpec(bias_shape, lambda bi, r, j: (0, 0, 0, 0), pipeline_mode=pl.Buffered(1)))
    args = [q, k, k, k, v, v, v, bias3.reshape(bias_shape)]
    if stacked:
        assert sinks is None and want_lse
        body = functools.partial(_band_attn_heads_kernel, n_heads=n_kv, tq=tq, hw=hw)
    else:
        body = functools.partial(_band_attn_kernel, n_kv=n_kv, grp=grp, tq=tq, hw=hw,
                                 has_sink=sinks is not None, want_lse=want_lse)
    if sinks is not None:
        in_specs.append(pl.BlockSpec(memory_space=pltpu.SMEM))
        args.append(sinks)
    out_specs = [pl.BlockSpec((1, block, qw), cur)]
    out_shape = [jax.ShapeDtypeStruct((b, l, reps * qw), BF16)]
    if want_lse:
        out_specs.append(pl.BlockSpec((1, block, qw), cur))
        out_shape.append(jax.ShapeDtypeStruct((b, l, reps * qw), F32))
    return pl.pallas_call(
        body,
        grid=(b, reps, nt),
        in_specs=in_specs,
        out_specs=out_specs,
        out_shape=out_shape,
        compiler_params=pltpu.CompilerParams(
            dimension_semantics=("parallel", "parallel", "parallel"),
            vmem_limit_bytes=V7X_VMEM_LIMIT_BYTES),
        name="band_attn_a" if sinks is not None else "band_attn_b",
    )(*args)


def _token_order(ref, scr, dil):
    if dil == 1:
        return ref[...].astype(F32)
    n = ref.shape[0]
    width = ref.shape[1] // dil
    for r in range(dil):
        for c in range(width // LANES):
            col = r * width + c * LANES
            scr[c, pl.ds(r, n, stride=dil), :] = ref[:, col:col + LANES].astype(F32)
    return jnp.concatenate([scr[c] for c in range(width // LANES)], axis=1)


def _post_kernel(x_ref, ya_ref, o1, o2, o3, l1, l2, l3, ga_ref, gb_ref, wa_ref, wb_ref, wo_ref,
                 gf_ref, rwt_ref, rb_ref, tri_ref, x1_ref, xg_ref, meta_ref, segs_ref, *scrs, dils):
    os_ = [_token_order(r, scrs[2 * i], dil) for i, (r, dil) in enumerate(zip((o1, o2, o3), dils))]
    ls = [_token_order(r, scrs[2 * i + 1], dil) for i, (r, dil) in enumerate(zip((l1, l2, l3), dils))]
    mx = jnp.maximum(jnp.maximum(ls[0], ls[1]), ls[2])
    es = [jnp.exp2(l - mx) for l in ls]
    den = es[0] + es[1] + es[2]
    yb = (es[0] * os_[0] + es[1] * os_[1] + es[2] * os_[2]) / den
    za = jnp.dot(ya_ref[...], wa_ref[...], preferred_element_type=F32)
    zb = jnp.dot(yb.astype(BF16), wb_ref[...], preferred_element_type=F32)
    merged = ga_ref[...].astype(F32) * za + gb_ref[...].astype(F32) * zb
    x1 = x_ref[...] + jnp.dot(merged.astype(BF16), wo_ref[...], preferred_element_type=F32)
    x1_ref[...] = x1
    h2 = _rms(x1, gf_ref[...])

    logits = lax.dot_general(rwt_ref[...], h2, (((1,), (1,)), ((), ())),
                             precision=lax.Precision.HIGHEST, preferred_element_type=F32) + rb_ref[...]
    n_e, tm = logits.shape
    iota_e = lax.broadcasted_iota(I32, (n_e, tm), 0)
    work = logits
    vals, hots = [], []
    for _ in range(TOP_K):
        mk = jnp.max(work, axis=0, keepdims=True)
        ik = jnp.min(jnp.where(work == mk, iota_e, n_e), axis=0, keepdims=True)
        hot = iota_e == ik
        vals.append(mk)
        hots.append(hot)
        work = jnp.where(hot, -jnp.inf, work)
    exps = [jnp.exp(vk - vals[0]) for vk in vals]
    tot = exps[0] + exps[1] + exps[2] + exps[3]
    probs = [ek / tot for ek in exps]

    sel = jnp.zeros((n_e, tm), F32)
    for hot in hots:
        sel = sel + jnp.where(hot, 1.0, 0.0)
    cnt = jnp.sum(sel, axis=1, keepdims=True)
    pcnt = jnp.floor((cnt + (CHUNK - 1)) / CHUNK) * CHUNK
    r_i = lax.broadcasted_iota(I32, (n_e, n_e), 0)
    c_i = lax.broadcasted_iota(I32, (n_e, n_e), 1)
    pcnt_row = jnp.sum(jnp.where(r_i == c_i, pcnt, 0.0), axis=0, keepdims=True)
    seg_off = jnp.sum(jnp.where(c_i < r_i, pcnt_row, 0.0), axis=1, keepdims=True)
    used = jnp.sum(pcnt, axis=0, keepdims=True)
    before = jnp.dot(sel.astype(BF16), tri_ref[...], preferred_element_type=F32)
    slot_of = seg_off + before
    slots = [jnp.sum(jnp.where(hot, slot_of, 0.0), axis=0, keepdims=True) for hot in hots]

    iota_s = lax.broadcasted_iota(I32, (SLOTS, tm), 0)
    perm = jnp.zeros((SLOTS, tm), F32)
    for sk in slots:
        perm = perm + jnp.where(iota_s == sk.astype(I32), 1.0, 0.0)
    xg_ref[...] = jnp.dot(perm.astype(BF16), h2.astype(BF16), preferred_element_type=F32).astype(BF16)

    rows = slots + probs
    meta_t = jnp.concatenate(rows + [jnp.zeros((LANES - len(rows), tm), F32)], axis=0)
    meta_ref[...] = meta_t.T
    lane = lax.broadcasted_iota(I32, (n_e, LANES), 1)
    segs_ref[...] = jnp.where(lane == 0, pcnt, jnp.where(lane == 1, seg_off, jnp.where(lane == 2, used, 0.0)))


def _post(x2d, ya, os_, ls_, dils, ga, gb, wa, wb, wo, gf, rwt, rb):
    t, d = x2d.shape
    bw = os_[0].shape[1] // dils[0]
    tm = TM_POST
    nt = t // tm
    tri = jnp.asarray(np.triu(np.ones((tm, tm), np.float32), k=1), dtype=BF16)
    row = lambda w, dil=1: pl.BlockSpec((tm // dil, dil * w), lambda i: (i, 0))
    full = lambda a: pl.BlockSpec(a.shape, lambda i: (0,) * a.ndim)
    return pl.pallas_call(
        functools.partial(_post_kernel, dils=dils),
        grid=(nt,),
        in_specs=[row(d), row(d)] + [row(bw, dil) for dil in dils] + [row(bw, dil) for dil in dils]
                 + [row(d), row(d), full(wa), full(wb), full(wo), full(gf), full(rwt), full(rb), full(tri)],
        scratch_shapes=[pltpu.VMEM((bw // LANES, tm, LANES), F32) for _ in range(2 * len(dils))],
        out_specs=[row(d), pl.BlockSpec((SLOTS, d), lambda i: (i, 0)), row(LANES),
                   pl.BlockSpec((N_EXPERTS, LANES), lambda i: (i, 0))],
        out_shape=[jax.ShapeDtypeStruct((t, d), F32), jax.ShapeDtypeStruct((nt * SLOTS, d), BF16),
                   jax.ShapeDtypeStruct((t, LANES), F32), jax.ShapeDtypeStruct((nt * N_EXPERTS, LANES), F32)],
        compiler_params=pltpu.CompilerParams(
            dimension_semantics=("parallel",), vmem_limit_bytes=V7X_VMEM_LIMIT_BYTES),
        name="post_attn",
    )(x2d, ya, *os_, *ls_, ga, gb, wa, wb, wo, gf, rwt, rb, tri)


def _expert_tiles(pcnt, seg_off, n_tiles):
    nt, n_e = pcnt.shape
    cpt = CHUNKS_PER_TILE
    nch = (pcnt // CHUNK).T
    cum = jnp.cumsum(nch, axis=1)
    total = cum[:, -1]
    tiles_e = (total + cpt - 1) // cpt
    tile_end = jnp.cumsum(tiles_e)
    n_active = tile_end[-1]
    i = jnp.arange(n_tiles, dtype=I32)
    last = jnp.minimum(i, n_active - 1)
    te = jnp.sum((last[:, None] >= tile_end[None, :]).astype(I32), axis=1)
    hot_e = (te[:, None] == jnp.arange(n_e, dtype=I32)[None, :])
    pick = lambda tab: jnp.sum(jnp.where(hot_e[:, :, None], tab[None], 0), axis=1)
    tile_start = jnp.sum(jnp.where(hot_e, (tile_end - tiles_e)[None, :], 0), axis=1)
    total_t = jnp.sum(jnp.where(hot_e, total[None, :], 0), axis=1)
    q = (last - tile_start)[:, None] * cpt + jnp.arange(cpt, dtype=I32)[None, :]
    valid = (q < total_t[:, None]) & (i < n_active)[:, None]
    cum_t, nch_t = pick(cum), pick(nch)
    chunk0_t = pick((seg_off.T + jnp.arange(nt, dtype=I32)[None, :] * SLOTS) // CHUNK)
    jj = jnp.sum((q[:, :, None] >= cum_t[:, None, :]).astype(I32), axis=2)
    hot_j = jj[:, :, None] == jnp.arange(nt, dtype=I32)[None, None, :]
    first = jnp.sum(jnp.where(hot_j, (cum_t - nch_t)[:, None, :], 0), axis=2)
    base = jnp.sum(jnp.where(hot_j, chunk0_t[:, None, :], 0), axis=2)
    src = jnp.where(valid, base + q - first, 0)
    trash = nt * SLOTS // CHUNK + (i % 2)[:, None] * cpt + jnp.arange(cpt, dtype=I32)[None, :]
    dst = jnp.where(valid, src, trash)
    group_end = jnp.sum(jnp.where(hot_e, tile_end[None, :], 0), axis=1)
    nxt = jnp.sum((group_end[:, None] >= tile_end[None, :]).astype(I32), axis=1)
    nxt = jnp.where(group_end < n_active, nxt, -1)
    return (te.astype(I32), nxt.astype(I32), n_active.astype(I32)[None],
            src.reshape(-1).astype(I32), dst.reshape(-1).astype(I32))


def _expert_kernel(te_ref, nx_ref, na_ref, cs_ref, cd_ref, xg_hbm, wg_hbm, bg_ref, wu_hbm, bu_ref, wd_hbm,
                   bd_ref, yg_hbm, xbuf, ybuf, wst, wbf, zbuf, sem_in, sem_out, sem_zero, sem_w, *,
                   n_token_tiles):
    i = pl.program_id(0)
    n_active = na_ref[0]
    slot = i % 2
    cpt = CHUNKS_PER_TILE
    w_hbm = (wg_hbm, wu_hbm, wd_hbm)

    def w_copy(n, e):
        return pltpu.make_async_copy(w_hbm[n].at[e], wst.at[n], sem_w.at[n])

    zero_starts = [j * SLOTS + TM_POST * TOP_K for j in range(n_token_tiles)]
    zero_starts += [n_token_tiles * SLOTS + j * SLACK_ROWS for j in range(2 * TM_EXPERT // SLACK_ROWS)]

    def zero_copy(n):
        return pltpu.make_async_copy(zbuf, yg_hbm.at[pl.ds(zero_starts[n], SLACK_ROWS), :], sem_zero.at[n])

    @pl.when(i == 0)
    def _():
        zbuf[...] = jnp.zeros_like(zbuf)
        for n in range(len(zero_starts)):
            zero_copy(n).start()

    def in_copy(c, sl, chunk):
        row = pl.multiple_of(chunk * CHUNK, CHUNK)
        return pltpu.make_async_copy(xg_hbm.at[pl.ds(row, CHUNK), :],
                                     xbuf.at[sl, pl.ds(c * CHUNK, CHUNK), :], sem_in.at[sl, c])

    def out_copy(c, sl, chunk):
        row = pl.multiple_of(chunk * CHUNK, CHUNK)
        return pltpu.make_async_copy(ybuf.at[sl, pl.ds(c * CHUNK, CHUNK), :],
                                     yg_hbm.at[pl.ds(row, CHUNK), :], sem_out.at[sl, c])

    def start_gather(tile, sl):
        for c in range(cpt):
            in_copy(c, sl, cs_ref[tile * cpt + c]).start()

    def wait_scatter(sl):
        for c in range(cpt):
            out_copy(c, sl, 0).wait()

    @pl.when(i == 0)
    def _():
        start_gather(0, 0)
        for n in range(len(w_hbm)):
            w_copy(n, te_ref[0]).start()

    @pl.when(i + 1 < n_active)
    def _():
        start_gather(i + 1, 1 - slot)

    @pl.when(i < n_active)
    def _():
        for c in range(cpt):
            in_copy(c, slot, 0).wait()

        @pl.when((i == 0) | (te_ref[i] != te_ref[jnp.maximum(i - 1, 0)]))
        def _():
            for n in range(len(w_hbm)):
                w_copy(n, 0).wait()
                wbf[n] = wst[n].astype(BF16)

                @pl.when(nx_ref[i] >= 0)
                def _():
                    w_copy(n, nx_ref[i]).start()

        @pl.when(i >= 1)
        def _():
            for c in range(cpt):
                out_copy(c, 1 - slot, cd_ref[(i - 1) * cpt + c]).start()

        x = xbuf[slot]
        gate = jnp.minimum(jnp.dot(x, wbf[0], preferred_element_type=F32) + bg_ref[0], SWIGLU_LIMIT)
        up = jnp.clip(jnp.dot(x, wbf[1], preferred_element_type=F32) + bu_ref[0], -SWIGLU_LIMIT, SWIGLU_LIMIT)
        act = gate * jax.nn.sigmoid(SWIGLU_ALPHA * gate) * (up + 1.0)
        y = jnp.dot(act.astype(BF16), wbf[2], preferred_element_type=F32) + bd_ref[0]

        @pl.when(i >= 2)
        def _():
            wait_scatter(slot)

        ybuf[slot] = y.astype(BF16)

        @pl.when(i == 0)
        def _():
            for n in range(len(zero_starts)):
                zero_copy(n).wait()

        @pl.when(i == n_active - 1)
        def _():
            for c in range(cpt):
                out_copy(c, slot, cd_ref[i * cpt + c]).start()
            wait_scatter(slot)

            @pl.when(i >= 1)
            def _():
                wait_scatter(1 - slot)


def _experts(xg, te, nxt, n_active, src, dst, wg, bg, wu, bu, wd, bd):
    rows, d = xg.shape
    n_e, _, dff = wg.shape
    assert d == dff
    n_tiles = te.shape[0]
    n_token_tiles = rows // SLOTS
    n_zero = n_token_tiles + 2 * TM_EXPERT // SLACK_ROWS
    b_spec = pl.BlockSpec((1, 1, d), lambda i, te_, *_: (te_[i], 0, 0))
    hbm = pl.BlockSpec(memory_space=pl.ANY)
    return pl.pallas_call(
        functools.partial(_expert_kernel, n_token_tiles=n_token_tiles),
        grid_spec=pltpu.PrefetchScalarGridSpec(
            num_scalar_prefetch=5,
            grid=(n_tiles,),
            in_specs=[hbm, hbm, b_spec, hbm, b_spec, hbm, b_spec],
            out_specs=hbm,
            scratch_shapes=[pltpu.VMEM((2, TM_EXPERT, d), BF16), pltpu.VMEM((2, TM_EXPERT, d), BF16),
                            pltpu.VMEM((3, d, dff), F32), pltpu.VMEM((3, d, dff), BF16),
                            pltpu.VMEM((SLACK_ROWS, d), BF16),
                            pltpu.SemaphoreType.DMA((2, CHUNKS_PER_TILE)),
                            pltpu.SemaphoreType.DMA((2, CHUNKS_PER_TILE)),
                            pltpu.SemaphoreType.DMA((n_zero,)),
                            pltpu.SemaphoreType.DMA((3,))]),
        out_shape=jax.ShapeDtypeStruct((rows + 2 * TM_EXPERT, d), BF16),
        compiler_params=pltpu.CompilerParams(
            dimension_semantics=("arbitrary",), vmem_limit_bytes=V7X_VMEM_LIMIT_BYTES),
        name="experts",
    )(te, nxt, n_active, src, dst, xg, wg, bg, wu, bu, wd, bd)


def _combine_kernel(x1_ref, yg_ref, meta_ref, segs_ref, gn_ref, o_ref, *, final_norm):
    meta = meta_ref[...]
    tm = meta.shape[0]
    iota_s = lax.broadcasted_iota(I32, (tm, SLOTS), 1)
    wperm = jnp.zeros((tm, SLOTS), F32)
    for k in range(TOP_K):
        wperm = wperm + jnp.where(iota_s == meta[:, k:k + 1].astype(I32), meta[:, TOP_K + k:TOP_K + k + 1], 0.0)
    used = segs_ref[0:1, 2:3].astype(I32)
    row = lax.broadcasted_iota(I32, (SLOTS, 1), 0)
    yg = jnp.where(row < used, yg_ref[...], jnp.zeros((), BF16))
    xo = x1_ref[...] + jnp.dot(wperm.astype(BF16), yg, preferred_element_type=F32)
    o_ref[...] = _rms(xo, gn_ref[...]) if final_norm else xo


def _combine(x1, yg, meta, segs, gn, final_norm):
    t, d = x1.shape
    tm = TM_POST
    return pl.pallas_call(
        functools.partial(_combine_kernel, final_norm=final_norm),
        grid=(t // tm,),
        in_specs=[pl.BlockSpec((tm, d), lambda i: (i, 0)),
                  pl.BlockSpec((SLOTS, d), lambda i: (i, 0)),
                  pl.BlockSpec((tm, LANES), lambda i: (i, 0)),
                  pl.BlockSpec((N_EXPERTS, LANES), lambda i: (i, 0)),
                  pl.BlockSpec((1, d), lambda i: (0, 0))],
        out_specs=pl.BlockSpec((tm, d), lambda i: (i, 0)),
        out_shape=jax.ShapeDtypeStruct((t, d), F32),
        compiler_params=pltpu.CompilerParams(
            dimension_semantics=("parallel",), vmem_limit_bytes=V7X_VMEM_LIMIT_BYTES),
        name="combine",
    )(x1, yg, meta, segs, gn)


def kernel(x, norm_mix, w_in, b_in, sinks, rel_bias, w_branch_a, w_branch_b, w_out, norm_ffn,
           router_w, router_b, w_gate, b_gate, w_up, b_up, w_down, b_down, norm_final):
    b, s, d = x.shape
    t = b * s
    depth = w_in.shape[0]
    a_q_w = A_KV_HEADS * A_GROUP * HEAD_DIM
    a_kv_w = A_KV_HEADS * HEAD_DIM
    b_w = B_HEADS * HEAD_DIM
    n_grp = len(B_GROUPS)
    dils = tuple(dil for _, dil in B_GROUPS)
    segs, col = [], 0
    for kind, width in (("q", a_q_w), ("k", a_kv_w), ("v", a_kv_w)):
        segs.append((col, width, kind, 1))
        col += width
    for kind in ("q", "k", "v"):
        for dil in dils:
            segs.append((col, b_w, kind, dil))
            col += b_w
    for _ in range(2):
        segs.append((col, d, "gate", 1))
        col += d
    segs = tuple(segs)
    n_a = A_KV_HEADS * A_GROUP
    nt = t // TM_POST
    max_chunks = nt * ((TM_POST * TOP_K + N_EXPERTS * (CHUNK - 1)) // CHUNK)
    n_tiles = -(-(max_chunks + N_EXPERTS * (CHUNKS_PER_TILE - 1)) // CHUNKS_PER_TILE)

    bias_a = _band_bias(rel_bias[:, :n_a], A_HALF_WINDOW, TQ_ATTN, 1)
    bias_b = [_band_bias(rel_bias[:, n_a + gi * B_HEADS:n_a + (gi + 1) * B_HEADS],
                         win // (2 * dil), TQ_ATTN, dil) for gi, (win, dil) in enumerate(B_GROUPS)]

    x2d = x.reshape(t, d)
    for layer in range(depth):
        proj = _inproj(x2d, norm_mix[layer][None], w_in[layer].astype(BF16), b_in[layer][None], segs)
        qa, ka, va = proj[:3]
        qb, kb, vb = proj[3:3 + n_grp], proj[3 + n_grp:3 + 2 * n_grp], proj[3 + 2 * n_grp:3 + 3 * n_grp]
        ga, gb = proj[-2:]

        (ya,) = _band_attn(qa.reshape(b, s, a_q_w), ka.reshape(b, s, a_kv_w), va.reshape(b, s, a_kv_w),
                           bias_a, sinks[layer], n_kv=A_KV_HEADS, grp=A_GROUP, hw=A_HALF_WINDOW,
                           tq=TQ_ATTN, block=BLOCK_ATTN_A, reps=1, want_lse=False)
        os_, ls_ = [], []
        for gi, (win, dil) in enumerate(B_GROUPS):
            sub = lambda a: a.reshape(b, s // dil, dil * b_w)
            o, lse = _band_attn(sub(qb[gi]), sub(kb[gi]), sub(vb[gi]), bias_b[gi], None, n_kv=B_HEADS, grp=1,
                                hw=win // (2 * dil), tq=TQ_ATTN, block=BLOCK_ATTN_B, reps=dil,
                                want_lse=True)
            os_.append(o.reshape(t // dil, dil * b_w))
            ls_.append(lse.reshape(t // dil, dil * b_w))

        x1, xg, meta, segs_out = _post(
            x2d, ya.reshape(t, a_q_w), os_, ls_, dils, ga, gb,
            w_branch_a[layer].astype(BF16), w_branch_b[layer].astype(BF16), w_out[layer].astype(BF16),
            norm_ffn[layer][None], router_w[layer].T, router_b[layer][:, None])

        segs3 = segs_out.reshape(nt, N_EXPERTS, LANES)
        te, nxt, n_active, src, dst = _expert_tiles(segs3[:, :, 0].astype(I32), segs3[:, :, 1].astype(I32), n_tiles)
        yg = _experts(xg, te, nxt, n_active, src, dst, w_gate[layer], b_gate[layer][:, None], w_up[layer],
                      b_up[layer][:, None], w_down[layer], b_down[layer][:, None])
        x2d = _combine(x1, yg, meta, segs_out, norm_final[None], layer == depth - 1)
    return x2d.reshape(b, s, d)
```

```python
import functools

import numpy as np
import jax
import jax.numpy as jnp
from jax import lax
from jax.experimental import pallas as pl
from jax.experimental.pallas import tpu as pltpu

F32 = jnp.float32
BF16 = jnp.bfloat16
I32 = jnp.int32

HEAD_DIM = 64
A_KV_HEADS = 4
A_GROUP = 4
A_HALF_WINDOW = 128
B_GROUPS = ((128, 1), (512, 4), (2048, 16))
B_HEADS = 4
N_BUCKETS = 32
MAX_DISTANCE = 1024
N_EXPERTS = 32
TOP_K = 4
SWIGLU_LIMIT = 7.0
SWIGLU_ALPHA = 1.702
EPS = 1e-5
NEG_INF = -1e30
LOG2E = 1.4426950408889634

V7X_VMEM_LIMIT_BYTES = 56 * 1024 * 1024
LANES = 128
BF16_SUBLANES = 16

TM_PROJ = 512
TQ_ATTN = 128
BLOCK_ATTN_A = 256
BLOCK_ATTN_B = 512
TM_POST = 512
TM_EXPERT = 512

CHUNK = BF16_SUBLANES
CHUNKS_PER_TILE = TM_EXPERT // CHUNK
SLOTS = -(-(TM_POST * TOP_K + N_EXPERTS * (CHUNK - 1)) // LANES) * LANES
SLACK_ROWS = SLOTS - TM_POST * TOP_K
assert (2 * TM_EXPERT) % SLACK_ROWS == 0


def _t5_buckets(rel):
    half = N_BUCKETS // 2
    max_exact = half // 2
    ret = np.where(rel > 0, half, 0)
    n = np.abs(rel)
    large = max_exact + (np.log(np.maximum(n, 1) / max_exact)
                         / np.log(MAX_DISTANCE / max_exact) * (half - max_exact)).astype(np.int32)
    large = np.minimum(large, half - 1)
    return (ret + np.where(n < max_exact, n, large)).astype(np.int32)


def _rms(x, g):
    return x * lax.rsqrt(jnp.mean(x * x, axis=-1, keepdims=True) + EPS) * g


def _inproj_kernel(x_ref, g_ref, w_ref, b_ref, *refs, segs):
    out_refs, scr = refs[:-1], refs[-1]
    h = _rms(x_ref[...], g_ref[...]).astype(BF16)
    for ref, (c0, width, kind, dil) in zip(out_refs, segs):
        acc = jnp.dot(h, w_ref[:, c0:c0 + width], preferred_element_type=F32) + b_ref[:, c0:c0 + width]
        if kind == "q":
            acc = acc * (HEAD_DIM ** -0.5 * LOG2E)
        elif kind == "gate":
            acc = jax.nn.sigmoid(acc)
        if dil == 1:
            ref[...] = acc.astype(ref.dtype)
        else:
            n = acc.shape[0] // dil
            for c in range(width // LANES):
                scr[c] = acc[:, c * LANES:(c + 1) * LANES]
            for r in range(dil):
                for c in range(width // LANES):
                    col = r * width + c * LANES
                    ref[:, col:col + LANES] = scr[c, pl.ds(r, n, stride=dil), :].astype(ref.dtype)


def _inproj(x2d, g, w_bf16, b, segs):
    t, d = x2d.shape
    n = w_bf16.shape[1]
    max_w = max(w for _, w, _, dil in segs if dil > 1)
    return pl.pallas_call(
        functools.partial(_inproj_kernel, segs=segs),
        grid=(t // TM_PROJ,),
        in_specs=[
            pl.BlockSpec((TM_PROJ, d), lambda i: (i, 0)),
            pl.BlockSpec((1, d), lambda i: (0, 0)),
            pl.BlockSpec((d, n), lambda i: (0, 0), pipeline_mode=pl.Buffered(1)),
            pl.BlockSpec((1, n), lambda i: (0, 0)),
        ],
        out_specs=[pl.BlockSpec((TM_PROJ // dil, dil * w), lambda i: (i, 0)) for _, w, _, dil in segs],
        out_shape=[jax.ShapeDtypeStruct((t // dil, dil * w), BF16) for _, w, _, dil in segs],
        scratch_shapes=[pltpu.VMEM((max_w // LANES, TM_PROJ, LANES), F32)],
        compiler_params=pltpu.CompilerParams(
            dimension_semantics=("parallel",), vmem_limit_bytes=V7X_VMEM_LIMIT_BYTES),
        name="inproj",
    )(x2d, g, w_bf16, b)


def _band_bias(table, hw, tq, dil):
    tk = tq + 2 * hw
    off = np.arange(tk)[None, :] - hw - np.arange(tq)[:, None]
    band = np.abs(off) <= hw
    col = np.arange(tk)[None, :]
    masks = np.stack([band & (col >= hw), band, band & (col < hw + tq)])
    onehot = (_t5_buckets(off * dil)[..., None] == np.arange(N_BUCKETS)).astype(np.float32)
    bias = jnp.einsum("qkn,nh->hqk", jnp.asarray(onehot), table.astype(F32),
                      precision=lax.Precision.HIGHEST)
    return jnp.where(masks[:, None], bias[None] * LOG2E, NEG_INF)


def _band_attn_kernel(*refs, n_kv, grp, tq, hw, has_sink, want_lse):
    q_ref, kp, kc, kn, vp, vc, vn, bias_ref = refs[:8]
    rest = refs[8:]
    sink_ref = None
    if has_sink:
        sink_ref, rest = rest[0], rest[1:]
    o_ref = rest[0]
    lse_ref = rest[1] if want_lse else None

    j, nt = pl.program_id(2), pl.num_programs(2)
    k = jnp.concatenate([kp[0], kc[0], kn[0]], axis=0)
    v = jnp.concatenate([vp[0], vc[0], vn[0]], axis=0)
    n_sub = q_ref.shape[1] // tq
    tk = tq + 2 * hw
    ones = jnp.ones((tk, HEAD_DIM), BF16)
    row = lax.broadcasted_iota(I32, (grp * tq, 1), 0)
    for sub in range(n_sub):
        var = 1
        if sub == 0:
            var = jnp.where(j == 0, 0, var)
        if sub == n_sub - 1:
            var = jnp.where(j == nt - 1, 2, var)
        r0 = sub * tq
        for h in range(n_kv):
            k_h = k[r0:r0 + tk, h * HEAD_DIM:(h + 1) * HEAD_DIM]
            v_h = jnp.concatenate([v[r0:r0 + tk, h * HEAD_DIM:(h + 1) * HEAD_DIM], ones], axis=1)
            c0 = h * grp * HEAD_DIM
            q = jnp.concatenate([q_ref[0, r0:r0 + tq, c0 + g * HEAD_DIM:c0 + (g + 1) * HEAD_DIM]
                                 for g in range(grp)], axis=0)
            s = lax.dot_general(q, k_h, (((1,), (1,)), ((), ())), preferred_element_type=F32)
            s = s + bias_ref[var, h]
            m = jnp.max(s, axis=-1, keepdims=True)
            if has_sink:
                sk = jnp.full((grp * tq, 1), sink_ref[h * grp] * LOG2E, F32)
                for g in range(1, grp):
                    sk = jnp.where(row >= g * tq, sink_ref[h * grp + g] * LOG2E, sk)
                m = jnp.maximum(m, sk)
            p = jnp.exp2((s - m).astype(BF16))
            ol = jnp.dot(p, v_h, preferred_element_type=F32)
            l = ol[:, HEAD_DIM:HEAD_DIM + 1]
            if has_sink:
                l = l + jnp.exp2(sk - m)
            o = ol[:, :HEAD_DIM] / l
            if want_lse:
                lse = jnp.broadcast_to(m + jnp.log2(l), o.shape)
            for g in range(grp):
                c = c0 + g * HEAD_DIM
                o_ref[0, r0:r0 + tq, c:c + HEAD_DIM] = o[g * tq:(g + 1) * tq].astype(o_ref.dtype)
                if want_lse:
                    lse_ref[0, r0:r0 + tq, c:c + HEAD_DIM] = lse[g * tq:(g + 1) * tq]


def _band_attn_heads_kernel(q_ref, kp, kc, kn, vp, vc, vn, bias_ref, o_ref, lse_ref, *, n_heads, tq, hw):
    j, nt = pl.program_id(2), pl.num_programs(2)
    k = jnp.concatenate([kp[0], kc[0], kn[0]], axis=0)
    v = jnp.concatenate([vp[0], vc[0], vn[0]], axis=0)
    n_sub = q_ref.shape[1] // tq
    tk = tq + 2 * hw
    width = n_heads * HEAD_DIM
    lane_head = lax.broadcasted_iota(I32, (tq, width), 1) // HEAD_DIM
    ones = jnp.ones((tk, LANES), BF16)
    for sub in range(n_sub):
        var = 1
        if sub == 0:
            var = jnp.where(j == 0, 0, var)
        if sub == n_sub - 1:
            var = jnp.where(j == nt - 1, 2, var)
        r0 = sub * tq
        q = q_ref[0, r0:r0 + tq, :]
        q_bd = jnp.concatenate([jnp.where(lane_head == h, q, jnp.zeros_like(q)) for h in range(n_heads)], axis=0)
        s = lax.dot_general(q_bd, k[r0:r0 + tk], (((1,), (1,)), ((), ())), preferred_element_type=F32)
        s = s + bias_ref[var, 0]
        m = jnp.max(s, axis=-1, keepdims=True)
        p = jnp.exp2((s - m).astype(BF16))
        o_full = jnp.dot(p, v[r0:r0 + tk], preferred_element_type=F32)
        l = jnp.dot(p, ones, preferred_element_type=F32)[:, :1]
        o_sel = jnp.zeros((tq, width), F32)
        m_sel = jnp.zeros((tq, width), F32)
        l_sel = jnp.ones((tq, width), F32)
        for h in range(n_heads):
            hit = lane_head == h
            o_sel = jnp.where(hit, o_full[h * tq:(h + 1) * tq], o_sel)
            m_sel = jnp.where(hit, m[h * tq:(h + 1) * tq], m_sel)
            l_sel = jnp.where(hit, l[h * tq:(h + 1) * tq], l_sel)
        o_ref[0, r0:r0 + tq, :] = (o_sel / l_sel).astype(o_ref.dtype)
        lse_ref[0, r0:r0 + tq, :] = m_sel + jnp.log2(l_sel)


def _band_attn(q, k, v, bias3, sinks, *, n_kv, grp, hw, tq, block, reps, want_lse):
    b, l, _ = q.shape
    hq = n_kv * grp
    qw, kw = hq * HEAD_DIM, n_kv * HEAD_DIM
    nt = l // block
    ratio = block // hw
    nhw = l // hw
    tk = tq + 2 * hw
    assert nt * (block // tq) >= 2

    prev = lambda bi, r, j: (bi, jnp.maximum(j * ratio - 1, 0), r)
    cur = lambda bi, r, j: (bi, j, r)
    nxt = lambda bi, r, j: (bi, jnp.minimum((j + 1) * ratio, nhw - 1), r)
    in_specs = [
        pl.BlockSpec((1, block, qw), cur),
        pl.BlockSpec((1, hw, kw), prev), pl.BlockSpec((1, block, kw), cur), pl.BlockSpec((1, hw, kw), nxt),
        pl.BlockSpec((1, hw, kw), prev), pl.BlockSpec((1, block, kw), cur), pl.BlockSpec((1, hw, kw), nxt),
    ]
    stacked = grp == 1
    bias_shape = (3, 1, hq * tq, tk) if stacked else (3, n_kv, grp * tq, tk)
    in_specs.append(pl.BlockSpec(bias_shape, lambda bi, r, j: (0, 0, 0, 0), pipeline_mode=pl.Buffered(1)))
    args = [q, k, k, k, v, v, v, bias3.reshape(bias_shape)]
    if stacked:
        assert sinks is None and want_lse
        body = functools.partial(_band_attn_heads_kernel, n_heads=n_kv, tq=tq, hw=hw)
    else:
        body = functools.partial(_band_attn_kernel, n_kv=n_kv, grp=grp, tq=tq, hw=hw,
                                 has_sink=sinks is not None, want_lse=want_lse)
    if sinks is not None:
        in_specs.append(pl.BlockSpec(memory_space=pltpu.SMEM))
        args.append(sinks)
    out_specs = [pl.BlockSpec((1, block, qw), cur)]
    out_shape = [jax.ShapeDtypeStruct((b, l, reps * qw), BF16)]
    if want_lse:
        out_specs.append(pl.BlockSpec((1, block, qw), cur))
        out_shape.append(jax.ShapeDtypeStruct((b, l, reps * qw), F32))
    return pl.pallas_call(
        body,
        grid=(b, reps, nt),
        in_specs=in_specs,
        out_specs=out_specs,
        out_shape=out_shape,
        compiler_params=pltpu.CompilerParams(
            dimension_semantics=("parallel", "parallel", "parallel"),
            vmem_limit_bytes=V7X_VMEM_LIMIT_BYTES),
        name="band_attn_a" if sinks is not None else "band_attn_b",
    )(*args)


def _token_order(ref, scr, dil):
    if dil == 1:
        return ref[...].astype(F32)
    n = ref.shape[0]
    width = ref.shape[1] // dil
    for r in range(dil):
        for c in range(width // LANES):
            col = r * width + c * LANES
            scr[c, pl.ds(r, n, stride=dil), :] = ref[:, col:col + LANES].astype(F32)
    return jnp.concatenate([scr[c] for c in range(width // LANES)], axis=1)


def _post_kernel(x_ref, ya_ref, o1, o2, o3, l1, l2, l3, ga_ref, gb_ref, wa_ref, wb_ref, wo_ref,
                 gf_ref, rwt_ref, rb_ref, tri_ref, x1_ref, xg_ref, meta_ref, segs_ref, *scrs, dils):
    os_ = [_token_order(r, scrs[2 * i], dil) for i, (r, dil) in enumerate(zip((o1, o2, o3), dils))]
    ls = [_token_order(r, scrs[2 * i + 1], dil) for i, (r, dil) in enumerate(zip((l1, l2, l3), dils))]
    mx = jnp.maximum(jnp.maximum(ls[0], ls[1]), ls[2])
    es = [jnp.exp2(l - mx) for l in ls]
    den = es[0] + es[1] + es[2]
    yb = (es[0] * os_[0] + es[1] * os_[1] + es[2] * os_[2]) / den
    za = jnp.dot(ya_ref[...], wa_ref[...], preferred_element_type=F32)
    zb = jnp.dot(yb.astype(BF16), wb_ref[...], preferred_element_type=F32)
    merged = ga_ref[...].astype(F32) * za + gb_ref[...].astype(F32) * zb
    x1 = x_ref[...] + jnp.dot(merged.astype(BF16), wo_ref[...], preferred_element_type=F32)
    x1_ref[...] = x1
    h2 = _rms(x1, gf_ref[...])

    logits = lax.dot_general(rwt_ref[...], h2, (((1,), (1,)), ((), ())),
                             precision=lax.Precision.HIGHEST, preferred_element_type=F32) + rb_ref[...]
    n_e, tm = logits.shape
    iota_e = lax.broadcasted_iota(I32, (n_e, tm), 0)
    work = logits
    vals, hots = [], []
    for _ in range(TOP_K):
        mk = jnp.max(work, axis=0, keepdims=True)
        ik = jnp.min(jnp.where(work == mk, iota_e, n_e), axis=0, keepdims=True)
        hot = iota_e == ik
        vals.append(mk)
        hots.append(hot)
        work = jnp.where(hot, -jnp.inf, work)
    exps = [jnp.exp(vk - vals[0]) for vk in vals]
    tot = exps[0] + exps[1] + exps[2] + exps[3]
    probs = [ek / tot for ek in exps]

    sel = jnp.zeros((n_e, tm), F32)
    for hot in hots:
        sel = sel + jnp.where(hot, 1.0, 0.0)
    cnt = jnp.sum(sel, axis=1, keepdims=True)
    pcnt = jnp.floor((cnt + (CHUNK - 1)) / CHUNK) * CHUNK
    r_i = lax.broadcasted_iota(I32, (n_e, n_e), 0)
    c_i = lax.broadcasted_iota(I32, (n_e, n_e), 1)
    pcnt_row = jnp.sum(jnp.where(r_i == c_i, pcnt, 0.0), axis=0, keepdims=True)
    seg_off = jnp.sum(jnp.where(c_i < r_i, pcnt_row, 0.0), axis=1, keepdims=True)
    used = jnp.sum(pcnt, axis=0, keepdims=True)
    before = jnp.dot(sel.astype(BF16), tri_ref[...], preferred_element_type=F32)
    slot_of = seg_off + before
    slots = [jnp.sum(jnp.where(hot, slot_of, 0.0), axis=0, keepdims=True) for hot in hots]

    iota_s = lax.broadcasted_iota(I32, (SLOTS, tm), 0)
    perm = jnp.zeros((SLOTS, tm), F32)
    for sk in slots:
        perm = perm + jnp.where(iota_s == sk.astype(I32), 1.0, 0.0)
    xg_ref[...] = jnp.dot(perm.astype(BF16), h2.astype(BF16), preferred_element_type=F32).astype(BF16)

    rows = slots + probs
    meta_t = jnp.concatenate(rows + [jnp.zeros((LANES - len(rows), tm), F32)], axis=0)
    meta_ref[...] = meta_t.T
    lane = lax.broadcasted_iota(I32, (n_e, LANES), 1)
    segs_ref[...] = jnp.where(lane == 0, pcnt, jnp.where(lane == 1, seg_off, jnp.where(lane == 2, used, 0.0)))


def _post(x2d, ya, os_, ls_, dils, ga, gb, wa, wb, wo, gf, rwt, rb):
    t, d = x2d.shape
    bw = os_[0].shape[1] // dils[0]
    tm = TM_POST
    nt = t // tm
    tri = jnp.asarray(np.triu(np.ones((tm, tm), np.float32), k=1), dtype=BF16)
    row = lambda w, dil=1: pl.BlockSpec((tm // dil, dil * w), lambda i: (i, 0))
    full = lambda a: pl.BlockSpec(a.shape, lambda i: (0,) * a.ndim)
    return pl.pallas_call(
        functools.partial(_post_kernel, dils=dils),
        grid=(nt,),
        in_specs=[row(d), row(d)] + [row(bw, dil) for dil in dils] + [row(bw, dil) for dil in dils]
                 + [row(d), row(d), full(wa), full(wb), full(wo), full(gf), full(rwt), full(rb), full(tri)],
        scratch_shapes=[pltpu.VMEM((bw // LANES, tm, LANES), F32) for _ in range(2 * len(dils))],
        out_specs=[row(d), pl.BlockSpec((SLOTS, d), lambda i: (i, 0)), row(LANES),
                   pl.BlockSpec((N_EXPERTS, LANES), lambda i: (i, 0))],
        out_shape=[jax.ShapeDtypeStruct((t, d), F32), jax.ShapeDtypeStruct((nt * SLOTS, d), BF16),
                   jax.ShapeDtypeStruct((t, LANES), F32), jax.ShapeDtypeStruct((nt * N_EXPERTS, LANES), F32)],
        compiler_params=pltpu.CompilerParams(
            dimension_semantics=("parallel",), vmem_limit_bytes=V7X_VMEM_LIMIT_BYTES),
        name="post_attn",
    )(x2d, ya, *os_, *ls_, ga, gb, wa, wb, wo, gf, rwt, rb, tri)


def _expert_tiles(pcnt, seg_off, n_tiles):
    nt, n_e = pcnt.shape
    cpt = CHUNKS_PER_TILE
    nch = (pcnt // CHUNK).T
    cum = jnp.cumsum(nch, axis=1)
    total = cum[:, -1]
    tiles_e = (total + cpt - 1) // cpt
    tile_end = jnp.cumsum(tiles_e)
    n_active = tile_end[-1]
    i = jnp.arange(n_tiles, dtype=I32)
    last = jnp.minimum(i, n_active - 1)
    te = jnp.sum((last[:, None] >= tile_end[None, :]).astype(I32), axis=1)
    hot_e = (te[:, None] == jnp.arange(n_e, dtype=I32)[None, :])
    pick = lambda tab: jnp.sum(jnp.where(hot_e[:, :, None], tab[None], 0), axis=1)
    tile_start = jnp.sum(jnp.where(hot_e, (tile_end - tiles_e)[None, :], 0), axis=1)
    total_t = jnp.sum(jnp.where(hot_e, total[None, :], 0), axis=1)
    q = (last - tile_start)[:, None] * cpt + jnp.arange(cpt, dtype=I32)[None, :]
    valid = (q < total_t[:, None]) & (i < n_active)[:, None]
    cum_t, nch_t = pick(cum), pick(nch)
    chunk0_t = pick((seg_off.T + jnp.arange(nt, dtype=I32)[None, :] * SLOTS) // CHUNK)
    jj = jnp.sum((q[:, :, None] >= cum_t[:, None, :]).astype(I32), axis=2)
    hot_j = jj[:, :, None] == jnp.arange(nt, dtype=I32)[None, None, :]
    first = jnp.sum(jnp.where(hot_j, (cum_t - nch_t)[:, None, :], 0), axis=2)
    base = jnp.sum(jnp.where(hot_j, chunk0_t[:, None, :], 0), axis=2)
    src = jnp.where(valid, base + q - first, 0)
    trash = nt * SLOTS // CHUNK + (i % 2)[:, None] * cpt + jnp.arange(cpt, dtype=I32)[None, :]
    dst = jnp.where(valid, src, trash)
    group_end = jnp.sum(jnp.where(hot_e, tile_end[None, :], 0), axis=1)
    nxt = jnp.sum((group_end[:, None] >= tile_end[None, :]).astype(I32), axis=1)
    nxt = jnp.where(group_end < n_active, nxt, -1)
    src = jnp.concatenate([src, jnp.zeros((1, cpt), src.dtype)], axis=0)
    dst = jnp.concatenate([nt * SLOTS // CHUNK + cpt + jnp.arange(cpt, dtype=dst.dtype)[None, :], dst], axis=0)
    return (te.astype(I32), nxt.astype(I32), n_active.astype(I32)[None],
            src.reshape(-1).astype(I32), dst.reshape(-1).astype(I32))


def _expert_kernel(te_ref, nx_ref, na_ref, cs_ref, cd_ref, xg_hbm, wg_hbm, bg_ref, wu_hbm, bu_ref, wd_hbm,
                   bd_ref, yg_hbm, xbuf0, xbuf1, ybuf0, ybuf1, wst, wbf, zbuf, sem_in, sem_out, sem_zero, sem_w,
                   *, n_token_tiles):
    i = pl.program_id(0)
    n_active = na_ref[0]
    cpt = CHUNKS_PER_TILE
    w_hbm = (wg_hbm, wu_hbm, wd_hbm)
    xbufs, ybufs = (xbuf0, xbuf1), (ybuf0, ybuf1)

    def w_copy(n, e):
        return pltpu.make_async_copy(w_hbm[n].at[e], wst.at[n], sem_w.at[n])

    slack_starts = [j * SLOTS + TM_POST * TOP_K for j in range(n_token_tiles)]
    sink_starts = [n_token_tiles * SLOTS + j * SLACK_ROWS for j in range(2 * TM_EXPERT // SLACK_ROWS)]
    zero_starts = slack_starts + sink_starts

    def zero_copy(n):
        return pltpu.make_async_copy(
            zbuf, yg_hbm.at[pl.ds(zero_starts[n] // CHUNK, SLACK_ROWS // CHUNK)], sem_zero.at[n])

    def in_copy(c, s, chunk):
        return pltpu.make_async_copy(xg_hbm.at[chunk], xbufs[s].at[c], sem_in.at[s, c])

    def out_copy(c, s, chunk):
        return pltpu.make_async_copy(ybufs[s].at[c], yg_hbm.at[chunk], sem_out.at[s, c])

    @pl.when(i == 0)
    def _():
        zbuf[...] = jnp.zeros_like(zbuf)
        ybuf1[...] = jnp.zeros_like(ybuf1)
        for n in range(len(zero_starts)):
            zero_copy(n).start()
        for c in range(cpt):
            in_copy(c, 0, cs_ref[c]).start()
        for n in range(len(w_hbm)):
            w_copy(n, te_ref[0]).start()
        for n in range(len(slack_starts), len(zero_starts)):
            zero_copy(n).wait()

    def tile(s):
        for c in range(cpt):
            in_copy(c, s, 0).wait()

        @pl.when(i >= 1)
        def _():
            for c in range(cpt):
                out_copy(c, s, 0).wait()

        @pl.when((i == 0) | (te_ref[i] != te_ref[jnp.maximum(i - 1, 0)]))
        def _():
            for n in range(len(w_hbm)):
                w_copy(n, 0).wait()
                wbf[n] = wst[n].astype(BF16)

                @pl.when(nx_ref[i] >= 0)
                def _():
                    w_copy(n, nx_ref[i]).start()

        for c in range(cpt):
            out_copy(c, 1 - s, cd_ref[i * cpt + c]).start()
        for c in range(cpt):
            in_copy(c, 1 - s, cs_ref[(i + 1) * cpt + c]).start()
        x = xbufs[s][...].reshape(TM_EXPERT, -1)
        gate = jnp.minimum(jnp.dot(x, wbf[0], preferred_element_type=F32) + bg_ref[0], SWIGLU_LIMIT)
        up = jnp.clip(jnp.dot(x, wbf[1], preferred_element_type=F32) + bu_ref[0], -SWIGLU_LIMIT, SWIGLU_LIMIT)
        act = gate * jax.nn.sigmoid(SWIGLU_ALPHA * gate) * (up + 1.0)
        y = jnp.dot(act.astype(BF16), wbf[2], preferred_element_type=F32) + bd_ref[0]
        ybufs[s][...] = y.astype(BF16).reshape(ybufs[s].shape)

        @pl.when(i == 0)
        def _():
            for n in range(len(slack_starts)):
                zero_copy(n).wait()

        @pl.when(i == n_active - 1)
        def _():
            for c in range(cpt):
                out_copy(c, s, cd_ref[(i + 1) * cpt + c]).start()
            for c in range(cpt):
                out_copy(c, s, 0).wait()
                out_copy(c, 1 - s, 0).wait()
                in_copy(c, 1 - s, 0).wait()

    for s in range(2):
        pl.when((i < n_active) & (i % 2 == s))(functools.partial(tile, s))


def _experts(xg, te, nxt, n_active, src, dst, wg, bg, wu, bu, wd, bd):
    rows, d = xg.shape
    n_e, _, dff = wg.shape
    assert d == dff
    n_tiles = te.shape[0]
    n_token_tiles = rows // SLOTS
    n_zero = n_token_tiles + 2 * TM_EXPERT // SLACK_ROWS
    b_spec = pl.BlockSpec((1, 1, d), lambda i, te_, *_: (te_[i], 0, 0))
    hbm = pl.BlockSpec(memory_space=pl.ANY)
    tile_buf = pltpu.VMEM((CHUNKS_PER_TILE, CHUNK, d), BF16)
    yg = pl.pallas_call(
        functools.partial(_expert_kernel, n_token_tiles=n_token_tiles),
        grid_spec=pltpu.PrefetchScalarGridSpec(
            num_scalar_prefetch=5,
            grid=(n_tiles,),
            in_specs=[hbm, hbm, b_spec, hbm, b_spec, hbm, b_spec],
            out_specs=hbm,
            scratch_shapes=[tile_buf, tile_buf, tile_buf, tile_buf,
                            pltpu.VMEM((3, d, dff), F32), pltpu.VMEM((3, d, dff), BF16),
                            pltpu.VMEM((SLACK_ROWS // CHUNK, CHUNK, d), BF16),
                            pltpu.SemaphoreType.DMA((2, CHUNKS_PER_TILE)),
                            pltpu.SemaphoreType.DMA((2, CHUNKS_PER_TILE)),
                            pltpu.SemaphoreType.DMA((n_zero,)),
                            pltpu.SemaphoreType.DMA((3,))]),
        out_shape=jax.ShapeDtypeStruct(((rows + 2 * TM_EXPERT) // CHUNK, CHUNK, d), BF16),
        compiler_params=pltpu.CompilerParams(
            dimension_semantics=("arbitrary",), vmem_limit_bytes=V7X_VMEM_LIMIT_BYTES),
        name="experts",
    )(te, nxt, n_active, src, dst, xg.reshape(rows // CHUNK, CHUNK, d), wg, bg, wu, bu, wd, bd)
    return yg.reshape(rows + 2 * TM_EXPERT, d)


def _combine_kernel(x1_ref, yg_ref, meta_ref, segs_ref, gn_ref, o_ref, *, final_norm):
    meta = meta_ref[...]
    tm = meta.shape[0]
    iota_s = lax.broadcasted_iota(I32, (tm, SLOTS), 1)
    wperm = jnp.zeros((tm, SLOTS), F32)
    for k in range(TOP_K):
        wperm = wperm + jnp.where(iota_s == meta[:, k:k + 1].astype(I32), meta[:, TOP_K + k:TOP_K + k + 1], 0.0)
    used = segs_ref[0:1, 2:3].astype(I32)
    row = lax.broadcasted_iota(I32, (SLOTS, 1), 0)
    yg = jnp.where(row < used, yg_ref[...], jnp.zeros((), BF16))
    xo = x1_ref[...] + jnp.dot(wperm.astype(BF16), yg, preferred_element_type=F32)
    o_ref[...] = _rms(xo, gn_ref[...]) if final_norm else xo


def _combine(x1, yg, meta, segs, gn, final_norm):
    t, d = x1.shape
    tm = TM_POST
    return pl.pallas_call(
        functools.partial(_combine_kernel, final_norm=final_norm),
        grid=(t // tm,),
        in_specs=[pl.BlockSpec((tm, d), lambda i: (i, 0)),
                  pl.BlockSpec((SLOTS, d), lambda i: (i, 0)),
                  pl.BlockSpec((tm, LANES), lambda i: (i, 0)),
                  pl.BlockSpec((N_EXPERTS, LANES), lambda i: (i, 0)),
                  pl.BlockSpec((1, d), lambda i: (0, 0))],
        out_specs=pl.BlockSpec((tm, d), lambda i: (i, 0)),
        out_shape=jax.ShapeDtypeStruct((t, d), F32),
        compiler_params=pltpu.CompilerParams(
            dimension_semantics=("parallel",), vmem_limit_bytes=V7X_VMEM_LIMIT_BYTES),
        name="combine",
    )(x1, yg, meta, segs, gn)


def kernel(x, norm_mix, w_in, b_in, sinks, rel_bias, w_branch_a, w_branch_b, w_out, norm_ffn,
           router_w, router_b, w_gate, b_gate, w_up, b_up, w_down, b_down, norm_final):
    b, s, d = x.shape
    t = b * s
    depth = w_in.shape[0]
    a_q_w = A_KV_HEADS * A_GROUP * HEAD_DIM
    a_kv_w = A_KV_HEADS * HEAD_DIM
    b_w = B_HEADS * HEAD_DIM
    n_grp = len(B_GROUPS)
    dils = tuple(dil for _, dil in B_GROUPS)
    segs, col = [], 0
    for kind, width in (("q", a_q_w), ("k", a_kv_w), ("v", a_kv_w)):
        segs.append((col, width, kind, 1))
        col += width
    for kind in ("q", "k", "v"):
        for dil in dils:
            segs.append((col, b_w, kind, dil))
            col += b_w
    for _ in range(2):
        segs.append((col, d, "gate", 1))
        col += d
    segs = tuple(segs)
    n_a = A_KV_HEADS * A_GROUP
    nt = t // TM_POST
    max_chunks = nt * ((TM_POST * TOP_K + N_EXPERTS * (CHUNK - 1)) // CHUNK)
    n_tiles = -(-(max_chunks + N_EXPERTS * (CHUNKS_PER_TILE - 1)) // CHUNKS_PER_TILE)

    bias_a = _band_bias(rel_bias[:, :n_a], A_HALF_WINDOW, TQ_ATTN, 1)
    bias_b = [_band_bias(rel_bias[:, n_a + gi * B_HEADS:n_a + (gi + 1) * B_HEADS],
                         win // (2 * dil), TQ_ATTN, dil) for gi, (win, dil) in enumerate(B_GROUPS)]

    x2d = x.reshape(t, d)
    for layer in range(depth):
        proj = _inproj(x2d, norm_mix[layer][None], w_in[layer].astype(BF16), b_in[layer][None], segs)
        qa, ka, va = proj[:3]
        qb, kb, vb = proj[3:3 + n_grp], proj[3 + n_grp:3 + 2 * n_grp], proj[3 + 2 * n_grp:3 + 3 * n_grp]
        ga, gb = proj[-2:]

        (ya,) = _band_attn(qa.reshape(b, s, a_q_w), ka.reshape(b, s, a_kv_w), va.reshape(b, s, a_kv_w),
                           bias_a, sinks[layer], n_kv=A_KV_HEADS, grp=A_GROUP, hw=A_HALF_WINDOW,
                           tq=TQ_ATTN, block=BLOCK_ATTN_A, reps=1, want_lse=False)
        os_, ls_ = [], []
        for gi, (win, dil) in enumerate(B_GROUPS):
            sub = lambda a: a.reshape(b, s // dil, dil * b_w)
            o, lse = _band_attn(sub(qb[gi]), sub(kb[gi]), sub(vb[gi]), bias_b[gi], None, n_kv=B_HEADS, grp=1,
                                hw=win // (2 * dil), tq=TQ_ATTN, block=BLOCK_ATTN_B, reps=dil,
                                want_lse=True)
            os_.append(o.reshape(t // dil, dil * b_w))
            ls_.append(lse.reshape(t // dil, dil * b_w))

        x1, xg, meta, segs_out = _post(
            x2d, ya.reshape(t, a_q_w), os_, ls_, dils, ga, gb,
            w_branch_a[layer].astype(BF16), w_branch_b[layer].astype(BF16), w_out[layer].astype(BF16),
            norm_ffn[layer][None], router_w[layer].T, router_b[layer][:, None])

        segs3 = segs_out.reshape(nt, N_EXPERTS, LANES)
        te, nxt, n_active, src, dst = _expert_tiles(segs3[:, :, 0].astype(I32), segs3[:, :, 1].astype(I32), n_tiles)
        yg = _experts(xg, te, nxt, n_active, src, dst, w_gate[layer], b_gate[layer][:, None], w_up[layer],
                      b_up[layer][:, None], w_down[layer], b_down[layer][:, None])
        x2d = _combine(x1, yg, meta, segs_out, norm_final[None], layer == depth - 1)
    return x2d.reshape(b, s, d)
```

```python
import functools

import numpy as np
import jax
import jax.numpy as jnp
from jax import lax
from jax.experimental import pallas as pl
from jax.experimental.pallas import tpu as pltpu

F32 = jnp.float32
BF16 = jnp.bfloat16
I32 = jnp.int32

HEAD_DIM = 64
A_KV_HEADS = 4
A_GROUP = 4
A_HALF_WINDOW = 128
B_GROUPS = ((128, 1), (512, 4), (2048, 16))
B_HEADS = 4
N_BUCKETS = 32
MAX_DISTANCE = 1024
N_EXPERTS = 32
TOP_K = 4
SWIGLU_LIMIT = 7.0
SWIGLU_ALPHA = 1.702
EPS = 1e-5
NEG_INF = -1e30
LOG2E = 1.4426950408889634

V7X_VMEM_LIMIT_BYTES = 56 * 1024 * 1024
LANES = 128
BF16_SUBLANES = 16

TM_PROJ = 512
TQ_ATTN = 128
BLOCK_ATTN_A = 256
BLOCK_ATTN_B = 512
TM_POST = 512
TM_EXPERT = 512

CHUNK = BF16_SUBLANES
CHUNKS_PER_TILE = TM_EXPERT // CHUNK
SLOTS = -(-(TM_POST * TOP_K + N_EXPERTS * (CHUNK - 1)) // LANES) * LANES
SLACK_ROWS = SLOTS - TM_POST * TOP_K
assert (2 * TM_EXPERT) % SLACK_ROWS == 0
PERM_BLOCK = 512
assert SLOTS % PERM_BLOCK == 0


def _t5_buckets(rel):
    half = N_BUCKETS // 2
    max_exact = half // 2
    ret = np.where(rel > 0, half, 0)
    n = np.abs(rel)
    large = max_exact + (np.log(np.maximum(n, 1) / max_exact)
                         / np.log(MAX_DISTANCE / max_exact) * (half - max_exact)).astype(np.int32)
    large = np.minimum(large, half - 1)
    return (ret + np.where(n < max_exact, n, large)).astype(np.int32)


def _rms(x, g):
    return x * lax.rsqrt(jnp.mean(x * x, axis=-1, keepdims=True) + EPS) * g


def _inproj_kernel(x_ref, g_ref, w_ref, b_ref, *refs, segs):
    out_refs, scr = refs[:-1], refs[-1]
    h = _rms(x_ref[...], g_ref[...]).astype(BF16)
    for ref, (c0, width, kind, dil) in zip(out_refs, segs):
        acc = jnp.dot(h, w_ref[:, c0:c0 + width], preferred_element_type=F32) + b_ref[:, c0:c0 + width]
        if kind == "q":
            acc = acc * (HEAD_DIM ** -0.5 * LOG2E)
        elif kind == "gate":
            acc = jax.nn.sigmoid(acc)
        if dil == 1:
            ref[...] = acc.astype(ref.dtype)
        else:
            n = acc.shape[0] // dil
            for c in range(width // LANES):
                scr[c] = acc[:, c * LANES:(c + 1) * LANES]
            for r in range(dil):
                for c in range(width // LANES):
                    col = r * width + c * LANES
                    ref[:, col:col + LANES] = scr[c, pl.ds(r, n, stride=dil), :].astype(ref.dtype)


def _inproj(x2d, g, w_bf16, b, segs):
    t, d = x2d.shape
    n = w_bf16.shape[1]
    max_w = max(w for _, w, _, dil in segs if dil > 1)
    return pl.pallas_call(
        functools.partial(_inproj_kernel, segs=segs),
        grid=(t // TM_PROJ,),
        in_specs=[
            pl.BlockSpec((TM_PROJ, d), lambda i: (i, 0)),
            pl.BlockSpec((1, d), lambda i: (0, 0)),
            pl.BlockSpec((d, n), lambda i: (0, 0), pipeline_mode=pl.Buffered(1)),
            pl.BlockSpec((1, n), lambda i: (0, 0)),
        ],
        out_specs=[pl.BlockSpec((TM_PROJ // dil, dil * w), lambda i: (i, 0)) for _, w, _, dil in segs],
        out_shape=[jax.ShapeDtypeStruct((t // dil, dil * w), BF16) for _, w, _, dil in segs],
        scratch_shapes=[pltpu.VMEM((max_w // LANES, TM_PROJ, LANES), F32)],
        compiler_params=pltpu.CompilerParams(
            dimension_semantics=("parallel",), vmem_limit_bytes=V7X_VMEM_LIMIT_BYTES),
        name="inproj",
    )(x2d, g, w_bf16, b)


def _band_bias(table, hw, tq, dil):
    tk = tq + 2 * hw
    off = np.arange(tk)[None, :] - hw - np.arange(tq)[:, None]
    band = np.abs(off) <= hw
    col = np.arange(tk)[None, :]
    masks = np.stack([band & (col >= hw), band, band & (col < hw + tq)])
    onehot = (_t5_buckets(off * dil)[..., None] == np.arange(N_BUCKETS)).astype(np.float32)
    bias = jnp.einsum("qkn,nh->hqk", jnp.asarray(onehot), table.astype(F32),
                      precision=lax.Precision.HIGHEST)
    return jnp.where(masks[:, None], bias[None] * LOG2E, NEG_INF)


def _band_attn_kernel(*refs, n_kv, grp, tq, hw, has_sink, want_lse):
    q_ref, kp, kc, kn, vp, vc, vn, bias_ref = refs[:8]
    rest = refs[8:]
    sink_ref = None
    if has_sink:
        sink_ref, rest = rest[0], rest[1:]
    o_ref = rest[0]
    lse_ref = rest[1] if want_lse else None

    j, nt = pl.program_id(2), pl.num_programs(2)
    k = jnp.concatenate([kp[0], kc[0], kn[0]], axis=0)
    v = jnp.concatenate([vp[0], vc[0], vn[0]], axis=0)
    n_sub = q_ref.shape[1] // tq
    tk = tq + 2 * hw
    ones = jnp.ones((tk, HEAD_DIM), BF16)
    row = lax.broadcasted_iota(I32, (grp * tq, 1), 0)
    for sub in range(n_sub):
        var = 1
        if sub == 0:
            var = jnp.where(j == 0, 0, var)
        if sub == n_sub - 1:
            var = jnp.where(j == nt - 1, 2, var)
        r0 = sub * tq
        for h in range(n_kv):
            k_h = k[r0:r0 + tk, h * HEAD_DIM:(h + 1) * HEAD_DIM]
            v_h = jnp.concatenate([v[r0:r0 + tk, h * HEAD_DIM:(h + 1) * HEAD_DIM], ones], axis=1)
            c0 = h * grp * HEAD_DIM
            q = jnp.concatenate([q_ref[0, r0:r0 + tq, c0 + g * HEAD_DIM:c0 + (g + 1) * HEAD_DIM]
                                 for g in range(grp)], axis=0)
            s = lax.dot_general(q, k_h, (((1,), (1,)), ((), ())), preferred_element_type=F32)
            s = s + bias_ref[var, h]
            m = jnp.max(s, axis=-1, keepdims=True)
            if has_sink:
                sk = jnp.full((grp * tq, 1), sink_ref[h * grp] * LOG2E, F32)
                for g in range(1, grp):
                    sk = jnp.where(row >= g * tq, sink_ref[h * grp + g] * LOG2E, sk)
                m = jnp.maximum(m, sk)
            p = jnp.exp2((s - m).astype(BF16))
            ol = jnp.dot(p, v_h, preferred_element_type=F32)
            l = ol[:, HEAD_DIM:HEAD_DIM + 1]
            if has_sink:
                l = l + jnp.exp2(sk - m)
            o = ol[:, :HEAD_DIM] / l
            if want_lse:
                lse = jnp.broadcast_to(m + jnp.log2(l), o.shape)
            for g in range(grp):
                c = c0 + g * HEAD_DIM
                o_ref[0, r0:r0 + tq, c:c + HEAD_DIM] = o[g * tq:(g + 1) * tq].astype(o_ref.dtype)
                if want_lse:
                    lse_ref[0, r0:r0 + tq, c:c + HEAD_DIM] = lse[g * tq:(g + 1) * tq]


def _band_attn_heads_kernel(q_ref, kp, kc, kn, vp, vc, vn, bias_ref, o_ref, lse_ref, *, n_heads, tq, hw):
    j, nt = pl.program_id(2), pl.num_programs(2)
    k = jnp.concatenate([kp[0], kc[0], kn[0]], axis=0)
    v = jnp.concatenate([vp[0], vc[0], vn[0]], axis=0)
    n_sub = q_ref.shape[1] // tq
    tk = tq + 2 * hw
    width = n_heads * HEAD_DIM
    lane_head = lax.broadcasted_iota(I32, (tq, width), 1) // HEAD_DIM
    ones = jnp.ones((tk, LANES), BF16)
    for sub in range(n_sub):
        var = 1
        if sub == 0:
            var = jnp.where(j == 0, 0, var)
        if sub == n_sub - 1:
            var = jnp.where(j == nt - 1, 2, var)
        r0 = sub * tq
        q = q_ref[0, r0:r0 + tq, :]
        q_bd = jnp.concatenate([jnp.where(lane_head == h, q, jnp.zeros_like(q)) for h in range(n_heads)], axis=0)
        s = lax.dot_general(q_bd, k[r0:r0 + tk], (((1,), (1,)), ((), ())), preferred_element_type=F32)
        s = s + bias_ref[var, 0]
        m = jnp.max(s, axis=-1, keepdims=True)
        p = jnp.exp2((s - m).astype(BF16))
        o_full = jnp.dot(p, v[r0:r0 + tk], preferred_element_type=F32)
        l = jnp.dot(p, ones, preferred_element_type=F32)[:, :1]
        o_sel = jnp.zeros((tq, width), F32)
        m_sel = jnp.zeros((tq, width), F32)
        l_sel = jnp.ones((tq, width), F32)
        for h in range(n_heads):
            hit = lane_head == h
            o_sel = jnp.where(hit, o_full[h * tq:(h + 1) * tq], o_sel)
            m_sel = jnp.where(hit, m[h * tq:(h + 1) * tq], m_sel)
            l_sel = jnp.where(hit, l[h * tq:(h + 1) * tq], l_sel)
        o_ref[0, r0:r0 + tq, :] = (o_sel / l_sel).astype(o_ref.dtype)
        lse_ref[0, r0:r0 + tq, :] = m_sel + jnp.log2(l_sel)


def _band_attn(q, k, v, bias3, sinks, *, n_kv, grp, hw, tq, block, reps, want_lse):
    b, l, _ = q.shape
    hq = n_kv * grp
    qw, kw = hq * HEAD_DIM, n_kv * HEAD_DIM
    nt = l // block
    ratio = block // hw
    nhw = l // hw
    tk = tq + 2 * hw
    assert nt * (block // tq) >= 2

    prev = lambda bi, r, j: (bi, jnp.maximum(j * ratio - 1, 0), r)
    cur = lambda bi, r, j: (bi, j, r)
    nxt = lambda bi, r, j: (bi, jnp.minimum((j + 1) * ratio, nhw - 1), r)
    in_specs = [
        pl.BlockSpec((1, block, qw), cur),
        pl.BlockSpec((1, hw, kw), prev), pl.BlockSpec((1, block, kw), cur), pl.BlockSpec((1, hw, kw), nxt),
        pl.BlockSpec((1, hw, kw), prev), pl.BlockSpec((1, block, kw), cur), pl.BlockSpec((1, hw, kw), nxt),
    ]
    stacked = grp == 1
    bias_shape = (3, 1, hq * tq, tk) if stacked else (3, n_kv, grp * tq, tk)
    in_specs.append(pl.BlockSpec(bias_shape, lambda bi, r, j: (0, 0, 0, 0), pipeline_mode=pl.Buffered(1)))
    args = [q, k, k, k, v, v, v, bias3.reshape(bias_shape)]
    if stacked:
        assert sinks is None and want_lse
        body = functools.partial(_band_attn_heads_kernel, n_heads=n_kv, tq=tq, hw=hw)
    else:
        body = functools.partial(_band_attn_kernel, n_kv=n_kv, grp=grp, tq=tq, hw=hw,
                                 has_sink=sinks is not None, want_lse=want_lse)
    if sinks is not None:
        in_specs.append(pl.BlockSpec(memory_space=pltpu.SMEM))
        args.append(sinks)
    out_specs = [pl.BlockSpec((1, block, qw), cur)]
    out_shape = [jax.ShapeDtypeStruct((b, l, reps * qw), BF16)]
    if want_lse:
        out_specs.append(pl.BlockSpec((1, block, qw), cur))
        out_shape.append(jax.ShapeDtypeStruct((b, l, reps * qw), F32))
    return pl.pallas_call(
        body,
        grid=(b, reps, nt),
        in_specs=in_specs,
        out_specs=out_specs,
        out_shape=out_shape,
        compiler_params=pltpu.CompilerParams(
            dimension_semantics=("parallel", "parallel", "parallel"),
            vmem_limit_bytes=V7X_VMEM_LIMIT_BYTES),
        name="band_attn_a" if sinks is not None else "band_attn_b",
    )(*args)


def _token_order(ref, scr, dil):
    if dil == 1:
        return ref[...].astype(F32)
    n = ref.shape[0]
    width = ref.shape[1] // dil
    for r in range(dil):
        for c in range(width // LANES):
            col = r * width + c * LANES
            scr[c, pl.ds(r, n, stride=dil), :] = ref[:, col:col + LANES].astype(F32)
    return jnp.concatenate([scr[c] for c in range(width // LANES)], axis=1)


def _post_kernel(x_ref, ya_ref, o1, o2, o3, l1, l2, l3, ga_ref, gb_ref, wa_ref, wb_ref, wo_ref,
                 gf_ref, rwt_ref, rb_ref, tri_ref, x1_ref, xg_ref, meta_ref, segs_ref, *scrs, dils):
    os_ = [_token_order(r, scrs[2 * i], dil) for i, (r, dil) in enumerate(zip((o1, o2, o3), dils))]
    ls = [_token_order(r, scrs[2 * i + 1], dil) for i, (r, dil) in enumerate(zip((l1, l2, l3), dils))]
    mx = jnp.maximum(jnp.maximum(ls[0], ls[1]), ls[2])
    es = [jnp.exp2(l - mx) for l in ls]
    den = es[0] + es[1] + es[2]
    yb = (es[0] * os_[0] + es[1] * os_[1] + es[2] * os_[2]) / den
    za = jnp.dot(ya_ref[...], wa_ref[...], preferred_element_type=F32)
    zb = jnp.dot(yb.astype(BF16), wb_ref[...], preferred_element_type=F32)
    merged = ga_ref[...].astype(F32) * za + gb_ref[...].astype(F32) * zb
    x1 = x_ref[...] + jnp.dot(merged.astype(BF16), wo_ref[...], preferred_element_type=F32)
    x1_ref[...] = x1
    h2 = _rms(x1, gf_ref[...])

    h_hi = h2.astype(BF16)
    h_lo = (h2 - h_hi.astype(F32)).astype(BF16)
    nt_dot = lambda a, b_: lax.dot_general(a, b_, (((1,), (1,)), ((), ())), preferred_element_type=F32)
    logits = (nt_dot(rwt_ref[0], h_hi) + nt_dot(rwt_ref[0], h_lo) + nt_dot(rwt_ref[1], h_hi)) + rb_ref[...]
    n_e, tm = logits.shape
    iota_e = lax.broadcasted_iota(I32, (n_e, tm), 0)
    work = logits
    vals, hots = [], []
    for _ in range(TOP_K):
        mk = jnp.max(work, axis=0, keepdims=True)
        ik = jnp.min(jnp.where(work == mk, iota_e, n_e), axis=0, keepdims=True)
        hot = iota_e == ik
        vals.append(mk)
        hots.append(hot)
        work = jnp.where(hot, -jnp.inf, work)
    exps = [jnp.exp(vk - vals[0]) for vk in vals]
    tot = exps[0] + exps[1] + exps[2] + exps[3]
    probs = [ek / tot for ek in exps]

    sel = jnp.zeros((n_e, tm), F32)
    for hot in hots:
        sel = sel + jnp.where(hot, 1.0, 0.0)
    cnt = jnp.sum(sel, axis=1, keepdims=True)
    pcnt = jnp.floor((cnt + (CHUNK - 1)) / CHUNK) * CHUNK
    r_i = lax.broadcasted_iota(I32, (n_e, n_e), 0)
    c_i = lax.broadcasted_iota(I32, (n_e, n_e), 1)
    pcnt_row = jnp.sum(jnp.where(r_i == c_i, pcnt, 0.0), axis=0, keepdims=True)
    seg_off = jnp.sum(jnp.where(c_i < r_i, pcnt_row, 0.0), axis=1, keepdims=True)
    before = jnp.dot(sel.astype(BF16), tri_ref[...], preferred_element_type=F32)
    slot_of = seg_off + before
    slots = [jnp.sum(jnp.where(hot, slot_of, 0.0), axis=0, keepdims=True) for hot in hots]

    iota_s = lax.broadcasted_iota(I32, (PERM_BLOCK, tm), 0)
    slots_i = [sk.astype(I32) for sk in slots]
    for blk in range(SLOTS // PERM_BLOCK):
        perm = jnp.zeros((PERM_BLOCK, tm), F32)
        for sk in slots_i:
            perm = perm + jnp.where(iota_s == sk - blk * PERM_BLOCK, 1.0, 0.0)
        xg_ref[blk * PERM_BLOCK:(blk + 1) * PERM_BLOCK, :] = jnp.dot(
            perm.astype(BF16), h_hi, preferred_element_type=F32).astype(BF16)

    rows = slots + probs
    meta_t = jnp.concatenate(rows + [jnp.zeros((LANES - len(rows), tm), F32)], axis=0)
    meta_ref[...] = meta_t.T
    lane = lax.broadcasted_iota(I32, (n_e, LANES), 1)
    segs_ref[...] = jnp.where(lane == 0, pcnt, jnp.where(lane == 1, seg_off, 0.0))


def _post(x2d, ya, os_, ls_, dils, ga, gb, wa, wb, wo, gf, rwt, rb):
    t, d = x2d.shape
    bw = os_[0].shape[1] // dils[0]
    tm = TM_POST
    nt = t // tm
    tri = jnp.asarray(np.triu(np.ones((tm, tm), np.float32), k=1), dtype=BF16)
    row = lambda w, dil=1: pl.BlockSpec((tm // dil, dil * w), lambda i: (i, 0))
    full = lambda a: pl.BlockSpec(a.shape, lambda i: (0,) * a.ndim)
    return pl.pallas_call(
        functools.partial(_post_kernel, dils=dils),
        grid=(nt,),
        in_specs=[row(d), row(d)] + [row(bw, dil) for dil in dils] + [row(bw, dil) for dil in dils]
                 + [row(d), row(d), full(wa), full(wb), full(wo), full(gf), full(rwt), full(rb), full(tri)],
        scratch_shapes=[pltpu.VMEM((bw // LANES, tm, LANES), F32) for _ in range(2 * len(dils))],
        out_specs=[row(d), pl.BlockSpec((SLOTS, d), lambda i: (i, 0)), row(LANES),
                   pl.BlockSpec((N_EXPERTS, LANES), lambda i: (i, 0))],
        out_shape=[jax.ShapeDtypeStruct((t, d), F32), jax.ShapeDtypeStruct((nt * SLOTS, d), BF16),
                   jax.ShapeDtypeStruct((t, LANES), F32), jax.ShapeDtypeStruct((nt * N_EXPERTS, LANES), F32)],
        compiler_params=pltpu.CompilerParams(
            dimension_semantics=("parallel",), vmem_limit_bytes=V7X_VMEM_LIMIT_BYTES),
        name="post_attn",
    )(x2d, ya, *os_, *ls_, ga, gb, wa, wb, wo, gf, rwt, rb, tri)


def _expert_tiles(pcnt, seg_off, n_tiles):
    nt, n_e = pcnt.shape
    cpt = CHUNKS_PER_TILE
    nch = (pcnt // CHUNK).T
    cum = jnp.cumsum(nch, axis=1)
    total = cum[:, -1]
    tiles_e = (total + cpt - 1) // cpt
    tile_end = jnp.cumsum(tiles_e)
    n_active = tile_end[-1]
    i = jnp.arange(n_tiles, dtype=I32)
    last = jnp.minimum(i, n_active - 1)
    te = jnp.sum((last[:, None] >= tile_end[None, :]).astype(I32), axis=1)
    hot_e = (te[:, None] == jnp.arange(n_e, dtype=I32)[None, :])
    pick = lambda tab: jnp.sum(jnp.where(hot_e[:, :, None], tab[None], 0), axis=1)
    tile_start = jnp.sum(jnp.where(hot_e, (tile_end - tiles_e)[None, :], 0), axis=1)
    total_t = jnp.sum(jnp.where(hot_e, total[None, :], 0), axis=1)
    q = (last - tile_start)[:, None] * cpt + jnp.arange(cpt, dtype=I32)[None, :]
    valid = (q < total_t[:, None]) & (i < n_active)[:, None]
    cum_t, nch_t = pick(cum), pick(nch)
    chunk0_t = pick((seg_off.T + jnp.arange(nt, dtype=I32)[None, :] * SLOTS) // CHUNK)
    jj = jnp.sum((q[:, :, None] >= cum_t[:, None, :]).astype(I32), axis=2)
    hot_j = jj[:, :, None] == jnp.arange(nt, dtype=I32)[None, None, :]
    first = jnp.sum(jnp.where(hot_j, (cum_t - nch_t)[:, None, :], 0), axis=2)
    base = jnp.sum(jnp.where(hot_j, chunk0_t[:, None, :], 0), axis=2)
    src = jnp.where(valid, base + q - first, 0)
    trash = nt * SLOTS // CHUNK + (i % 2)[:, None] * cpt + jnp.arange(cpt, dtype=I32)[None, :]
    dst = jnp.where(valid, src, trash)
    group_end = jnp.sum(jnp.where(hot_e, tile_end[None, :], 0), axis=1)
    nxt = jnp.sum((group_end[:, None] >= tile_end[None, :]).astype(I32), axis=1)
    nxt = jnp.where(group_end < n_active, nxt, -1)
    return (te.astype(I32), nxt.astype(I32), n_active.astype(I32)[None],
            src.reshape(-1).astype(I32), dst.reshape(-1).astype(I32))


def _expert_kernel(te_ref, nx_ref, na_ref, cs_ref, cd_ref, xg_hbm, wg_hbm, bg_ref, wu_hbm, bu_ref, wd_hbm,
                   bd_ref, yg_hbm, xbuf, ybuf, wst, wbf, zbuf, sem_in, sem_out, sem_zero, sem_w, *,
                   n_token_tiles):
    i = pl.program_id(0)
    n_active = na_ref[0]
    slot = i % 2
    cpt = CHUNKS_PER_TILE
    w_hbm = (wg_hbm, wu_hbm, wd_hbm)

    def w_copy(n, e):
        return pltpu.make_async_copy(w_hbm[n].at[e], wst.at[n], sem_w.at[n])

    zero_starts = [j * SLOTS + TM_POST * TOP_K for j in range(n_token_tiles)]
    zero_starts += [n_token_tiles * SLOTS + j * SLACK_ROWS for j in range(2 * TM_EXPERT // SLACK_ROWS)]

    def zero_copy(n):
        return pltpu.make_async_copy(zbuf, yg_hbm.at[pl.ds(zero_starts[n], SLACK_ROWS), :], sem_zero.at[n])

    @pl.when(i == 0)
    def _():
        zbuf[...] = jnp.zeros_like(zbuf)
        for n in range(len(zero_starts)):
            zero_copy(n).start()

    def in_copy(c, sl, chunk):
        row = pl.multiple_of(chunk * CHUNK, CHUNK)
        return pltpu.make_async_copy(xg_hbm.at[pl.ds(row, CHUNK), :],
                                     xbuf.at[sl, pl.ds(c * CHUNK, CHUNK), :], sem_in.at[sl, c])

    def out_copy(c, sl, chunk):
        row = pl.multiple_of(chunk * CHUNK, CHUNK)
        return pltpu.make_async_copy(ybuf.at[sl, pl.ds(c * CHUNK, CHUNK), :],
                                     yg_hbm.at[pl.ds(row, CHUNK), :], sem_out.at[sl, c])

    def start_gather(tile, sl):
        for c in range(cpt):
            in_copy(c, sl, cs_ref[tile * cpt + c]).start()

    def wait_scatter(sl):
        for c in range(cpt):
            out_copy(c, sl, 0).wait()

    @pl.when(i == 0)
    def _():
        start_gather(0, 0)
        for n in range(len(w_hbm)):
            w_copy(n, te_ref[0]).start()

    @pl.when(i + 1 < n_active)
    def _():
        start_gather(i + 1, 1 - slot)

    @pl.when(i < n_active)
    def _():
        for c in range(cpt):
            in_copy(c, slot, 0).wait()

        @pl.when((i == 0) | (te_ref[i] != te_ref[jnp.maximum(i - 1, 0)]))
        def _():
            for n in range(len(w_hbm)):
                w_copy(n, 0).wait()
                wbf[n] = wst[n].astype(BF16)

                @pl.when(nx_ref[i] >= 0)
                def _():
                    w_copy(n, nx_ref[i]).start()

        @pl.when(i >= 1)
        def _():
            for c in range(cpt):
                out_copy(c, 1 - slot, cd_ref[(i - 1) * cpt + c]).start()

        x = xbuf[slot]
        gate = jnp.minimum(jnp.dot(x, wbf[0], preferred_element_type=F32) + bg_ref[0], SWIGLU_LIMIT)
        up = jnp.clip(jnp.dot(x, wbf[1], preferred_element_type=F32) + bu_ref[0], -SWIGLU_LIMIT, SWIGLU_LIMIT)
        act = gate * jax.nn.sigmoid(SWIGLU_ALPHA * gate) * (up + 1.0)
        y = jnp.dot(act.astype(BF16), wbf[2], preferred_element_type=F32) + bd_ref[0]

        @pl.when(i >= 2)
        def _():
            wait_scatter(slot)

        ybuf[slot] = y.astype(BF16)

        @pl.when(i == 0)
        def _():
            for n in range(len(zero_starts)):
                zero_copy(n).wait()

        @pl.when(i == n_active - 1)
        def _():
            for c in range(cpt):
                out_copy(c, slot, cd_ref[i * cpt + c]).start()
            wait_scatter(slot)

            @pl.when(i >= 1)
            def _():
                wait_scatter(1 - slot)


def _experts(xg, te, nxt, n_active, src, dst, wg, bg, wu, bu, wd, bd):
    rows, d = xg.shape
    n_e, _, dff = wg.shape
    assert d == dff
    n_tiles = te.shape[0]
    n_token_tiles = rows // SLOTS
    n_zero = n_token_tiles + 2 * TM_EXPERT // SLACK_ROWS
    b_spec = pl.BlockSpec((1, 1, d), lambda i, te_, *_: (te_[i], 0, 0))
    hbm = pl.BlockSpec(memory_space=pl.ANY)
    return pl.pallas_call(
        functools.partial(_expert_kernel, n_token_tiles=n_token_tiles),
        grid_spec=pltpu.PrefetchScalarGridSpec(
            num_scalar_prefetch=5,
            grid=(n_tiles,),
            in_specs=[hbm, hbm, b_spec, hbm, b_spec, hbm, b_spec],
            out_specs=hbm,
            scratch_shapes=[pltpu.VMEM((2, TM_EXPERT, d), BF16), pltpu.VMEM((2, TM_EXPERT, d), BF16),
                            pltpu.VMEM((3, d, dff), F32), pltpu.VMEM((3, d, dff), BF16),
                            pltpu.VMEM((SLACK_ROWS, d), BF16),
                            pltpu.SemaphoreType.DMA((2, CHUNKS_PER_TILE)),
                            pltpu.SemaphoreType.DMA((2, CHUNKS_PER_TILE)),
                            pltpu.SemaphoreType.DMA((n_zero,)),
                            pltpu.SemaphoreType.DMA((3,))]),
        out_shape=jax.ShapeDtypeStruct((rows + 2 * TM_EXPERT, d), BF16),
        compiler_params=pltpu.CompilerParams(
            dimension_semantics=("arbitrary",), vmem_limit_bytes=V7X_VMEM_LIMIT_BYTES),
        name="experts",
    )(te, nxt, n_active, src, dst, xg, wg, bg, wu, bu, wd, bd)


def _combine_kernel(x1_ref, yg_ref, meta_ref, gn_ref, o_ref, *, final_norm):
    meta = meta_ref[...]
    tm = meta.shape[0]
    slots_i = [meta[:, k:k + 1].astype(I32) for k in range(TOP_K)]
    probs = [meta[:, TOP_K + k:TOP_K + k + 1] for k in range(TOP_K)]
    iota_s = lax.broadcasted_iota(I32, (tm, PERM_BLOCK), 1)
    xo = x1_ref[...]
    for blk in range(SLOTS // PERM_BLOCK):
        base = blk * PERM_BLOCK
        wperm = jnp.zeros((tm, PERM_BLOCK), F32)
        for sk, pk in zip(slots_i, probs):
            wperm = wperm + jnp.where(iota_s == sk - base, pk, 0.0)
        xo = xo + jnp.dot(wperm.astype(BF16), yg_ref[base:base + PERM_BLOCK, :], preferred_element_type=F32)
    o_ref[...] = _rms(xo, gn_ref[...]) if final_norm else xo


def _combine(x1, yg, meta, gn, final_norm):
    t, d = x1.shape
    tm = TM_POST
    return pl.pallas_call(
        functools.partial(_combine_kernel, final_norm=final_norm),
        grid=(t // tm,),
        in_specs=[pl.BlockSpec((tm, d), lambda i: (i, 0)),
                  pl.BlockSpec((SLOTS, d), lambda i: (i, 0)),
                  pl.BlockSpec((tm, LANES), lambda i: (i, 0)),
                  pl.BlockSpec((1, d), lambda i: (0, 0))],
        out_specs=pl.BlockSpec((tm, d), lambda i: (i, 0)),
        out_shape=jax.ShapeDtypeStruct((t, d), F32),
        compiler_params=pltpu.CompilerParams(
            dimension_semantics=("parallel",), vmem_limit_bytes=V7X_VMEM_LIMIT_BYTES),
        name="combine",
    )(x1, yg, meta, gn)


def kernel(x, norm_mix, w_in, b_in, sinks, rel_bias, w_branch_a, w_branch_b, w_out, norm_ffn,
           router_w, router_b, w_gate, b_gate, w_up, b_up, w_down, b_down, norm_final):
    b, s, d = x.shape
    t = b * s
    depth = w_in.shape[0]
    a_q_w = A_KV_HEADS * A_GROUP * HEAD_DIM
    a_kv_w = A_KV_HEADS * HEAD_DIM
    b_w = B_HEADS * HEAD_DIM
    n_grp = len(B_GROUPS)
    dils = tuple(dil for _, dil in B_GROUPS)
    segs, col = [], 0
    for kind, width in (("q", a_q_w), ("k", a_kv_w), ("v", a_kv_w)):
        segs.append((col, width, kind, 1))
        col += width
    for kind in ("q", "k", "v"):
        for dil in dils:
            segs.append((col, b_w, kind, dil))
            col += b_w
    for _ in range(2):
        segs.append((col, d, "gate", 1))
        col += d
    segs = tuple(segs)
    n_a = A_KV_HEADS * A_GROUP
    nt = t // TM_POST
    max_chunks = nt * ((TM_POST * TOP_K + N_EXPERTS * (CHUNK - 1)) // CHUNK)
    n_tiles = -(-(max_chunks + N_EXPERTS * (CHUNKS_PER_TILE - 1)) // CHUNKS_PER_TILE)

    bias_a = _band_bias(rel_bias[:, :n_a], A_HALF_WINDOW, TQ_ATTN, 1)
    bias_b = [_band_bias(rel_bias[:, n_a + gi * B_HEADS:n_a + (gi + 1) * B_HEADS],
                         win // (2 * dil), TQ_ATTN, dil) for gi, (win, dil) in enumerate(B_GROUPS)]

    x2d = x.reshape(t, d)
    for layer in range(depth):
        proj = _inproj(x2d, norm_mix[layer][None], w_in[layer].astype(BF16), b_in[layer][None], segs)
        qa, ka, va = proj[:3]
        qb, kb, vb = proj[3:3 + n_grp], proj[3 + n_grp:3 + 2 * n_grp], proj[3 + 2 * n_grp:3 + 3 * n_grp]
        ga, gb = proj[-2:]

        (ya,) = _band_attn(qa.reshape(b, s, a_q_w), ka.reshape(b, s, a_kv_w), va.reshape(b, s, a_kv_w),
                           bias_a, sinks[layer], n_kv=A_KV_HEADS, grp=A_GROUP, hw=A_HALF_WINDOW,
                           tq=TQ_ATTN, block=BLOCK_ATTN_A, reps=1, want_lse=False)
        os_, ls_ = [], []
        for gi, (win, dil) in enumerate(B_GROUPS):
            sub = lambda a: a.reshape(b, s // dil, dil * b_w)
            o, lse = _band_attn(sub(qb[gi]), sub(kb[gi]), sub(vb[gi]), bias_b[gi], None, n_kv=B_HEADS, grp=1,
                                hw=win // (2 * dil), tq=TQ_ATTN, block=BLOCK_ATTN_B, reps=dil,
                                want_lse=True)
            os_.append(o.reshape(t // dil, dil * b_w))
            ls_.append(lse.reshape(t // dil, dil * b_w))

        rwt = router_w[layer].T
        rwt_hi = rwt.astype(BF16)
        rwt_split = jnp.stack([rwt_hi, (rwt - rwt_hi.astype(F32)).astype(BF16)])
        x1, xg, meta, segs_out = _post(
            x2d, ya.reshape(t, a_q_w), os_, ls_, dils, ga, gb,
            w_branch_a[layer].astype(BF16), w_branch_b[layer].astype(BF16), w_out[layer].astype(BF16),
            norm_ffn[layer][None], rwt_split, router_b[layer][:, None])

        segs3 = segs_out.reshape(nt, N_EXPERTS, LANES)
        te, nxt, n_active, src, dst = _expert_tiles(segs3[:, :, 0].astype(I32), segs3[:, :, 1].astype(I32), n_tiles)
        yg = _experts(xg, te, nxt, n_active, src, dst, w_gate[layer], b_gate[layer][:, None], w_up[layer],
                      b_up[layer][:, None], w_down[layer], b_down[layer][:, None])
        x2d = _combine(x1, yg, meta, norm_final[None], layer == depth - 1)
    return x2d.reshape(b, s, d)
```

```python
import functools

import numpy as np
import jax
import jax.numpy as jnp
from jax import lax
from jax.experimental import pallas as pl
from jax.experimental.pallas import tpu as pltpu

F32 = jnp.float32
BF16 = jnp.bfloat16
I32 = jnp.int32

HEAD_DIM = 64
A_KV_HEADS = 4
A_GROUP = 4
A_HALF_WINDOW = 128
B_GROUPS = ((128, 1), (512, 4), (2048, 16))
B_HEADS = 4
N_BUCKETS = 32
MAX_DISTANCE = 1024
N_EXPERTS = 32
TOP_K = 4
SWIGLU_LIMIT = 7.0
SWIGLU_ALPHA = 1.702
EPS = 1e-5
NEG_INF = -1e30
LOG2E = 1.4426950408889634

V7X_VMEM_LIMIT_BYTES = 56 * 1024 * 1024
LANES = 128
BF16_SUBLANES = 16

TM_PROJ = 512
TQ_ATTN = 128
BLOCK_ATTN_A = 256
BLOCK_ATTN_B = 512
TM_POST = 512
TM_EXPERT = 512

CHUNK = BF16_SUBLANES
CHUNKS_PER_TILE = TM_EXPERT // CHUNK
SLOTS = -(-(TM_POST * TOP_K + N_EXPERTS * (CHUNK - 1)) // LANES) * LANES
SLACK_ROWS = SLOTS - TM_POST * TOP_K
assert (2 * TM_EXPERT) % SLACK_ROWS == 0
PERM_BLOCK = 512
assert SLOTS % PERM_BLOCK == 0


def _t5_buckets(rel):
    half = N_BUCKETS // 2
    max_exact = half // 2
    ret = np.where(rel > 0, half, 0)
    n = np.abs(rel)
    large = max_exact + (np.log(np.maximum(n, 1) / max_exact)
                         / np.log(MAX_DISTANCE / max_exact) * (half - max_exact)).astype(np.int32)
    large = np.minimum(large, half - 1)
    return (ret + np.where(n < max_exact, n, large)).astype(np.int32)


def _rms(x, g):
    return x * lax.rsqrt(jnp.mean(x * x, axis=-1, keepdims=True) + EPS) * g


def _inproj_kernel(x_ref, g_ref, w_ref, b_ref, *refs, segs):
    out_refs, scr = refs[:-1], refs[-1]
    h = _rms(x_ref[...], g_ref[...]).astype(BF16)
    for ref, (c0, width, kind, dil) in zip(out_refs, segs):
        acc = jnp.dot(h, w_ref[:, c0:c0 + width], preferred_element_type=F32) + b_ref[:, c0:c0 + width]
        if kind == "q":
            acc = acc * (HEAD_DIM ** -0.5 * LOG2E)
        elif kind == "gate":
            acc = jax.nn.sigmoid(acc)
        if dil == 1:
            ref[...] = acc.astype(ref.dtype)
        else:
            n = acc.shape[0] // dil
            for c in range(width // LANES):
                scr[c] = acc[:, c * LANES:(c + 1) * LANES]
            for r in range(dil):
                for c in range(width // LANES):
                    col = r * width + c * LANES
                    ref[:, col:col + LANES] = scr[c, pl.ds(r, n, stride=dil), :].astype(ref.dtype)


def _inproj(x2d, g, w_bf16, b, segs):
    t, d = x2d.shape
    n = w_bf16.shape[1]
    max_w = max(w for _, w, _, dil in segs if dil > 1)
    return pl.pallas_call(
        functools.partial(_inproj_kernel, segs=segs),
        grid=(t // TM_PROJ,),
        in_specs=[
            pl.BlockSpec((TM_PROJ, d), lambda i: (i, 0)),
            pl.BlockSpec((1, d), lambda i: (0, 0)),
            pl.BlockSpec((d, n), lambda i: (0, 0), pipeline_mode=pl.Buffered(1)),
            pl.BlockSpec((1, n), lambda i: (0, 0)),
        ],
        out_specs=[pl.BlockSpec((TM_PROJ // dil, dil * w), lambda i: (i, 0)) for _, w, _, dil in segs],
        out_shape=[jax.ShapeDtypeStruct((t // dil, dil * w), BF16) for _, w, _, dil in segs],
        scratch_shapes=[pltpu.VMEM((max_w // LANES, TM_PROJ, LANES), F32)],
        compiler_params=pltpu.CompilerParams(
            dimension_semantics=("parallel",), vmem_limit_bytes=V7X_VMEM_LIMIT_BYTES),
        name="inproj",
    )(x2d, g, w_bf16, b)


def _band_bias(table, hw, tq, dil):
    tk = tq + 2 * hw
    off = np.arange(tk)[None, :] - hw - np.arange(tq)[:, None]
    band = np.abs(off) <= hw
    col = np.arange(tk)[None, :]
    masks = np.stack([band & (col >= hw), band, band & (col < hw + tq)])
    onehot = (_t5_buckets(off * dil)[..., None] == np.arange(N_BUCKETS)).astype(np.float32)
    bias = jnp.einsum("qkn,nh->hqk", jnp.asarray(onehot), table.astype(F32),
                      precision=lax.Precision.HIGHEST)
    return jnp.where(masks[:, None], bias[None] * LOG2E, NEG_INF)


def _band_attn_kernel(*refs, n_kv, grp, tq, hw, has_sink, want_lse):
    q_ref, kp, kc, kn, vp, vc, vn, bias_ref = refs[:8]
    rest = refs[8:]
    sink_ref = None
    if has_sink:
        sink_ref, rest = rest[0], rest[1:]
    o_ref = rest[0]
    lse_ref = rest[1] if want_lse else None

    j, nt = pl.program_id(2), pl.num_programs(2)
    k = jnp.concatenate([kp[0], kc[0], kn[0]], axis=0)
    v = jnp.concatenate([vp[0], vc[0], vn[0]], axis=0)
    n_sub = q_ref.shape[1] // tq
    tk = tq + 2 * hw
    ones = jnp.ones((tk, HEAD_DIM), BF16)
    row = lax.broadcasted_iota(I32, (grp * tq, 1), 0)
    for sub in range(n_sub):
        var = 1
        if sub == 0:
            var = jnp.where(j == 0, 0, var)
        if sub == n_sub - 1:
            var = jnp.where(j == nt - 1, 2, var)
        r0 = sub * tq
        for h in range(n_kv):
            k_h = k[r0:r0 + tk, h * HEAD_DIM:(h + 1) * HEAD_DIM]
            v_h = jnp.concatenate([v[r0:r0 + tk, h * HEAD_DIM:(h + 1) * HEAD_DIM], ones], axis=1)
            c0 = h * grp * HEAD_DIM
            q = jnp.concatenate([q_ref[0, r0:r0 + tq, c0 + g * HEAD_DIM:c0 + (g + 1) * HEAD_DIM]
                                 for g in range(grp)], axis=0)
            s = lax.dot_general(q, k_h, (((1,), (1,)), ((), ())), preferred_element_type=F32)
            s = s + bias_ref[var, h]
            m = jnp.max(s, axis=-1, keepdims=True)
            if has_sink:
                sk = jnp.full((grp * tq, 1), sink_ref[h * grp] * LOG2E, F32)
                for g in range(1, grp):
                    sk = jnp.where(row >= g * tq, sink_ref[h * grp + g] * LOG2E, sk)
                m = jnp.maximum(m, sk)
            p = jnp.exp2((s - m).astype(BF16))
            ol = jnp.dot(p, v_h, preferred_element_type=F32)
            l = ol[:, HEAD_DIM:HEAD_DIM + 1]
            if has_sink:
                l = l + jnp.exp2(sk - m)
            o = ol[:, :HEAD_DIM] / l
            if want_lse:
                lse = jnp.broadcast_to(m + jnp.log2(l), o.shape)
            for g in range(grp):
                c = c0 + g * HEAD_DIM
                o_ref[0, r0:r0 + tq, c:c + HEAD_DIM] = o[g * tq:(g + 1) * tq].astype(o_ref.dtype)
                if want_lse:
                    lse_ref[0, r0:r0 + tq, c:c + HEAD_DIM] = lse[g * tq:(g + 1) * tq]


def _band_attn_heads_kernel(q_ref, kp, kc, kn, vp, vc, vn, bias_ref, o_ref, lse_ref, *, n_heads, tq, hw):
    j, nt = pl.program_id(2), pl.num_programs(2)
    k = jnp.concatenate([kp[0], kc[0], kn[0]], axis=0)
    v = jnp.concatenate([vp[0], vc[0], vn[0]], axis=0)
    n_sub = q_ref.shape[1] // tq
    tk = tq + 2 * hw
    width = n_heads * HEAD_DIM
    lane_head = lax.broadcasted_iota(I32, (tq, width), 1) // HEAD_DIM
    ones = jnp.ones((tk, LANES), BF16)
    for sub in range(n_sub):
        var = 1
        if sub == 0:
            var = jnp.where(j == 0, 0, var)
        if sub == n_sub - 1:
            var = jnp.where(j == nt - 1, 2, var)
        r0 = sub * tq
        q = q_ref[0, r0:r0 + tq, :]
        q_bd = jnp.concatenate([jnp.where(lane_head == h, q, jnp.zeros_like(q)) for h in range(n_heads)], axis=0)
        s = lax.dot_general(q_bd, k[r0:r0 + tk], (((1,), (1,)), ((), ())), preferred_element_type=F32)
        s = s + bias_ref[var, 0]
        m = jnp.max(s, axis=-1, keepdims=True)
        p = jnp.exp2((s - m).astype(BF16))
        o_full = jnp.dot(p, v[r0:r0 + tk], preferred_element_type=F32)
        l = jnp.dot(p, ones, preferred_element_type=F32)[:, :1]
        o_sel = jnp.zeros((tq, width), F32)
        m_sel = jnp.zeros((tq, width), F32)
        l_sel = jnp.ones((tq, width), F32)
        for h in range(n_heads):
            hit = lane_head == h
            o_sel = jnp.where(hit, o_full[h * tq:(h + 1) * tq], o_sel)
            m_sel = jnp.where(hit, m[h * tq:(h + 1) * tq], m_sel)
            l_sel = jnp.where(hit, l[h * tq:(h + 1) * tq], l_sel)
        o_ref[0, r0:r0 + tq, :] = (o_sel / l_sel).astype(o_ref.dtype)
        lse_ref[0, r0:r0 + tq, :] = m_sel + jnp.log2(l_sel)


def _band_attn(q, k, v, bias3, sinks, *, n_kv, grp, hw, tq, block, reps, want_lse):
    b, l, _ = q.shape
    hq = n_kv * grp
    qw, kw = hq * HEAD_DIM, n_kv * HEAD_DIM
    nt = l // block
    ratio = block // hw
    nhw = l // hw
    tk = tq + 2 * hw
    assert nt * (block // tq) >= 2

    prev = lambda bi, r, j: (bi, jnp.maximum(j * ratio - 1, 0), r)
    cur = lambda bi, r, j: (bi, j, r)
    nxt = lambda bi, r, j: (bi, jnp.minimum((j + 1) * ratio, nhw - 1), r)
    in_specs = [
        pl.BlockSpec((1, block, qw), cur),
        pl.BlockSpec((1, hw, kw), prev), pl.BlockSpec((1, block, kw), cur), pl.BlockSpec((1, hw, kw), nxt),
        pl.BlockSpec((1, hw, kw), prev), pl.BlockSpec((1, block, kw), cur), pl.BlockSpec((1, hw, kw), nxt),
    ]
    stacked = grp == 1
    bias_shape = (3, 1, hq * tq, tk) if stacked else (3, n_kv, grp * tq, tk)
    in_specs.append(pl.BlockSpec(bias_shape, lambda bi, r, j: (0, 0, 0, 0), pipeline_mode=pl.Buffered(1)))
    args = [q, k, k, k, v, v, v, bias3.reshape(bias_shape)]
    if stacked:
        assert sinks is None and want_lse
        body = functools.partial(_band_attn_heads_kernel, n_heads=n_kv, tq=tq, hw=hw)
    else:
        body = functools.partial(_band_attn_kernel, n_kv=n_kv, grp=grp, tq=tq, hw=hw,
                                 has_sink=sinks is not None, want_lse=want_lse)
    if sinks is not None:
        in_specs.append(pl.BlockSpec(memory_space=pltpu.SMEM))
        args.append(sinks)
    out_specs = [pl.BlockSpec((1, block, qw), cur)]
    out_shape = [jax.ShapeDtypeStruct((b, l, reps * qw), BF16)]
    if want_lse:
        out_specs.append(pl.BlockSpec((1, block, qw), cur))
        out_shape.append(jax.ShapeDtypeStruct((b, l, reps * qw), F32))
    return pl.pallas_call(
        body,
        grid=(b, reps, nt),
        in_specs=in_specs,
        out_specs=out_specs,
        out_shape=out_shape,
        compiler_params=pltpu.CompilerParams(
            dimension_semantics=("parallel", "parallel", "parallel"),
            vmem_limit_bytes=V7X_VMEM_LIMIT_BYTES),
        name="band_attn_a" if sinks is not None else "band_attn_b",
    )(*args)


def _token_order(ref, scr, dil):
    if dil == 1:
        return ref[...].astype(F32)
    n = ref.shape[0]
    width = ref.shape[1] // dil
    for r in range(dil):
        for c in range(width // LANES):
            col = r * width + c * LANES
            scr[c, pl.ds(r, n, stride=dil), :] = ref[:, col:col + LANES].astype(F32)
    return jnp.concatenate([scr[c] for c in range(width // LANES)], axis=1)


def _post_kernel(x_ref, ya_ref, o1, o2, o3, l1, l2, l3, ga_ref, gb_ref, wa_ref, wb_ref, wo_ref,
                 gf_ref, rwt_ref, rb_ref, tri_ref, x1_ref, xg_ref, meta_ref, segs_ref, *scrs, dils):
    os_ = [_token_order(r, scrs[2 * i], dil) for i, (r, dil) in enumerate(zip((o1, o2, o3), dils))]
    ls = [_token_order(r, scrs[2 * i + 1], dil) for i, (r, dil) in enumerate(zip((l1, l2, l3), dils))]
    mx = jnp.maximum(jnp.maximum(ls[0], ls[1]), ls[2])
    es = [jnp.exp2(l - mx) for l in ls]
    den = es[0] + es[1] + es[2]
    yb = (es[0] * os_[0] + es[1] * os_[1] + es[2] * os_[2]) / den
    za = jnp.dot(ya_ref[...], wa_ref[...], preferred_element_type=F32)
    zb = jnp.dot(yb.astype(BF16), wb_ref[...], preferred_element_type=F32)
    merged = ga_ref[...].astype(F32) * za + gb_ref[...].astype(F32) * zb
    x1 = x_ref[...] + jnp.dot(merged.astype(BF16), wo_ref[...], preferred_element_type=F32)
    x1_ref[...] = x1
    h2 = _rms(x1, gf_ref[...])

    h_hi = h2.astype(BF16)
    h_lo = (h2 - h_hi.astype(F32)).astype(BF16)
    nt_dot = lambda a, b_: lax.dot_general(a, b_, (((1,), (1,)), ((), ())), preferred_element_type=F32)
    logits = (nt_dot(rwt_ref[0], h_hi) + nt_dot(rwt_ref[0], h_lo) + nt_dot(rwt_ref[1], h_hi)) + rb_ref[...]
    n_e, tm = logits.shape
    iota_e = lax.broadcasted_iota(I32, (n_e, tm), 0)
    work = logits
    vals, hots = [], []
    for _ in range(TOP_K):
        mk = jnp.max(work, axis=0, keepdims=True)
        ik = jnp.min(jnp.where(work == mk, iota_e, n_e), axis=0, keepdims=True)
        hot = iota_e == ik
        vals.append(mk)
        hots.append(hot)
        work = jnp.where(hot, -jnp.inf, work)
    exps = [jnp.exp(vk - vals[0]) for vk in vals]
    tot = exps[0] + exps[1] + exps[2] + exps[3]
    probs = [ek / tot for ek in exps]

    sel = jnp.zeros((n_e, tm), F32)
    for hot in hots:
        sel = sel + jnp.where(hot, 1.0, 0.0)
    cnt = jnp.sum(sel, axis=1, keepdims=True)
    pcnt = jnp.floor((cnt + (CHUNK - 1)) / CHUNK) * CHUNK
    r_i = lax.broadcasted_iota(I32, (n_e, n_e), 0)
    c_i = lax.broadcasted_iota(I32, (n_e, n_e), 1)
    pcnt_row = jnp.sum(jnp.where(r_i == c_i, pcnt, 0.0), axis=0, keepdims=True)
    seg_off = jnp.sum(jnp.where(c_i < r_i, pcnt_row, 0.0), axis=1, keepdims=True)
    before = jnp.dot(sel.astype(BF16), tri_ref[...], preferred_element_type=F32)
    slot_of = seg_off + before
    slots = [jnp.sum(jnp.where(hot, slot_of, 0.0), axis=0, keepdims=True) for hot in hots]

    iota_s = lax.broadcasted_iota(I32, (PERM_BLOCK, tm), 0)
    slots_i = [sk.astype(I32) for sk in slots]
    for blk in range(SLOTS // PERM_BLOCK):
        perm = jnp.zeros((PERM_BLOCK, tm), F32)
        for sk in slots_i:
            perm = perm + jnp.where(iota_s == sk - blk * PERM_BLOCK, 1.0, 0.0)
        xg_ref[blk * PERM_BLOCK:(blk + 1) * PERM_BLOCK, :] = jnp.dot(
            perm.astype(BF16), h_hi, preferred_element_type=F32).astype(BF16)

    rows = slots + probs
    meta_t = jnp.concatenate(rows + [jnp.zeros((LANES - len(rows), tm), F32)], axis=0)
    meta_ref[...] = meta_t.T
    lane = lax.broadcasted_iota(I32, (n_e, LANES), 1)
    segs_ref[...] = jnp.where(lane == 0, pcnt, jnp.where(lane == 1, seg_off, 0.0))


def _post(x2d, ya, os_, ls_, dils, ga, gb, wa, wb, wo, gf, rwt, rb):
    t, d = x2d.shape
    bw = os_[0].shape[1] // dils[0]
    tm = TM_POST
    nt = t // tm
    tri = jnp.asarray(np.triu(np.ones((tm, tm), np.float32), k=1), dtype=BF16)
    row = lambda w, dil=1: pl.BlockSpec((tm // dil, dil * w), lambda i: (i, 0))
    full = lambda a: pl.BlockSpec(a.shape, lambda i: (0,) * a.ndim)
    return pl.pallas_call(
        functools.partial(_post_kernel, dils=dils),
        grid=(nt,),
        in_specs=[row(d), row(d)] + [row(bw, dil) for dil in dils] + [row(bw, dil) for dil in dils]
                 + [row(d), row(d), full(wa), full(wb), full(wo), full(gf), full(rwt), full(rb), full(tri)],
        scratch_shapes=[pltpu.VMEM((bw // LANES, tm, LANES), F32) for _ in range(2 * len(dils))],
        out_specs=[row(d), pl.BlockSpec((SLOTS, d), lambda i: (i, 0)), row(LANES),
                   pl.BlockSpec((N_EXPERTS, LANES), lambda i: (i, 0))],
        out_shape=[jax.ShapeDtypeStruct((t, d), F32), jax.ShapeDtypeStruct((nt * SLOTS, d), BF16),
                   jax.ShapeDtypeStruct((t, LANES), F32), jax.ShapeDtypeStruct((nt * N_EXPERTS, LANES), F32)],
        compiler_params=pltpu.CompilerParams(
            dimension_semantics=("parallel",), vmem_limit_bytes=V7X_VMEM_LIMIT_BYTES),
        name="post_attn",
    )(x2d, ya, *os_, *ls_, ga, gb, wa, wb, wo, gf, rwt, rb, tri)


def _expert_tiles(pcnt, seg_off, n_tiles):
    nt, n_e = pcnt.shape
    cpt = CHUNKS_PER_TILE
    nch = (pcnt // CHUNK).T
    cum = jnp.cumsum(nch, axis=1)
    total = cum[:, -1]
    tiles_e = (total + cpt - 1) // cpt
    tile_end = jnp.cumsum(tiles_e)
    n_active = tile_end[-1]
    i = jnp.arange(n_tiles, dtype=I32)
    last = jnp.minimum(i, n_active - 1)
    te = jnp.sum((last[:, None] >= tile_end[None, :]).astype(I32), axis=1)
    hot_e = (te[:, None] == jnp.arange(n_e, dtype=I32)[None, :])
    pick = lambda tab: jnp.sum(jnp.where(hot_e[:, :, None], tab[None], 0), axis=1)
    tile_start = jnp.sum(jnp.where(hot_e, (tile_end - tiles_e)[None, :], 0), axis=1)
    total_t = jnp.sum(jnp.where(hot_e, total[None, :], 0), axis=1)
    q = (last - tile_start)[:, None] * cpt + jnp.arange(cpt, dtype=I32)[None, :]
    valid = (q < total_t[:, None]) & (i < n_active)[:, None]
    cum_t, nch_t = pick(cum), pick(nch)
    chunk0_t = pick((seg_off.T + jnp.arange(nt, dtype=I32)[None, :] * SLOTS) // CHUNK)
    jj = jnp.sum((q[:, :, None] >= cum_t[:, None, :]).astype(I32), axis=2)
    hot_j = jj[:, :, None] == jnp.arange(nt, dtype=I32)[None, None, :]
    first = jnp.sum(jnp.where(hot_j, (cum_t - nch_t)[:, None, :], 0), axis=2)
    base = jnp.sum(jnp.where(hot_j, chunk0_t[:, None, :], 0), axis=2)
    src = jnp.where(valid, base + q - first, 0)
    trash = nt * SLOTS // CHUNK + (i % 2)[:, None] * cpt + jnp.arange(cpt, dtype=I32)[None, :]
    dst = jnp.where(valid, src, trash)
    group_end = jnp.sum(jnp.where(hot_e, tile_end[None, :], 0), axis=1)
    nxt = jnp.sum((group_end[:, None] >= tile_end[None, :]).astype(I32), axis=1)
    nxt = jnp.where(group_end < n_active, nxt, -1)
    return (te.astype(I32), nxt.astype(I32), n_active.astype(I32)[None],
            src.reshape(-1).astype(I32), dst.reshape(-1).astype(I32))


def _expert_kernel(te_ref, nx_ref, na_ref, cs_ref, cd_ref, xg_hbm, wg_hbm, bg_ref, wu_hbm, bu_ref, wd_hbm,
                   bd_ref, yg_hbm, xbuf, ybuf, wst, wbf, zbuf, sem_in, sem_out, sem_zero, sem_w, *,
                   n_token_tiles):
    i = pl.program_id(0)
    n_active = na_ref[0]
    slot = i % 2
    cpt = CHUNKS_PER_TILE
    w_hbm = (wg_hbm, wu_hbm, wd_hbm)

    def w_copy(n, e):
        return pltpu.make_async_copy(w_hbm[n].at[e], wst.at[n], sem_w.at[n])

    zero_starts = [j * SLOTS + TM_POST * TOP_K for j in range(n_token_tiles)]
    zero_starts += [n_token_tiles * SLOTS + j * SLACK_ROWS for j in range(2 * TM_EXPERT // SLACK_ROWS)]

    def zero_copy(n):
        return pltpu.make_async_copy(
            zbuf, yg_hbm.at[pl.ds(zero_starts[n] // CHUNK, SLACK_ROWS // CHUNK)], sem_zero.at[n])

    @pl.when(i == 0)
    def _():
        zbuf[...] = jnp.zeros_like(zbuf)
        for n in range(len(zero_starts)):
            zero_copy(n).start()

    def in_copy(c, sl, chunk):
        return pltpu.make_async_copy(xg_hbm.at[chunk], xbuf.at[sl, c], sem_in.at[sl, c])

    def out_copy(c, sl, chunk):
        return pltpu.make_async_copy(ybuf.at[sl, c], yg_hbm.at[chunk], sem_out.at[sl, c])

    def start_gather(tile, sl):
        for c in range(cpt):
            in_copy(c, sl, cs_ref[tile * cpt + c]).start()

    def wait_scatter(sl):
        for c in range(cpt):
            out_copy(c, sl, 0).wait()

    @pl.when(i == 0)
    def _():
        start_gather(0, 0)
        for n in range(len(w_hbm)):
            w_copy(n, te_ref[0]).start()

    @pl.when(i + 1 < n_active)
    def _():
        start_gather(i + 1, 1 - slot)

    @pl.when(i < n_active)
    def _():
        for c in range(cpt):
            in_copy(c, slot, 0).wait()

        @pl.when((i == 0) | (te_ref[i] != te_ref[jnp.maximum(i - 1, 0)]))
        def _():
            for n in range(len(w_hbm)):
                w_copy(n, 0).wait()
                wbf[n] = wst[n].astype(BF16)

                @pl.when(nx_ref[i] >= 0)
                def _():
                    w_copy(n, nx_ref[i]).start()

        @pl.when(i >= 1)
        def _():
            for c in range(cpt):
                out_copy(c, 1 - slot, cd_ref[(i - 1) * cpt + c]).start()

        x = xbuf[slot].reshape(TM_EXPERT, -1)
        e = te_ref[i]
        gate = jnp.minimum(jnp.dot(x, wbf[0], preferred_element_type=F32) + bg_ref[e], SWIGLU_LIMIT)
        up = jnp.clip(jnp.dot(x, wbf[1], preferred_element_type=F32) + bu_ref[e], -SWIGLU_LIMIT, SWIGLU_LIMIT)
        act = gate * jax.nn.sigmoid(SWIGLU_ALPHA * gate) * (up + 1.0)
        y = jnp.dot(act.astype(BF16), wbf[2], preferred_element_type=F32) + bd_ref[e]

        @pl.when(i >= 2)
        def _():
            wait_scatter(slot)

        ybuf[slot] = y.astype(BF16).reshape(ybuf.shape[1:])

        @pl.when(i == 0)
        def _():
            for n in range(len(zero_starts)):
                zero_copy(n).wait()

        @pl.when(i == n_active - 1)
        def _():
            for c in range(cpt):
                out_copy(c, slot, cd_ref[i * cpt + c]).start()
            wait_scatter(slot)

            @pl.when(i >= 1)
            def _():
                wait_scatter(1 - slot)


def _experts(xg, te, nxt, n_active, src, dst, wg, bg, wu, bu, wd, bd):
    rows, d = xg.shape
    n_e, _, dff = wg.shape
    assert d == dff
    n_tiles = te.shape[0]
    n_token_tiles = rows // SLOTS
    n_zero = n_token_tiles + 2 * TM_EXPERT // SLACK_ROWS
    b_spec = pl.BlockSpec((n_e, 1, d), lambda i, *_: (0, 0, 0))
    hbm = pl.BlockSpec(memory_space=pl.ANY)
    tile_bufs = pltpu.VMEM((2, CHUNKS_PER_TILE, CHUNK, d), BF16)
    yg = pl.pallas_call(
        functools.partial(_expert_kernel, n_token_tiles=n_token_tiles),
        grid_spec=pltpu.PrefetchScalarGridSpec(
            num_scalar_prefetch=5,
            grid=(n_tiles,),
            in_specs=[hbm, hbm, b_spec, hbm, b_spec, hbm, b_spec],
            out_specs=hbm,
            scratch_shapes=[tile_bufs, tile_bufs,
                            pltpu.VMEM((3, d, dff), F32), pltpu.VMEM((3, d, dff), BF16),
                            pltpu.VMEM((SLACK_ROWS // CHUNK, CHUNK, d), BF16),
                            pltpu.SemaphoreType.DMA((2, CHUNKS_PER_TILE)),
                            pltpu.SemaphoreType.DMA((2, CHUNKS_PER_TILE)),
                            pltpu.SemaphoreType.DMA((n_zero,)),
                            pltpu.SemaphoreType.DMA((3,))]),
        out_shape=jax.ShapeDtypeStruct(((rows + 2 * TM_EXPERT) // CHUNK, CHUNK, d), BF16),
        compiler_params=pltpu.CompilerParams(
            dimension_semantics=("arbitrary",), vmem_limit_bytes=V7X_VMEM_LIMIT_BYTES),
        name="experts",
    )(te, nxt, n_active, src, dst, xg.reshape(rows // CHUNK, CHUNK, d), wg, bg, wu, bu, wd, bd)
    return yg.reshape(rows + 2 * TM_EXPERT, d)


def _combine_kernel(x1_ref, yg_ref, meta_ref, gn_ref, o_ref, *, final_norm):
    meta = meta_ref[...]
    tm = meta.shape[0]
    slots_i = [meta[:, k:k + 1].astype(I32) for k in range(TOP_K)]
    probs = [meta[:, TOP_K + k:TOP_K + k + 1] for k in range(TOP_K)]
    iota_s = lax.broadcasted_iota(I32, (tm, PERM_BLOCK), 1)
    xo = x1_ref[...]
    for blk in range(SLOTS // PERM_BLOCK):
        base = blk * PERM_BLOCK
        wperm = jnp.zeros((tm, PERM_BLOCK), F32)
        for sk, pk in zip(slots_i, probs):
            wperm = wperm + jnp.where(iota_s == sk - base, pk, 0.0)
        xo = xo + jnp.dot(wperm.astype(BF16), yg_ref[base:base + PERM_BLOCK, :], preferred_element_type=F32)
    o_ref[...] = _rms(xo, gn_ref[...]) if final_norm else xo


def _combine(x1, yg, meta, gn, final_norm):
    t, d = x1.shape
    tm = TM_POST
    return pl.pallas_call(
        functools.partial(_combine_kernel, final_norm=final_norm),
        grid=(t // tm,),
        in_specs=[pl.BlockSpec((tm, d), lambda i: (i, 0)),
                  pl.BlockSpec((SLOTS, d), lambda i: (i, 0)),
                  pl.BlockSpec((tm, LANES), lambda i: (i, 0)),
                  pl.BlockSpec((1, d), lambda i: (0, 0))],
        out_specs=pl.BlockSpec((tm, d), lambda i: (i, 0)),
        out_shape=jax.ShapeDtypeStruct((t, d), F32),
        compiler_params=pltpu.CompilerParams(
            dimension_semantics=("parallel",), vmem_limit_bytes=V7X_VMEM_LIMIT_BYTES),
        name="combine",
    )(x1, yg, meta, gn)


def kernel(x, norm_mix, w_in, b_in, sinks, rel_bias, w_branch_a, w_branch_b, w_out, norm_ffn,
           router_w, router_b, w_gate, b_gate, w_up, b_up, w_down, b_down, norm_final):
    b, s, d = x.shape
    t = b * s
    depth = w_in.shape[0]
    a_q_w = A_KV_HEADS * A_GROUP * HEAD_DIM
    a_kv_w = A_KV_HEADS * HEAD_DIM
    b_w = B_HEADS * HEAD_DIM
    n_grp = len(B_GROUPS)
    dils = tuple(dil for _, dil in B_GROUPS)
    segs, col = [], 0
    for kind, width in (("q", a_q_w), ("k", a_kv_w), ("v", a_kv_w)):
        segs.append((col, width, kind, 1))
        col += width
    for kind in ("q", "k", "v"):
        for dil in dils:
            segs.append((col, b_w, kind, dil))
            col += b_w
    for _ in range(2):
        segs.append((col, d, "gate", 1))
        col += d
    segs = tuple(segs)
    n_a = A_KV_HEADS * A_GROUP
    nt = t // TM_POST
    max_chunks = nt * ((TM_POST * TOP_K + N_EXPERTS * (CHUNK - 1)) // CHUNK)
    n_tiles = -(-(max_chunks + N_EXPERTS * (CHUNKS_PER_TILE - 1)) // CHUNKS_PER_TILE)

    bias_a = _band_bias(rel_bias[:, :n_a], A_HALF_WINDOW, TQ_ATTN, 1)
    bias_b = [_band_bias(rel_bias[:, n_a + gi * B_HEADS:n_a + (gi + 1) * B_HEADS],
                         win // (2 * dil), TQ_ATTN, dil) for gi, (win, dil) in enumerate(B_GROUPS)]

    x2d = x.reshape(t, d)
    for layer in range(depth):
        proj = _inproj(x2d, norm_mix[layer][None], w_in[layer].astype(BF16), b_in[layer][None], segs)
        qa, ka, va = proj[:3]
        qb, kb, vb = proj[3:3 + n_grp], proj[3 + n_grp:3 + 2 * n_grp], proj[3 + 2 * n_grp:3 + 3 * n_grp]
        ga, gb = proj[-2:]

        (ya,) = _band_attn(qa.reshape(b, s, a_q_w), ka.reshape(b, s, a_kv_w), va.reshape(b, s, a_kv_w),
                           bias_a, sinks[layer], n_kv=A_KV_HEADS, grp=A_GROUP, hw=A_HALF_WINDOW,
                           tq=TQ_ATTN, block=BLOCK_ATTN_A, reps=1, want_lse=False)
        os_, ls_ = [], []
        for gi, (win, dil) in enumerate(B_GROUPS):
            sub = lambda a: a.reshape(b, s // dil, dil * b_w)
            o, lse = _band_attn(sub(qb[gi]), sub(kb[gi]), sub(vb[gi]), bias_b[gi], None, n_kv=B_HEADS, grp=1,
                                hw=win // (2 * dil), tq=TQ_ATTN, block=BLOCK_ATTN_B, reps=dil,
                                want_lse=True)
            os_.append(o.reshape(t // dil, dil * b_w))
            ls_.append(lse.reshape(t // dil, dil * b_w))

        rwt = router_w[layer].T
        rwt_hi = rwt.astype(BF16)
        rwt_split = jnp.stack([rwt_hi, (rwt - rwt_hi.astype(F32)).astype(BF16)])
        x1, xg, meta, segs_out = _post(
            x2d, ya.reshape(t, a_q_w), os_, ls_, dils, ga, gb,
            w_branch_a[layer].astype(BF16), w_branch_b[layer].astype(BF16), w_out[layer].astype(BF16),
            norm_ffn[layer][None], rwt_split, router_b[layer][:, None])

        segs3 = segs_out.reshape(nt, N_EXPERTS, LANES)
        te, nxt, n_active, src, dst = _expert_tiles(segs3[:, :, 0].astype(I32), segs3[:, :, 1].astype(I32), n_tiles)
        yg = _experts(xg, te, nxt, n_active, src, dst, w_gate[layer], b_gate[layer][:, None], w_up[layer],
                      b_up[layer][:, None], w_down[layer], b_down[layer][:, None])
        x2d = _combine(x1, yg, meta, norm_final[None], layer == depth - 1)
    return x2d.reshape(b, s, d)
```

```python
import functools

import numpy as np
import jax
import jax.numpy as jnp
from jax import lax
from jax.experimental import pallas as pl
from jax.experimental.pallas import tpu as pltpu

F32 = jnp.float32
BF16 = jnp.bfloat16
I32 = jnp.int32

HEAD_DIM = 64
A_KV_HEADS = 4
A_GROUP = 4
A_HALF_WINDOW = 128
B_GROUPS = ((128, 1), (512, 4), (2048, 16))
B_HEADS = 4
N_BUCKETS = 32
MAX_DISTANCE = 1024
N_EXPERTS = 32
TOP_K = 4
SWIGLU_LIMIT = 7.0
SWIGLU_ALPHA = 1.702
EPS = 1e-5
NEG_INF = -1e30
LOG2E = 1.4426950408889634

V7X_VMEM_LIMIT_BYTES = 56 * 1024 * 1024
LANES = 128
BF16_SUBLANES = 16

TM_PROJ = 1024
TQ_ATTN = 128
BLOCK_ATTN_A = 512
BLOCK_ATTN_B = 512
TM_POST = 512
TM_EXPERT = 512

CHUNK = BF16_SUBLANES
CHUNKS_PER_TILE = TM_EXPERT // CHUNK
SLOTS = -(-(TM_POST * TOP_K + N_EXPERTS * (CHUNK - 1)) // LANES) * LANES
SLACK_ROWS = SLOTS - TM_POST * TOP_K
assert (2 * TM_EXPERT) % SLACK_ROWS == 0
PERM_BLOCK = 512
assert SLOTS % PERM_BLOCK == 0


def _t5_buckets(rel):
    half = N_BUCKETS // 2
    max_exact = half // 2
    ret = np.where(rel > 0, half, 0)
    n = np.abs(rel)
    large = max_exact + (np.log(np.maximum(n, 1) / max_exact)
                         / np.log(MAX_DISTANCE / max_exact) * (half - max_exact)).astype(np.int32)
    large = np.minimum(large, half - 1)
    return (ret + np.where(n < max_exact, n, large)).astype(np.int32)


def _rms(x, g):
    return x * lax.rsqrt(jnp.mean(x * x, axis=-1, keepdims=True) + EPS) * g


def _inproj_kernel(x_ref, g_ref, w_ref, b_ref, *refs, segs):
    out_refs, scr = refs[:-1], refs[-1]
    h = _rms(x_ref[...], g_ref[...]).astype(BF16)
    for ref, (c0, width, kind, dil) in zip(out_refs, segs):
        acc = jnp.dot(h, w_ref[:, c0:c0 + width], preferred_element_type=F32) + b_ref[:, c0:c0 + width]
        if kind == "q":
            acc = acc * (HEAD_DIM ** -0.5 * LOG2E)
        elif kind == "gate":
            acc = jax.nn.sigmoid(acc)
        if dil == 1:
            ref[...] = acc.astype(ref.dtype)
        else:
            n = acc.shape[0] // dil
            for c in range(width // LANES):
                scr[c] = acc[:, c * LANES:(c + 1) * LANES]
            for r in range(dil):
                for c in range(width // LANES):
                    col = r * width + c * LANES
                    ref[:, col:col + LANES] = scr[c, pl.ds(r, n, stride=dil), :].astype(ref.dtype)


def _inproj(x2d, g, w_bf16, b, segs):
    t, d = x2d.shape
    n = w_bf16.shape[1]
    max_w = max(w for _, w, _, dil in segs if dil > 1)
    return pl.pallas_call(
        functools.partial(_inproj_kernel, segs=segs),
        grid=(t // TM_PROJ,),
        in_specs=[
            pl.BlockSpec((TM_PROJ, d), lambda i: (i, 0)),
            pl.BlockSpec((1, d), lambda i: (0, 0)),
            pl.BlockSpec((d, n), lambda i: (0, 0), pipeline_mode=pl.Buffered(1)),
            pl.BlockSpec((1, n), lambda i: (0, 0)),
        ],
        out_specs=[pl.BlockSpec((TM_PROJ // dil, dil * w), lambda i: (i, 0)) for _, w, _, dil in segs],
        out_shape=[jax.ShapeDtypeStruct((t // dil, dil * w), BF16) for _, w, _, dil in segs],
        scratch_shapes=[pltpu.VMEM((max_w // LANES, TM_PROJ, LANES), F32)],
        compiler_params=pltpu.CompilerParams(
            dimension_semantics=("parallel",), vmem_limit_bytes=V7X_VMEM_LIMIT_BYTES),
        name="inproj",
    )(x2d, g, w_bf16, b)


def _band_bias(table, hw, tq, dil):
    tk = tq + 2 * hw
    off = np.arange(tk)[None, :] - hw - np.arange(tq)[:, None]
    band = np.abs(off) <= hw
    col = np.arange(tk)[None, :]
    masks = np.stack([band & (col >= hw), band, band & (col < hw + tq)])
    onehot = (_t5_buckets(off * dil)[..., None] == np.arange(N_BUCKETS)).astype(np.float32)
    bias = jnp.einsum("qkn,nh->hqk", jnp.asarray(onehot), table.astype(F32),
                      precision=lax.Precision.HIGHEST)
    return jnp.where(masks[:, None], bias[None] * LOG2E, NEG_INF)


def _band_attn_kernel(*refs, n_kv, grp, tq, hw, has_sink, want_lse):
    q_ref, kp, kc, kn, vp, vc, vn, bias_ref = refs[:8]
    rest = refs[8:]
    sink_ref = None
    if has_sink:
        sink_ref, rest = rest[0], rest[1:]
    o_ref = rest[0]
    lse_ref = rest[1] if want_lse else None

    j, nt = pl.program_id(2), pl.num_programs(2)
    k = jnp.concatenate([kp[0], kc[0], kn[0]], axis=0)
    v = jnp.concatenate([vp[0], vc[0], vn[0]], axis=0)
    n_sub = q_ref.shape[1] // tq
    tk = tq + 2 * hw
    ones = jnp.ones((tk, HEAD_DIM), BF16)
    sinks = []
    if has_sink:
        row = lax.broadcasted_iota(I32, (grp * tq, 1), 0)
        for h in range(n_kv):
            sk = jnp.full((grp * tq, 1), sink_ref[h * grp] * LOG2E, F32)
            for g in range(1, grp):
                sk = jnp.where(row >= g * tq, sink_ref[h * grp + g] * LOG2E, sk)
            sinks.append(sk)
    for sub in range(n_sub):
        var = 1
        if sub == 0:
            var = jnp.where(j == 0, 0, var)
        if sub == n_sub - 1:
            var = jnp.where(j == nt - 1, 2, var)
        r0 = sub * tq
        for h in range(n_kv):
            k_h = k[r0:r0 + tk, h * HEAD_DIM:(h + 1) * HEAD_DIM]
            v_h = jnp.concatenate([v[r0:r0 + tk, h * HEAD_DIM:(h + 1) * HEAD_DIM], ones], axis=1)
            c0 = h * grp * HEAD_DIM
            q = jnp.concatenate([q_ref[0, r0:r0 + tq, c0 + g * HEAD_DIM:c0 + (g + 1) * HEAD_DIM]
                                 for g in range(grp)], axis=0)
            s = lax.dot_general(q, k_h, (((1,), (1,)), ((), ())), preferred_element_type=F32)
            s = s + bias_ref[var, h]
            m = jnp.max(s, axis=-1, keepdims=True)
            if has_sink:
                sk = sinks[h]
                m = jnp.maximum(m, sk)
            p = jnp.exp2((s - m).astype(BF16))
            ol = jnp.dot(p, v_h, preferred_element_type=F32)
            l = ol[:, HEAD_DIM:HEAD_DIM + 1]
            if has_sink:
                l = l + jnp.exp2(sk - m)
            o = ol[:, :HEAD_DIM] / l
            if want_lse:
                lse = jnp.broadcast_to(m + jnp.log2(l), o.shape)
            for g in range(grp):
                c = c0 + g * HEAD_DIM
                o_ref[0, r0:r0 + tq, c:c + HEAD_DIM] = o[g * tq:(g + 1) * tq].astype(o_ref.dtype)
                if want_lse:
                    lse_ref[0, r0:r0 + tq, c:c + HEAD_DIM] = lse[g * tq:(g + 1) * tq]


def _band_attn_heads_kernel(q_ref, kp, kc, kn, vp, vc, vn, bias_ref, o_ref, lse_ref, *, n_heads, tq, hw):
    j, nt = pl.program_id(2), pl.num_programs(2)
    k = jnp.concatenate([kp[0], kc[0], kn[0]], axis=0)
    v = jnp.concatenate([vp[0], vc[0], vn[0]], axis=0)
    n_sub = q_ref.shape[1] // tq
    tk = tq + 2 * hw
    width = n_heads * HEAD_DIM
    lane_head = lax.broadcasted_iota(I32, (tq, width), 1) // HEAD_DIM
    ones = jnp.ones((tk, LANES), BF16)
    for sub in range(n_sub):
        var = 1
        if sub == 0:
            var = jnp.where(j == 0, 0, var)
        if sub == n_sub - 1:
            var = jnp.where(j == nt - 1, 2, var)
        r0 = sub * tq
        q = q_ref[0, r0:r0 + tq, :]
        q_bd = jnp.concatenate([jnp.where(lane_head == h, q, jnp.zeros_like(q)) for h in range(n_heads)], axis=0)
        s = lax.dot_general(q_bd, k[r0:r0 + tk], (((1,), (1,)), ((), ())), preferred_element_type=F32)
        s = s + bias_ref[var, 0]
        m = jnp.max(s, axis=-1, keepdims=True)
        p = jnp.exp2((s - m).astype(BF16))
        o_full = jnp.dot(p, v[r0:r0 + tk], preferred_element_type=F32)
        l = jnp.dot(p, ones, preferred_element_type=F32)[:, :1]
        o_sel = jnp.zeros((tq, width), F32)
        m_sel = jnp.zeros((tq, width), F32)
        l_sel = jnp.ones((tq, width), F32)
        for h in range(n_heads):
            hit = lane_head == h
            o_sel = jnp.where(hit, o_full[h * tq:(h + 1) * tq], o_sel)
            m_sel = jnp.where(hit, m[h * tq:(h + 1) * tq], m_sel)
            l_sel = jnp.where(hit, l[h * tq:(h + 1) * tq], l_sel)
        o_ref[0, r0:r0 + tq, :] = (o_sel / l_sel).astype(o_ref.dtype)
        lse_ref[0, r0:r0 + tq, :] = m_sel + jnp.log2(l_sel)


def _band_attn(q, k, v, bias3, sinks, *, n_kv, grp, hw, tq, block, reps, want_lse):
    b, l, _ = q.shape
    hq = n_kv * grp
    qw, kw = hq * HEAD_DIM, n_kv * HEAD_DIM
    nt = l // block
    ratio = block // hw
    nhw = l // hw
    tk = tq + 2 * hw
    assert nt * (block // tq) >= 2

    prev = lambda bi, r, j: (bi, jnp.maximum(j * ratio - 1, 0), r)
    cur = lambda bi, r, j: (bi, j, r)
    nxt = lambda bi, r, j: (bi, jnp.minimum((j + 1) * ratio, nhw - 1), r)
    in_specs = [
        pl.BlockSpec((1, block, qw), cur),
        pl.BlockSpec((1, hw, kw), prev), pl.BlockSpec((1, block, kw), cur), pl.BlockSpec((1, hw, kw), nxt),
        pl.BlockSpec((1, hw, kw), prev), pl.BlockSpec((1, block, kw), cur), pl.BlockSpec((1, hw, kw), nxt),
    ]
    stacked = grp == 1
    bias_shape = (3, 1, hq * tq, tk) if stacked else (3, n_kv, grp * tq, tk)
    in_specs.append(pl.BlockSpec(bias_shape, lambda bi, r, j: (0, 0, 0, 0), pipeline_mode=pl.Buffered(1)))
    args = [q, k, k, k, v, v, v, bias3.reshape(bias_shape)]
    if stacked:
        assert sinks is None and want_lse
        body = functools.partial(_band_attn_heads_kernel, n_heads=n_kv, tq=tq, hw=hw)
    else:
        body = functools.partial(_band_attn_kernel, n_kv=n_kv, grp=grp, tq=tq, hw=hw,
                                 has_sink=sinks is not None, want_lse=want_lse)
    if sinks is not None:
        in_specs.append(pl.BlockSpec(memory_space=pltpu.SMEM))
        args.append(sinks)
    out_specs = [pl.BlockSpec((1, block, qw), cur)]
    out_shape = [jax.ShapeDtypeStruct((b, l, reps * qw), BF16)]
    if want_lse:
        out_specs.append(pl.BlockSpec((1, block, qw), cur))
        out_shape.append(jax.ShapeDtypeStruct((b, l, reps * qw), F32))
    return pl.pallas_call(
        body,
        grid=(b, reps, nt),
        in_specs=in_specs,
        out_specs=out_specs,
        out_shape=out_shape,
        compiler_params=pltpu.CompilerParams(
            dimension_semantics=("parallel", "parallel", "parallel"),
            vmem_limit_bytes=V7X_VMEM_LIMIT_BYTES),
        name="band_attn_a" if sinks is not None else "band_attn_b",
    )(*args)


def _token_order(ref, scr, dil):
    if dil == 1:
        return ref[...].astype(F32)
    n = ref.shape[0]
    width = ref.shape[1] // dil
    for r in range(dil):
        for c in range(width // LANES):
            col = r * width + c * LANES
            scr[c, pl.ds(r, n, stride=dil), :] = ref[:, col:col + LANES].astype(F32)
    return jnp.concatenate([scr[c] for c in range(width // LANES)], axis=1)


def _post_kernel(x_ref, ya_ref, o1, o2, o3, l1, l2, l3, ga_ref, gb_ref, wa_ref, wb_ref, wo_ref,
                 gf_ref, rwt_ref, rb_ref, tri_ref, x1_ref, xg_ref, meta_ref, segs_ref, *scrs, dils):
    os_ = [_token_order(r, scrs[2 * i], dil) for i, (r, dil) in enumerate(zip((o1, o2, o3), dils))]
    ls = [_token_order(r, scrs[2 * i + 1], dil) for i, (r, dil) in enumerate(zip((l1, l2, l3), dils))]
    mx = jnp.maximum(jnp.maximum(ls[0], ls[1]), ls[2])
    es = [jnp.exp2(l - mx) for l in ls]
    den = es[0] + es[1] + es[2]
    yb = (es[0] * os_[0] + es[1] * os_[1] + es[2] * os_[2]) / den
    za = jnp.dot(ya_ref[...], wa_ref[...], preferred_element_type=F32)
    zb = jnp.dot(yb.astype(BF16), wb_ref[...], preferred_element_type=F32)
    merged = ga_ref[...].astype(F32) * za + gb_ref[...].astype(F32) * zb
    x1 = x_ref[...] + jnp.dot(merged.astype(BF16), wo_ref[...], preferred_element_type=F32)
    x1_ref[...] = x1
    h2 = _rms(x1, gf_ref[...])

    h_hi = h2.astype(BF16)
    h_lo = (h2 - h_hi.astype(F32)).astype(BF16)
    nt_dot = lambda a, b_: lax.dot_general(a, b_, (((1,), (1,)), ((), ())), preferred_element_type=F32)
    logits = (nt_dot(rwt_ref[0], h_hi) + nt_dot(rwt_ref[0], h_lo) + nt_dot(rwt_ref[1], h_hi)) + rb_ref[...]
    n_e, tm = logits.shape
    iota_e = lax.broadcasted_iota(I32, (n_e, tm), 0)
    work = logits
    vals, hots = [], []
    for _ in range(TOP_K):
        mk = jnp.max(work, axis=0, keepdims=True)
        ik = jnp.min(jnp.where(work == mk, iota_e, n_e), axis=0, keepdims=True)
        hot = iota_e == ik
        vals.append(mk)
        hots.append(hot)
        work = jnp.where(hot, -jnp.inf, work)
    exps = [jnp.exp(vk - vals[0]) for vk in vals]
    tot = exps[0] + exps[1] + exps[2] + exps[3]
    probs = [ek / tot for ek in exps]

    sel = jnp.zeros((n_e, tm), F32)
    for hot in hots:
        sel = sel + jnp.where(hot, 1.0, 0.0)
    cnt = jnp.sum(sel, axis=1, keepdims=True)
    pcnt = jnp.floor((cnt + (CHUNK - 1)) / CHUNK) * CHUNK
    r_i = lax.broadcasted_iota(I32, (n_e, n_e), 0)
    c_i = lax.broadcasted_iota(I32, (n_e, n_e), 1)
    pcnt_row = jnp.sum(jnp.where(r_i == c_i, pcnt, 0.0), axis=0, keepdims=True)
    seg_off = jnp.sum(jnp.where(c_i < r_i, pcnt_row, 0.0), axis=1, keepdims=True)
    before = jnp.dot(sel.astype(BF16), tri_ref[...], preferred_element_type=F32)
    slot_of = seg_off + before
    slots = [jnp.sum(jnp.where(hot, slot_of, 0.0), axis=0, keepdims=True) for hot in hots]

    iota_s = lax.broadcasted_iota(I32, (PERM_BLOCK, tm), 0)
    slots_i = [sk.astype(I32) for sk in slots]
    for blk in range(SLOTS // PERM_BLOCK):
        perm = jnp.zeros((PERM_BLOCK, tm), F32)
        for sk in slots_i:
            perm = perm + jnp.where(iota_s == sk - blk * PERM_BLOCK, 1.0, 0.0)
        xg_ref[blk * PERM_BLOCK:(blk + 1) * PERM_BLOCK, :] = jnp.dot(
            perm.astype(BF16), h_hi, preferred_element_type=F32).astype(BF16)

    rows = slots + probs
    meta_t = jnp.concatenate(rows + [jnp.zeros((LANES - len(rows), tm), F32)], axis=0)
    meta_ref[...] = meta_t.T
    lane = lax.broadcasted_iota(I32, (n_e, LANES), 1)
    segs_ref[...] = jnp.where(lane == 0, pcnt, jnp.where(lane == 1, seg_off, 0.0))


def _post(x2d, ya, os_, ls_, dils, ga, gb, wa, wb, wo, gf, rwt, rb):
    t, d = x2d.shape
    bw = os_[0].shape[1] // dils[0]
    tm = TM_POST
    nt = t // tm
    tri = jnp.asarray(np.triu(np.ones((tm, tm), np.float32), k=1), dtype=BF16)
    row = lambda w, dil=1: pl.BlockSpec((tm // dil, dil * w), lambda i: (i, 0))
    full = lambda a: pl.BlockSpec(a.shape, lambda i: (0,) * a.ndim)
    return pl.pallas_call(
        functools.partial(_post_kernel, dils=dils),
        grid=(nt,),
        in_specs=[row(d), row(d)] + [row(bw, dil) for dil in dils] + [row(bw, dil) for dil in dils]
                 + [row(d), row(d), full(wa), full(wb), full(wo), full(gf), full(rwt), full(rb), full(tri)],
        scratch_shapes=[pltpu.VMEM((bw // LANES, tm, LANES), F32) for _ in range(2 * len(dils))],
        out_specs=[row(d), pl.BlockSpec((SLOTS, d), lambda i: (i, 0)), row(LANES),
                   pl.BlockSpec((N_EXPERTS, LANES), lambda i: (i, 0))],
        out_shape=[jax.ShapeDtypeStruct((t, d), F32), jax.ShapeDtypeStruct((nt * SLOTS, d), BF16),
                   jax.ShapeDtypeStruct((t, LANES), F32), jax.ShapeDtypeStruct((nt * N_EXPERTS, LANES), F32)],
        compiler_params=pltpu.CompilerParams(
            dimension_semantics=("parallel",), vmem_limit_bytes=V7X_VMEM_LIMIT_BYTES),
        name="post_attn",
    )(x2d, ya, *os_, *ls_, ga, gb, wa, wb, wo, gf, rwt, rb, tri)


def _expert_tiles(pcnt, seg_off, n_tiles):
    nt, n_e = pcnt.shape
    cpt = CHUNKS_PER_TILE
    nch = (pcnt // CHUNK).T
    cum = jnp.cumsum(nch, axis=1)
    total = cum[:, -1]
    tiles_e = (total + cpt - 1) // cpt
    tile_end = jnp.cumsum(tiles_e)
    n_active = tile_end[-1]
    i = jnp.arange(n_tiles, dtype=I32)
    last = jnp.minimum(i, n_active - 1)
    te = jnp.sum((last[:, None] >= tile_end[None, :]).astype(I32), axis=1)
    hot_e = (te[:, None] == jnp.arange(n_e, dtype=I32)[None, :])
    pick = lambda tab: jnp.sum(jnp.where(hot_e[:, :, None], tab[None], 0), axis=1)
    tile_start = jnp.sum(jnp.where(hot_e, (tile_end - tiles_e)[None, :], 0), axis=1)
    total_t = jnp.sum(jnp.where(hot_e, total[None, :], 0), axis=1)
    q = (last - tile_start)[:, None] * cpt + jnp.arange(cpt, dtype=I32)[None, :]
    valid = (q < total_t[:, None]) & (i < n_active)[:, None]
    cum_t, nch_t = pick(cum), pick(nch)
    chunk0_t = pick((seg_off.T + jnp.arange(nt, dtype=I32)[None, :] * SLOTS) // CHUNK)
    jj = jnp.sum((q[:, :, None] >= cum_t[:, None, :]).astype(I32), axis=2)
    hot_j = jj[:, :, None] == jnp.arange(nt, dtype=I32)[None, None, :]
    first = jnp.sum(jnp.where(hot_j, (cum_t - nch_t)[:, None, :], 0), axis=2)
    base = jnp.sum(jnp.where(hot_j, chunk0_t[:, None, :], 0), axis=2)
    src = jnp.where(valid, base + q - first, 0)
    trash = nt * SLOTS // CHUNK + (i % 2)[:, None] * cpt + jnp.arange(cpt, dtype=I32)[None, :]
    dst = jnp.where(valid, src, trash)
    group_end = jnp.sum(jnp.where(hot_e, tile_end[None, :], 0), axis=1)
    nxt = jnp.sum((group_end[:, None] >= tile_end[None, :]).astype(I32), axis=1)
    nxt = jnp.where(group_end < n_active, nxt, -1)
    return (te.astype(I32), nxt.astype(I32), n_active.astype(I32)[None],
            src.reshape(-1).astype(I32), dst.reshape(-1).astype(I32))


def _expert_kernel(te_ref, nx_ref, na_ref, cs_ref, cd_ref, xg_hbm, wg_hbm, bg_ref, wu_hbm, bu_ref, wd_hbm,
                   bd_ref, yg_hbm, xbuf, ybuf, wst, wbf, zbuf, sem_in, sem_out, sem_zero, sem_w, *,
                   n_token_tiles):
    i = pl.program_id(0)
    n_active = na_ref[0]
    slot = i % 2
    cpt = CHUNKS_PER_TILE
    w_hbm = (wg_hbm, wu_hbm, wd_hbm)

    def w_copy(n, e):
        return pltpu.make_async_copy(w_hbm[n].at[e], wst.at[n], sem_w.at[n])

    zero_starts = [j * SLOTS + TM_POST * TOP_K for j in range(n_token_tiles)]
    zero_starts += [n_token_tiles * SLOTS + j * SLACK_ROWS for j in range(2 * TM_EXPERT // SLACK_ROWS)]

    def zero_copy(n):
        return pltpu.make_async_copy(
            zbuf, yg_hbm.at[pl.ds(zero_starts[n] // CHUNK, SLACK_ROWS // CHUNK)], sem_zero.at[n])

    @pl.when(i == 0)
    def _():
        zbuf[...] = jnp.zeros_like(zbuf)
        for n in range(len(zero_starts)):
            zero_copy(n).start()

    def in_copy(c, sl, chunk):
        return pltpu.make_async_copy(xg_hbm.at[chunk], xbuf.at[sl, c], sem_in.at[sl, c])

    def out_copy(c, sl, chunk):
        return pltpu.make_async_copy(ybuf.at[sl, c], yg_hbm.at[chunk], sem_out.at[sl, c])

    def start_gather(tile, sl):
        for c in range(cpt):
            in_copy(c, sl, cs_ref[tile * cpt + c]).start()

    def wait_scatter(sl):
        for c in range(cpt):
            out_copy(c, sl, 0).wait()

    @pl.when(i == 0)
    def _():
        start_gather(0, 0)
        for n in range(len(w_hbm)):
            w_copy(n, te_ref[0]).start()

    @pl.when(i + 1 < n_active)
    def _():
        start_gather(i + 1, 1 - slot)

    @pl.when(i < n_active)
    def _():
        for c in range(cpt):
            in_copy(c, slot, 0).wait()

        @pl.when((i == 0) | (te_ref[i] != te_ref[jnp.maximum(i - 1, 0)]))
        def _():
            for n in range(len(w_hbm)):
                w_copy(n, 0).wait()
                wbf[n] = wst[n].astype(BF16)

                @pl.when(nx_ref[i] >= 0)
                def _():
                    w_copy(n, nx_ref[i]).start()

        @pl.when(i >= 1)
        def _():
            for c in range(cpt):
                out_copy(c, 1 - slot, cd_ref[(i - 1) * cpt + c]).start()

        x = xbuf[slot].reshape(TM_EXPERT, -1)
        e = te_ref[i]
        gate = jnp.minimum(jnp.dot(x, wbf[0], preferred_element_type=F32) + bg_ref[e], SWIGLU_LIMIT)
        up = jnp.clip(jnp.dot(x, wbf[1], preferred_element_type=F32) + bu_ref[e], -SWIGLU_LIMIT, SWIGLU_LIMIT)
        act = gate * jax.nn.sigmoid(SWIGLU_ALPHA * gate) * (up + 1.0)
        y = jnp.dot(act.astype(BF16), wbf[2], preferred_element_type=F32) + bd_ref[e]

        @pl.when(i >= 2)
        def _():
            wait_scatter(slot)

        ybuf[slot] = y.astype(BF16).reshape(ybuf.shape[1:])

        @pl.when(i == 0)
        def _():
            for n in range(len(zero_starts)):
                zero_copy(n).wait()

        @pl.when(i == n_active - 1)
        def _():
            for c in range(cpt):
                out_copy(c, slot, cd_ref[i * cpt + c]).start()
            wait_scatter(slot)

            @pl.when(i >= 1)
            def _():
                wait_scatter(1 - slot)


def _experts(xg, te, nxt, n_active, src, dst, wg, bg, wu, bu, wd, bd):
    rows, d = xg.shape
    n_e, _, dff = wg.shape
    assert d == dff
    n_tiles = te.shape[0]
    n_token_tiles = rows // SLOTS
    n_zero = n_token_tiles + 2 * TM_EXPERT // SLACK_ROWS
    b_spec = pl.BlockSpec((n_e, 1, d), lambda i, *_: (0, 0, 0))
    hbm = pl.BlockSpec(memory_space=pl.ANY)
    tile_bufs = pltpu.VMEM((2, CHUNKS_PER_TILE, CHUNK, d), BF16)
    yg = pl.pallas_call(
        functools.partial(_expert_kernel, n_token_tiles=n_token_tiles),
        grid_spec=pltpu.PrefetchScalarGridSpec(
            num_scalar_prefetch=5,
            grid=(n_tiles,),
            in_specs=[hbm, hbm, b_spec, hbm, b_spec, hbm, b_spec],
            out_specs=hbm,
            scratch_shapes=[tile_bufs, tile_bufs,
                            pltpu.VMEM((3, d, dff), F32), pltpu.VMEM((3, d, dff), BF16),
                            pltpu.VMEM((SLACK_ROWS // CHUNK, CHUNK, d), BF16),
                            pltpu.SemaphoreType.DMA((2, CHUNKS_PER_TILE)),
                            pltpu.SemaphoreType.DMA((2, CHUNKS_PER_TILE)),
                            pltpu.SemaphoreType.DMA((n_zero,)),
                            pltpu.SemaphoreType.DMA((3,))]),
        out_shape=jax.ShapeDtypeStruct(((rows + 2 * TM_EXPERT) // CHUNK, CHUNK, d), BF16),
        compiler_params=pltpu.CompilerParams(
            dimension_semantics=("arbitrary",), vmem_limit_bytes=V7X_VMEM_LIMIT_BYTES),
        name="experts",
    )(te, nxt, n_active, src, dst, xg.reshape(rows // CHUNK, CHUNK, d), wg, bg, wu, bu, wd, bd)
    return yg.reshape(rows + 2 * TM_EXPERT, d)


def _combine_kernel(x1_ref, yg_ref, meta_ref, gn_ref, o_ref, *, final_norm):
    meta = meta_ref[...]
    tm = meta.shape[0]
    slots_i = [meta[:, k:k + 1].astype(I32) for k in range(TOP_K)]
    probs = [meta[:, TOP_K + k:TOP_K + k + 1] for k in range(TOP_K)]
    iota_s = lax.broadcasted_iota(I32, (tm, PERM_BLOCK), 1)
    xo = x1_ref[...]
    for blk in range(SLOTS // PERM_BLOCK):
        base = blk * PERM_BLOCK
        wperm = jnp.zeros((tm, PERM_BLOCK), F32)
        for sk, pk in zip(slots_i, probs):
            wperm = wperm + jnp.where(iota_s == sk - base, pk, 0.0)
        xo = xo + jnp.dot(wperm.astype(BF16), yg_ref[base:base + PERM_BLOCK, :], preferred_element_type=F32)
    o_ref[...] = _rms(xo, gn_ref[...]) if final_norm else xo


def _combine(x1, yg, meta, gn, final_norm):
    t, d = x1.shape
    tm = TM_POST
    return pl.pallas_call(
        functools.partial(_combine_kernel, final_norm=final_norm),
        grid=(t // tm,),
        in_specs=[pl.BlockSpec((tm, d), lambda i: (i, 0)),
                  pl.BlockSpec((SLOTS, d), lambda i: (i, 0)),
                  pl.BlockSpec((tm, LANES), lambda i: (i, 0)),
                  pl.BlockSpec((1, d), lambda i: (0, 0))],
        out_specs=pl.BlockSpec((tm, d), lambda i: (i, 0)),
        out_shape=jax.ShapeDtypeStruct((t, d), F32),
        compiler_params=pltpu.CompilerParams(
            dimension_semantics=("parallel",), vmem_limit_bytes=V7X_VMEM_LIMIT_BYTES),
        name="combine",
    )(x1, yg, meta, gn)


def kernel(x, norm_mix, w_in, b_in, sinks, rel_bias, w_branch_a, w_branch_b, w_out, norm_ffn,
           router_w, router_b, w_gate, b_gate, w_up, b_up, w_down, b_down, norm_final):
    b, s, d = x.shape
    t = b * s
    depth = w_in.shape[0]
    a_q_w = A_KV_HEADS * A_GROUP * HEAD_DIM
    a_kv_w = A_KV_HEADS * HEAD_DIM
    b_w = B_HEADS * HEAD_DIM
    n_grp = len(B_GROUPS)
    dils = tuple(dil for _, dil in B_GROUPS)
    segs, col = [], 0
    for kind, width in (("q", a_q_w), ("k", a_kv_w), ("v", a_kv_w)):
        segs.append((col, width, kind, 1))
        col += width
    for kind in ("q", "k", "v"):
        for dil in dils:
            segs.append((col, b_w, kind, dil))
            col += b_w
    for _ in range(2):
        segs.append((col, d, "gate", 1))
        col += d
    segs = tuple(segs)
    n_a = A_KV_HEADS * A_GROUP
    nt = t // TM_POST
    max_chunks = nt * ((TM_POST * TOP_K + N_EXPERTS * (CHUNK - 1)) // CHUNK)
    n_tiles = -(-(max_chunks + N_EXPERTS * (CHUNKS_PER_TILE - 1)) // CHUNKS_PER_TILE)

    bias_a = _band_bias(rel_bias[:, :n_a], A_HALF_WINDOW, TQ_ATTN, 1)
    bias_b = [_band_bias(rel_bias[:, n_a + gi * B_HEADS:n_a + (gi + 1) * B_HEADS],
                         win // (2 * dil), TQ_ATTN, dil) for gi, (win, dil) in enumerate(B_GROUPS)]

    x2d = x.reshape(t, d)
    for layer in range(depth):
        proj = _inproj(x2d, norm_mix[layer][None], w_in[layer].astype(BF16), b_in[layer][None], segs)
        qa, ka, va = proj[:3]
        qb, kb, vb = proj[3:3 + n_grp], proj[3 + n_grp:3 + 2 * n_grp], proj[3 + 2 * n_grp:3 + 3 * n_grp]
        ga, gb = proj[-2:]

        (ya,) = _band_attn(qa.reshape(b, s, a_q_w), ka.reshape(b, s, a_kv_w), va.reshape(b, s, a_kv_w),
                           bias_a, sinks[layer], n_kv=A_KV_HEADS, grp=A_GROUP, hw=A_HALF_WINDOW,
                           tq=TQ_ATTN, block=BLOCK_ATTN_A, reps=1, want_lse=False)
        os_, ls_ = [], []
        for gi, (win, dil) in enumerate(B_GROUPS):
            sub = lambda a: a.reshape(b, s // dil, dil * b_w)
            o, lse = _band_attn(sub(qb[gi]), sub(kb[gi]), sub(vb[gi]), bias_b[gi], None, n_kv=B_HEADS, grp=1,
                                hw=win // (2 * dil), tq=TQ_ATTN, block=BLOCK_ATTN_B, reps=dil,
                                want_lse=True)
            os_.append(o.reshape(t // dil, dil * b_w))
            ls_.append(lse.reshape(t // dil, dil * b_w))

        rwt = router_w[layer].T
        rwt_hi = rwt.astype(BF16)
        rwt_split = jnp.stack([rwt_hi, (rwt - rwt_hi.astype(F32)).astype(BF16)])
        x1, xg, meta, segs_out = _post(
            x2d, ya.reshape(t, a_q_w), os_, ls_, dils, ga, gb,
            w_branch_a[layer].astype(BF16), w_branch_b[layer].astype(BF16), w_out[layer].astype(BF16),
            norm_ffn[layer][None], rwt_split, router_b[layer][:, None])

        segs3 = segs_out.reshape(nt, N_EXPERTS, LANES)
        te, nxt, n_active, src, dst = _expert_tiles(segs3[:, :, 0].astype(I32), segs3[:, :, 1].astype(I32), n_tiles)
        yg = _experts(xg, te, nxt, n_active, src, dst, w_gate[layer], b_gate[layer][:, None], w_up[layer],
                      b_up[layer][:, None], w_down[layer], b_down[layer][:, None])
        x2d = _combine(x1, yg, meta, norm_final[None], layer == depth - 1)
    return x2d.reshape(b, s, d)
```

```python
import functools

import numpy as np
import jax
import jax.numpy as jnp
from jax import lax
from jax.experimental import pallas as pl
from jax.experimental.pallas import tpu as pltpu

F32 = jnp.float32
BF16 = jnp.bfloat16
I32 = jnp.int32

HEAD_DIM = 64
A_KV_HEADS = 4
A_GROUP = 4
A_HALF_WINDOW = 128
B_GROUPS = ((128, 1), (512, 4), (2048, 16))
B_HEADS = 4
N_BUCKETS = 32
MAX_DISTANCE = 1024
N_EXPERTS = 32
TOP_K = 4
SWIGLU_LIMIT = 7.0
SWIGLU_ALPHA = 1.702
EPS = 1e-5
NEG_INF = -1e30
LOG2E = 1.4426950408889634

V7X_VMEM_LIMIT_BYTES = 56 * 1024 * 1024
LANES = 128
BF16_SUBLANES = 16

TM_PROJ = 1024
TQ_ATTN = 128
BLOCK_ATTN_A = 512
BLOCK_ATTN_B = 512
TM_POST = 512
TM_EXPERT = 512

CHUNK = BF16_SUBLANES
CHUNKS_PER_TILE = TM_EXPERT // CHUNK
SLOTS = -(-(TM_POST * TOP_K + N_EXPERTS * (CHUNK - 1)) // LANES) * LANES
SLACK_ROWS = SLOTS - TM_POST * TOP_K
assert (2 * TM_EXPERT) % SLACK_ROWS == 0
PERM_BLOCK = 512
assert SLOTS % PERM_BLOCK == 0


def _t5_buckets(rel):
    half = N_BUCKETS // 2
    max_exact = half // 2
    ret = np.where(rel > 0, half, 0)
    n = np.abs(rel)
    large = max_exact + (np.log(np.maximum(n, 1) / max_exact)
                         / np.log(MAX_DISTANCE / max_exact) * (half - max_exact)).astype(np.int32)
    large = np.minimum(large, half - 1)
    return (ret + np.where(n < max_exact, n, large)).astype(np.int32)


def _rms(x, g):
    return x * lax.rsqrt(jnp.mean(x * x, axis=-1, keepdims=True) + EPS) * g


def _inproj_kernel(x_ref, g_ref, w_ref, b_ref, *refs, segs):
    out_refs, scr = refs[:-1], refs[-1]
    h = _rms(x_ref[...], g_ref[...]).astype(BF16)
    for ref, (c0, width, kind, dil) in zip(out_refs, segs):
        acc = jnp.dot(h, w_ref[:, c0:c0 + width], preferred_element_type=F32) + b_ref[:, c0:c0 + width]
        if kind == "q":
            acc = acc * (HEAD_DIM ** -0.5 * LOG2E)
        elif kind == "gate":
            acc = jax.nn.sigmoid(acc)
        if dil == 1:
            ref[...] = acc.astype(ref.dtype)
        else:
            n = acc.shape[0] // dil
            for c in range(width // LANES):
                scr[c] = acc[:, c * LANES:(c + 1) * LANES]
            for r in range(dil):
                for c in range(width // LANES):
                    col = r * width + c * LANES
                    ref[:, col:col + LANES] = scr[c, pl.ds(r, n, stride=dil), :].astype(ref.dtype)


def _inproj(x2d, g, w_bf16, b, segs):
    t, d = x2d.shape
    n = w_bf16.shape[1]
    max_w = max(w for _, w, _, dil in segs if dil > 1)
    return pl.pallas_call(
        functools.partial(_inproj_kernel, segs=segs),
        grid=(t // TM_PROJ,),
        in_specs=[
            pl.BlockSpec((TM_PROJ, d), lambda i: (i, 0)),
            pl.BlockSpec((1, d), lambda i: (0, 0)),
            pl.BlockSpec((d, n), lambda i: (0, 0), pipeline_mode=pl.Buffered(1)),
            pl.BlockSpec((1, n), lambda i: (0, 0)),
        ],
        out_specs=[pl.BlockSpec((TM_PROJ // dil, dil * w), lambda i: (i, 0)) for _, w, _, dil in segs],
        out_shape=[jax.ShapeDtypeStruct((t // dil, dil * w), BF16) for _, w, _, dil in segs],
        scratch_shapes=[pltpu.VMEM((max_w // LANES, TM_PROJ, LANES), F32)],
        compiler_params=pltpu.CompilerParams(
            dimension_semantics=("parallel",), vmem_limit_bytes=V7X_VMEM_LIMIT_BYTES),
        name="inproj",
    )(x2d, g, w_bf16, b)


def _band_bias(table, hw, tq, dil):
    tk = tq + 2 * hw
    off = np.arange(tk)[None, :] - hw - np.arange(tq)[:, None]
    band = np.abs(off) <= hw
    col = np.arange(tk)[None, :]
    masks = np.stack([band & (col >= hw), band, band & (col < hw + tq)])
    onehot = (_t5_buckets(off * dil)[..., None] == np.arange(N_BUCKETS)).astype(np.float32)
    bias = jnp.einsum("qkn,nh->hqk", jnp.asarray(onehot), table.astype(F32),
                      precision=lax.Precision.HIGHEST)
    return jnp.where(masks[:, None], bias[None] * LOG2E, NEG_INF)


def _band_attn_kernel(*refs, n_kv, grp, tq, hw, has_sink, want_lse):
    q_ref, kp, kc, kn, vp, vc, vn, bias_ref = refs[:8]
    rest = refs[8:]
    sink_ref = None
    if has_sink:
        sink_ref, rest = rest[0], rest[1:]
    o_ref = rest[0]
    lse_ref = rest[1] if want_lse else None

    j, nt = pl.program_id(2), pl.num_programs(2)
    k = jnp.concatenate([kp[0], kc[0], kn[0]], axis=0)
    v = jnp.concatenate([vp[0], vc[0], vn[0]], axis=0)
    n_sub = q_ref.shape[1] // tq
    tk = tq + 2 * hw
    ones = jnp.ones((tk, HEAD_DIM), BF16)
    sinks = []
    if has_sink:
        row = lax.broadcasted_iota(I32, (grp * tq, 1), 0)
        for h in range(n_kv):
            sk = jnp.full((grp * tq, 1), sink_ref[h * grp] * LOG2E, F32)
            for g in range(1, grp):
                sk = jnp.where(row >= g * tq, sink_ref[h * grp + g] * LOG2E, sk)
            sinks.append(sk)
    for sub in range(n_sub):
        var = 1
        if sub == 0:
            var = jnp.where(j == 0, 0, var)
        if sub == n_sub - 1:
            var = jnp.where(j == nt - 1, 2, var)
        r0 = sub * tq
        for h in range(n_kv):
            k_h = k[r0:r0 + tk, h * HEAD_DIM:(h + 1) * HEAD_DIM]
            v_h = jnp.concatenate([v[r0:r0 + tk, h * HEAD_DIM:(h + 1) * HEAD_DIM], ones], axis=1)
            c0 = h * grp * HEAD_DIM
            q = jnp.concatenate([q_ref[0, r0:r0 + tq, c0 + g * HEAD_DIM:c0 + (g + 1) * HEAD_DIM]
                                 for g in range(grp)], axis=0)
            s = lax.dot_general(q, k_h, (((1,), (1,)), ((), ())), preferred_element_type=F32)
            s = s + bias_ref[var, h]
            m = jnp.max(s, axis=-1, keepdims=True)
            if has_sink:
                sk = sinks[h]
                m = jnp.maximum(m, sk)
            p = jnp.exp2((s - m).astype(BF16))
            ol = jnp.dot(p, v_h, preferred_element_type=F32)
            l = ol[:, HEAD_DIM:HEAD_DIM + 1]
            if has_sink:
                l = l + jnp.exp2(sk - m)
            o = ol[:, :HEAD_DIM] / l
            if want_lse:
                lse = jnp.broadcast_to(m + jnp.log2(l), o.shape)
            for g in range(grp):
                c = c0 + g * HEAD_DIM
                o_ref[0, r0:r0 + tq, c:c + HEAD_DIM] = o[g * tq:(g + 1) * tq].astype(o_ref.dtype)
                if want_lse:
                    lse_ref[0, r0:r0 + tq, c:c + HEAD_DIM] = lse[g * tq:(g + 1) * tq]


def _band_attn_heads_kernel(q_ref, kp, kc, kn, vp, vc, vn, bias_ref, o_ref, lse_ref, *, n_heads, tq, hw):
    j, nt = pl.program_id(2), pl.num_programs(2)
    k = jnp.concatenate([kp[0], kc[0], kn[0]], axis=0)
    v = jnp.concatenate([vp[0], vc[0], vn[0]], axis=0)
    n_sub = q_ref.shape[1] // tq
    tk = tq + 2 * hw
    width = n_heads * HEAD_DIM
    lane_head = lax.broadcasted_iota(I32, (tq, width), 1) // HEAD_DIM
    ones = jnp.ones((tk, LANES), BF16)
    for sub in range(n_sub):
        var = 1
        if sub == 0:
            var = jnp.where(j == 0, 0, var)
        if sub == n_sub - 1:
            var = jnp.where(j == nt - 1, 2, var)
        r0 = sub * tq
        q = q_ref[0, r0:r0 + tq, :]
        q_bd = jnp.concatenate([jnp.where(lane_head == h, q, jnp.zeros_like(q)) for h in range(n_heads)], axis=0)
        s = lax.dot_general(q_bd, k[r0:r0 + tk], (((1,), (1,)), ((), ())), preferred_element_type=F32)
        s = s + bias_ref[var, 0]
        m = jnp.max(s, axis=-1, keepdims=True)
        p = jnp.exp2((s - m).astype(BF16))
        o_full = jnp.dot(p, v[r0:r0 + tk], preferred_element_type=F32)
        l = jnp.dot(p, ones, preferred_element_type=F32)[:, :1]
        o_sel = jnp.zeros((tq, width), F32)
        m_sel = jnp.zeros((tq, width), F32)
        l_sel = jnp.ones((tq, width), F32)
        for h in range(n_heads):
            hit = lane_head == h
            o_sel = jnp.where(hit, o_full[h * tq:(h + 1) * tq], o_sel)
            m_sel = jnp.where(hit, m[h * tq:(h + 1) * tq], m_sel)
            l_sel = jnp.where(hit, l[h * tq:(h + 1) * tq], l_sel)
        o_ref[0, r0:r0 + tq, :] = (o_sel / l_sel).astype(o_ref.dtype)
        lse_ref[0, r0:r0 + tq, :] = m_sel + jnp.log2(l_sel)


def _band_attn(q, k, v, bias3, sinks, *, n_kv, grp, hw, tq, block, reps, want_lse):
    b, l, _ = q.shape
    hq = n_kv * grp
    qw, kw = hq * HEAD_DIM, n_kv * HEAD_DIM
    nt = l // block
    ratio = block // hw
    nhw = l // hw
    tk = tq + 2 * hw
    assert nt * (block // tq) >= 2

    prev = lambda bi, r, j: (bi, jnp.maximum(j * ratio - 1, 0), r)
    cur = lambda bi, r, j: (bi, j, r)
    nxt = lambda bi, r, j: (bi, jnp.minimum((j + 1) * ratio, nhw - 1), r)
    in_specs = [
        pl.BlockSpec((1, block, qw), cur),
        pl.BlockSpec((1, hw, kw), prev), pl.BlockSpec((1, block, kw), cur), pl.BlockSpec((1, hw, kw), nxt),
        pl.BlockSpec((1, hw, kw), prev), pl.BlockSpec((1, block, kw), cur), pl.BlockSpec((1, hw, kw), nxt),
    ]
    stacked = grp == 1
    bias_shape = (3, 1, hq * tq, tk) if stacked else (3, n_kv, grp * tq, tk)
    in_specs.append(pl.BlockSpec(bias_shape, lambda bi, r, j: (0, 0, 0, 0), pipeline_mode=pl.Buffered(1)))
    args = [q, k, k, k, v, v, v, bias3.reshape(bias_shape)]
    if stacked:
        assert sinks is None and want_lse
        body = functools.partial(_band_attn_heads_kernel, n_heads=n_kv, tq=tq, hw=hw)
    else:
        body = functools.partial(_band_attn_kernel, n_kv=n_kv, grp=grp, tq=tq, hw=hw,
                                 has_sink=sinks is not None, want_lse=want_lse)
    if sinks is not None:
        in_specs.append(pl.BlockSpec(memory_space=pltpu.SMEM))
        args.append(sinks)
    out_specs = [pl.BlockSpec((1, block, qw), cur)]
    out_shape = [jax.ShapeDtypeStruct((b, l, reps * qw), BF16)]
    if want_lse:
        out_specs.append(pl.BlockSpec((1, block, qw), cur))
        out_shape.append(jax.ShapeDtypeStruct((b, l, reps * qw), F32))
    return pl.pallas_call(
        body,
        grid=(b, reps, nt),
        in_specs=in_specs,
        out_specs=out_specs,
        out_shape=out_shape,
        compiler_params=pltpu.CompilerParams(
            dimension_semantics=("parallel", "parallel", "parallel"),
            vmem_limit_bytes=V7X_VMEM_LIMIT_BYTES),
        name="band_attn_a" if sinks is not None else "band_attn_b",
    )(*args)


def _token_order(ref, scr, dil):
    if dil == 1:
        return ref[...].astype(F32)
    n = ref.shape[0]
    width = ref.shape[1] // dil
    for r in range(dil):
        for c in range(width // LANES):
            col = r * width + c * LANES
            scr[c, pl.ds(r, n, stride=dil), :] = ref[:, col:col + LANES].astype(F32)
    return jnp.concatenate([scr[c] for c in range(width // LANES)], axis=1)


def _post_kernel(x_ref, ya_ref, o1, o2, o3, l1, l2, l3, ga_ref, gb_ref, wa_ref, wb_ref, wo_ref,
                 gf_ref, rwt_ref, rb_ref, tri_ref, x1_ref, xg_ref, meta_ref, segs_ref, *scrs, dils):
    os_ = [_token_order(r, scrs[2 * i], dil) for i, (r, dil) in enumerate(zip((o1, o2, o3), dils))]
    ls = [_token_order(r, scrs[2 * i + 1], dil) for i, (r, dil) in enumerate(zip((l1, l2, l3), dils))]
    mx = jnp.maximum(jnp.maximum(ls[0], ls[1]), ls[2])
    es = [jnp.exp2(l - mx) for l in ls]
    den = es[0] + es[1] + es[2]
    yb = (es[0] * os_[0] + es[1] * os_[1] + es[2] * os_[2]) / den
    za = jnp.dot(ya_ref[...], wa_ref[...], preferred_element_type=F32)
    zb = jnp.dot(yb.astype(BF16), wb_ref[...], preferred_element_type=F32)
    merged = ga_ref[...].astype(F32) * za + gb_ref[...].astype(F32) * zb
    x1 = x_ref[...] + jnp.dot(merged.astype(BF16), wo_ref[...], preferred_element_type=F32)
    x1_ref[...] = x1
    h2 = _rms(x1, gf_ref[...])

    h_hi = h2.astype(BF16)
    h_lo = (h2 - h_hi.astype(F32)).astype(BF16)
    nt_dot = lambda a, b_: lax.dot_general(a, b_, (((1,), (1,)), ((), ())), preferred_element_type=F32)
    n_e = rwt_ref.shape[1]
    both = nt_dot(rwt_ref[...].reshape(2 * n_e, -1), h_hi)
    logits = (both[:n_e] + nt_dot(rwt_ref[0], h_lo) + both[n_e:]) + rb_ref[...]
    n_e, tm = logits.shape
    iota_e = lax.broadcasted_iota(I32, (n_e, tm), 0)
    work = logits
    vals, hots = [], []
    for _ in range(TOP_K):
        mk = jnp.max(work, axis=0, keepdims=True)
        ik = jnp.min(jnp.where(work == mk, iota_e, n_e), axis=0, keepdims=True)
        hot = iota_e == ik
        vals.append(mk)
        hots.append(hot)
        work = jnp.where(hot, -jnp.inf, work)
    exps = [jnp.exp(vk - vals[0]) for vk in vals]
    tot = exps[0] + exps[1] + exps[2] + exps[3]
    probs = [ek / tot for ek in exps]

    sel = jnp.zeros((n_e, tm), F32)
    for hot in hots:
        sel = sel + jnp.where(hot, 1.0, 0.0)
    cnt = jnp.sum(sel, axis=1, keepdims=True)
    pcnt = jnp.floor((cnt + (CHUNK - 1)) / CHUNK) * CHUNK
    r_i = lax.broadcasted_iota(I32, (n_e, n_e), 0)
    c_i = lax.broadcasted_iota(I32, (n_e, n_e), 1)
    pcnt_row = jnp.sum(jnp.where(r_i == c_i, pcnt, 0.0), axis=0, keepdims=True)
    seg_off = jnp.sum(jnp.where(c_i < r_i, pcnt_row, 0.0), axis=1, keepdims=True)
    before = jnp.dot(sel.astype(BF16), tri_ref[...], preferred_element_type=F32)
    slot_of = seg_off + before
    slots = [jnp.sum(jnp.where(hot, slot_of, 0.0), axis=0, keepdims=True) for hot in hots]

    iota_s = lax.broadcasted_iota(I32, (PERM_BLOCK, tm), 0)
    slots_i = [sk.astype(I32) for sk in slots]
    for blk in range(SLOTS // PERM_BLOCK):
        perm = jnp.zeros((PERM_BLOCK, tm), F32)
        for sk in slots_i:
            perm = jnp.where(iota_s == sk - blk * PERM_BLOCK, 1.0, perm)
        xg_ref[blk * PERM_BLOCK:(blk + 1) * PERM_BLOCK, :] = jnp.dot(
            perm.astype(BF16), h_hi, preferred_element_type=F32).astype(BF16)

    rows = slots + probs
    meta_t = jnp.concatenate(rows + [jnp.zeros((LANES - len(rows), tm), F32)], axis=0)
    meta_ref[...] = meta_t.T
    lane = lax.broadcasted_iota(I32, (n_e, LANES), 1)
    segs_ref[...] = jnp.where(lane == 0, pcnt, jnp.where(lane == 1, seg_off, 0.0))


def _post(x2d, ya, os_, ls_, dils, ga, gb, wa, wb, wo, gf, rwt, rb):
    t, d = x2d.shape
    bw = os_[0].shape[1] // dils[0]
    tm = TM_POST
    nt = t // tm
    tri = jnp.asarray(np.triu(np.ones((tm, tm), np.float32), k=1), dtype=BF16)
    row = lambda w, dil=1: pl.BlockSpec((tm // dil, dil * w), lambda i: (i, 0))
    full = lambda a: pl.BlockSpec(a.shape, lambda i: (0,) * a.ndim)
    return pl.pallas_call(
        functools.partial(_post_kernel, dils=dils),
        grid=(nt,),
        in_specs=[row(d), row(d)] + [row(bw, dil) for dil in dils] + [row(bw, dil) for dil in dils]
                 + [row(d), row(d), full(wa), full(wb), full(wo), full(gf), full(rwt), full(rb), full(tri)],
        scratch_shapes=[pltpu.VMEM((bw // LANES, tm, LANES), F32) for _ in range(2 * len(dils))],
        out_specs=[row(d), pl.BlockSpec((SLOTS, d), lambda i: (i, 0)), row(LANES),
                   pl.BlockSpec((N_EXPERTS, LANES), lambda i: (i, 0))],
        out_shape=[jax.ShapeDtypeStruct((t, d), F32), jax.ShapeDtypeStruct((nt * SLOTS, d), BF16),
                   jax.ShapeDtypeStruct((t, LANES), F32), jax.ShapeDtypeStruct((nt * N_EXPERTS, LANES), F32)],
        compiler_params=pltpu.CompilerParams(
            dimension_semantics=("parallel",), vmem_limit_bytes=V7X_VMEM_LIMIT_BYTES),
        name="post_attn",
    )(x2d, ya, *os_, *ls_, ga, gb, wa, wb, wo, gf, rwt, rb, tri)


def _expert_tiles(pcnt, seg_off, n_tiles):
    nt, n_e = pcnt.shape
    cpt = CHUNKS_PER_TILE
    nch = (pcnt // CHUNK).T
    cum = jnp.cumsum(nch, axis=1)
    total = cum[:, -1]
    tiles_e = (total + cpt - 1) // cpt
    tile_end = jnp.cumsum(tiles_e)
    n_active = tile_end[-1]
    i = jnp.arange(n_tiles, dtype=I32)
    last = jnp.minimum(i, n_active - 1)
    te = jnp.sum((last[:, None] >= tile_end[None, :]).astype(I32), axis=1)
    hot_e = (te[:, None] == jnp.arange(n_e, dtype=I32)[None, :])
    pick = lambda tab: jnp.sum(jnp.where(hot_e[:, :, None], tab[None], 0), axis=1)
    tile_start = jnp.sum(jnp.where(hot_e, (tile_end - tiles_e)[None, :], 0), axis=1)
    total_t = jnp.sum(jnp.where(hot_e, total[None, :], 0), axis=1)
    q = (last - tile_start)[:, None] * cpt + jnp.arange(cpt, dtype=I32)[None, :]
    valid = (q < total_t[:, None]) & (i < n_active)[:, None]
    cum_t, nch_t = pick(cum), pick(nch)
    chunk0_t = pick((seg_off.T + jnp.arange(nt, dtype=I32)[None, :] * SLOTS) // CHUNK)
    jj = jnp.sum((q[:, :, None] >= cum_t[:, None, :]).astype(I32), axis=2)
    hot_j = jj[:, :, None] == jnp.arange(nt, dtype=I32)[None, None, :]
    first = jnp.sum(jnp.where(hot_j, (cum_t - nch_t)[:, None, :], 0), axis=2)
    base = jnp.sum(jnp.where(hot_j, chunk0_t[:, None, :], 0), axis=2)
    src = jnp.where(valid, base + q - first, 0)
    trash = nt * SLOTS // CHUNK + (i % 2)[:, None] * cpt + jnp.arange(cpt, dtype=I32)[None, :]
    dst = jnp.where(valid, src, trash)
    group_end = jnp.sum(jnp.where(hot_e, tile_end[None, :], 0), axis=1)
    nxt = jnp.sum((group_end[:, None] >= tile_end[None, :]).astype(I32), axis=1)
    nxt = jnp.where(group_end < n_active, nxt, -1)
    return (te.astype(I32), nxt.astype(I32), n_active.astype(I32)[None],
            src.reshape(-1).astype(I32), dst.reshape(-1).astype(I32))


def _expert_kernel(te_ref, nx_ref, na_ref, cs_ref, cd_ref, xg_hbm, wg_hbm, bg_ref, wu_hbm, bu_ref, wd_hbm,
                   bd_ref, yg_hbm, xbuf, ybuf, wst, wbf, zbuf, sem_in, sem_out, sem_zero, sem_w, *,
                   n_token_tiles):
    i = pl.program_id(0)
    n_active = na_ref[0]
    slot = i % 2
    cpt = CHUNKS_PER_TILE
    w_hbm = (wg_hbm, wu_hbm, wd_hbm)

    def w_copy(n, e):
        return pltpu.make_async_copy(w_hbm[n].at[e], wst.at[n], sem_w.at[n])

    zero_starts = [j * SLOTS + TM_POST * TOP_K for j in range(n_token_tiles)]
    zero_starts += [n_token_tiles * SLOTS + j * SLACK_ROWS for j in range(2 * TM_EXPERT // SLACK_ROWS)]

    def zero_copy(n):
        return pltpu.make_async_copy(
            zbuf, yg_hbm.at[pl.ds(zero_starts[n] // CHUNK, SLACK_ROWS // CHUNK)], sem_zero.at[n])

    @pl.when(i == 0)
    def _():
        zbuf[...] = jnp.zeros_like(zbuf)
        for n in range(len(zero_starts)):
            zero_copy(n).start()

    def in_copy(c, sl, chunk):
        return pltpu.make_async_copy(xg_hbm.at[chunk], xbuf.at[sl, c], sem_in.at[sl, c])

    def out_copy(c, sl, chunk):
        return pltpu.make_async_copy(ybuf.at[sl, c], yg_hbm.at[chunk], sem_out.at[sl, c])

    def start_gather(tile, sl):
        for c in range(cpt):
            in_copy(c, sl, cs_ref[tile * cpt + c]).start()

    def wait_scatter(sl):
        for c in range(cpt):
            out_copy(c, sl, 0).wait()

    @pl.when(i == 0)
    def _():
        start_gather(0, 0)
        for n in range(len(w_hbm)):
            w_copy(n, te_ref[0]).start()

    @pl.when(i + 1 < n_active)
    def _():
        start_gather(i + 1, 1 - slot)

    @pl.when(i < n_active)
    def _():
        for c in range(cpt):
            in_copy(c, slot, 0).wait()

        @pl.when((i == 0) | (te_ref[i] != te_ref[jnp.maximum(i - 1, 0)]))
        def _():
            for n in range(len(w_hbm)):
                w_copy(n, 0).wait()
                wbf[n] = wst[n].astype(BF16)

                @pl.when(nx_ref[i] >= 0)
                def _():
                    w_copy(n, nx_ref[i]).start()

        @pl.when(i >= 1)
        def _():
            for c in range(cpt):
                out_copy(c, 1 - slot, cd_ref[(i - 1) * cpt + c]).start()

        x = xbuf[slot].reshape(TM_EXPERT, -1)
        e = te_ref[i]
        gate = jnp.minimum(jnp.dot(x, wbf[0], preferred_element_type=F32) + bg_ref[e], SWIGLU_LIMIT)
        up = jnp.clip(jnp.dot(x, wbf[1], preferred_element_type=F32) + bu_ref[e], -SWIGLU_LIMIT, SWIGLU_LIMIT)
        act = gate * jax.nn.sigmoid(SWIGLU_ALPHA * gate) * (up + 1.0)
        y = jnp.dot(act.astype(BF16), wbf[2], preferred_element_type=F32) + bd_ref[e]

        @pl.when(i >= 2)
        def _():
            wait_scatter(slot)

        ybuf[slot] = y.astype(BF16).reshape(ybuf.shape[1:])

        @pl.when(i == 0)
        def _():
            for n in range(len(zero_starts)):
                zero_copy(n).wait()

        @pl.when(i == n_active - 1)
        def _():
            for c in range(cpt):
                out_copy(c, slot, cd_ref[i * cpt + c]).start()
            wait_scatter(slot)

            @pl.when(i >= 1)
            def _():
                wait_scatter(1 - slot)


def _experts(xg, te, nxt, n_active, src, dst, wg, bg, wu, bu, wd, bd):
    rows, d = xg.shape
    n_e, _, dff = wg.shape
    assert d == dff
    n_tiles = te.shape[0]
    n_token_tiles = rows // SLOTS
    n_zero = n_token_tiles + 2 * TM_EXPERT // SLACK_ROWS
    b_spec = pl.BlockSpec((n_e, 1, d), lambda i, *_: (0, 0, 0))
    hbm = pl.BlockSpec(memory_space=pl.ANY)
    tile_bufs = pltpu.VMEM((2, CHUNKS_PER_TILE, CHUNK, d), BF16)
    yg = pl.pallas_call(
        functools.partial(_expert_kernel, n_token_tiles=n_token_tiles),
        grid_spec=pltpu.PrefetchScalarGridSpec(
            num_scalar_prefetch=5,
            grid=(n_tiles,),
            in_specs=[hbm, hbm, b_spec, hbm, b_spec, hbm, b_spec],
            out_specs=hbm,
            scratch_shapes=[tile_bufs, tile_bufs,
                            pltpu.VMEM((3, d, dff), F32), pltpu.VMEM((3, d, dff), BF16),
                            pltpu.VMEM((SLACK_ROWS // CHUNK, CHUNK, d), BF16),
                            pltpu.SemaphoreType.DMA((2, CHUNKS_PER_TILE)),
                            pltpu.SemaphoreType.DMA((2, CHUNKS_PER_TILE)),
                            pltpu.SemaphoreType.DMA((n_zero,)),
                            pltpu.SemaphoreType.DMA((3,))]),
        out_shape=jax.ShapeDtypeStruct(((rows + 2 * TM_EXPERT) // CHUNK, CHUNK, d), BF16),
        compiler_params=pltpu.CompilerParams(
            dimension_semantics=("arbitrary",), vmem_limit_bytes=V7X_VMEM_LIMIT_BYTES),
        name="experts",
    )(te, nxt, n_active, src, dst, xg.reshape(rows // CHUNK, CHUNK, d), wg, bg, wu, bu, wd, bd)
    return yg.reshape(rows + 2 * TM_EXPERT, d)


def _combine_kernel(x1_ref, yg_ref, meta_ref, gn_ref, o_ref, *, final_norm):
    meta = meta_ref[...]
    tm = meta.shape[0]
    slots_i = [meta[:, k:k + 1].astype(I32) for k in range(TOP_K)]
    probs = [meta[:, TOP_K + k:TOP_K + k + 1] for k in range(TOP_K)]
    iota_s = lax.broadcasted_iota(I32, (tm, PERM_BLOCK), 1)
    xo = x1_ref[...]
    for blk in range(SLOTS // PERM_BLOCK):
        base = blk * PERM_BLOCK
        wperm = jnp.zeros((tm, PERM_BLOCK), F32)
        for sk, pk in zip(slots_i, probs):
            wperm = jnp.where(iota_s == sk - base, pk, wperm)
        xo = xo + jnp.dot(wperm.astype(BF16), yg_ref[base:base + PERM_BLOCK, :], preferred_element_type=F32)
    o_ref[...] = _rms(xo, gn_ref[...]) if final_norm else xo


def _combine(x1, yg, meta, gn, final_norm):
    t, d = x1.shape
    tm = TM_POST
    return pl.pallas_call(
        functools.partial(_combine_kernel, final_norm=final_norm),
        grid=(t // tm,),
        in_specs=[pl.BlockSpec((tm, d), lambda i: (i, 0)),
                  pl.BlockSpec((SLOTS, d), lambda i: (i, 0)),
                  pl.BlockSpec((tm, LANES), lambda i: (i, 0)),
                  pl.BlockSpec((1, d), lambda i: (0, 0))],
        out_specs=pl.BlockSpec((tm, d), lambda i: (i, 0)),
        out_shape=jax.ShapeDtypeStruct((t, d), F32),
        compiler_params=pltpu.CompilerParams(
            dimension_semantics=("parallel",), vmem_limit_bytes=V7X_VMEM_LIMIT_BYTES),
        name="combine",
    )(x1, yg, meta, gn)


def kernel(x, norm_mix, w_in, b_in, sinks, rel_bias, w_branch_a, w_branch_b, w_out, norm_ffn,
           router_w, router_b, w_gate, b_gate, w_up, b_up, w_down, b_down, norm_final):
    b, s, d = x.shape
    t = b * s
    depth = w_in.shape[0]
    a_q_w = A_KV_HEADS * A_GROUP * HEAD_DIM
    a_kv_w = A_KV_HEADS * HEAD_DIM
    b_w = B_HEADS * HEAD_DIM
    n_grp = len(B_GROUPS)
    dils = tuple(dil for _, dil in B_GROUPS)
    segs, col = [], 0
    for kind, width in (("q", a_q_w), ("k", a_kv_w), ("v", a_kv_w)):
        segs.append((col, width, kind, 1))
        col += width
    for kind in ("q", "k", "v"):
        for dil in dils:
            segs.append((col, b_w, kind, dil))
            col += b_w
    for _ in range(2):
        segs.append((col, d, "gate", 1))
        col += d
    segs = tuple(segs)
    n_a = A_KV_HEADS * A_GROUP
    nt = t // TM_POST
    max_chunks = nt * ((TM_POST * TOP_K + N_EXPERTS * (CHUNK - 1)) // CHUNK)
    n_tiles = -(-(max_chunks + N_EXPERTS * (CHUNKS_PER_TILE - 1)) // CHUNKS_PER_TILE)

    bias_a = _band_bias(rel_bias[:, :n_a], A_HALF_WINDOW, TQ_ATTN, 1)
    bias_b = [_band_bias(rel_bias[:, n_a + gi * B_HEADS:n_a + (gi + 1) * B_HEADS],
                         win // (2 * dil), TQ_ATTN, dil) for gi, (win, dil) in enumerate(B_GROUPS)]

    x2d = x.reshape(t, d)
    for layer in range(depth):
        proj = _inproj(x2d, norm_mix[layer][None], w_in[layer].astype(BF16), b_in[layer][None], segs)
        qa, ka, va = proj[:3]
        qb, kb, vb = proj[3:3 + n_grp], proj[3 + n_grp:3 + 2 * n_grp], proj[3 + 2 * n_grp:3 + 3 * n_grp]
        ga, gb = proj[-2:]

        (ya,) = _band_attn(qa.reshape(b, s, a_q_w), ka.reshape(b, s, a_kv_w), va.reshape(b, s, a_kv_w),
                           bias_a, sinks[layer], n_kv=A_KV_HEADS, grp=A_GROUP, hw=A_HALF_WINDOW,
                           tq=TQ_ATTN, block=BLOCK_ATTN_A, reps=1, want_lse=False)
        os_, ls_ = [], []
        for gi, (win, dil) in enumerate(B_GROUPS):
            sub = lambda a: a.reshape(b, s // dil, dil * b_w)
            o, lse = _band_attn(sub(qb[gi]), sub(kb[gi]), sub(vb[gi]), bias_b[gi], None, n_kv=B_HEADS, grp=1,
                                hw=win // (2 * dil), tq=TQ_ATTN, block=BLOCK_ATTN_B, reps=dil,
                                want_lse=True)
            os_.append(o.reshape(t // dil, dil * b_w))
            ls_.append(lse.reshape(t // dil, dil * b_w))

        rwt = router_w[layer].T
        rwt_hi = rwt.astype(BF16)
        rwt_split = jnp.stack([rwt_hi, (rwt - rwt_hi.astype(F32)).astype(BF16)])
        x1, xg, meta, segs_out = _post(
            x2d, ya.reshape(t, a_q_w), os_, ls_, dils, ga, gb,
            w_branch_a[layer].astype(BF16), w_branch_b[layer].astype(BF16), w_out[layer].astype(BF16),
            norm_ffn[layer][None], rwt_split, router_b[layer][:, None])

        segs3 = segs_out.reshape(nt, N_EXPERTS, LANES)
        te, nxt, n_active, src, dst = _expert_tiles(segs3[:, :, 0].astype(I32), segs3[:, :, 1].astype(I32), n_tiles)
        yg = _experts(xg, te, nxt, n_active, src, dst, w_gate[layer], b_gate[layer][:, None], w_up[layer],
                      b_up[layer][:, None], w_down[layer], b_down[layer][:, None])
        x2d = _combine(x1, yg, meta, norm_final[None], layer == depth - 1)
    return x2d.reshape(b, s, d)
```

```python
import functools

import numpy as np
import jax
import jax.numpy as jnp
from jax import lax
from jax.experimental import pallas as pl
from jax.experimental.pallas import tpu as pltpu

F32 = jnp.float32
BF16 = jnp.bfloat16
I32 = jnp.int32

HEAD_DIM = 64
A_KV_HEADS = 4
A_GROUP = 4
A_HALF_WINDOW = 128
B_GROUPS = ((128, 1), (512, 4), (2048, 16))
B_HEADS = 4
N_BUCKETS = 32
MAX_DISTANCE = 1024
N_EXPERTS = 32
TOP_K = 4
SWIGLU_LIMIT = 7.0
SWIGLU_ALPHA = 1.702
EPS = 1e-5
NEG_INF = -1e30
LOG2E = 1.4426950408889634

V7X_VMEM_LIMIT_BYTES = 56 * 1024 * 1024
LANES = 128
BF16_SUBLANES = 16

TM_PROJ = 1024
TQ_ATTN = 128
BLOCK_ATTN_A = 512
BLOCK_ATTN_B = 512
TM_POST = 512
TM_EXPERT = 512

CHUNK = BF16_SUBLANES
CHUNKS_PER_TILE = TM_EXPERT // CHUNK
SLOTS = -(-(TM_POST * TOP_K + N_EXPERTS * (CHUNK - 1)) // LANES) * LANES
SLACK_ROWS = SLOTS - TM_POST * TOP_K
assert (2 * TM_EXPERT) % SLACK_ROWS == 0
PERM_BLOCK = 512
assert SLOTS % PERM_BLOCK == 0


def _t5_buckets(rel):
    half = N_BUCKETS // 2
    max_exact = half // 2
    ret = np.where(rel > 0, half, 0)
    n = np.abs(rel)
    large = max_exact + (np.log(np.maximum(n, 1) / max_exact)
                         / np.log(MAX_DISTANCE / max_exact) * (half - max_exact)).astype(np.int32)
    large = np.minimum(large, half - 1)
    return (ret + np.where(n < max_exact, n, large)).astype(np.int32)


def _rms(x, g):
    return x * lax.rsqrt(jnp.mean(x * x, axis=-1, keepdims=True) + EPS) * g


def _inproj_kernel(x_ref, g_ref, w_ref, b_ref, *refs, segs):
    out_refs, scr = refs[:-1], refs[-1]
    h = _rms(x_ref[...], g_ref[...]).astype(BF16)
    for ref, (c0, width, kind, dil) in zip(out_refs, segs):
        acc = jnp.dot(h, w_ref[:, c0:c0 + width], preferred_element_type=F32) + b_ref[:, c0:c0 + width]
        if kind == "q":
            acc = acc * (HEAD_DIM ** -0.5 * LOG2E)
        elif kind == "gate":
            acc = jax.nn.sigmoid(acc)
        if dil == 1:
            ref[...] = acc.astype(ref.dtype)
        else:
            n = acc.shape[0] // dil
            for c in range(width // LANES):
                scr[c] = acc[:, c * LANES:(c + 1) * LANES]
            for r in range(dil):
                for c in range(width // LANES):
                    col = r * width + c * LANES
                    ref[:, col:col + LANES] = scr[c, pl.ds(r, n, stride=dil), :].astype(ref.dtype)


def _inproj(x2d, g, w_bf16, b, segs):
    t, d = x2d.shape
    n = w_bf16.shape[1]
    max_w = max(w for _, w, _, dil in segs if dil > 1)
    return pl.pallas_call(
        functools.partial(_inproj_kernel, segs=segs),
        grid=(t // TM_PROJ,),
        in_specs=[
            pl.BlockSpec((TM_PROJ, d), lambda i: (i, 0)),
            pl.BlockSpec((1, d), lambda i: (0, 0)),
            pl.BlockSpec((d, n), lambda i: (0, 0), pipeline_mode=pl.Buffered(1)),
            pl.BlockSpec((1, n), lambda i: (0, 0)),
        ],
        out_specs=[pl.BlockSpec((TM_PROJ // dil, dil * w), lambda i: (i, 0)) for _, w, _, dil in segs],
        out_shape=[jax.ShapeDtypeStruct((t // dil, dil * w), BF16) for _, w, _, dil in segs],
        scratch_shapes=[pltpu.VMEM((max_w // LANES, TM_PROJ, LANES), F32)],
        compiler_params=pltpu.CompilerParams(
            dimension_semantics=("parallel",), vmem_limit_bytes=V7X_VMEM_LIMIT_BYTES),
        name="inproj",
    )(x2d, g, w_bf16, b)


def _band_bias(table, hw, tq, dil):
    tk = tq + 2 * hw
    off = np.arange(tk)[None, :] - hw - np.arange(tq)[:, None]
    band = np.abs(off) <= hw
    col = np.arange(tk)[None, :]
    masks = np.stack([band & (col >= hw), band, band & (col < hw + tq)])
    onehot = (_t5_buckets(off * dil)[..., None] == np.arange(N_BUCKETS)).astype(np.float32)
    bias = jnp.einsum("qkn,nh->hqk", jnp.asarray(onehot), table.astype(F32),
                      precision=lax.Precision.HIGHEST)
    return jnp.where(masks[:, None], bias[None] * LOG2E, NEG_INF)


def _band_attn_kernel(*refs, n_kv, grp, tq, hw, has_sink, want_lse):
    q_ref, kp, kc, kn, vp, vc, vn, bias_ref = refs[:8]
    rest = refs[8:]
    sink_ref = None
    if has_sink:
        sink_ref, rest = rest[0], rest[1:]
    o_ref = rest[0]
    lse_ref = rest[1] if want_lse else None

    j, nt = pl.program_id(2), pl.num_programs(2)
    k = jnp.concatenate([kp[0], kc[0], kn[0]], axis=0)
    v = jnp.concatenate([vp[0], vc[0], vn[0]], axis=0)
    n_sub = q_ref.shape[1] // tq
    tk = tq + 2 * hw
    ones = jnp.ones((tk, HEAD_DIM), BF16)
    sinks = []
    if has_sink:
        row = lax.broadcasted_iota(I32, (grp * tq, 1), 0)
        for h in range(n_kv):
            sk = jnp.full((grp * tq, 1), sink_ref[h * grp] * LOG2E, F32)
            for g in range(1, grp):
                sk = jnp.where(row >= g * tq, sink_ref[h * grp + g] * LOG2E, sk)
            sinks.append(sk)
    for sub in range(n_sub):
        var = 1
        if sub == 0:
            var = jnp.where(j == 0, 0, var)
        if sub == n_sub - 1:
            var = jnp.where(j == nt - 1, 2, var)
        r0 = sub * tq
        for h in range(n_kv):
            k_h = k[r0:r0 + tk, h * HEAD_DIM:(h + 1) * HEAD_DIM]
            v_h = jnp.concatenate([v[r0:r0 + tk, h * HEAD_DIM:(h + 1) * HEAD_DIM], ones], axis=1)
            c0 = h * grp * HEAD_DIM
            q = jnp.concatenate([q_ref[0, r0:r0 + tq, c0 + g * HEAD_DIM:c0 + (g + 1) * HEAD_DIM]
                                 for g in range(grp)], axis=0)
            s = lax.dot_general(q, k_h, (((1,), (1,)), ((), ())), preferred_element_type=F32)
            s = s + bias_ref[var, h]
            m = jnp.max(s, axis=-1, keepdims=True)
            if has_sink:
                sk = sinks[h]
                m = jnp.maximum(m, sk)
            p = jnp.exp2((s - m).astype(BF16))
            ol = jnp.dot(p, v_h, preferred_element_type=F32)
            l = ol[:, HEAD_DIM:HEAD_DIM + 1]
            if has_sink:
                l = l + jnp.exp2(sk - m)
            o = ol[:, :HEAD_DIM] / l
            if want_lse:
                lse = jnp.broadcast_to(m + jnp.log2(l), o.shape)
            for g in range(grp):
                c = c0 + g * HEAD_DIM
                o_ref[0, r0:r0 + tq, c:c + HEAD_DIM] = o[g * tq:(g + 1) * tq].astype(o_ref.dtype)
                if want_lse:
                    lse_ref[0, r0:r0 + tq, c:c + HEAD_DIM] = lse[g * tq:(g + 1) * tq]


def _band_attn_heads_kernel(q_ref, kp, kc, kn, vp, vc, vn, bias_ref, o_ref, lse_ref, *, n_heads, tq, hw):
    j, nt = pl.program_id(2), pl.num_programs(2)
    k = jnp.concatenate([kp[0], kc[0], kn[0]], axis=0)
    v = jnp.concatenate([vp[0], vc[0], vn[0]], axis=0)
    n_sub = q_ref.shape[1] // tq
    tk = tq + 2 * hw
    width = n_heads * HEAD_DIM
    lane_head = lax.broadcasted_iota(I32, (tq, width), 1) // HEAD_DIM
    ones = jnp.ones((tk, LANES), BF16)
    for sub in range(n_sub):
        var = 1
        if sub == 0:
            var = jnp.where(j == 0, 0, var)
        if sub == n_sub - 1:
            var = jnp.where(j == nt - 1, 2, var)
        r0 = sub * tq
        q = q_ref[0, r0:r0 + tq, :]
        q_bd = jnp.concatenate([jnp.where(lane_head == h, q, jnp.zeros_like(q)) for h in range(n_heads)], axis=0)
        s = lax.dot_general(q_bd, k[r0:r0 + tk], (((1,), (1,)), ((), ())), preferred_element_type=F32)
        s = s + bias_ref[var, 0]
        m = jnp.max(s, axis=-1, keepdims=True)
        p = jnp.exp2((s - m).astype(BF16))
        o_full = jnp.dot(p, v[r0:r0 + tk], preferred_element_type=F32)
        l = jnp.dot(p, ones, preferred_element_type=F32)[:, :1]
        o_sel = jnp.zeros((tq, width), F32)
        m_sel = jnp.zeros((tq, width), F32)
        l_sel = jnp.ones((tq, width), F32)
        for h in range(n_heads):
            hit = lane_head == h
            o_sel = jnp.where(hit, o_full[h * tq:(h + 1) * tq], o_sel)
            m_sel = jnp.where(hit, m[h * tq:(h + 1) * tq], m_sel)
            l_sel = jnp.where(hit, l[h * tq:(h + 1) * tq], l_sel)
        o_ref[0, r0:r0 + tq, :] = (o_sel / l_sel).astype(o_ref.dtype)
        lse_ref[0, r0:r0 + tq, :] = m_sel + jnp.log2(l_sel)


def _band_attn(q, k, v, bias3, sinks, *, n_kv, grp, hw, tq, block, reps, want_lse):
    b, l, _ = q.shape
    hq = n_kv * grp
    qw, kw = hq * HEAD_DIM, n_kv * HEAD_DIM
    nt = l // block
    ratio = block // hw
    nhw = l // hw
    tk = tq + 2 * hw
    assert nt * (block // tq) >= 2

    prev = lambda bi, r, j: (bi, jnp.maximum(j * ratio - 1, 0), r)
    cur = lambda bi, r, j: (bi, j, r)
    nxt = lambda bi, r, j: (bi, jnp.minimum((j + 1) * ratio, nhw - 1), r)
    in_specs = [
        pl.BlockSpec((1, block, qw), cur),
        pl.BlockSpec((1, hw, kw), prev), pl.BlockSpec((1, block, kw), cur), pl.BlockSpec((1, hw, kw), nxt),
        pl.BlockSpec((1, hw, kw), prev), pl.BlockSpec((1, block, kw), cur), pl.BlockSpec((1, hw, kw), nxt),
    ]
    stacked = grp == 1
    bias_shape = (3, 1, hq * tq, tk) if stacked else (3, n_kv, grp * tq, tk)
    in_specs.append(pl.BlockSpec(bias_shape, lambda bi, r, j: (0, 0, 0, 0), pipeline_mode=pl.Buffered(1)))
    args = [q, k, k, k, v, v, v, bias3.reshape(bias_shape)]
    if stacked:
        assert sinks is None and want_lse
        body = functools.partial(_band_attn_heads_kernel, n_heads=n_kv, tq=tq, hw=hw)
    else:
        body = functools.partial(_band_attn_kernel, n_kv=n_kv, grp=grp, tq=tq, hw=hw,
                                 has_sink=sinks is not None, want_lse=want_lse)
    if sinks is not None:
        in_specs.append(pl.BlockSpec(memory_space=pltpu.SMEM))
        args.append(sinks)
    out_specs = [pl.BlockSpec((1, block, qw), cur)]
    out_shape = [jax.ShapeDtypeStruct((b, l, reps * qw), BF16)]
    if want_lse:
        out_specs.append(pl.BlockSpec((1, block, qw), cur))
        out_shape.append(jax.ShapeDtypeStruct((b, l, reps * qw), F32))
    return pl.pallas_call(
        body,
        grid=(b, reps, nt),
        in_specs=in_specs,
        out_specs=out_specs,
        out_shape=out_shape,
        compiler_params=pltpu.CompilerParams(
            dimension_semantics=("parallel", "parallel", "parallel"),
            vmem_limit_bytes=V7X_VMEM_LIMIT_BYTES),
        name="band_attn_a" if sinks is not None else "band_attn_b",
    )(*args)


def _token_order(ref, scr, dil):
    if dil == 1:
        return ref[...].astype(F32)
    n = ref.shape[0]
    width = ref.shape[1] // dil
    for r in range(dil):
        for c in range(width // LANES):
            col = r * width + c * LANES
            scr[c, pl.ds(r, n, stride=dil), :] = ref[:, col:col + LANES].astype(F32)
    return jnp.concatenate([scr[c] for c in range(width // LANES)], axis=1)


def _post_kernel(x_ref, ya_ref, o1, o2, o3, l1, l2, l3, ga_ref, gb_ref, wa_ref, wb_ref, wo_ref,
                 gf_ref, rwt_ref, rb_ref, tri_ref, x1_ref, xg_ref, meta_ref, segs_ref, *scrs, dils):
    os_ = [_token_order(r, scrs[2 * i], dil) for i, (r, dil) in enumerate(zip((o1, o2, o3), dils))]
    ls = [_token_order(r, scrs[2 * i + 1], dil) for i, (r, dil) in enumerate(zip((l1, l2, l3), dils))]
    mx = jnp.maximum(jnp.maximum(ls[0], ls[1]), ls[2])
    es = [jnp.exp2(l - mx) for l in ls]
    den = es[0] + es[1] + es[2]
    yb = (es[0] * os_[0] + es[1] * os_[1] + es[2] * os_[2]) / den
    za = jnp.dot(ya_ref[...], wa_ref[...], preferred_element_type=F32)
    zb = jnp.dot(yb.astype(BF16), wb_ref[...], preferred_element_type=F32)
    merged = ga_ref[...].astype(F32) * za + gb_ref[...].astype(F32) * zb
    x1 = x_ref[...] + jnp.dot(merged.astype(BF16), wo_ref[...], preferred_element_type=F32)
    x1_ref[...] = x1
    h2 = _rms(x1, gf_ref[...])

    h_hi = h2.astype(BF16)
    h_lo = (h2 - h_hi.astype(F32)).astype(BF16)
    nt_dot = lambda a, b_: lax.dot_general(a, b_, (((1,), (1,)), ((), ())), preferred_element_type=F32)
    n_e = rwt_ref.shape[1]
    both = nt_dot(rwt_ref[...].reshape(2 * n_e, -1), h_hi)
    logits = (both[:n_e] + nt_dot(rwt_ref[0], h_lo) + both[n_e:]) + rb_ref[...]
    n_e, tm = logits.shape
    iota_e = lax.broadcasted_iota(I32, (n_e, tm), 0)
    work = logits
    vals, hots = [], []
    for _ in range(TOP_K):
        mk = jnp.max(work, axis=0, keepdims=True)
        ik = jnp.min(jnp.where(work == mk, iota_e, n_e), axis=0, keepdims=True)
        hot = iota_e == ik
        vals.append(mk)
        hots.append(hot)
        work = jnp.where(hot, -jnp.inf, work)
    exps = [jnp.exp(vk - vals[0]) for vk in vals]
    tot = exps[0] + exps[1] + exps[2] + exps[3]
    probs = [ek / tot for ek in exps]

    sel = jnp.zeros((n_e, tm), F32)
    for hot in hots:
        sel = sel + jnp.where(hot, 1.0, 0.0)
    cnt = jnp.sum(sel, axis=1, keepdims=True)
    pcnt = jnp.floor((cnt + (CHUNK - 1)) / CHUNK) * CHUNK
    r_i = lax.broadcasted_iota(I32, (n_e, n_e), 0)
    c_i = lax.broadcasted_iota(I32, (n_e, n_e), 1)
    pcnt_row = jnp.sum(jnp.where(r_i == c_i, pcnt, 0.0), axis=0, keepdims=True)
    seg_off = jnp.sum(jnp.where(c_i < r_i, pcnt_row, 0.0), axis=1, keepdims=True)
    before = jnp.dot(sel.astype(BF16), tri_ref[...], preferred_element_type=F32)
    slot_of = seg_off + before
    slots = [jnp.sum(jnp.where(hot, slot_of, 0.0), axis=0, keepdims=True) for hot in hots]

    iota_s = lax.broadcasted_iota(I32, (PERM_BLOCK, tm), 0)
    slots_i = [sk.astype(I32) for sk in slots]
    for blk in range(SLOTS // PERM_BLOCK):
        perm = jnp.zeros((PERM_BLOCK, tm), F32)
        for sk in slots_i:
            perm = jnp.where(iota_s == sk - blk * PERM_BLOCK, 1.0, perm)
        xg_ref[blk * PERM_BLOCK:(blk + 1) * PERM_BLOCK, :] = jnp.dot(
            perm.astype(BF16), h_hi, preferred_element_type=F32).astype(BF16)

    rows = slots + probs
    meta_t = jnp.concatenate(rows + [jnp.zeros((LANES - len(rows), tm), F32)], axis=0)
    meta_ref[...] = meta_t.T
    lane = lax.broadcasted_iota(I32, (n_e, LANES), 1)
    segs_ref[...] = jnp.where(lane == 0, pcnt, jnp.where(lane == 1, seg_off, 0.0))


def _post(x2d, ya, os_, ls_, dils, ga, gb, wa, wb, wo, gf, rwt, rb):
    t, d = x2d.shape
    bw = os_[0].shape[1] // dils[0]
    tm = TM_POST
    nt = t // tm
    tri = jnp.asarray(np.triu(np.ones((tm, tm), np.float32), k=1), dtype=BF16)
    row = lambda w, dil=1: pl.BlockSpec((tm // dil, dil * w), lambda i: (i, 0))
    full = lambda a: pl.BlockSpec(a.shape, lambda i: (0,) * a.ndim)
    return pl.pallas_call(
        functools.partial(_post_kernel, dils=dils),
        grid=(nt,),
        in_specs=[row(d), row(d)] + [row(bw, dil) for dil in dils] + [row(bw, dil) for dil in dils]
                 + [row(d), row(d), full(wa), full(wb), full(wo), full(gf), full(rwt), full(rb), full(tri)],
        scratch_shapes=[pltpu.VMEM((bw // LANES, tm, LANES), F32) for _ in range(2 * len(dils))],
        out_specs=[row(d), pl.BlockSpec((SLOTS, d), lambda i: (i, 0)), row(LANES),
                   pl.BlockSpec((N_EXPERTS, LANES), lambda i: (i, 0))],
        out_shape=[jax.ShapeDtypeStruct((t, d), F32), jax.ShapeDtypeStruct((nt * SLOTS, d), BF16),
                   jax.ShapeDtypeStruct((t, LANES), F32), jax.ShapeDtypeStruct((nt * N_EXPERTS, LANES), F32)],
        compiler_params=pltpu.CompilerParams(
            dimension_semantics=("parallel",), vmem_limit_bytes=V7X_VMEM_LIMIT_BYTES),
        name="post_attn",
    )(x2d, ya, *os_, *ls_, ga, gb, wa, wb, wo, gf, rwt, rb, tri)


def _expert_tiles(pcnt, seg_off, n_tiles):
    nt, n_e = pcnt.shape
    cpt = CHUNKS_PER_TILE
    nch = (pcnt // CHUNK).T
    cum = jnp.cumsum(nch, axis=1)
    total = cum[:, -1]
    tiles_e = (total + cpt - 1) // cpt
    tile_end = jnp.cumsum(tiles_e)
    n_active = tile_end[-1]
    i = jnp.arange(n_tiles, dtype=I32)
    last = jnp.minimum(i, n_active - 1)
    te = jnp.sum((last[:, None] >= tile_end[None, :]).astype(I32), axis=1)
    hot_e = (te[:, None] == jnp.arange(n_e, dtype=I32)[None, :])
    pick = lambda tab: jnp.sum(jnp.where(hot_e[:, :, None], tab[None], 0), axis=1)
    tile_start = jnp.sum(jnp.where(hot_e, (tile_end - tiles_e)[None, :], 0), axis=1)
    total_t = jnp.sum(jnp.where(hot_e, total[None, :], 0), axis=1)
    q = (last - tile_start)[:, None] * cpt + jnp.arange(cpt, dtype=I32)[None, :]
    valid = (q < total_t[:, None]) & (i < n_active)[:, None]
    cum_t, nch_t = pick(cum), pick(nch)
    chunk0_t = pick((seg_off.T + jnp.arange(nt, dtype=I32)[None, :] * SLOTS) // CHUNK)
    jj = jnp.sum((q[:, :, None] >= cum_t[:, None, :]).astype(I32), axis=2)
    hot_j = jj[:, :, None] == jnp.arange(nt, dtype=I32)[None, None, :]
    first = jnp.sum(jnp.where(hot_j, (cum_t - nch_t)[:, None, :], 0), axis=2)
    base = jnp.sum(jnp.where(hot_j, chunk0_t[:, None, :], 0), axis=2)
    src = jnp.where(valid, base + q - first, 0)
    trash = nt * SLOTS // CHUNK + (i % 2)[:, None] * cpt + jnp.arange(cpt, dtype=I32)[None, :]
    dst = jnp.where(valid, src, trash)
    group_end = jnp.sum(jnp.where(hot_e, tile_end[None, :], 0), axis=1)
    nxt = jnp.sum((group_end[:, None] >= tile_end[None, :]).astype(I32), axis=1)
    nxt = jnp.where(group_end < n_active, nxt, -1)
    return (te.astype(I32), nxt.astype(I32), n_active.astype(I32)[None],
            src.reshape(-1).astype(I32), dst.reshape(-1).astype(I32))


def _expert_kernel(te_ref, nx_ref, na_ref, cs_ref, cd_ref, xg_hbm, wg_hbm, bg_ref, wu_hbm, bu_ref, wd_hbm,
                   bd_ref, yg_hbm, xbuf, ybuf, wst, wbf, zbuf, sem_in, sem_out, sem_zero, sem_w, *,
                   n_token_tiles):
    i = pl.program_id(0)
    n_active = na_ref[0]
    slot = i % 2
    cpt = CHUNKS_PER_TILE
    w_hbm = (wg_hbm, wu_hbm, wd_hbm)

    def w_copy(n, e):
        return pltpu.make_async_copy(w_hbm[n].at[e], wst.at[n], sem_w.at[n])

    zero_starts = [j * SLOTS + TM_POST * TOP_K for j in range(n_token_tiles)]
    zero_starts += [n_token_tiles * SLOTS + j * SLACK_ROWS for j in range(2 * TM_EXPERT // SLACK_ROWS)]

    def zero_copy(n):
        return pltpu.make_async_copy(
            zbuf, yg_hbm.at[pl.ds(zero_starts[n] // CHUNK, SLACK_ROWS // CHUNK)], sem_zero.at[n])

    @pl.when(i == 0)
    def _():
        zbuf[...] = jnp.zeros_like(zbuf)
        for n in range(len(zero_starts)):
            zero_copy(n).start()

    def in_copy(c, sl, chunk):
        return pltpu.make_async_copy(xg_hbm.at[chunk], xbuf.at[sl, c], sem_in.at[sl])

    def out_copy(c, sl, chunk):
        return pltpu.make_async_copy(ybuf.at[sl, c], yg_hbm.at[chunk], sem_out.at[sl])

    def start_gather(tile, sl):
        for c in range(cpt):
            in_copy(c, sl, cs_ref[tile * cpt + c]).start()

    def wait_gather(sl):
        pltpu.make_async_copy(xg_hbm.at[pl.ds(0, cpt)], xbuf.at[sl], sem_in.at[sl]).wait()

    def wait_scatter(sl):
        pltpu.make_async_copy(ybuf.at[sl], yg_hbm.at[pl.ds(0, cpt)], sem_out.at[sl]).wait()

    @pl.when(i == 0)
    def _():
        start_gather(0, 0)
        for n in range(len(w_hbm)):
            w_copy(n, te_ref[0]).start()

    @pl.when(i + 1 < n_active)
    def _():
        start_gather(i + 1, 1 - slot)

    @pl.when(i < n_active)
    def _():
        wait_gather(slot)

        @pl.when((i == 0) | (te_ref[i] != te_ref[jnp.maximum(i - 1, 0)]))
        def _():
            for n in range(len(w_hbm)):
                w_copy(n, 0).wait()
                wbf[n] = wst[n].astype(BF16)

                @pl.when(nx_ref[i] >= 0)
                def _():
                    w_copy(n, nx_ref[i]).start()

        @pl.when(i >= 1)
        def _():
            for c in range(cpt):
                out_copy(c, 1 - slot, cd_ref[(i - 1) * cpt + c]).start()

        x = xbuf[slot].reshape(TM_EXPERT, -1)
        e = te_ref[i]
        gate = jnp.minimum(jnp.dot(x, wbf[0], preferred_element_type=F32) + bg_ref[e], SWIGLU_LIMIT)
        up = jnp.clip(jnp.dot(x, wbf[1], preferred_element_type=F32) + bu_ref[e], -SWIGLU_LIMIT, SWIGLU_LIMIT)
        act = gate * jax.nn.sigmoid(SWIGLU_ALPHA * gate) * (up + 1.0)
        y = jnp.dot(act.astype(BF16), wbf[2], preferred_element_type=F32) + bd_ref[e]

        @pl.when(i >= 2)
        def _():
            wait_scatter(slot)

        ybuf[slot] = y.astype(BF16).reshape(ybuf.shape[1:])

        @pl.when(i == 0)
        def _():
            for n in range(len(zero_starts)):
                zero_copy(n).wait()

        @pl.when(i == n_active - 1)
        def _():
            for c in range(cpt):
                out_copy(c, slot, cd_ref[i * cpt + c]).start()
            wait_scatter(slot)

            @pl.when(i >= 1)
            def _():
                wait_scatter(1 - slot)


def _experts(xg, te, nxt, n_active, src, dst, wg, bg, wu, bu, wd, bd):
    rows, d = xg.shape
    n_e, _, dff = wg.shape
    assert d == dff
    n_tiles = te.shape[0]
    n_token_tiles = rows // SLOTS
    n_zero = n_token_tiles + 2 * TM_EXPERT // SLACK_ROWS
    b_spec = pl.BlockSpec((n_e, 1, d), lambda i, *_: (0, 0, 0))
    hbm = pl.BlockSpec(memory_space=pl.ANY)
    tile_bufs = pltpu.VMEM((2, CHUNKS_PER_TILE, CHUNK, d), BF16)
    yg = pl.pallas_call(
        functools.partial(_expert_kernel, n_token_tiles=n_token_tiles),
        grid_spec=pltpu.PrefetchScalarGridSpec(
            num_scalar_prefetch=5,
            grid=(n_tiles,),
            in_specs=[hbm, hbm, b_spec, hbm, b_spec, hbm, b_spec],
            out_specs=hbm,
            scratch_shapes=[tile_bufs, tile_bufs,
                            pltpu.VMEM((3, d, dff), F32), pltpu.VMEM((3, d, dff), BF16),
                            pltpu.VMEM((SLACK_ROWS // CHUNK, CHUNK, d), BF16),
                            pltpu.SemaphoreType.DMA((2,)),
                            pltpu.SemaphoreType.DMA((2,)),
                            pltpu.SemaphoreType.DMA((n_zero,)),
                            pltpu.SemaphoreType.DMA((3,))]),
        out_shape=jax.ShapeDtypeStruct(((rows + 2 * TM_EXPERT) // CHUNK, CHUNK, d), BF16),
        compiler_params=pltpu.CompilerParams(
            dimension_semantics=("arbitrary",), vmem_limit_bytes=V7X_VMEM_LIMIT_BYTES),
        name="experts",
    )(te, nxt, n_active, src, dst, xg.reshape(rows // CHUNK, CHUNK, d), wg, bg, wu, bu, wd, bd)
    return yg.reshape(rows + 2 * TM_EXPERT, d)


def _combine_kernel(x1_ref, yg_ref, meta_ref, gn_ref, o_ref, *, final_norm):
    meta = meta_ref[...]
    tm = meta.shape[0]
    slots_i = [meta[:, k:k + 1].astype(I32) for k in range(TOP_K)]
    probs = [meta[:, TOP_K + k:TOP_K + k + 1] for k in range(TOP_K)]
    iota_s = lax.broadcasted_iota(I32, (tm, PERM_BLOCK), 1)
    xo = x1_ref[...]
    for blk in range(SLOTS // PERM_BLOCK):
        base = blk * PERM_BLOCK
        wperm = jnp.zeros((tm, PERM_BLOCK), F32)
        for sk, pk in zip(slots_i, probs):
            wperm = jnp.where(iota_s == sk - base, pk, wperm)
        xo = xo + jnp.dot(wperm.astype(BF16), yg_ref[base:base + PERM_BLOCK, :], preferred_element_type=F32)
    o_ref[...] = _rms(xo, gn_ref[...]) if final_norm else xo


def _combine(x1, yg, meta, gn, final_norm):
    t, d = x1.shape
    tm = TM_POST
    return pl.pallas_call(
        functools.partial(_combine_kernel, final_norm=final_norm),
        grid=(t // tm,),
        in_specs=[pl.BlockSpec((tm, d), lambda i: (i, 0)),
                  pl.BlockSpec((SLOTS, d), lambda i: (i, 0)),
                  pl.BlockSpec((tm, LANES), lambda i: (i, 0)),
                  pl.BlockSpec((1, d), lambda i: (0, 0))],
        out_specs=pl.BlockSpec((tm, d), lambda i: (i, 0)),
        out_shape=jax.ShapeDtypeStruct((t, d), F32),
        compiler_params=pltpu.CompilerParams(
            dimension_semantics=("parallel",), vmem_limit_bytes=V7X_VMEM_LIMIT_BYTES),
        name="combine",
    )(x1, yg, meta, gn)


def kernel(x, norm_mix, w_in, b_in, sinks, rel_bias, w_branch_a, w_branch_b, w_out, norm_ffn,
           router_w, router_b, w_gate, b_gate, w_up, b_up, w_down, b_down, norm_final):
    b, s, d = x.shape
    t = b * s
    depth = w_in.shape[0]
    a_q_w = A_KV_HEADS * A_GROUP * HEAD_DIM
    a_kv_w = A_KV_HEADS * HEAD_DIM
    b_w = B_HEADS * HEAD_DIM
    n_grp = len(B_GROUPS)
    dils = tuple(dil for _, dil in B_GROUPS)
    segs, col = [], 0
    for kind, width in (("q", a_q_w), ("k", a_kv_w), ("v", a_kv_w)):
        segs.append((col, width, kind, 1))
        col += width
    for kind in ("q", "k", "v"):
        for dil in dils:
            segs.append((col, b_w, kind, dil))
            col += b_w
    for _ in range(2):
        segs.append((col, d, "gate", 1))
        col += d
    segs = tuple(segs)
    n_a = A_KV_HEADS * A_GROUP
    nt = t // TM_POST
    max_chunks = nt * ((TM_POST * TOP_K + N_EXPERTS * (CHUNK - 1)) // CHUNK)
    n_tiles = -(-(max_chunks + N_EXPERTS * (CHUNKS_PER_TILE - 1)) // CHUNKS_PER_TILE)

    bias_a = _band_bias(rel_bias[:, :n_a], A_HALF_WINDOW, TQ_ATTN, 1)
    bias_b = [_band_bias(rel_bias[:, n_a + gi * B_HEADS:n_a + (gi + 1) * B_HEADS],
                         win // (2 * dil), TQ_ATTN, dil) for gi, (win, dil) in enumerate(B_GROUPS)]

    x2d = x.reshape(t, d)
    for layer in range(depth):
        proj = _inproj(x2d, norm_mix[layer][None], w_in[layer].astype(BF16), b_in[layer][None], segs)
        qa, ka, va = proj[:3]
        qb, kb, vb = proj[3:3 + n_grp], proj[3 + n_grp:3 + 2 * n_grp], proj[3 + 2 * n_grp:3 + 3 * n_grp]
        ga, gb = proj[-2:]

        (ya,) = _band_attn(qa.reshape(b, s, a_q_w), ka.reshape(b, s, a_kv_w), va.reshape(b, s, a_kv_w),
                           bias_a, sinks[layer], n_kv=A_KV_HEADS, grp=A_GROUP, hw=A_HALF_WINDOW,
                           tq=TQ_ATTN, block=BLOCK_ATTN_A, reps=1, want_lse=False)
        os_, ls_ = [], []
        for gi, (win, dil) in enumerate(B_GROUPS):
            sub = lambda a: a.reshape(b, s // dil, dil * b_w)
            o, lse = _band_attn(sub(qb[gi]), sub(kb[gi]), sub(vb[gi]), bias_b[gi], None, n_kv=B_HEADS, grp=1,
                                hw=win // (2 * dil), tq=TQ_ATTN, block=BLOCK_ATTN_B, reps=dil,
                                want_lse=True)
            os_.append(o.reshape(t // dil, dil * b_w))
            ls_.append(lse.reshape(t // dil, dil * b_w))

        rwt = router_w[layer].T
        rwt_hi = rwt.astype(BF16)
        rwt_split = jnp.stack([rwt_hi, (rwt - rwt_hi.astype(F32)).astype(BF16)])
        x1, xg, meta, segs_out = _post(
            x2d, ya.reshape(t, a_q_w), os_, ls_, dils, ga, gb,
            w_branch_a[layer].astype(BF16), w_branch_b[layer].astype(BF16), w_out[layer].astype(BF16),
            norm_ffn[layer][None], rwt_split, router_b[layer][:, None])

        segs3 = segs_out.reshape(nt, N_EXPERTS, LANES)
        te, nxt, n_active, src, dst = _expert_tiles(segs3[:, :, 0].astype(I32), segs3[:, :, 1].astype(I32), n_tiles)
        yg = _experts(xg, te, nxt, n_active, src, dst, w_gate[layer], b_gate[layer][:, None], w_up[layer],
                      b_up[layer][:, None], w_down[layer], b_down[layer][:, None])
        x2d = _combine(x1, yg, meta, norm_final[None], layer == depth - 1)
    return x2d.reshape(b, s, d)
```

```python
import functools

import numpy as np
import jax
import jax.numpy as jnp
from jax import lax
from jax.experimental import pallas as pl
from jax.experimental.pallas import tpu as pltpu

F32 = jnp.float32
BF16 = jnp.bfloat16
I32 = jnp.int32

HEAD_DIM = 64
A_KV_HEADS = 4
A_GROUP = 4
A_HALF_WINDOW = 128
B_GROUPS = ((128, 1), (512, 4), (2048, 16))
B_HEADS = 4
N_BUCKETS = 32
MAX_DISTANCE = 1024
N_EXPERTS = 32
TOP_K = 4
SWIGLU_LIMIT = 7.0
SWIGLU_ALPHA = 1.702
EPS = 1e-5
NEG_INF = -1e30
LOG2E = 1.4426950408889634

V7X_VMEM_LIMIT_BYTES = 56 * 1024 * 1024
LANES = 128
BF16_SUBLANES = 16

TM_PROJ = 1024
TQ_ATTN = 128
BLOCK_ATTN_A = 512
BLOCK_ATTN_B = 512
TM_POST = 512
TM_EXPERT = 512

CHUNK = BF16_SUBLANES
CHUNKS_PER_TILE = TM_EXPERT // CHUNK
SLOTS = -(-(TM_POST * TOP_K + N_EXPERTS * (CHUNK - 1)) // LANES) * LANES
SLACK_ROWS = SLOTS - TM_POST * TOP_K
assert (2 * TM_EXPERT) % SLACK_ROWS == 0
PERM_BLOCK = 512
assert SLOTS % PERM_BLOCK == 0


def _t5_buckets(rel):
    half = N_BUCKETS // 2
    max_exact = half // 2
    ret = np.where(rel > 0, half, 0)
    n = np.abs(rel)
    large = max_exact + (np.log(np.maximum(n, 1) / max_exact)
                         / np.log(MAX_DISTANCE / max_exact) * (half - max_exact)).astype(np.int32)
    large = np.minimum(large, half - 1)
    return (ret + np.where(n < max_exact, n, large)).astype(np.int32)


def _rms(x, g):
    return x * lax.rsqrt(jnp.mean(x * x, axis=-1, keepdims=True) + EPS) * g


def _inproj_kernel(x_ref, g_ref, w_ref, b_ref, *refs, segs):
    out_refs, scr = refs[:-1], refs[-1]
    h = _rms(x_ref[...], g_ref[...]).astype(BF16)
    for ref, (c0, width, kind, dil) in zip(out_refs, segs):
        acc = jnp.dot(h, w_ref[:, c0:c0 + width], preferred_element_type=F32) + b_ref[:, c0:c0 + width]
        if kind == "q":
            acc = acc * (HEAD_DIM ** -0.5 * LOG2E)
        elif kind == "gate":
            acc = jax.nn.sigmoid(acc)
        if kind == "vT":
            ref[...] = acc.T.astype(ref.dtype)
        elif dil == 1:
            ref[...] = acc.astype(ref.dtype)
        else:
            n = acc.shape[0] // dil
            for c in range(width // LANES):
                scr[c] = acc[:, c * LANES:(c + 1) * LANES]
            for r in range(dil):
                for c in range(width // LANES):
                    col = r * width + c * LANES
                    ref[:, col:col + LANES] = scr[c, pl.ds(r, n, stride=dil), :].astype(ref.dtype)


def _inproj(x2d, g, w_bf16, b, segs):
    t, d = x2d.shape
    n = w_bf16.shape[1]
    max_w = max(w for _, w, _, dil in segs if dil > 1)
    out_specs, out_shape = [], []
    for _, w, kind, dil in segs:
        if kind == "vT":
            out_specs.append(pl.BlockSpec((w, TM_PROJ), lambda i: (0, i)))
            out_shape.append(jax.ShapeDtypeStruct((w, t), BF16))
        else:
            out_specs.append(pl.BlockSpec((TM_PROJ // dil, dil * w), lambda i: (i, 0)))
            out_shape.append(jax.ShapeDtypeStruct((t // dil, dil * w), BF16))
    return pl.pallas_call(
        functools.partial(_inproj_kernel, segs=segs),
        grid=(t // TM_PROJ,),
        in_specs=[
            pl.BlockSpec((TM_PROJ, d), lambda i: (i, 0)),
            pl.BlockSpec((1, d), lambda i: (0, 0)),
            pl.BlockSpec((d, n), lambda i: (0, 0), pipeline_mode=pl.Buffered(1)),
            pl.BlockSpec((1, n), lambda i: (0, 0)),
        ],
        out_specs=out_specs,
        out_shape=out_shape,
        scratch_shapes=[pltpu.VMEM((max_w // LANES, TM_PROJ, LANES), F32)],
        compiler_params=pltpu.CompilerParams(
            dimension_semantics=("parallel",), vmem_limit_bytes=V7X_VMEM_LIMIT_BYTES),
        name="inproj",
    )(x2d, g, w_bf16, b)


def _band_bias(table, hw, tq, dil):
    tk = tq + 2 * hw
    off = np.arange(tk)[None, :] - hw - np.arange(tq)[:, None]
    band = np.abs(off) <= hw
    col = np.arange(tk)[None, :]
    masks = np.stack([band & (col >= hw), band, band & (col < hw + tq)])
    onehot = (_t5_buckets(off * dil)[..., None] == np.arange(N_BUCKETS)).astype(np.float32)
    bias = jnp.einsum("qkn,nh->hqk", jnp.asarray(onehot), table.astype(F32),
                      precision=lax.Precision.HIGHEST)
    return jnp.where(masks[:, None], bias[None] * LOG2E, NEG_INF)


def _band_attn_gqa_kernel(q_ref, kp, kc, kn, vtp, vtc, vtn, bias_ref, sink_ref, o_ref, *, n_kv, grp, tq, hw):
    j, nt = pl.program_id(1), pl.num_programs(1)
    k = jnp.concatenate([kp[0], kc[0], kn[0]], axis=0)
    vt = jnp.concatenate([vtp[...], vtc[...], vtn[...]], axis=1)
    n_sub = q_ref.shape[1] // tq
    tk = tq + 2 * hw
    ones = jnp.ones((BF16_SUBLANES, tk), BF16)
    lane = lax.broadcasted_iota(I32, (1, grp * tq), 1)
    sinks = []
    for h in range(n_kv):
        sk = jnp.full((1, grp * tq), sink_ref[h * grp] * LOG2E, F32)
        for g in range(1, grp):
            sk = jnp.where(lane >= g * tq, sink_ref[h * grp + g] * LOG2E, sk)
        sinks.append(sk)
    for sub in range(n_sub):
        var = 1
        if sub == 0:
            var = jnp.where(j == 0, 0, var)
        if sub == n_sub - 1:
            var = jnp.where(j == nt - 1, 2, var)
        r0 = sub * tq
        for h in range(n_kv):
            k_h = k[r0:r0 + tk, h * HEAD_DIM:(h + 1) * HEAD_DIM]
            c0 = h * grp * HEAD_DIM
            q = jnp.concatenate([q_ref[0, r0:r0 + tq, c0 + g * HEAD_DIM:c0 + (g + 1) * HEAD_DIM]
                                 for g in range(grp)], axis=0)
            st = lax.dot_general(k_h, q, (((1,), (1,)), ((), ())), preferred_element_type=F32)
            st = st + bias_ref[var, h]
            sk = sinks[h]
            m = jnp.maximum(jnp.max(st, axis=0, keepdims=True), sk)
            pt = jnp.exp2((st - m).astype(BF16))
            vt_h = jnp.concatenate([vt[h * HEAD_DIM:(h + 1) * HEAD_DIM, r0:r0 + tk], ones], axis=0)
            ot = jnp.dot(vt_h, pt, preferred_element_type=F32)
            l = ot[HEAD_DIM:HEAD_DIM + 1] + jnp.exp2(sk - m)
            o = ot[:HEAD_DIM] / l
            for g in range(grp):
                c = c0 + g * HEAD_DIM
                o_ref[c:c + HEAD_DIM, r0:r0 + tq] = o[:, g * tq:(g + 1) * tq].astype(o_ref.dtype)


def _band_attn_gqa(q, k, vt, bias3, sinks, *, n_kv, grp, hw, tq, block):
    b, s, _ = q.shape
    hq = n_kv * grp
    qw, kw = hq * HEAD_DIM, n_kv * HEAD_DIM
    nt = s // block
    ratio = block // hw
    nhw = s // hw
    tk = tq + 2 * hw
    assert nt * (block // tq) >= 2
    prev_j = lambda j: jnp.maximum(j * ratio - 1, 0)
    next_j = lambda j: jnp.minimum((j + 1) * ratio, nhw - 1)
    bias_t = jnp.swapaxes(bias3.reshape(3, n_kv, grp * tq, tk), 2, 3)
    return pl.pallas_call(
        functools.partial(_band_attn_gqa_kernel, n_kv=n_kv, grp=grp, tq=tq, hw=hw),
        grid=(b, nt),
        in_specs=[
            pl.BlockSpec((1, block, qw), lambda bi, j: (bi, j, 0)),
            pl.BlockSpec((1, hw, kw), lambda bi, j: (bi, prev_j(j), 0)),
            pl.BlockSpec((1, block, kw), lambda bi, j: (bi, j, 0)),
            pl.BlockSpec((1, hw, kw), lambda bi, j: (bi, next_j(j), 0)),
            pl.BlockSpec((kw, hw), lambda bi, j: (0, bi * nhw + prev_j(j))),
            pl.BlockSpec((kw, block), lambda bi, j: (0, bi * nt + j)),
            pl.BlockSpec((kw, hw), lambda bi, j: (0, bi * nhw + next_j(j))),
            pl.BlockSpec((3, n_kv, tk, grp * tq), lambda bi, j: (0, 0, 0, 0), pipeline_mode=pl.Buffered(1)),
            pl.BlockSpec(memory_space=pltpu.SMEM),
        ],
        out_specs=pl.BlockSpec((qw, block), lambda bi, j: (0, bi * nt + j)),
        out_shape=jax.ShapeDtypeStruct((qw, b * s), BF16),
        compiler_params=pltpu.CompilerParams(
            dimension_semantics=("parallel", "parallel"), vmem_limit_bytes=V7X_VMEM_LIMIT_BYTES),
        name="band_attn_a",
    )(q, k, k, k, vt, vt, vt, bias_t, sinks)


def _band_attn_heads_kernel(q_ref, kp, kc, kn, vp, vc, vn, bias_ref, o_ref, lse_ref, *, n_heads, tq, hw):
    j, nt = pl.program_id(2), pl.num_programs(2)
    k = jnp.concatenate([kp[0], kc[0], kn[0]], axis=0)
    v = jnp.concatenate([vp[0], vc[0], vn[0]], axis=0)
    n_sub = q_ref.shape[1] // tq
    tk = tq + 2 * hw
    width = n_heads * HEAD_DIM
    lane_head = lax.broadcasted_iota(I32, (tq, width), 1) // HEAD_DIM
    ones = jnp.ones((tk, LANES), BF16)
    for sub in range(n_sub):
        var = 1
        if sub == 0:
            var = jnp.where(j == 0, 0, var)
        if sub == n_sub - 1:
            var = jnp.where(j == nt - 1, 2, var)
        r0 = sub * tq
        q = q_ref[0, r0:r0 + tq, :]
        q_bd = jnp.concatenate([jnp.where(lane_head == h, q, jnp.zeros_like(q)) for h in range(n_heads)], axis=0)
        s = lax.dot_general(q_bd, k[r0:r0 + tk], (((1,), (1,)), ((), ())), preferred_element_type=F32)
        s = s + bias_ref[var, 0]
        m = jnp.max(s, axis=-1, keepdims=True)
        p = jnp.exp2((s - m).astype(BF16))
        o_full = jnp.dot(p, v[r0:r0 + tk], preferred_element_type=F32)
        l = jnp.dot(p, ones, preferred_element_type=F32)[:, :1]
        o_sel = jnp.zeros((tq, width), F32)
        m_sel = jnp.zeros((tq, width), F32)
        l_sel = jnp.ones((tq, width), F32)
        for h in range(n_heads):
            hit = lane_head == h
            o_sel = jnp.where(hit, o_full[h * tq:(h + 1) * tq], o_sel)
            m_sel = jnp.where(hit, m[h * tq:(h + 1) * tq], m_sel)
            l_sel = jnp.where(hit, l[h * tq:(h + 1) * tq], l_sel)
        o_ref[0, r0:r0 + tq, :] = (o_sel / l_sel).astype(o_ref.dtype)
        lse_ref[0, r0:r0 + tq, :] = m_sel + jnp.log2(l_sel)


def _band_attn_heads(q, k, v, bias3, *, n_heads, hw, tq, block, reps):
    b, l, _ = q.shape
    qw = kw = n_heads * HEAD_DIM
    nt = l // block
    ratio = block // hw
    nhw = l // hw
    tk = tq + 2 * hw
    assert nt * (block // tq) >= 2

    prev = lambda bi, r, j: (bi, jnp.maximum(j * ratio - 1, 0), r)
    cur = lambda bi, r, j: (bi, j, r)
    nxt = lambda bi, r, j: (bi, jnp.minimum((j + 1) * ratio, nhw - 1), r)
    in_specs = [
        pl.BlockSpec((1, block, qw), cur),
        pl.BlockSpec((1, hw, kw), prev), pl.BlockSpec((1, block, kw), cur), pl.BlockSpec((1, hw, kw), nxt),
        pl.BlockSpec((1, hw, kw), prev), pl.BlockSpec((1, block, kw), cur), pl.BlockSpec((1, hw, kw), nxt),
        pl.BlockSpec((3, 1, n_heads * tq, tk), lambda bi, r, j: (0, 0, 0, 0), pipeline_mode=pl.Buffered(1)),
    ]
    return pl.pallas_call(
        functools.partial(_band_attn_heads_kernel, n_heads=n_heads, tq=tq, hw=hw),
        grid=(b, reps, nt),
        in_specs=in_specs,
        out_specs=[pl.BlockSpec((1, block, qw), cur), pl.BlockSpec((1, block, qw), cur)],
        out_shape=[jax.ShapeDtypeStruct((b, l, reps * qw), BF16), jax.ShapeDtypeStruct((b, l, reps * qw), F32)],
        compiler_params=pltpu.CompilerParams(
            dimension_semantics=("parallel", "parallel", "parallel"),
            vmem_limit_bytes=V7X_VMEM_LIMIT_BYTES),
        name="band_attn_b",
    )(q, k, k, k, v, v, v, bias3.reshape(3, 1, n_heads * tq, tk))


def _token_order(ref, scr, dil):
    if dil == 1:
        return ref[...].astype(F32)
    n = ref.shape[0]
    width = ref.shape[1] // dil
    for r in range(dil):
        for c in range(width // LANES):
            col = r * width + c * LANES
            scr[c, pl.ds(r, n, stride=dil), :] = ref[:, col:col + LANES].astype(F32)
    return jnp.concatenate([scr[c] for c in range(width // LANES)], axis=1)


def _post_kernel(x_ref, ya_ref, o1, o2, o3, l1, l2, l3, ga_ref, gb_ref, wa_ref, wb_ref, wo_ref,
                 gf_ref, rwt_ref, rb_ref, tri_ref, x1_ref, xg_ref, meta_ref, segs_ref, *scrs, dils):
    os_ = [_token_order(r, scrs[2 * i], dil) for i, (r, dil) in enumerate(zip((o1, o2, o3), dils))]
    ls = [_token_order(r, scrs[2 * i + 1], dil) for i, (r, dil) in enumerate(zip((l1, l2, l3), dils))]
    mx = jnp.maximum(jnp.maximum(ls[0], ls[1]), ls[2])
    es = [jnp.exp2(l - mx) for l in ls]
    den = es[0] + es[1] + es[2]
    yb = (es[0] * os_[0] + es[1] * os_[1] + es[2] * os_[2]) / den
    za = lax.dot_general(ya_ref[...], wa_ref[...], (((0,), (0,)), ((), ())), preferred_element_type=F32)
    zb = jnp.dot(yb.astype(BF16), wb_ref[...], preferred_element_type=F32)
    merged = ga_ref[...].astype(F32) * za + gb_ref[...].astype(F32) * zb
    x1 = x_ref[...] + jnp.dot(merged.astype(BF16), wo_ref[...], preferred_element_type=F32)
    x1_ref[...] = x1
    h2 = _rms(x1, gf_ref[...])

    h_hi = h2.astype(BF16)
    h_lo = (h2 - h_hi.astype(F32)).astype(BF16)
    nt_dot = lambda a, b_: lax.dot_general(a, b_, (((1,), (1,)), ((), ())), preferred_element_type=F32)
    n_e = rwt_ref.shape[1]
    both = nt_dot(rwt_ref[...].reshape(2 * n_e, -1), h_hi)
    logits = (both[:n_e] + nt_dot(rwt_ref[0], h_lo) + both[n_e:]) + rb_ref[...]
    n_e, tm = logits.shape
    iota_e = lax.broadcasted_iota(I32, (n_e, tm), 0)
    work = logits
    vals, hots = [], []
    for _ in range(TOP_K):
        mk = jnp.max(work, axis=0, keepdims=True)
        ik = jnp.min(jnp.where(work == mk, iota_e, n_e), axis=0, keepdims=True)
        hot = iota_e == ik
        vals.append(mk)
        hots.append(hot)
        work = jnp.where(hot, -jnp.inf, work)
    exps = [jnp.exp(vk - vals[0]) for vk in vals]
    tot = exps[0] + exps[1] + exps[2] + exps[3]
    probs = [ek / tot for ek in exps]

    sel = jnp.zeros((n_e, tm), F32)
    for hot in hots:
        sel = sel + jnp.where(hot, 1.0, 0.0)
    cnt = jnp.sum(sel, axis=1, keepdims=True)
    pcnt = jnp.floor((cnt + (CHUNK - 1)) / CHUNK) * CHUNK
    r_i = lax.broadcasted_iota(I32, (n_e, n_e), 0)
    c_i = lax.broadcasted_iota(I32, (n_e, n_e), 1)
    pcnt_row = jnp.sum(jnp.where(r_i == c_i, pcnt, 0.0), axis=0, keepdims=True)
    seg_off = jnp.sum(jnp.where(c_i < r_i, pcnt_row, 0.0), axis=1, keepdims=True)
    before = jnp.dot(sel.astype(BF16), tri_ref[...], preferred_element_type=F32)
    slot_of = seg_off + before
    slots = [jnp.sum(jnp.where(hot, slot_of, 0.0), axis=0, keepdims=True) for hot in hots]

    iota_s = lax.broadcasted_iota(I32, (PERM_BLOCK, tm), 0)
    slots_i = [sk.astype(I32) for sk in slots]
    for blk in range(SLOTS // PERM_BLOCK):
        perm = jnp.zeros((PERM_BLOCK, tm), F32)
        for sk in slots_i:
            perm = jnp.where(iota_s == sk - blk * PERM_BLOCK, 1.0, perm)
        xg_ref[blk * PERM_BLOCK:(blk + 1) * PERM_BLOCK, :] = jnp.dot(
            perm.astype(BF16), h_hi, preferred_element_type=F32).astype(BF16)

    rows = slots + probs
    meta_t = jnp.concatenate(rows + [jnp.zeros((LANES - len(rows), tm), F32)], axis=0)
    meta_ref[...] = meta_t.T
    lane = lax.broadcasted_iota(I32, (n_e, LANES), 1)
    segs_ref[...] = jnp.where(lane == 0, pcnt, jnp.where(lane == 1, seg_off, 0.0))


def _post(x2d, ya, os_, ls_, dils, ga, gb, wa, wb, wo, gf, rwt, rb):
    t, d = x2d.shape
    bw = os_[0].shape[1] // dils[0]
    tm = TM_POST
    nt = t // tm
    tri = jnp.asarray(np.triu(np.ones((tm, tm), np.float32), k=1), dtype=BF16)
    row = lambda w, dil=1: pl.BlockSpec((tm // dil, dil * w), lambda i: (i, 0))
    full = lambda a: pl.BlockSpec(a.shape, lambda i: (0,) * a.ndim)
    return pl.pallas_call(
        functools.partial(_post_kernel, dils=dils),
        grid=(nt,),
        in_specs=[row(d), pl.BlockSpec((ya.shape[0], tm), lambda i: (0, i))]
                 + [row(bw, dil) for dil in dils] + [row(bw, dil) for dil in dils]
                 + [row(d), row(d), full(wa), full(wb), full(wo), full(gf), full(rwt), full(rb), full(tri)],
        scratch_shapes=[pltpu.VMEM((bw // LANES, tm, LANES), F32) for _ in range(2 * len(dils))],
        out_specs=[row(d), pl.BlockSpec((SLOTS, d), lambda i: (i, 0)), row(LANES),
                   pl.BlockSpec((N_EXPERTS, LANES), lambda i: (i, 0))],
        out_shape=[jax.ShapeDtypeStruct((t, d), F32), jax.ShapeDtypeStruct((nt * SLOTS, d), BF16),
                   jax.ShapeDtypeStruct((t, LANES), F32), jax.ShapeDtypeStruct((nt * N_EXPERTS, LANES), F32)],
        compiler_params=pltpu.CompilerParams(
            dimension_semantics=("parallel",), vmem_limit_bytes=V7X_VMEM_LIMIT_BYTES),
        name="post_attn",
    )(x2d, ya, *os_, *ls_, ga, gb, wa, wb, wo, gf, rwt, rb, tri)


def _expert_tiles(pcnt, seg_off, n_tiles):
    nt, n_e = pcnt.shape
    cpt = CHUNKS_PER_TILE
    nch = (pcnt // CHUNK).T
    cum = jnp.cumsum(nch, axis=1)
    total = cum[:, -1]
    tiles_e = (total + cpt - 1) // cpt
    tile_end = jnp.cumsum(tiles_e)
    n_active = tile_end[-1]
    i = jnp.arange(n_tiles, dtype=I32)
    last = jnp.minimum(i, n_active - 1)
    te = jnp.sum((last[:, None] >= tile_end[None, :]).astype(I32), axis=1)
    hot_e = (te[:, None] == jnp.arange(n_e, dtype=I32)[None, :])
    pick = lambda tab: jnp.sum(jnp.where(hot_e[:, :, None], tab[None], 0), axis=1)
    tile_start = jnp.sum(jnp.where(hot_e, (tile_end - tiles_e)[None, :], 0), axis=1)
    total_t = jnp.sum(jnp.where(hot_e, total[None, :], 0), axis=1)
    q = (last - tile_start)[:, None] * cpt + jnp.arange(cpt, dtype=I32)[None, :]
    valid = (q < total_t[:, None]) & (i < n_active)[:, None]
    cum_t, nch_t = pick(cum), pick(nch)
    chunk0_t = pick((seg_off.T + jnp.arange(nt, dtype=I32)[None, :] * SLOTS) // CHUNK)
    jj = jnp.sum((q[:, :, None] >= cum_t[:, None, :]).astype(I32), axis=2)
    hot_j = jj[:, :, None] == jnp.arange(nt, dtype=I32)[None, None, :]
    first = jnp.sum(jnp.where(hot_j, (cum_t - nch_t)[:, None, :], 0), axis=2)
    base = jnp.sum(jnp.where(hot_j, chunk0_t[:, None, :], 0), axis=2)
    src = jnp.where(valid, base + q - first, 0)
    trash = nt * SLOTS // CHUNK + (i % 2)[:, None] * cpt + jnp.arange(cpt, dtype=I32)[None, :]
    dst = jnp.where(valid, src, trash)
    group_end = jnp.sum(jnp.where(hot_e, tile_end[None, :], 0), axis=1)
    nxt = jnp.sum((group_end[:, None] >= tile_end[None, :]).astype(I32), axis=1)
    nxt = jnp.where(group_end < n_active, nxt, -1)
    return (te.astype(I32), nxt.astype(I32), n_active.astype(I32)[None],
            src.reshape(-1).astype(I32), dst.reshape(-1).astype(I32))


def _expert_kernel(te_ref, nx_ref, na_ref, cs_ref, cd_ref, xg_hbm, wg_hbm, bg_ref, wu_hbm, bu_ref, wd_hbm,
                   bd_ref, yg_hbm, xbuf, ybuf, wst, wbf, zbuf, sem_in, sem_out, sem_zero, sem_w, *,
                   n_token_tiles):
    i = pl.program_id(0)
    n_active = na_ref[0]
    slot = i % 2
    cpt = CHUNKS_PER_TILE
    w_hbm = (wg_hbm, wu_hbm, wd_hbm)

    def w_copy(n, e):
        return pltpu.make_async_copy(w_hbm[n].at[e], wst.at[n], sem_w.at[n])

    zero_starts = [j * SLOTS + TM_POST * TOP_K for j in range(n_token_tiles)]
    zero_starts += [n_token_tiles * SLOTS + j * SLACK_ROWS for j in range(2 * TM_EXPERT // SLACK_ROWS)]

    def zero_copy(n):
        return pltpu.make_async_copy(
            zbuf, yg_hbm.at[pl.ds(zero_starts[n] // CHUNK, SLACK_ROWS // CHUNK)], sem_zero.at[n])

    @pl.when(i == 0)
    def _():
        zbuf[...] = jnp.zeros_like(zbuf)
        for n in range(len(zero_starts)):
            zero_copy(n).start()

    def in_copy(c, sl, chunk):
        return pltpu.make_async_copy(xg_hbm.at[chunk], xbuf.at[sl, c], sem_in.at[sl])

    def out_copy(c, sl, chunk):
        return pltpu.make_async_copy(ybuf.at[sl, c], yg_hbm.at[chunk], sem_out.at[sl])

    def start_gather(tile, sl):
        for c in range(cpt):
            in_copy(c, sl, cs_ref[tile * cpt + c]).start()

    def wait_gather(sl):
        pltpu.make_async_copy(xg_hbm.at[pl.ds(0, cpt)], xbuf.at[sl], sem_in.at[sl]).wait()

    def wait_scatter(sl):
        pltpu.make_async_copy(ybuf.at[sl], yg_hbm.at[pl.ds(0, cpt)], sem_out.at[sl]).wait()

    @pl.when(i == 0)
    def _():
        start_gather(0, 0)
        for n in range(len(w_hbm)):
            w_copy(n, te_ref[0]).start()

    @pl.when(i + 1 < n_active)
    def _():
        start_gather(i + 1, 1 - slot)

    @pl.when(i < n_active)
    def _():
        wait_gather(slot)

        @pl.when((i == 0) | (te_ref[i] != te_ref[jnp.maximum(i - 1, 0)]))
        def _():
            for n in range(len(w_hbm)):
                w_copy(n, 0).wait()
                wbf[n] = wst[n].astype(BF16)

                @pl.when(nx_ref[i] >= 0)
                def _():
                    w_copy(n, nx_ref[i]).start()

        @pl.when(i >= 1)
        def _():
            for c in range(cpt):
                out_copy(c, 1 - slot, cd_ref[(i - 1) * cpt + c]).start()

        x = xbuf[slot].reshape(TM_EXPERT, -1)
        e = te_ref[i]
        gate = jnp.minimum(jnp.dot(x, wbf[0], preferred_element_type=F32) + bg_ref[e], SWIGLU_LIMIT)
        up = jnp.clip(jnp.dot(x, wbf[1], preferred_element_type=F32) + bu_ref[e], -SWIGLU_LIMIT, SWIGLU_LIMIT)
        act = gate * jax.nn.sigmoid(SWIGLU_ALPHA * gate) * (up + 1.0)
        y = jnp.dot(act.astype(BF16), wbf[2], preferred_element_type=F32) + bd_ref[e]

        @pl.when(i >= 2)
        def _():
            wait_scatter(slot)

        ybuf[slot] = y.astype(BF16).reshape(ybuf.shape[1:])

        @pl.when(i == 0)
        def _():
            for n in range(len(zero_starts)):
                zero_copy(n).wait()

        @pl.when(i == n_active - 1)
        def _():
            for c in range(cpt):
                out_copy(c, slot, cd_ref[i * cpt + c]).start()
            wait_scatter(slot)

            @pl.when(i >= 1)
            def _():
                wait_scatter(1 - slot)


def _experts(xg, te, nxt, n_active, src, dst, wg, bg, wu, bu, wd, bd):
    rows, d = xg.shape
    n_e, _, dff = wg.shape
    assert d == dff
    n_tiles = te.shape[0]
    n_token_tiles = rows // SLOTS
    n_zero = n_token_tiles + 2 * TM_EXPERT // SLACK_ROWS
    b_spec = pl.BlockSpec((n_e, 1, d), lambda i, *_: (0, 0, 0))
    hbm = pl.BlockSpec(memory_space=pl.ANY)
    tile_bufs = pltpu.VMEM((2, CHUNKS_PER_TILE, CHUNK, d), BF16)
    yg = pl.pallas_call(
        functools.partial(_expert_kernel, n_token_tiles=n_token_tiles),
        grid_spec=pltpu.PrefetchScalarGridSpec(
            num_scalar_prefetch=5,
            grid=(n_tiles,),
            in_specs=[hbm, hbm, b_spec, hbm, b_spec, hbm, b_spec],
            out_specs=hbm,
            scratch_shapes=[tile_bufs, tile_bufs,
                            pltpu.VMEM((3, d, dff), F32), pltpu.VMEM((3, d, dff), BF16),
                            pltpu.VMEM((SLACK_ROWS // CHUNK, CHUNK, d), BF16),
                            pltpu.SemaphoreType.DMA((2,)),
                            pltpu.SemaphoreType.DMA((2,)),
                            pltpu.SemaphoreType.DMA((n_zero,)),
                            pltpu.SemaphoreType.DMA((3,))]),
        out_shape=jax.ShapeDtypeStruct(((rows + 2 * TM_EXPERT) // CHUNK, CHUNK, d), BF16),
        compiler_params=pltpu.CompilerParams(
            dimension_semantics=("arbitrary",), vmem_limit_bytes=V7X_VMEM_LIMIT_BYTES),
        name="experts",
    )(te, nxt, n_active, src, dst, xg.reshape(rows // CHUNK, CHUNK, d), wg, bg, wu, bu, wd, bd)
    return yg.reshape(rows + 2 * TM_EXPERT, d)


def _combine_kernel(x1_ref, yg_ref, meta_ref, gn_ref, o_ref, *, final_norm):
    meta = meta_ref[...]
    tm = meta.shape[0]
    slots_i = [meta[:, k:k + 1].astype(I32) for k in range(TOP_K)]
    probs = [meta[:, TOP_K + k:TOP_K + k + 1] for k in range(TOP_K)]
    iota_s = lax.broadcasted_iota(I32, (tm, PERM_BLOCK), 1)
    xo = x1_ref[...]
    for blk in range(SLOTS // PERM_BLOCK):
        base = blk * PERM_BLOCK
        wperm = jnp.zeros((tm, PERM_BLOCK), F32)
        for sk, pk in zip(slots_i, probs):
            wperm = jnp.where(iota_s == sk - base, pk, wperm)
        xo = xo + jnp.dot(wperm.astype(BF16), yg_ref[base:base + PERM_BLOCK, :], preferred_element_type=F32)
    o_ref[...] = _rms(xo, gn_ref[...]) if final_norm else xo


def _combine(x1, yg, meta, gn, final_norm):
    t, d = x1.shape
    tm = TM_POST
    return pl.pallas_call(
        functools.partial(_combine_kernel, final_norm=final_norm),
        grid=(t // tm,),
        in_specs=[pl.BlockSpec((tm, d), lambda i: (i, 0)),
                  pl.BlockSpec((SLOTS, d), lambda i: (i, 0)),
                  pl.BlockSpec((tm, LANES), lambda i: (i, 0)),
                  pl.BlockSpec((1, d), lambda i: (0, 0))],
        out_specs=pl.BlockSpec((tm, d), lambda i: (i, 0)),
        out_shape=jax.ShapeDtypeStruct((t, d), F32),
        compiler_params=pltpu.CompilerParams(
            dimension_semantics=("parallel",), vmem_limit_bytes=V7X_VMEM_LIMIT_BYTES),
        name="combine",
    )(x1, yg, meta, gn)


def kernel(x, norm_mix, w_in, b_in, sinks, rel_bias, w_branch_a, w_branch_b, w_out, norm_ffn,
           router_w, router_b, w_gate, b_gate, w_up, b_up, w_down, b_down, norm_final):
    b, s, d = x.shape
    t = b * s
    depth = w_in.shape[0]
    a_q_w = A_KV_HEADS * A_GROUP * HEAD_DIM
    a_kv_w = A_KV_HEADS * HEAD_DIM
    b_w = B_HEADS * HEAD_DIM
    n_grp = len(B_GROUPS)
    dils = tuple(dil for _, dil in B_GROUPS)
    segs, col = [], 0
    for kind, width in (("q", a_q_w), ("k", a_kv_w), ("vT", a_kv_w)):
        segs.append((col, width, kind, 1))
        col += width
    for kind in ("q", "k", "v"):
        for dil in dils:
            segs.append((col, b_w, kind, dil))
            col += b_w
    for _ in range(2):
        segs.append((col, d, "gate", 1))
        col += d
    segs = tuple(segs)
    n_a = A_KV_HEADS * A_GROUP
    nt = t // TM_POST
    max_chunks = nt * ((TM_POST * TOP_K + N_EXPERTS * (CHUNK - 1)) // CHUNK)
    n_tiles = -(-(max_chunks + N_EXPERTS * (CHUNKS_PER_TILE - 1)) // CHUNKS_PER_TILE)

    bias_a = _band_bias(rel_bias[:, :n_a], A_HALF_WINDOW, TQ_ATTN, 1)
    bias_b = [_band_bias(rel_bias[:, n_a + gi * B_HEADS:n_a + (gi + 1) * B_HEADS],
                         win // (2 * dil), TQ_ATTN, dil) for gi, (win, dil) in enumerate(B_GROUPS)]

    x2d = x.reshape(t, d)
    for layer in range(depth):
        proj = _inproj(x2d, norm_mix[layer][None], w_in[layer].astype(BF16), b_in[layer][None], segs)
        qa, ka, vat = proj[:3]
        qb, kb, vb = proj[3:3 + n_grp], proj[3 + n_grp:3 + 2 * n_grp], proj[3 + 2 * n_grp:3 + 3 * n_grp]
        ga, gb = proj[-2:]

        yat = _band_attn_gqa(qa.reshape(b, s, a_q_w), ka.reshape(b, s, a_kv_w), vat, bias_a, sinks[layer],
                             n_kv=A_KV_HEADS, grp=A_GROUP, hw=A_HALF_WINDOW, tq=TQ_ATTN, block=BLOCK_ATTN_A)
        os_, ls_ = [], []
        for gi, (win, dil) in enumerate(B_GROUPS):
            sub = lambda a: a.reshape(b, s // dil, dil * b_w)
            o, lse = _band_attn_heads(sub(qb[gi]), sub(kb[gi]), sub(vb[gi]), bias_b[gi], n_heads=B_HEADS,
                                      hw=win // (2 * dil), tq=TQ_ATTN, block=BLOCK_ATTN_B, reps=dil)
            os_.append(o.reshape(t // dil, dil * b_w))
            ls_.append(lse.reshape(t // dil, dil * b_w))

        rwt = router_w[layer].T
        rwt_hi = rwt.astype(BF16)
        rwt_split = jnp.stack([rwt_hi, (rwt - rwt_hi.astype(F32)).astype(BF16)])
        x1, xg, meta, segs_out = _post(
            x2d, yat, os_, ls_, dils, ga, gb,
            w_branch_a[layer].astype(BF16), w_branch_b[layer].astype(BF16), w_out[layer].astype(BF16),
            norm_ffn[layer][None], rwt_split, router_b[layer][:, None])

        segs3 = segs_out.reshape(nt, N_EXPERTS, LANES)
        te, nxt, n_active, src, dst = _expert_tiles(segs3[:, :, 0].astype(I32), segs3[:, :, 1].astype(I32), n_tiles)
        yg = _experts(xg, te, nxt, n_active, src, dst, w_gate[layer], b_gate[layer][:, None], w_up[layer],
                      b_up[layer][:, None], w_down[layer], b_down[layer][:, None])
        x2d = _combine(x1, yg, meta, norm_final[None], layer == depth - 1)
    return x2d.reshape(b, s, d)
```

```python
import functools

import numpy as np
import jax
import jax.numpy as jnp
from jax import lax
from jax.experimental import pallas as pl
from jax.experimental.pallas import tpu as pltpu

F32 = jnp.float32
BF16 = jnp.bfloat16
I32 = jnp.int32

HEAD_DIM = 64
A_KV_HEADS = 4
A_GROUP = 4
A_HALF_WINDOW = 128
B_GROUPS = ((128, 1), (512, 4), (2048, 16))
B_HEADS = 4
N_BUCKETS = 32
MAX_DISTANCE = 1024
N_EXPERTS = 32
TOP_K = 4
SWIGLU_LIMIT = 7.0
SWIGLU_ALPHA = 1.702
EPS = 1e-5
NEG_INF = -1e30
LOG2E = 1.4426950408889634

V7X_VMEM_LIMIT_BYTES = 56 * 1024 * 1024
LANES = 128
BF16_SUBLANES = 16

TM_PROJ = 1024
TQ_ATTN = 128
BLOCK_ATTN_A = 512
BLOCK_ATTN_B = 512
TM_POST = 512
TM_EXPERT = 512

CHUNK = BF16_SUBLANES
CHUNKS_PER_TILE = TM_EXPERT // CHUNK
SLOTS = -(-(TM_POST * TOP_K + N_EXPERTS * (CHUNK - 1)) // LANES) * LANES
SLACK_ROWS = SLOTS - TM_POST * TOP_K
assert (2 * TM_EXPERT) % SLACK_ROWS == 0
PERM_BLOCK = 512
assert SLOTS % PERM_BLOCK == 0


def _t5_buckets(rel):
    half = N_BUCKETS // 2
    max_exact = half // 2
    ret = np.where(rel > 0, half, 0)
    n = np.abs(rel)
    large = max_exact + (np.log(np.maximum(n, 1) / max_exact)
                         / np.log(MAX_DISTANCE / max_exact) * (half - max_exact)).astype(np.int32)
    large = np.minimum(large, half - 1)
    return (ret + np.where(n < max_exact, n, large)).astype(np.int32)


def _rms(x, g):
    return x * lax.rsqrt(jnp.mean(x * x, axis=-1, keepdims=True) + EPS) * g


def _inproj_kernel(x_ref, g_ref, w_ref, b_ref, *refs, segs):
    out_refs, scr = refs[:-1], refs[-1]
    h = _rms(x_ref[...], g_ref[...]).astype(BF16)
    for ref, (c0, width, kind, dil) in zip(out_refs, segs):
        acc = jnp.dot(h, w_ref[:, c0:c0 + width], preferred_element_type=F32) + b_ref[:, c0:c0 + width]
        if kind == "q":
            acc = acc * (HEAD_DIM ** -0.5 * LOG2E)
        elif kind == "gate":
            acc = jax.nn.sigmoid(acc)
        if kind == "vT":
            ref[...] = acc.T.astype(ref.dtype)
        elif dil == 1:
            ref[...] = acc.astype(ref.dtype)
        else:
            n = acc.shape[0] // dil
            for c in range(width // LANES):
                scr[c] = acc[:, c * LANES:(c + 1) * LANES]
            for r in range(dil):
                for c in range(width // LANES):
                    col = r * width + c * LANES
                    ref[:, col:col + LANES] = scr[c, pl.ds(r, n, stride=dil), :].astype(ref.dtype)


def _inproj(x2d, g, w_bf16, b, segs):
    t, d = x2d.shape
    n = w_bf16.shape[1]
    max_w = max(w for _, w, _, dil in segs if dil > 1)
    out_specs, out_shape = [], []
    for _, w, kind, dil in segs:
        if kind == "vT":
            out_specs.append(pl.BlockSpec((w, TM_PROJ), lambda i: (0, i)))
            out_shape.append(jax.ShapeDtypeStruct((w, t), BF16))
        else:
            out_specs.append(pl.BlockSpec((TM_PROJ // dil, dil * w), lambda i: (i, 0)))
            out_shape.append(jax.ShapeDtypeStruct((t // dil, dil * w), BF16))
    return pl.pallas_call(
        functools.partial(_inproj_kernel, segs=segs),
        grid=(t // TM_PROJ,),
        in_specs=[
            pl.BlockSpec((TM_PROJ, d), lambda i: (i, 0)),
            pl.BlockSpec((1, d), lambda i: (0, 0)),
            pl.BlockSpec((d, n), lambda i: (0, 0), pipeline_mode=pl.Buffered(1)),
            pl.BlockSpec((1, n), lambda i: (0, 0)),
        ],
        out_specs=out_specs,
        out_shape=out_shape,
        scratch_shapes=[pltpu.VMEM((max_w // LANES, TM_PROJ, LANES), F32)],
        compiler_params=pltpu.CompilerParams(
            dimension_semantics=("parallel",), vmem_limit_bytes=V7X_VMEM_LIMIT_BYTES),
        name="inproj",
    )(x2d, g, w_bf16, b)


def _band_bias(table, hw, tq, dil):
    tk = tq + 2 * hw
    off = np.arange(tk)[None, :] - hw - np.arange(tq)[:, None]
    band = np.abs(off) <= hw
    col = np.arange(tk)[None, :]
    masks = np.stack([band & (col >= hw), band, band & (col < hw + tq)])
    onehot = (_t5_buckets(off * dil)[..., None] == np.arange(N_BUCKETS)).astype(np.float32)
    bias = jnp.einsum("qkn,nh->hqk", jnp.asarray(onehot), table.astype(F32),
                      precision=lax.Precision.HIGHEST)
    return jnp.where(masks[:, None], bias[None] * LOG2E, NEG_INF)


def _band_attn_gqa_kernel(q_ref, kp, kc, kn, vtp, vtc, vtn, bias_ref, sink_ref, o_ref, *, n_kv, grp, tq, hw):
    j, nt = pl.program_id(1), pl.num_programs(1)
    k = jnp.concatenate([kp[0], kc[0], kn[0]], axis=0)
    vt = jnp.concatenate([vtp[...], vtc[...], vtn[...]], axis=1)
    n_sub = q_ref.shape[1] // tq
    tk = tq + 2 * hw
    ones = jnp.ones((BF16_SUBLANES, tk), BF16)
    lane = lax.broadcasted_iota(I32, (1, grp * tq), 1)
    sinks = []
    for h in range(n_kv):
        sk = jnp.full((1, grp * tq), sink_ref[h * grp] * LOG2E, F32)
        for g in range(1, grp):
            sk = jnp.where(lane >= g * tq, sink_ref[h * grp + g] * LOG2E, sk)
        sinks.append(sk)
    units = []
    for sub in range(n_sub):
        var = 1
        if sub == 0:
            var = jnp.where(j == 0, 0, var)
        if sub == n_sub - 1:
            var = jnp.where(j == nt - 1, 2, var)
        units.extend((var, sub * tq, h) for h in range(n_kv))

    def scores(var, r0, h):
        k_h = k[r0:r0 + tk, h * HEAD_DIM:(h + 1) * HEAD_DIM]
        c0 = h * grp * HEAD_DIM
        q = jnp.concatenate([q_ref[0, r0:r0 + tq, c0 + g * HEAD_DIM:c0 + (g + 1) * HEAD_DIM]
                             for g in range(grp)], axis=0)
        st = lax.dot_general(k_h, q, (((1,), (1,)), ((), ())), preferred_element_type=F32)
        return st + bias_ref[var, h]

    st_next = scores(*units[0])
    for n, (var, r0, h) in enumerate(units):
        st = st_next
        if n + 1 < len(units):
            st_next = scores(*units[n + 1])
        sk = sinks[h]
        m = jnp.maximum(jnp.max(st, axis=0, keepdims=True), sk)
        pt = jnp.exp2((st - m).astype(BF16))
        vt_h = jnp.concatenate([vt[h * HEAD_DIM:(h + 1) * HEAD_DIM, r0:r0 + tk], ones], axis=0)
        ot = jnp.dot(vt_h, pt, preferred_element_type=F32)
        l = ot[HEAD_DIM:HEAD_DIM + 1] + jnp.exp2(sk - m)
        o = ot[:HEAD_DIM] / l
        for g in range(grp):
            c = (h * grp + g) * HEAD_DIM
            o_ref[c:c + HEAD_DIM, r0:r0 + tq] = o[:, g * tq:(g + 1) * tq].astype(o_ref.dtype)


def _band_attn_gqa(q, k, vt, bias3, sinks, *, n_kv, grp, hw, tq, block):
    b, s, _ = q.shape
    hq = n_kv * grp
    qw, kw = hq * HEAD_DIM, n_kv * HEAD_DIM
    nt = s // block
    ratio = block // hw
    nhw = s // hw
    tk = tq + 2 * hw
    assert nt * (block // tq) >= 2
    prev_j = lambda j: jnp.maximum(j * ratio - 1, 0)
    next_j = lambda j: jnp.minimum((j + 1) * ratio, nhw - 1)
    bias_t = jnp.swapaxes(bias3.reshape(3, n_kv, grp * tq, tk), 2, 3)
    return pl.pallas_call(
        functools.partial(_band_attn_gqa_kernel, n_kv=n_kv, grp=grp, tq=tq, hw=hw),
        grid=(b, nt),
        in_specs=[
            pl.BlockSpec((1, block, qw), lambda bi, j: (bi, j, 0)),
            pl.BlockSpec((1, hw, kw), lambda bi, j: (bi, prev_j(j), 0)),
            pl.BlockSpec((1, block, kw), lambda bi, j: (bi, j, 0)),
            pl.BlockSpec((1, hw, kw), lambda bi, j: (bi, next_j(j), 0)),
            pl.BlockSpec((kw, hw), lambda bi, j: (0, bi * nhw + prev_j(j))),
            pl.BlockSpec((kw, block), lambda bi, j: (0, bi * nt + j)),
            pl.BlockSpec((kw, hw), lambda bi, j: (0, bi * nhw + next_j(j))),
            pl.BlockSpec((3, n_kv, tk, grp * tq), lambda bi, j: (0, 0, 0, 0), pipeline_mode=pl.Buffered(1)),
            pl.BlockSpec(memory_space=pltpu.SMEM),
        ],
        out_specs=pl.BlockSpec((qw, block), lambda bi, j: (0, bi * nt + j)),
        out_shape=jax.ShapeDtypeStruct((qw, b * s), BF16),
        compiler_params=pltpu.CompilerParams(
            dimension_semantics=("parallel", "parallel"), vmem_limit_bytes=V7X_VMEM_LIMIT_BYTES),
        name="band_attn_a",
    )(q, k, k, k, vt, vt, vt, bias_t, sinks)


def _band_attn_heads_kernel(q_ref, kp, kc, kn, vp, vc, vn, bias_ref, o_ref, lse_ref, *, n_heads, tq, hw):
    j, nt = pl.program_id(2), pl.num_programs(2)
    k = jnp.concatenate([kp[0], kc[0], kn[0]], axis=0)
    v = jnp.concatenate([vp[0], vc[0], vn[0]], axis=0)
    n_sub = q_ref.shape[1] // tq
    tk = tq + 2 * hw
    width = n_heads * HEAD_DIM
    lane_head = lax.broadcasted_iota(I32, (tq, width), 1) // HEAD_DIM
    ones = jnp.ones((tk, LANES), BF16)

    def scores(sub):
        var = 1
        if sub == 0:
            var = jnp.where(j == 0, 0, var)
        if sub == n_sub - 1:
            var = jnp.where(j == nt - 1, 2, var)
        r0 = sub * tq
        q = q_ref[0, r0:r0 + tq, :]
        q_bd = jnp.concatenate([jnp.where(lane_head == h, q, jnp.zeros_like(q)) for h in range(n_heads)], axis=0)
        s = lax.dot_general(q_bd, k[r0:r0 + tk], (((1,), (1,)), ((), ())), preferred_element_type=F32)
        return s + bias_ref[var, 0]

    s_next = scores(0)
    for sub in range(n_sub):
        r0 = sub * tq
        s = s_next
        if sub + 1 < n_sub:
            s_next = scores(sub + 1)
        m = jnp.max(s, axis=-1, keepdims=True)
        p = jnp.exp2((s - m).astype(BF16))
        o_full = jnp.dot(p, v[r0:r0 + tk], preferred_element_type=F32)
        l = jnp.dot(p, ones, preferred_element_type=F32)[:, :1]
        o_sel = jnp.zeros((tq, width), F32)
        m_sel = jnp.zeros((tq, width), F32)
        l_sel = jnp.ones((tq, width), F32)
        for h in range(n_heads):
            hit = lane_head == h
            o_sel = jnp.where(hit, o_full[h * tq:(h + 1) * tq], o_sel)
            m_sel = jnp.where(hit, m[h * tq:(h + 1) * tq], m_sel)
            l_sel = jnp.where(hit, l[h * tq:(h + 1) * tq], l_sel)
        o_ref[0, r0:r0 + tq, :] = (o_sel / l_sel).astype(o_ref.dtype)
        lse_ref[0, r0:r0 + tq, :] = m_sel + jnp.log2(l_sel)


def _band_attn_heads(q, k, v, bias3, *, n_heads, hw, tq, block, reps):
    b, l, _ = q.shape
    qw = kw = n_heads * HEAD_DIM
    nt = l // block
    ratio = block // hw
    nhw = l // hw
    tk = tq + 2 * hw
    assert nt * (block // tq) >= 2

    prev = lambda bi, r, j: (bi, jnp.maximum(j * ratio - 1, 0), r)
    cur = lambda bi, r, j: (bi, j, r)
    nxt = lambda bi, r, j: (bi, jnp.minimum((j + 1) * ratio, nhw - 1), r)
    in_specs = [
        pl.BlockSpec((1, block, qw), cur),
        pl.BlockSpec((1, hw, kw), prev), pl.BlockSpec((1, block, kw), cur), pl.BlockSpec((1, hw, kw), nxt),
        pl.BlockSpec((1, hw, kw), prev), pl.BlockSpec((1, block, kw), cur), pl.BlockSpec((1, hw, kw), nxt),
        pl.BlockSpec((3, 1, n_heads * tq, tk), lambda bi, r, j: (0, 0, 0, 0), pipeline_mode=pl.Buffered(1)),
    ]
    return pl.pallas_call(
        functools.partial(_band_attn_heads_kernel, n_heads=n_heads, tq=tq, hw=hw),
        grid=(b, reps, nt),
        in_specs=in_specs,
        out_specs=[pl.BlockSpec((1, block, qw), cur), pl.BlockSpec((1, block, qw), cur)],
        out_shape=[jax.ShapeDtypeStruct((b, l, reps * qw), BF16), jax.ShapeDtypeStruct((b, l, reps * qw), F32)],
        compiler_params=pltpu.CompilerParams(
            dimension_semantics=("parallel", "parallel", "parallel"),
            vmem_limit_bytes=V7X_VMEM_LIMIT_BYTES),
        name="band_attn_b",
    )(q, k, k, k, v, v, v, bias3.reshape(3, 1, n_heads * tq, tk))


def _token_order(ref, scr, dil):
    if dil == 1:
        return ref[...].astype(F32)
    n = ref.shape[0]
    width = ref.shape[1] // dil
    for r in range(dil):
        for c in range(width // LANES):
            col = r * width + c * LANES
            scr[c, pl.ds(r, n, stride=dil), :] = ref[:, col:col + LANES].astype(F32)
    return jnp.concatenate([scr[c] for c in range(width // LANES)], axis=1)


def _post_kernel(x_ref, ya_ref, o1, o2, o3, l1, l2, l3, ga_ref, gb_ref, wa_ref, wb_ref, wo_ref,
                 gf_ref, rwt_ref, rb_ref, tri_ref, x1_ref, xg_ref, meta_ref, segs_ref, *scrs, dils):
    os_ = [_token_order(r, scrs[2 * i], dil) for i, (r, dil) in enumerate(zip((o1, o2, o3), dils))]
    ls = [_token_order(r, scrs[2 * i + 1], dil) for i, (r, dil) in enumerate(zip((l1, l2, l3), dils))]
    mx = jnp.maximum(jnp.maximum(ls[0], ls[1]), ls[2])
    es = [jnp.exp2(l - mx) for l in ls]
    den = es[0] + es[1] + es[2]
    yb = (es[0] * os_[0] + es[1] * os_[1] + es[2] * os_[2]) / den
    za = lax.dot_general(ya_ref[...], wa_ref[...], (((0,), (0,)), ((), ())), preferred_element_type=F32)
    zb = jnp.dot(yb.astype(BF16), wb_ref[...], preferred_element_type=F32)
    merged = ga_ref[...].astype(F32) * za + gb_ref[...].astype(F32) * zb
    x1 = x_ref[...] + jnp.dot(merged.astype(BF16), wo_ref[...], preferred_element_type=F32)
    x1_ref[...] = x1
    h2 = _rms(x1, gf_ref[...])

    h_hi = h2.astype(BF16)
    h_lo = (h2 - h_hi.astype(F32)).astype(BF16)
    nt_dot = lambda a, b_: lax.dot_general(a, b_, (((1,), (1,)), ((), ())), preferred_element_type=F32)
    n_e = rwt_ref.shape[1]
    both = nt_dot(rwt_ref[...].reshape(2 * n_e, -1), h_hi)
    logits = (both[:n_e] + nt_dot(rwt_ref[0], h_lo) + both[n_e:]) + rb_ref[...]
    n_e, tm = logits.shape
    iota_e = lax.broadcasted_iota(I32, (n_e, tm), 0)
    work = logits
    vals, hots = [], []
    for _ in range(TOP_K):
        mk = jnp.max(work, axis=0, keepdims=True)
        ik = jnp.min(jnp.where(work == mk, iota_e, n_e), axis=0, keepdims=True)
        hot = iota_e == ik
        vals.append(mk)
        hots.append(hot)
        work = jnp.where(hot, -jnp.inf, work)
    exps = [jnp.exp(vk - vals[0]) for vk in vals]
    tot = exps[0] + exps[1] + exps[2] + exps[3]
    probs = [ek / tot for ek in exps]

    sel = jnp.zeros((n_e, tm), F32)
    for hot in hots:
        sel = sel + jnp.where(hot, 1.0, 0.0)
    cnt = jnp.sum(sel, axis=1, keepdims=True)
    pcnt = jnp.floor((cnt + (CHUNK - 1)) / CHUNK) * CHUNK
    r_i = lax.broadcasted_iota(I32, (n_e, n_e), 0)
    c_i = lax.broadcasted_iota(I32, (n_e, n_e), 1)
    pcnt_row = jnp.sum(jnp.where(r_i == c_i, pcnt, 0.0), axis=0, keepdims=True)
    seg_off = jnp.sum(jnp.where(c_i < r_i, pcnt_row, 0.0), axis=1, keepdims=True)
    before = jnp.dot(sel.astype(BF16), tri_ref[...], preferred_element_type=F32)
    slot_of = seg_off + before
    slots = [jnp.sum(jnp.where(hot, slot_of, 0.0), axis=0, keepdims=True) for hot in hots]

    iota_s = lax.broadcasted_iota(I32, (PERM_BLOCK, tm), 0)
    slots_i = [sk.astype(I32) for sk in slots]
    for blk in range(SLOTS // PERM_BLOCK):
        perm = jnp.zeros((PERM_BLOCK, tm), F32)
        for sk in slots_i:
            perm = jnp.where(iota_s == sk - blk * PERM_BLOCK, 1.0, perm)
        xg_ref[blk * PERM_BLOCK:(blk + 1) * PERM_BLOCK, :] = jnp.dot(
            perm.astype(BF16), h_hi, preferred_element_type=F32).astype(BF16)

    rows = slots + probs
    meta_t = jnp.concatenate(rows + [jnp.zeros((LANES - len(rows), tm), F32)], axis=0)
    meta_ref[...] = meta_t.T
    lane = lax.broadcasted_iota(I32, (n_e, LANES), 1)
    segs_ref[...] = jnp.where(lane == 0, pcnt, jnp.where(lane == 1, seg_off, 0.0))


def _post(x2d, ya, os_, ls_, dils, ga, gb, wa, wb, wo, gf, rwt, rb):
    t, d = x2d.shape
    bw = os_[0].shape[1] // dils[0]
    tm = TM_POST
    nt = t // tm
    tri = jnp.asarray(np.triu(np.ones((tm, tm), np.float32), k=1), dtype=BF16)
    row = lambda w, dil=1: pl.BlockSpec((tm // dil, dil * w), lambda i: (i, 0))
    full = lambda a: pl.BlockSpec(a.shape, lambda i: (0,) * a.ndim)
    return pl.pallas_call(
        functools.partial(_post_kernel, dils=dils),
        grid=(nt,),
        in_specs=[row(d), pl.BlockSpec((ya.shape[0], tm), lambda i: (0, i))]
                 + [row(bw, dil) for dil in dils] + [row(bw, dil) for dil in dils]
                 + [row(d), row(d), full(wa), full(wb), full(wo), full(gf), full(rwt), full(rb), full(tri)],
        scratch_shapes=[pltpu.VMEM((bw // LANES, tm, LANES), F32) for _ in range(2 * len(dils))],
        out_specs=[row(d), pl.BlockSpec((SLOTS, d), lambda i: (i, 0)), row(LANES),
                   pl.BlockSpec((N_EXPERTS, LANES), lambda i: (i, 0))],
        out_shape=[jax.ShapeDtypeStruct((t, d), F32), jax.ShapeDtypeStruct((nt * SLOTS, d), BF16),
                   jax.ShapeDtypeStruct((t, LANES), F32), jax.ShapeDtypeStruct((nt * N_EXPERTS, LANES), F32)],
        compiler_params=pltpu.CompilerParams(
            dimension_semantics=("parallel",), vmem_limit_bytes=V7X_VMEM_LIMIT_BYTES),
        name="post_attn",
    )(x2d, ya, *os_, *ls_, ga, gb, wa, wb, wo, gf, rwt, rb, tri)


def _expert_tiles(pcnt, seg_off, n_tiles):
    nt, n_e = pcnt.shape
    cpt = CHUNKS_PER_TILE
    nch = (pcnt // CHUNK).T
    cum = jnp.cumsum(nch, axis=1)
    total = cum[:, -1]
    tiles_e = (total + cpt - 1) // cpt
    tile_end = jnp.cumsum(tiles_e)
    n_active = tile_end[-1]
    i = jnp.arange(n_tiles, dtype=I32)
    last = jnp.minimum(i, n_active - 1)
    te = jnp.sum((last[:, None] >= tile_end[None, :]).astype(I32), axis=1)
    hot_e = (te[:, None] == jnp.arange(n_e, dtype=I32)[None, :])
    pick = lambda tab: jnp.sum(jnp.where(hot_e[:, :, None], tab[None], 0), axis=1)
    tile_start = jnp.sum(jnp.where(hot_e, (tile_end - tiles_e)[None, :], 0), axis=1)
    total_t = jnp.sum(jnp.where(hot_e, total[None, :], 0), axis=1)
    q = (last - tile_start)[:, None] * cpt + jnp.arange(cpt, dtype=I32)[None, :]
    valid = (q < total_t[:, None]) & (i < n_active)[:, None]
    cum_t, nch_t = pick(cum), pick(nch)
    chunk0_t = pick((seg_off.T + jnp.arange(nt, dtype=I32)[None, :] * SLOTS) // CHUNK)
    jj = jnp.sum((q[:, :, None] >= cum_t[:, None, :]).astype(I32), axis=2)
    hot_j = jj[:, :, None] == jnp.arange(nt, dtype=I32)[None, None, :]
    first = jnp.sum(jnp.where(hot_j, (cum_t - nch_t)[:, None, :], 0), axis=2)
    base = jnp.sum(jnp.where(hot_j, chunk0_t[:, None, :], 0), axis=2)
    src = jnp.where(valid, base + q - first, 0)
    trash = nt * SLOTS // CHUNK + (i % 2)[:, None] * cpt + jnp.arange(cpt, dtype=I32)[None, :]
    dst = jnp.where(valid, src, trash)
    group_end = jnp.sum(jnp.where(hot_e, tile_end[None, :], 0), axis=1)
    nxt = jnp.sum((group_end[:, None] >= tile_end[None, :]).astype(I32), axis=1)
    nxt = jnp.where(group_end < n_active, nxt, -1)
    return (te.astype(I32), nxt.astype(I32), n_active.astype(I32)[None],
            src.reshape(-1).astype(I32), dst.reshape(-1).astype(I32))


def _expert_kernel(te_ref, nx_ref, na_ref, cs_ref, cd_ref, xg_hbm, wg_hbm, bg_ref, wu_hbm, bu_ref, wd_hbm,
                   bd_ref, yg_hbm, xbuf, ybuf, wst, wbf, zbuf, sem_in, sem_out, sem_zero, sem_w, *,
                   n_token_tiles):
    i = pl.program_id(0)
    n_active = na_ref[0]
    slot = i % 2
    cpt = CHUNKS_PER_TILE
    w_hbm = (wg_hbm, wu_hbm, wd_hbm)

    def w_copy(n, e):
        return pltpu.make_async_copy(w_hbm[n].at[e], wst.at[n], sem_w.at[n])

    zero_starts = [j * SLOTS + TM_POST * TOP_K for j in range(n_token_tiles)]
    zero_starts += [n_token_tiles * SLOTS + j * SLACK_ROWS for j in range(2 * TM_EXPERT // SLACK_ROWS)]

    def zero_copy(n):
        return pltpu.make_async_copy(
            zbuf, yg_hbm.at[pl.ds(zero_starts[n] // CHUNK, SLACK_ROWS // CHUNK)], sem_zero.at[n])

    @pl.when(i == 0)
    def _():
        zbuf[...] = jnp.zeros_like(zbuf)
        for n in range(len(zero_starts)):
            zero_copy(n).start()

    def in_copy(c, sl, chunk):
        return pltpu.make_async_copy(xg_hbm.at[chunk], xbuf.at[sl, c], sem_in.at[sl])

    def out_copy(c, sl, chunk):
        return pltpu.make_async_copy(ybuf.at[sl, c], yg_hbm.at[chunk], sem_out.at[sl])

    def start_gather(tile, sl):
        for c in range(cpt):
            in_copy(c, sl, cs_ref[tile * cpt + c]).start()

    def wait_gather(sl):
        pltpu.make_async_copy(xg_hbm.at[pl.ds(0, cpt)], xbuf.at[sl], sem_in.at[sl]).wait()

    def wait_scatter(sl):
        pltpu.make_async_copy(ybuf.at[sl], yg_hbm.at[pl.ds(0, cpt)], sem_out.at[sl]).wait()

    @pl.when(i == 0)
    def _():
        start_gather(0, 0)
        for n in range(len(w_hbm)):
            w_copy(n, te_ref[0]).start()

    @pl.when(i + 1 < n_active)
    def _():
        start_gather(i + 1, 1 - slot)

    @pl.when(i < n_active)
    def _():
        wait_gather(slot)

        @pl.when((i == 0) | (te_ref[i] != te_ref[jnp.maximum(i - 1, 0)]))
        def _():
            for n in range(len(w_hbm)):
                w_copy(n, 0).wait()
                wbf[n] = wst[n].astype(BF16)

                @pl.when(nx_ref[i] >= 0)
                def _():
                    w_copy(n, nx_ref[i]).start()

        @pl.when(i >= 1)
        def _():
            for c in range(cpt):
                out_copy(c, 1 - slot, cd_ref[(i - 1) * cpt + c]).start()

        x = xbuf[slot].reshape(TM_EXPERT, -1)
        e = te_ref[i]
        gate = jnp.minimum(jnp.dot(x, wbf[0], preferred_element_type=F32) + bg_ref[e], SWIGLU_LIMIT)
        up = jnp.clip(jnp.dot(x, wbf[1], preferred_element_type=F32) + bu_ref[e], -SWIGLU_LIMIT, SWIGLU_LIMIT)
        act = gate * jax.nn.sigmoid(SWIGLU_ALPHA * gate) * (up + 1.0)
        y = jnp.dot(act.astype(BF16), wbf[2], preferred_element_type=F32) + bd_ref[e]

        @pl.when(i >= 2)
        def _():
            wait_scatter(slot)

        ybuf[slot] = y.astype(BF16).reshape(ybuf.shape[1:])

        @pl.when(i == 0)
        def _():
            for n in range(len(zero_starts)):
                zero_copy(n).wait()

        @pl.when(i == n_active - 1)
        def _():
            for c in range(cpt):
                out_copy(c, slot, cd_ref[i * cpt + c]).start()
            wait_scatter(slot)

            @pl.when(i >= 1)
            def _():
                wait_scatter(1 - slot)


def _experts(xg, te, nxt, n_active, src, dst, wg, bg, wu, bu, wd, bd):
    rows, d = xg.shape
    n_e, _, dff = wg.shape
    assert d == dff
    n_tiles = te.shape[0]
    n_token_tiles = rows // SLOTS
    n_zero = n_token_tiles + 2 * TM_EXPERT // SLACK_ROWS
    b_spec = pl.BlockSpec((n_e, 1, d), lambda i, *_: (0, 0, 0))
    hbm = pl.BlockSpec(memory_space=pl.ANY)
    tile_bufs = pltpu.VMEM((2, CHUNKS_PER_TILE, CHUNK, d), BF16)
    yg = pl.pallas_call(
        functools.partial(_expert_kernel, n_token_tiles=n_token_tiles),
        grid_spec=pltpu.PrefetchScalarGridSpec(
            num_scalar_prefetch=5,
            grid=(n_tiles,),
            in_specs=[hbm, hbm, b_spec, hbm, b_spec, hbm, b_spec],
            out_specs=hbm,
            scratch_shapes=[tile_bufs, tile_bufs,
                            pltpu.VMEM((3, d, dff), F32), pltpu.VMEM((3, d, dff), BF16),
                            pltpu.VMEM((SLACK_ROWS // CHUNK, CHUNK, d), BF16),
                            pltpu.SemaphoreType.DMA((2,)),
                            pltpu.SemaphoreType.DMA((2,)),
                            pltpu.SemaphoreType.DMA((n_zero,)),
                            pltpu.SemaphoreType.DMA((3,))]),
        out_shape=jax.ShapeDtypeStruct(((rows + 2 * TM_EXPERT) // CHUNK, CHUNK, d), BF16),
        compiler_params=pltpu.CompilerParams(
            dimension_semantics=("arbitrary",), vmem_limit_bytes=V7X_VMEM_LIMIT_BYTES),
        name="experts",
    )(te, nxt, n_active, src, dst, xg.reshape(rows // CHUNK, CHUNK, d), wg, bg, wu, bu, wd, bd)
    return yg.reshape(rows + 2 * TM_EXPERT, d)


def _combine_kernel(x1_ref, yg_ref, meta_ref, gn_ref, o_ref, *, final_norm):
    meta = meta_ref[...]
    tm = meta.shape[0]
    slots_i = [meta[:, k:k + 1].astype(I32) for k in range(TOP_K)]
    probs = [meta[:, TOP_K + k:TOP_K + k + 1] for k in range(TOP_K)]
    iota_s = lax.broadcasted_iota(I32, (tm, PERM_BLOCK), 1)
    xo = x1_ref[...]
    for blk in range(SLOTS // PERM_BLOCK):
        base = blk * PERM_BLOCK
        wperm = jnp.zeros((tm, PERM_BLOCK), F32)
        for sk, pk in zip(slots_i, probs):
            wperm = jnp.where(iota_s == sk - base, pk, wperm)
        xo = xo + jnp.dot(wperm.astype(BF16), yg_ref[base:base + PERM_BLOCK, :], preferred_element_type=F32)
    o_ref[...] = _rms(xo, gn_ref[...]) if final_norm else xo


def _combine(x1, yg, meta, gn, final_norm):
    t, d = x1.shape
    tm = TM_POST
    return pl.pallas_call(
        functools.partial(_combine_kernel, final_norm=final_norm),
        grid=(t // tm,),
        in_specs=[pl.BlockSpec((tm, d), lambda i: (i, 0)),
                  pl.BlockSpec((SLOTS, d), lambda i: (i, 0)),
                  pl.BlockSpec((tm, LANES), lambda i: (i, 0)),
                  pl.BlockSpec((1, d), lambda i: (0, 0))],
        out_specs=pl.BlockSpec((tm, d), lambda i: (i, 0)),
        out_shape=jax.ShapeDtypeStruct((t, d), F32),
        compiler_params=pltpu.CompilerParams(
            dimension_semantics=("parallel",), vmem_limit_bytes=V7X_VMEM_LIMIT_BYTES),
        name="combine",
    )(x1, yg, meta, gn)


def kernel(x, norm_mix, w_in, b_in, sinks, rel_bias, w_branch_a, w_branch_b, w_out, norm_ffn,
           router_w, router_b, w_gate, b_gate, w_up, b_up, w_down, b_down, norm_final):
    b, s, d = x.shape
    t = b * s
    depth = w_in.shape[0]
    a_q_w = A_KV_HEADS * A_GROUP * HEAD_DIM
    a_kv_w = A_KV_HEADS * HEAD_DIM
    b_w = B_HEADS * HEAD_DIM
    n_grp = len(B_GROUPS)
    dils = tuple(dil for _, dil in B_GROUPS)
    segs, col = [], 0
    for kind, width in (("q", a_q_w), ("k", a_kv_w), ("vT", a_kv_w)):
        segs.append((col, width, kind, 1))
        col += width
    for kind in ("q", "k", "v"):
        for dil in dils:
            segs.append((col, b_w, kind, dil))
            col += b_w
    for _ in range(2):
        segs.append((col, d, "gate", 1))
        col += d
    segs = tuple(segs)
    n_a = A_KV_HEADS * A_GROUP
    nt = t // TM_POST
    max_chunks = nt * ((TM_POST * TOP_K + N_EXPERTS * (CHUNK - 1)) // CHUNK)
    n_tiles = -(-(max_chunks + N_EXPERTS * (CHUNKS_PER_TILE - 1)) // CHUNKS_PER_TILE)

    bias_a = _band_bias(rel_bias[:, :n_a], A_HALF_WINDOW, TQ_ATTN, 1)
    bias_b = [_band_bias(rel_bias[:, n_a + gi * B_HEADS:n_a + (gi + 1) * B_HEADS],
                         win // (2 * dil), TQ_ATTN, dil) for gi, (win, dil) in enumerate(B_GROUPS)]

    x2d = x.reshape(t, d)
    for layer in range(depth):
        proj = _inproj(x2d, norm_mix[layer][None], w_in[layer].astype(BF16), b_in[layer][None], segs)
        qa, ka, vat = proj[:3]
        qb, kb, vb = proj[3:3 + n_grp], proj[3 + n_grp:3 + 2 * n_grp], proj[3 + 2 * n_grp:3 + 3 * n_grp]
        ga, gb = proj[-2:]

        yat = _band_attn_gqa(qa.reshape(b, s, a_q_w), ka.reshape(b, s, a_kv_w), vat, bias_a, sinks[layer],
                             n_kv=A_KV_HEADS, grp=A_GROUP, hw=A_HALF_WINDOW, tq=TQ_ATTN, block=BLOCK_ATTN_A)
        os_, ls_ = [], []
        for gi, (win, dil) in enumerate(B_GROUPS):
            sub = lambda a: a.reshape(b, s // dil, dil * b_w)
            o, lse = _band_attn_heads(sub(qb[gi]), sub(kb[gi]), sub(vb[gi]), bias_b[gi], n_heads=B_HEADS,
                                      hw=win // (2 * dil), tq=TQ_ATTN, block=BLOCK_ATTN_B, reps=dil)
            os_.append(o.reshape(t // dil, dil * b_w))
            ls_.append(lse.reshape(t // dil, dil * b_w))

        rwt = router_w[layer].T
        rwt_hi = rwt.astype(BF16)
        rwt_split = jnp.stack([rwt_hi, (rwt - rwt_hi.astype(F32)).astype(BF16)])
        x1, xg, meta, segs_out = _post(
            x2d, yat, os_, ls_, dils, ga, gb,
            w_branch_a[layer].astype(BF16), w_branch_b[layer].astype(BF16), w_out[layer].astype(BF16),
            norm_ffn[layer][None], rwt_split, router_b[layer][:, None])

        segs3 = segs_out.reshape(nt, N_EXPERTS, LANES)
        te, nxt, n_active, src, dst = _expert_tiles(segs3[:, :, 0].astype(I32), segs3[:, :, 1].astype(I32), n_tiles)
        yg = _experts(xg, te, nxt, n_active, src, dst, w_gate[layer], b_gate[layer][:, None], w_up[layer],
                      b_up[layer][:, None], w_down[layer], b_down[layer][:, None])
        x2d = _combine(x1, yg, meta, norm_final[None], layer == depth - 1)
    return x2d.reshape(b, s, d)
```

```python
import functools

import numpy as np
import jax
import jax.numpy as jnp
from jax import lax
from jax.experimental import pallas as pl
from jax.experimental.pallas import tpu as pltpu

F32 = jnp.float32
BF16 = jnp.bfloat16
I32 = jnp.int32

HEAD_DIM = 64
A_KV_HEADS = 4
A_GROUP = 4
A_HALF_WINDOW = 128
B_GROUPS = ((128, 1), (512, 4), (2048, 16))
B_HEADS = 4
N_BUCKETS = 32
MAX_DISTANCE = 1024
N_EXPERTS = 32
TOP_K = 4
SWIGLU_LIMIT = 7.0
SWIGLU_ALPHA = 1.702
EPS = 1e-5
NEG_INF = -1e30
LOG2E = 1.4426950408889634

V7X_VMEM_LIMIT_BYTES = 56 * 1024 * 1024
LANES = 128
BF16_SUBLANES = 16

TM_PROJ = 1024
TQ_ATTN = 128
BLOCK_ATTN_A = 512
BLOCK_ATTN_B = 512
TM_POST = 512
TM_EXPERT = 512

CHUNK = BF16_SUBLANES
CHUNKS_PER_TILE = TM_EXPERT // CHUNK
SLOTS = -(-(TM_POST * TOP_K + N_EXPERTS * (CHUNK - 1)) // LANES) * LANES
SLACK_ROWS = SLOTS - TM_POST * TOP_K
assert (2 * TM_EXPERT) % SLACK_ROWS == 0
PERM_BLOCK = 512
assert SLOTS % PERM_BLOCK == 0


def _t5_buckets(rel):
    half = N_BUCKETS // 2
    max_exact = half // 2
    ret = np.where(rel > 0, half, 0)
    n = np.abs(rel)
    large = max_exact + (np.log(np.maximum(n, 1) / max_exact)
                         / np.log(MAX_DISTANCE / max_exact) * (half - max_exact)).astype(np.int32)
    large = np.minimum(large, half - 1)
    return (ret + np.where(n < max_exact, n, large)).astype(np.int32)


def _rms(x, g):
    return x * lax.rsqrt(jnp.mean(x * x, axis=-1, keepdims=True) + EPS) * g


def _inproj_kernel(x_ref, g_ref, w_ref, b_ref, *refs, segs):
    out_refs, scr = refs[:-1], refs[-1]
    h = _rms(x_ref[...], g_ref[...]).astype(BF16)

    def project(n):
        c0, width = segs[n][:2]
        return jnp.dot(h, w_ref[:, c0:c0 + width], preferred_element_type=F32) + b_ref[:, c0:c0 + width]

    acc_next = project(0)
    for n, (ref, (c0, width, kind, dil)) in enumerate(zip(out_refs, segs)):
        acc = acc_next
        if n + 1 < len(segs):
            acc_next = project(n + 1)
        if kind == "q":
            acc = acc * (HEAD_DIM ** -0.5 * LOG2E)
        elif kind == "gate":
            acc = jax.nn.sigmoid(acc)
        if kind == "vT":
            ref[...] = acc.T.astype(ref.dtype)
        elif dil == 1:
            ref[...] = acc.astype(ref.dtype)
        else:
            n = acc.shape[0] // dil
            for c in range(width // LANES):
                scr[c] = acc[:, c * LANES:(c + 1) * LANES]
            for r in range(dil):
                for c in range(width // LANES):
                    col = r * width + c * LANES
                    ref[:, col:col + LANES] = scr[c, pl.ds(r, n, stride=dil), :].astype(ref.dtype)


def _inproj(x2d, g, w_bf16, b, segs):
    t, d = x2d.shape
    n = w_bf16.shape[1]
    max_w = max(w for _, w, _, dil in segs if dil > 1)
    out_specs, out_shape = [], []
    for _, w, kind, dil in segs:
        if kind == "vT":
            out_specs.append(pl.BlockSpec((w, TM_PROJ), lambda i: (0, i)))
            out_shape.append(jax.ShapeDtypeStruct((w, t), BF16))
        else:
            out_specs.append(pl.BlockSpec((TM_PROJ // dil, dil * w), lambda i: (i, 0)))
            out_shape.append(jax.ShapeDtypeStruct((t // dil, dil * w), BF16))
    return pl.pallas_call(
        functools.partial(_inproj_kernel, segs=segs),
        grid=(t // TM_PROJ,),
        in_specs=[
            pl.BlockSpec((TM_PROJ, d), lambda i: (i, 0)),
            pl.BlockSpec((1, d), lambda i: (0, 0)),
            pl.BlockSpec((d, n), lambda i: (0, 0), pipeline_mode=pl.Buffered(1)),
            pl.BlockSpec((1, n), lambda i: (0, 0)),
        ],
        out_specs=out_specs,
        out_shape=out_shape,
        scratch_shapes=[pltpu.VMEM((max_w // LANES, TM_PROJ, LANES), F32)],
        compiler_params=pltpu.CompilerParams(
            dimension_semantics=("parallel",), vmem_limit_bytes=V7X_VMEM_LIMIT_BYTES),
        name="inproj",
    )(x2d, g, w_bf16, b)


def _band_bias(table, hw, tq, dil):
    tk = tq + 2 * hw
    off = np.arange(tk)[None, :] - hw - np.arange(tq)[:, None]
    band = np.abs(off) <= hw
    col = np.arange(tk)[None, :]
    masks = np.stack([band & (col >= hw), band, band & (col < hw + tq)])
    onehot = (_t5_buckets(off * dil)[..., None] == np.arange(N_BUCKETS)).astype(np.float32)
    bias = jnp.einsum("qkn,nh->hqk", jnp.asarray(onehot), table.astype(F32),
                      precision=lax.Precision.HIGHEST)
    return jnp.where(masks[:, None], bias[None] * LOG2E, NEG_INF)


def _band_attn_gqa_kernel(q_ref, kp, kc, kn, vtp, vtc, vtn, bias_ref, sink_ref, o_ref, *, n_kv, grp, tq, hw):
    j, nt = pl.program_id(1), pl.num_programs(1)
    k = jnp.concatenate([kp[0], kc[0], kn[0]], axis=0)
    vt = jnp.concatenate([vtp[...], vtc[...], vtn[...]], axis=1)
    n_sub = q_ref.shape[1] // tq
    tk = tq + 2 * hw
    ones = jnp.ones((BF16_SUBLANES, tk), BF16)
    lane = lax.broadcasted_iota(I32, (1, grp * tq), 1)
    sinks = []
    for h in range(n_kv):
        sk = jnp.full((1, grp * tq), sink_ref[h * grp] * LOG2E, F32)
        for g in range(1, grp):
            sk = jnp.where(lane >= g * tq, sink_ref[h * grp + g] * LOG2E, sk)
        sinks.append(sk)
    units = []
    for sub in range(n_sub):
        var = 1
        if sub == 0:
            var = jnp.where(j == 0, 0, var)
        if sub == n_sub - 1:
            var = jnp.where(j == nt - 1, 2, var)
        units.extend((var, sub * tq, h) for h in range(n_kv))

    def scores(var, r0, h):
        k_h = k[r0:r0 + tk, h * HEAD_DIM:(h + 1) * HEAD_DIM]
        c0 = h * grp * HEAD_DIM
        q = jnp.concatenate([q_ref[0, r0:r0 + tq, c0 + g * HEAD_DIM:c0 + (g + 1) * HEAD_DIM]
                             for g in range(grp)], axis=0)
        st = lax.dot_general(k_h, q, (((1,), (1,)), ((), ())), preferred_element_type=F32)
        return st + bias_ref[var, h]

    st_next = scores(*units[0])
    for n, (var, r0, h) in enumerate(units):
        st = st_next
        if n + 1 < len(units):
            st_next = scores(*units[n + 1])
        sk = sinks[h]
        m = jnp.maximum(jnp.max(st, axis=0, keepdims=True), sk)
        pt = jnp.exp2((st - m).astype(BF16))
        vt_h = jnp.concatenate([vt[h * HEAD_DIM:(h + 1) * HEAD_DIM, r0:r0 + tk], ones], axis=0)
        ot = jnp.dot(vt_h, pt, preferred_element_type=F32)
        l = ot[HEAD_DIM:HEAD_DIM + 1] + jnp.exp2(sk - m)
        o = ot[:HEAD_DIM] / l
        for g in range(grp):
            c = (h * grp + g) * HEAD_DIM
            o_ref[c:c + HEAD_DIM, r0:r0 + tq] = o[:, g * tq:(g + 1) * tq].astype(o_ref.dtype)


def _band_attn_gqa(q, k, vt, bias3, sinks, *, n_kv, grp, hw, tq, block):
    b, s, _ = q.shape
    hq = n_kv * grp
    qw, kw = hq * HEAD_DIM, n_kv * HEAD_DIM
    nt = s // block
    ratio = block // hw
    nhw = s // hw
    tk = tq + 2 * hw
    assert nt * (block // tq) >= 2
    prev_j = lambda j: jnp.maximum(j * ratio - 1, 0)
    next_j = lambda j: jnp.minimum((j + 1) * ratio, nhw - 1)
    bias_t = jnp.swapaxes(bias3.reshape(3, n_kv, grp * tq, tk), 2, 3)
    return pl.pallas_call(
        functools.partial(_band_attn_gqa_kernel, n_kv=n_kv, grp=grp, tq=tq, hw=hw),
        grid=(b, nt),
        in_specs=[
            pl.BlockSpec((1, block, qw), lambda bi, j: (bi, j, 0)),
            pl.BlockSpec((1, hw, kw), lambda bi, j: (bi, prev_j(j), 0)),
            pl.BlockSpec((1, block, kw), lambda bi, j: (bi, j, 0)),
            pl.BlockSpec((1, hw, kw), lambda bi, j: (bi, next_j(j), 0)),
            pl.BlockSpec((kw, hw), lambda bi, j: (0, bi * nhw + prev_j(j))),
            pl.BlockSpec((kw, block), lambda bi, j: (0, bi * nt + j)),
            pl.BlockSpec((kw, hw), lambda bi, j: (0, bi * nhw + next_j(j))),
            pl.BlockSpec((3, n_kv, tk, grp * tq), lambda bi, j: (0, 0, 0, 0), pipeline_mode=pl.Buffered(1)),
            pl.BlockSpec(memory_space=pltpu.SMEM),
        ],
        out_specs=pl.BlockSpec((qw, block), lambda bi, j: (0, bi * nt + j)),
        out_shape=jax.ShapeDtypeStruct((qw, b * s), BF16),
        compiler_params=pltpu.CompilerParams(
            dimension_semantics=("parallel", "parallel"), vmem_limit_bytes=V7X_VMEM_LIMIT_BYTES),
        name="band_attn_a",
    )(q, k, k, k, vt, vt, vt, bias_t, sinks)


def _band_attn_heads_kernel(q_ref, kp, kc, kn, vp, vc, vn, bias_ref, o_ref, lse_ref, *, n_heads, tq, hw):
    j, nt = pl.program_id(2), pl.num_programs(2)
    k = jnp.concatenate([kp[0], kc[0], kn[0]], axis=0)
    v = jnp.concatenate([vp[0], vc[0], vn[0]], axis=0)
    n_sub = q_ref.shape[1] // tq
    tk = tq + 2 * hw
    width = n_heads * HEAD_DIM
    lane_head = lax.broadcasted_iota(I32, (tq, width), 1) // HEAD_DIM
    ones = jnp.ones((BF16_SUBLANES, tk), BF16)

    def scores(sub):
        var = 1
        if sub == 0:
            var = jnp.where(j == 0, 0, var)
        if sub == n_sub - 1:
            var = jnp.where(j == nt - 1, 2, var)
        r0 = sub * tq
        q = q_ref[0, r0:r0 + tq, :]
        q_bd = jnp.concatenate([jnp.where(lane_head == h, q, jnp.zeros_like(q)) for h in range(n_heads)], axis=0)
        st = lax.dot_general(k[r0:r0 + tk], q_bd, (((1,), (1,)), ((), ())), preferred_element_type=F32)
        return st + bias_ref[var, 0]

    st_next = scores(0)
    for sub in range(n_sub):
        r0 = sub * tq
        st = st_next
        if sub + 1 < n_sub:
            st_next = scores(sub + 1)
        m = jnp.max(st, axis=0, keepdims=True)
        pt = jnp.exp2((st - m).astype(BF16))
        vt = jnp.concatenate([v[r0:r0 + tk].T, ones], axis=0)
        ot = jnp.dot(vt, pt, preferred_element_type=F32)
        l = ot[width:width + 1]
        lse = m + jnp.log2(l)
        o_sel = jnp.concatenate([ot[h * HEAD_DIM:(h + 1) * HEAD_DIM, h * tq:(h + 1) * tq] / l[:, h * tq:(h + 1) * tq]
                                 for h in range(n_heads)], axis=0)
        lse_sel = jnp.concatenate([jnp.broadcast_to(lse[:, h * tq:(h + 1) * tq], (HEAD_DIM, tq))
                                   for h in range(n_heads)], axis=0)
        o_ref[0, r0:r0 + tq, :] = o_sel.T.astype(o_ref.dtype)
        lse_ref[0, r0:r0 + tq, :] = lse_sel.T


def _band_attn_heads(q, k, v, bias3, *, n_heads, hw, tq, block, reps):
    b, l, _ = q.shape
    qw = kw = n_heads * HEAD_DIM
    nt = l // block
    ratio = block // hw
    nhw = l // hw
    tk = tq + 2 * hw
    assert nt * (block // tq) >= 2

    prev = lambda bi, r, j: (bi, jnp.maximum(j * ratio - 1, 0), r)
    cur = lambda bi, r, j: (bi, j, r)
    nxt = lambda bi, r, j: (bi, jnp.minimum((j + 1) * ratio, nhw - 1), r)
    in_specs = [
        pl.BlockSpec((1, block, qw), cur),
        pl.BlockSpec((1, hw, kw), prev), pl.BlockSpec((1, block, kw), cur), pl.BlockSpec((1, hw, kw), nxt),
        pl.BlockSpec((1, hw, kw), prev), pl.BlockSpec((1, block, kw), cur), pl.BlockSpec((1, hw, kw), nxt),
        pl.BlockSpec((3, 1, tk, n_heads * tq), lambda bi, r, j: (0, 0, 0, 0), pipeline_mode=pl.Buffered(1)),
    ]
    return pl.pallas_call(
        functools.partial(_band_attn_heads_kernel, n_heads=n_heads, tq=tq, hw=hw),
        grid=(b, reps, nt),
        in_specs=in_specs,
        out_specs=[pl.BlockSpec((1, block, qw), cur), pl.BlockSpec((1, block, qw), cur)],
        out_shape=[jax.ShapeDtypeStruct((b, l, reps * qw), BF16), jax.ShapeDtypeStruct((b, l, reps * qw), F32)],
        compiler_params=pltpu.CompilerParams(
            dimension_semantics=("parallel", "parallel", "parallel"),
            vmem_limit_bytes=V7X_VMEM_LIMIT_BYTES),
        name="band_attn_b",
    )(q, k, k, k, v, v, v, jnp.swapaxes(bias3.reshape(3, 1, n_heads * tq, tk), 2, 3))


def _token_order(ref, scr, dil):
    if dil == 1:
        return ref[...].astype(F32)
    n = ref.shape[0]
    width = ref.shape[1] // dil
    for r in range(dil):
        for c in range(width // LANES):
            col = r * width + c * LANES
            scr[c, pl.ds(r, n, stride=dil), :] = ref[:, col:col + LANES].astype(F32)
    return jnp.concatenate([scr[c] for c in range(width // LANES)], axis=1)


def _post_kernel(x_ref, ya_ref, o1, o2, o3, l1, l2, l3, ga_ref, gb_ref, wa_ref, wb_ref, wo_ref,
                 gf_ref, rwt_ref, rb_ref, tri_ref, x1_ref, xg_ref, meta_ref, segs_ref, *scrs, dils):
    os_ = [_token_order(r, scrs[2 * i], dil) for i, (r, dil) in enumerate(zip((o1, o2, o3), dils))]
    ls = [_token_order(r, scrs[2 * i + 1], dil) for i, (r, dil) in enumerate(zip((l1, l2, l3), dils))]
    mx = jnp.maximum(jnp.maximum(ls[0], ls[1]), ls[2])
    es = [jnp.exp2(l - mx) for l in ls]
    den = es[0] + es[1] + es[2]
    yb = (es[0] * os_[0] + es[1] * os_[1] + es[2] * os_[2]) / den
    za = lax.dot_general(ya_ref[...], wa_ref[...], (((0,), (0,)), ((), ())), preferred_element_type=F32)
    zb = jnp.dot(yb.astype(BF16), wb_ref[...], preferred_element_type=F32)
    merged = ga_ref[...].astype(F32) * za + gb_ref[...].astype(F32) * zb
    x1 = x_ref[...] + jnp.dot(merged.astype(BF16), wo_ref[...], preferred_element_type=F32)
    x1_ref[...] = x1
    h2 = _rms(x1, gf_ref[...])

    h_hi = h2.astype(BF16)
    h_lo = (h2 - h_hi.astype(F32)).astype(BF16)
    nt_dot = lambda a, b_: lax.dot_general(a, b_, (((1,), (1,)), ((), ())), preferred_element_type=F32)
    n_e = rwt_ref.shape[1]
    both = nt_dot(rwt_ref[...].reshape(2 * n_e, -1), h_hi)
    logits = (both[:n_e] + nt_dot(rwt_ref[0], h_lo) + both[n_e:]) + rb_ref[...]
    n_e, tm = logits.shape
    iota_e = lax.broadcasted_iota(I32, (n_e, tm), 0)
    work = logits
    vals, hots = [], []
    for _ in range(TOP_K):
        mk = jnp.max(work, axis=0, keepdims=True)
        ik = jnp.min(jnp.where(work == mk, iota_e, n_e), axis=0, keepdims=True)
        hot = iota_e == ik
        vals.append(mk)
        hots.append(hot)
        work = jnp.where(hot, -jnp.inf, work)
    exps = [jnp.exp(vk - vals[0]) for vk in vals]
    tot = exps[0] + exps[1] + exps[2] + exps[3]
    probs = [ek / tot for ek in exps]

    sel = jnp.zeros((n_e, tm), F32)
    for hot in hots:
        sel = sel + jnp.where(hot, 1.0, 0.0)
    cnt = jnp.sum(sel, axis=1, keepdims=True)
    pcnt = jnp.floor((cnt + (CHUNK - 1)) / CHUNK) * CHUNK
    r_i = lax.broadcasted_iota(I32, (n_e, n_e), 0)
    c_i = lax.broadcasted_iota(I32, (n_e, n_e), 1)
    pcnt_row = jnp.sum(jnp.where(r_i == c_i, pcnt, 0.0), axis=0, keepdims=True)
    seg_off = jnp.sum(jnp.where(c_i < r_i, pcnt_row, 0.0), axis=1, keepdims=True)
    before = jnp.dot(sel.astype(BF16), tri_ref[...], preferred_element_type=F32)
    slot_of = seg_off + before
    slots = [jnp.sum(jnp.where(hot, slot_of, 0.0), axis=0, keepdims=True) for hot in hots]

    iota_s = lax.broadcasted_iota(I32, (PERM_BLOCK, tm), 0)
    slots_i = [sk.astype(I32) for sk in slots]

    def one_hot_block(blk):
        perm = jnp.zeros((PERM_BLOCK, tm), F32)
        for sk in slots_i:
            perm = jnp.where(iota_s == sk - blk * PERM_BLOCK, 1.0, perm)
        return perm.astype(BF16)

    n_blk = SLOTS // PERM_BLOCK
    perm_next = one_hot_block(0)
    for blk in range(n_blk):
        perm = perm_next
        if blk + 1 < n_blk:
            perm_next = one_hot_block(blk + 1)
        xg_ref[blk * PERM_BLOCK:(blk + 1) * PERM_BLOCK, :] = jnp.dot(
            perm, h_hi, preferred_element_type=F32).astype(BF16)

    rows = slots + probs
    meta_t = jnp.concatenate(rows + [jnp.zeros((LANES - len(rows), tm), F32)], axis=0)
    meta_ref[...] = meta_t.T
    lane = lax.broadcasted_iota(I32, (n_e, LANES), 1)
    segs_ref[...] = jnp.where(lane == 0, pcnt, jnp.where(lane == 1, seg_off, 0.0))


def _post(x2d, ya, os_, ls_, dils, ga, gb, wa, wb, wo, gf, rwt, rb):
    t, d = x2d.shape
    bw = os_[0].shape[1] // dils[0]
    tm = TM_POST
    nt = t // tm
    tri = jnp.asarray(np.triu(np.ones((tm, tm), np.float32), k=1), dtype=BF16)
    row = lambda w, dil=1: pl.BlockSpec((tm // dil, dil * w), lambda i: (i, 0))
    full = lambda a: pl.BlockSpec(a.shape, lambda i: (0,) * a.ndim)
    return pl.pallas_call(
        functools.partial(_post_kernel, dils=dils),
        grid=(nt,),
        in_specs=[row(d), pl.BlockSpec((ya.shape[0], tm), lambda i: (0, i))]
                 + [row(bw, dil) for dil in dils] + [row(bw, dil) for dil in dils]
                 + [row(d), row(d), full(wa), full(wb), full(wo), full(gf), full(rwt), full(rb), full(tri)],
        scratch_shapes=[pltpu.VMEM((bw // LANES, tm, LANES), F32) for _ in range(2 * len(dils))],
        out_specs=[row(d), pl.BlockSpec((SLOTS, d), lambda i: (i, 0)), row(LANES),
                   pl.BlockSpec((N_EXPERTS, LANES), lambda i: (i, 0))],
        out_shape=[jax.ShapeDtypeStruct((t, d), F32), jax.ShapeDtypeStruct((nt * SLOTS, d), BF16),
                   jax.ShapeDtypeStruct((t, LANES), F32), jax.ShapeDtypeStruct((nt * N_EXPERTS, LANES), F32)],
        compiler_params=pltpu.CompilerParams(
            dimension_semantics=("parallel",), vmem_limit_bytes=V7X_VMEM_LIMIT_BYTES),
        name="post_attn",
    )(x2d, ya, *os_, *ls_, ga, gb, wa, wb, wo, gf, rwt, rb, tri)


def _expert_tiles(pcnt, seg_off, n_tiles):
    nt, n_e = pcnt.shape
    cpt = CHUNKS_PER_TILE
    nch = (pcnt // CHUNK).T
    cum = jnp.cumsum(nch, axis=1)
    total = cum[:, -1]
    tiles_e = (total + cpt - 1) // cpt
    tile_end = jnp.cumsum(tiles_e)
    n_active = tile_end[-1]
    i = jnp.arange(n_tiles, dtype=I32)
    last = jnp.minimum(i, n_active - 1)
    te = jnp.sum((last[:, None] >= tile_end[None, :]).astype(I32), axis=1)
    hot_e = (te[:, None] == jnp.arange(n_e, dtype=I32)[None, :])
    pick = lambda tab: jnp.sum(jnp.where(hot_e[:, :, None], tab[None], 0), axis=1)
    tile_start = jnp.sum(jnp.where(hot_e, (tile_end - tiles_e)[None, :], 0), axis=1)
    total_t = jnp.sum(jnp.where(hot_e, total[None, :], 0), axis=1)
    q = (last - tile_start)[:, None] * cpt + jnp.arange(cpt, dtype=I32)[None, :]
    valid = (q < total_t[:, None]) & (i < n_active)[:, None]
    cum_t, nch_t = pick(cum), pick(nch)
    chunk0_t = pick((seg_off.T + jnp.arange(nt, dtype=I32)[None, :] * SLOTS) // CHUNK)
    jj = jnp.sum((q[:, :, None] >= cum_t[:, None, :]).astype(I32), axis=2)
    hot_j = jj[:, :, None] == jnp.arange(nt, dtype=I32)[None, None, :]
    first = jnp.sum(jnp.where(hot_j, (cum_t - nch_t)[:, None, :], 0), axis=2)
    base = jnp.sum(jnp.where(hot_j, chunk0_t[:, None, :], 0), axis=2)
    src = jnp.where(valid, base + q - first, 0)
    trash = nt * SLOTS // CHUNK + (i % 2)[:, None] * cpt + jnp.arange(cpt, dtype=I32)[None, :]
    dst = jnp.where(valid, src, trash)
    group_end = jnp.sum(jnp.where(hot_e, tile_end[None, :], 0), axis=1)
    nxt = jnp.sum((group_end[:, None] >= tile_end[None, :]).astype(I32), axis=1)
    nxt = jnp.where(group_end < n_active, nxt, -1)
    return (te.astype(I32), nxt.astype(I32), n_active.astype(I32)[None],
            src.reshape(-1).astype(I32), dst.reshape(-1).astype(I32))


def _expert_kernel(te_ref, nx_ref, na_ref, cs_ref, cd_ref, xg_hbm, wg_hbm, bg_ref, wu_hbm, bu_ref, wd_hbm,
                   bd_ref, yg_hbm, xbuf, ybuf, wst, wbf, zbuf, sem_in, sem_out, sem_zero, sem_w, *,
                   n_token_tiles):
    i = pl.program_id(0)
    n_active = na_ref[0]
    slot = i % 2
    cpt = CHUNKS_PER_TILE
    w_hbm = (wg_hbm, wu_hbm, wd_hbm)

    def w_copy(n, e):
        return pltpu.make_async_copy(w_hbm[n].at[e], wst.at[n], sem_w.at[n])

    zero_starts = [j * SLOTS + TM_POST * TOP_K for j in range(n_token_tiles)]
    zero_starts += [n_token_tiles * SLOTS + j * SLACK_ROWS for j in range(2 * TM_EXPERT // SLACK_ROWS)]

    def zero_copy(n):
        return pltpu.make_async_copy(
            zbuf, yg_hbm.at[pl.ds(zero_starts[n] // CHUNK, SLACK_ROWS // CHUNK)], sem_zero.at[n])

    @pl.when(i == 0)
    def _():
        zbuf[...] = jnp.zeros_like(zbuf)
        for n in range(len(zero_starts)):
            zero_copy(n).start()

    def in_copy(c, sl, chunk):
        return pltpu.make_async_copy(xg_hbm.at[chunk], xbuf.at[sl, c], sem_in.at[sl])

    def out_copy(c, sl, chunk):
        return pltpu.make_async_copy(ybuf.at[sl, c], yg_hbm.at[chunk], sem_out.at[sl])

    def start_gather(tile, sl):
        for c in range(cpt):
            in_copy(c, sl, cs_ref[tile * cpt + c]).start()

    def wait_gather(sl):
        pltpu.make_async_copy(xg_hbm.at[pl.ds(0, cpt)], xbuf.at[sl], sem_in.at[sl]).wait()

    def wait_scatter(sl):
        pltpu.make_async_copy(ybuf.at[sl], yg_hbm.at[pl.ds(0, cpt)], sem_out.at[sl]).wait()

    @pl.when(i == 0)
    def _():
        start_gather(0, 0)
        for n in range(len(w_hbm)):
            w_copy(n, te_ref[0]).start()

    @pl.when(i + 1 < n_active)
    def _():
        start_gather(i + 1, 1 - slot)

    @pl.when(i < n_active)
    def _():
        wait_gather(slot)

        @pl.when((i == 0) | (te_ref[i] != te_ref[jnp.maximum(i - 1, 0)]))
        def _():
            for n in range(len(w_hbm)):
                w_copy(n, 0).wait()
                wbf[n] = wst[n].astype(BF16)

                @pl.when(nx_ref[i] >= 0)
                def _():
                    w_copy(n, nx_ref[i]).start()

        @pl.when(i >= 1)
        def _():
            for c in range(cpt):
                out_copy(c, 1 - slot, cd_ref[(i - 1) * cpt + c]).start()

        x = xbuf[slot].reshape(TM_EXPERT, -1)
        e = te_ref[i]
        gate = jnp.minimum(jnp.dot(x, wbf[0], preferred_element_type=F32) + bg_ref[e], SWIGLU_LIMIT)
        up = jnp.clip(jnp.dot(x, wbf[1], preferred_element_type=F32) + bu_ref[e], -SWIGLU_LIMIT, SWIGLU_LIMIT)
        act = gate * jax.nn.sigmoid(SWIGLU_ALPHA * gate) * (up + 1.0)
        y = jnp.dot(act.astype(BF16), wbf[2], preferred_element_type=F32) + bd_ref[e]

        @pl.when(i >= 2)
        def _():
            wait_scatter(slot)

        ybuf[slot] = y.astype(BF16).reshape(ybuf.shape[1:])

        @pl.when(i == 0)
        def _():
            for n in range(len(zero_starts)):
                zero_copy(n).wait()

        @pl.when(i == n_active - 1)
        def _():
            for c in range(cpt):
                out_copy(c, slot, cd_ref[i * cpt + c]).start()
            wait_scatter(slot)

            @pl.when(i >= 1)
            def _():
                wait_scatter(1 - slot)


def _experts(xg, te, nxt, n_active, src, dst, wg, bg, wu, bu, wd, bd):
    rows, d = xg.shape
    n_e, _, dff = wg.shape
    assert d == dff
    n_tiles = te.shape[0]
    n_token_tiles = rows // SLOTS
    n_zero = n_token_tiles + 2 * TM_EXPERT // SLACK_ROWS
    b_spec = pl.BlockSpec((n_e, 1, d), lambda i, *_: (0, 0, 0))
    hbm = pl.BlockSpec(memory_space=pl.ANY)
    tile_bufs = pltpu.VMEM((2, CHUNKS_PER_TILE, CHUNK, d), BF16)
    yg = pl.pallas_call(
        functools.partial(_expert_kernel, n_token_tiles=n_token_tiles),
        grid_spec=pltpu.PrefetchScalarGridSpec(
            num_scalar_prefetch=5,
            grid=(n_tiles,),
            in_specs=[hbm, hbm, b_spec, hbm, b_spec, hbm, b_spec],
            out_specs=hbm,
            scratch_shapes=[tile_bufs, tile_bufs,
                            pltpu.VMEM((3, d, dff), F32), pltpu.VMEM((3, d, dff), BF16),
                            pltpu.VMEM((SLACK_ROWS // CHUNK, CHUNK, d), BF16),
                            pltpu.SemaphoreType.DMA((2,)),
                            pltpu.SemaphoreType.DMA((2,)),
                            pltpu.SemaphoreType.DMA((n_zero,)),
                            pltpu.SemaphoreType.DMA((3,))]),
        out_shape=jax.ShapeDtypeStruct(((rows + 2 * TM_EXPERT) // CHUNK, CHUNK, d), BF16),
        compiler_params=pltpu.CompilerParams(
            dimension_semantics=("arbitrary",), vmem_limit_bytes=V7X_VMEM_LIMIT_BYTES),
        name="experts",
    )(te, nxt, n_active, src, dst, xg.reshape(rows // CHUNK, CHUNK, d), wg, bg, wu, bu, wd, bd)
    return yg.reshape(rows + 2 * TM_EXPERT, d)


def _combine_kernel(x1_ref, yg_ref, meta_ref, gn_ref, o_ref, *, final_norm):
    meta = meta_ref[...]
    tm = meta.shape[0]
    slots_i = [meta[:, k:k + 1].astype(I32) for k in range(TOP_K)]
    probs = [meta[:, TOP_K + k:TOP_K + k + 1] for k in range(TOP_K)]
    iota_s = lax.broadcasted_iota(I32, (tm, PERM_BLOCK), 1)
    xo = x1_ref[...]
    def weight_block(blk):
        wperm = jnp.zeros((tm, PERM_BLOCK), F32)
        for sk, pk in zip(slots_i, probs):
            wperm = jnp.where(iota_s == sk - blk * PERM_BLOCK, pk, wperm)
        return wperm.astype(BF16)

    n_blk = SLOTS // PERM_BLOCK
    w_next = weight_block(0)
    for blk in range(n_blk):
        w = w_next
        if blk + 1 < n_blk:
            w_next = weight_block(blk + 1)
        xo = xo + jnp.dot(w, yg_ref[blk * PERM_BLOCK:(blk + 1) * PERM_BLOCK, :], preferred_element_type=F32)
    o_ref[...] = _rms(xo, gn_ref[...]) if final_norm else xo


def _combine(x1, yg, meta, gn, final_norm):
    t, d = x1.shape
    tm = TM_POST
    return pl.pallas_call(
        functools.partial(_combine_kernel, final_norm=final_norm),
        grid=(t // tm,),
        in_specs=[pl.BlockSpec((tm, d), lambda i: (i, 0)),
                  pl.BlockSpec((SLOTS, d), lambda i: (i, 0)),
                  pl.BlockSpec((tm, LANES), lambda i: (i, 0)),
                  pl.BlockSpec((1, d), lambda i: (0, 0))],
        out_specs=pl.BlockSpec((tm, d), lambda i: (i, 0)),
        out_shape=jax.ShapeDtypeStruct((t, d), F32),
        compiler_params=pltpu.CompilerParams(
            dimension_semantics=("parallel",), vmem_limit_bytes=V7X_VMEM_LIMIT_BYTES),
        name="combine",
    )(x1, yg, meta, gn)


def kernel(x, norm_mix, w_in, b_in, sinks, rel_bias, w_branch_a, w_branch_b, w_out, norm_ffn,
           router_w, router_b, w_gate, b_gate, w_up, b_up, w_down, b_down, norm_final):
    b, s, d = x.shape
    t = b * s
    depth = w_in.shape[0]
    a_q_w = A_KV_HEADS * A_GROUP * HEAD_DIM
    a_kv_w = A_KV_HEADS * HEAD_DIM
    b_w = B_HEADS * HEAD_DIM
    n_grp = len(B_GROUPS)
    dils = tuple(dil for _, dil in B_GROUPS)
    segs, col = [], 0
    for kind, width in (("q", a_q_w), ("k", a_kv_w), ("vT", a_kv_w)):
        segs.append((col, width, kind, 1))
        col += width
    for kind in ("q", "k", "v"):
        for dil in dils:
            segs.append((col, b_w, kind, dil))
            col += b_w
    for _ in range(2):
        segs.append((col, d, "gate", 1))
        col += d
    segs = tuple(segs)
    n_a = A_KV_HEADS * A_GROUP
    nt = t // TM_POST
    max_chunks = nt * ((TM_POST * TOP_K + N_EXPERTS * (CHUNK - 1)) // CHUNK)
    n_tiles = -(-(max_chunks + N_EXPERTS * (CHUNKS_PER_TILE - 1)) // CHUNKS_PER_TILE)

    bias_a = _band_bias(rel_bias[:, :n_a], A_HALF_WINDOW, TQ_ATTN, 1)
    bias_b = [_band_bias(rel_bias[:, n_a + gi * B_HEADS:n_a + (gi + 1) * B_HEADS],
                         win // (2 * dil), TQ_ATTN, dil) for gi, (win, dil) in enumerate(B_GROUPS)]

    x2d = x.reshape(t, d)
    for layer in range(depth):
        proj = _inproj(x2d, norm_mix[layer][None], w_in[layer].astype(BF16), b_in[layer][None], segs)
        qa, ka, vat = proj[:3]
        qb, kb, vb = proj[3:3 + n_grp], proj[3 + n_grp:3 + 2 * n_grp], proj[3 + 2 * n_grp:3 + 3 * n_grp]
        ga, gb = proj[-2:]

        yat = _band_attn_gqa(qa.reshape(b, s, a_q_w), ka.reshape(b, s, a_kv_w), vat, bias_a, sinks[layer],
                             n_kv=A_KV_HEADS, grp=A_GROUP, hw=A_HALF_WINDOW, tq=TQ_ATTN, block=BLOCK_ATTN_A)
        os_, ls_ = [], []
        for gi, (win, dil) in enumerate(B_GROUPS):
            sub = lambda a: a.reshape(b, s // dil, dil * b_w)
            o, lse = _band_attn_heads(sub(qb[gi]), sub(kb[gi]), sub(vb[gi]), bias_b[gi], n_heads=B_HEADS,
                                      hw=win // (2 * dil), tq=TQ_ATTN, block=BLOCK_ATTN_B, reps=dil)
            os_.append(o.reshape(t // dil, dil * b_w))
            ls_.append(lse.reshape(t // dil, dil * b_w))

        rwt = router_w[layer].T
        rwt_hi = rwt.astype(BF16)
        rwt_split = jnp.stack([rwt_hi, (rwt - rwt_hi.astype(F32)).astype(BF16)])
        x1, xg, meta, segs_out = _post(
            x2d, yat, os_, ls_, dils, ga, gb,
            w_branch_a[layer].astype(BF16), w_branch_b[layer].astype(BF16), w_out[layer].astype(BF16),
            norm_ffn[layer][None], rwt_split, router_b[layer][:, None])

        segs3 = segs_out.reshape(nt, N_EXPERTS, LANES)
        te, nxt, n_active, src, dst = _expert_tiles(segs3[:, :, 0].astype(I32), segs3[:, :, 1].astype(I32), n_tiles)
        yg = _experts(xg, te, nxt, n_active, src, dst, w_gate[layer], b_gate[layer][:, None], w_up[layer],
                      b_up[layer][:, None], w_down[layer], b_down[layer][:, None])
        x2d = _combine(x1, yg, meta, norm_final[None], layer == depth - 1)
    return x2d.reshape(b, s, d)
```

```python
import functools

import numpy as np
import jax
import jax.numpy as jnp
from jax import lax
from jax.experimental import pallas as pl
from jax.experimental.pallas import tpu as pltpu

F32 = jnp.float32
BF16 = jnp.bfloat16
I32 = jnp.int32

HEAD_DIM = 64
A_KV_HEADS = 4
A_GROUP = 4
A_HALF_WINDOW = 128
B_GROUPS = ((128, 1), (512, 4), (2048, 16))
B_HEADS = 4
N_BUCKETS = 32
MAX_DISTANCE = 1024
N_EXPERTS = 32
TOP_K = 4
SWIGLU_LIMIT = 7.0
SWIGLU_ALPHA = 1.702
EPS = 1e-5
NEG_INF = -1e30
LOG2E = 1.4426950408889634

V7X_VMEM_LIMIT_BYTES = 56 * 1024 * 1024
LANES = 128
BF16_SUBLANES = 16

TM_PROJ = 1024
TQ_ATTN = 128
BLOCK_ATTN_A = 1024
BLOCK_ATTN_B = 1024
TM_POST = 512
TM_EXPERT = 512

CHUNK = BF16_SUBLANES
CHUNKS_PER_TILE = TM_EXPERT // CHUNK
SLOTS = -(-(TM_POST * TOP_K + N_EXPERTS * (CHUNK - 1)) // LANES) * LANES
SLACK_ROWS = SLOTS - TM_POST * TOP_K
assert (2 * TM_EXPERT) % SLACK_ROWS == 0
PERM_BLOCK = 512
assert SLOTS % PERM_BLOCK == 0


def _t5_buckets(rel):
    half = N_BUCKETS // 2
    max_exact = half // 2
    ret = np.where(rel > 0, half, 0)
    n = np.abs(rel)
    large = max_exact + (np.log(np.maximum(n, 1) / max_exact)
                         / np.log(MAX_DISTANCE / max_exact) * (half - max_exact)).astype(np.int32)
    large = np.minimum(large, half - 1)
    return (ret + np.where(n < max_exact, n, large)).astype(np.int32)


def _rms(x, g):
    return x * lax.rsqrt(jnp.mean(x * x, axis=-1, keepdims=True) + EPS) * g


def _inproj_kernel(x_ref, g_ref, w_ref, b_ref, *refs, segs):
    out_refs, scr = refs[:-1], refs[-1]
    h = _rms(x_ref[...], g_ref[...]).astype(BF16)

    def project(n):
        c0, width = segs[n][:2]
        return jnp.dot(h, w_ref[:, c0:c0 + width], preferred_element_type=F32) + b_ref[:, c0:c0 + width]

    acc_next = project(0)
    for n, (ref, (c0, width, kind, dil)) in enumerate(zip(out_refs, segs)):
        acc = acc_next
        if n + 1 < len(segs):
            acc_next = project(n + 1)
        if kind == "q":
            acc = acc * (HEAD_DIM ** -0.5 * LOG2E)
        elif kind == "gate":
            acc = jax.nn.sigmoid(acc)
        if kind == "vT":
            ref[...] = acc.T.astype(ref.dtype)
        elif dil == 1:
            ref[...] = acc.astype(ref.dtype)
        else:
            n = acc.shape[0] // dil
            for c in range(width // LANES):
                scr[c] = acc[:, c * LANES:(c + 1) * LANES]
            for r in range(dil):
                for c in range(width // LANES):
                    col = r * width + c * LANES
                    ref[:, col:col + LANES] = scr[c, pl.ds(r, n, stride=dil), :].astype(ref.dtype)


def _inproj(x2d, g, w_bf16, b, segs):
    t, d = x2d.shape
    n = w_bf16.shape[1]
    max_w = max(w for _, w, _, dil in segs if dil > 1)
    out_specs, out_shape = [], []
    for _, w, kind, dil in segs:
        if kind == "vT":
            out_specs.append(pl.BlockSpec((w, TM_PROJ), lambda i: (0, i)))
            out_shape.append(jax.ShapeDtypeStruct((w, t), BF16))
        else:
            out_specs.append(pl.BlockSpec((TM_PROJ // dil, dil * w), lambda i: (i, 0)))
            out_shape.append(jax.ShapeDtypeStruct((t // dil, dil * w), BF16))
    return pl.pallas_call(
        functools.partial(_inproj_kernel, segs=segs),
        grid=(t // TM_PROJ,),
        in_specs=[
            pl.BlockSpec((TM_PROJ, d), lambda i: (i, 0)),
            pl.BlockSpec((1, d), lambda i: (0, 0)),
            pl.BlockSpec((d, n), lambda i: (0, 0), pipeline_mode=pl.Buffered(1)),
            pl.BlockSpec((1, n), lambda i: (0, 0)),
        ],
        out_specs=out_specs,
        out_shape=out_shape,
        scratch_shapes=[pltpu.VMEM((max_w // LANES, TM_PROJ, LANES), F32)],
        compiler_params=pltpu.CompilerParams(
            dimension_semantics=("parallel",), vmem_limit_bytes=V7X_VMEM_LIMIT_BYTES),
        name="inproj",
    )(x2d, g, w_bf16, b)


def _band_bias(table, hw, tq, dil):
    tk = tq + 2 * hw
    off = np.arange(tk)[None, :] - hw - np.arange(tq)[:, None]
    band = np.abs(off) <= hw
    col = np.arange(tk)[None, :]
    masks = np.stack([band & (col >= hw), band, band & (col < hw + tq)])
    onehot = (_t5_buckets(off * dil)[..., None] == np.arange(N_BUCKETS)).astype(np.float32)
    bias = jnp.einsum("qkn,nh->hqk", jnp.asarray(onehot), table.astype(F32),
                      precision=lax.Precision.HIGHEST)
    return jnp.where(masks[:, None], bias[None] * LOG2E, NEG_INF)


def _band_attn_gqa_kernel(q_ref, kp, kc, kn, vtp, vtc, vtn, bias_ref, sink_ref, o_ref, *, n_kv, grp, tq, hw):
    j, nt = pl.program_id(1), pl.num_programs(1)
    k = jnp.concatenate([kp[0], kc[0], kn[0]], axis=0)
    vt = jnp.concatenate([vtp[...], vtc[...], vtn[...]], axis=1)
    n_sub = q_ref.shape[1] // tq
    tk = tq + 2 * hw
    ones = jnp.ones((BF16_SUBLANES, tk), BF16)
    lane = lax.broadcasted_iota(I32, (1, grp * tq), 1)
    sinks = []
    for h in range(n_kv):
        sk = jnp.full((1, grp * tq), sink_ref[h * grp] * LOG2E, F32)
        for g in range(1, grp):
            sk = jnp.where(lane >= g * tq, sink_ref[h * grp + g] * LOG2E, sk)
        sinks.append(sk)
    units = []
    for sub in range(n_sub):
        var = 1
        if sub == 0:
            var = jnp.where(j == 0, 0, var)
        if sub == n_sub - 1:
            var = jnp.where(j == nt - 1, 2, var)
        units.extend((var, sub * tq, h) for h in range(n_kv))

    def scores(var, r0, h):
        k_h = k[r0:r0 + tk, h * HEAD_DIM:(h + 1) * HEAD_DIM]
        c0 = h * grp * HEAD_DIM
        q = jnp.concatenate([q_ref[0, r0:r0 + tq, c0 + g * HEAD_DIM:c0 + (g + 1) * HEAD_DIM]
                             for g in range(grp)], axis=0)
        st = lax.dot_general(k_h, q, (((1,), (1,)), ((), ())), preferred_element_type=F32)
        return st + bias_ref[var, h]

    st_next = scores(*units[0])
    for n, (var, r0, h) in enumerate(units):
        st = st_next
        if n + 1 < len(units):
            st_next = scores(*units[n + 1])
        sk = sinks[h]
        m = jnp.maximum(jnp.max(st, axis=0, keepdims=True), sk)
        pt = jnp.exp2((st - m).astype(BF16))
        vt_h = jnp.concatenate([vt[h * HEAD_DIM:(h + 1) * HEAD_DIM, r0:r0 + tk], ones], axis=0)
        ot = jnp.dot(vt_h, pt, preferred_element_type=F32)
        l = ot[HEAD_DIM:HEAD_DIM + 1] + jnp.exp2(sk - m)
        o = ot[:HEAD_DIM] / l
        for g in range(grp):
            c = (h * grp + g) * HEAD_DIM
            o_ref[c:c + HEAD_DIM, r0:r0 + tq] = o[:, g * tq:(g + 1) * tq].astype(o_ref.dtype)


def _band_attn_gqa(q, k, vt, bias3, sinks, *, n_kv, grp, hw, tq, block):
    b, s, _ = q.shape
    hq = n_kv * grp
    qw, kw = hq * HEAD_DIM, n_kv * HEAD_DIM
    nt = s // block
    ratio = block // hw
    nhw = s // hw
    tk = tq + 2 * hw
    assert nt * (block // tq) >= 2
    prev_j = lambda j: jnp.maximum(j * ratio - 1, 0)
    next_j = lambda j: jnp.minimum((j + 1) * ratio, nhw - 1)
    bias_t = jnp.swapaxes(bias3.reshape(3, n_kv, grp * tq, tk), 2, 3)
    return pl.pallas_call(
        functools.partial(_band_attn_gqa_kernel, n_kv=n_kv, grp=grp, tq=tq, hw=hw),
        grid=(b, nt),
        in_specs=[
            pl.BlockSpec((1, block, qw), lambda bi, j: (bi, j, 0)),
            pl.BlockSpec((1, hw, kw), lambda bi, j: (bi, prev_j(j), 0)),
            pl.BlockSpec((1, block, kw), lambda bi, j: (bi, j, 0)),
            pl.BlockSpec((1, hw, kw), lambda bi, j: (bi, next_j(j), 0)),
            pl.BlockSpec((kw, hw), lambda bi, j: (0, bi * nhw + prev_j(j))),
            pl.BlockSpec((kw, block), lambda bi, j: (0, bi * nt + j)),
            pl.BlockSpec((kw, hw), lambda bi, j: (0, bi * nhw + next_j(j))),
            pl.BlockSpec((3, n_kv, tk, grp * tq), lambda bi, j: (0, 0, 0, 0), pipeline_mode=pl.Buffered(1)),
            pl.BlockSpec(memory_space=pltpu.SMEM),
        ],
        out_specs=pl.BlockSpec((qw, block), lambda bi, j: (0, bi * nt + j)),
        out_shape=jax.ShapeDtypeStruct((qw, b * s), BF16),
        compiler_params=pltpu.CompilerParams(
            dimension_semantics=("parallel", "parallel"), vmem_limit_bytes=V7X_VMEM_LIMIT_BYTES),
        name="band_attn_a",
    )(q, k, k, k, vt, vt, vt, bias_t, sinks)


def _band_attn_heads_kernel(q_ref, kp, kc, kn, vp, vc, vn, bias_ref, o_ref, lse_ref, *, n_heads, tq, hw):
    j, nt = pl.program_id(2), pl.num_programs(2)
    k = jnp.concatenate([kp[0], kc[0], kn[0]], axis=0)
    v = jnp.concatenate([vp[0], vc[0], vn[0]], axis=0)
    n_sub = q_ref.shape[1] // tq
    tk = tq + 2 * hw
    width = n_heads * HEAD_DIM
    lane_head = lax.broadcasted_iota(I32, (tq, width), 1) // HEAD_DIM
    ones = jnp.ones((BF16_SUBLANES, tk), BF16)

    units = [(c0, sub) for c0 in range(0, q_ref.shape[2], width) for sub in range(n_sub)]

    def scores(c0, sub):
        var = 1
        if sub == 0:
            var = jnp.where(j == 0, 0, var)
        if sub == n_sub - 1:
            var = jnp.where(j == nt - 1, 2, var)
        r0 = sub * tq
        q = q_ref[0, r0:r0 + tq, c0:c0 + width]
        q_bd = jnp.concatenate([jnp.where(lane_head == h, q, jnp.zeros_like(q)) for h in range(n_heads)], axis=0)
        st = lax.dot_general(k[r0:r0 + tk, c0:c0 + width], q_bd, (((1,), (1,)), ((), ())),
                             preferred_element_type=F32)
        return st + bias_ref[var, 0]

    st_next = scores(*units[0])
    for n, (c0, sub) in enumerate(units):
        r0 = sub * tq
        st = st_next
        if n + 1 < len(units):
            st_next = scores(*units[n + 1])
        m = jnp.max(st, axis=0, keepdims=True)
        pt = jnp.exp2((st - m).astype(BF16))
        vt = jnp.concatenate([v[r0:r0 + tk, c0:c0 + width].T, ones], axis=0)
        ot = jnp.dot(vt, pt, preferred_element_type=F32)
        l = ot[width:width + 1]
        lse = m + jnp.log2(l)
        o_sel = jnp.concatenate([ot[h * HEAD_DIM:(h + 1) * HEAD_DIM, h * tq:(h + 1) * tq] / l[:, h * tq:(h + 1) * tq]
                                 for h in range(n_heads)], axis=0)
        lse_sel = jnp.concatenate([jnp.broadcast_to(lse[:, h * tq:(h + 1) * tq], (HEAD_DIM, tq))
                                   for h in range(n_heads)], axis=0)
        o_ref[0, r0:r0 + tq, c0:c0 + width] = o_sel.T.astype(o_ref.dtype)
        lse_ref[0, r0:r0 + tq, c0:c0 + width] = lse_sel.T


def _band_attn_heads(q, k, v, bias3, *, n_heads, hw, tq, rows_per_step, reps):
    b, l, _ = q.shape
    block = min(rows_per_step, l)
    nt = l // block
    ratio = block // hw
    nhw = l // hw
    tk = tq + 2 * hw
    assert nt * (block // tq) >= 2
    res_per_step = max(1, min(reps, rows_per_step // block))
    qw = kw = res_per_step * n_heads * HEAD_DIM

    prev = lambda bi, r, j: (bi, jnp.maximum(j * ratio - 1, 0), r)
    cur = lambda bi, r, j: (bi, j, r)
    nxt = lambda bi, r, j: (bi, jnp.minimum((j + 1) * ratio, nhw - 1), r)
    in_specs = [
        pl.BlockSpec((1, block, qw), cur),
        pl.BlockSpec((1, hw, kw), prev), pl.BlockSpec((1, block, kw), cur), pl.BlockSpec((1, hw, kw), nxt),
        pl.BlockSpec((1, hw, kw), prev), pl.BlockSpec((1, block, kw), cur), pl.BlockSpec((1, hw, kw), nxt),
        pl.BlockSpec((3, 1, tk, n_heads * tq), lambda bi, r, j: (0, 0, 0, 0), pipeline_mode=pl.Buffered(1)),
    ]
    return pl.pallas_call(
        functools.partial(_band_attn_heads_kernel, n_heads=n_heads, tq=tq, hw=hw),
        grid=(b, reps // res_per_step, nt),
        in_specs=in_specs,
        out_specs=[pl.BlockSpec((1, block, qw), cur), pl.BlockSpec((1, block, qw), cur)],
        out_shape=[jax.ShapeDtypeStruct(q.shape, BF16), jax.ShapeDtypeStruct(q.shape, F32)],
        compiler_params=pltpu.CompilerParams(
            dimension_semantics=("parallel", "parallel", "parallel"),
            vmem_limit_bytes=V7X_VMEM_LIMIT_BYTES),
        name="band_attn_b",
    )(q, k, k, k, v, v, v, jnp.swapaxes(bias3.reshape(3, 1, n_heads * tq, tk), 2, 3))


def _token_order(ref, scr, dil):
    if dil == 1:
        return ref[...].astype(F32)
    n = ref.shape[0]
    width = ref.shape[1] // dil
    for r in range(dil):
        for c in range(width // LANES):
            col = r * width + c * LANES
            scr[c, pl.ds(r, n, stride=dil), :] = ref[:, col:col + LANES].astype(F32)
    return jnp.concatenate([scr[c] for c in range(width // LANES)], axis=1)


def _post_kernel(x_ref, ya_ref, o1, o2, o3, l1, l2, l3, ga_ref, gb_ref, wa_ref, wb_ref, wo_ref,
                 gf_ref, rwt_ref, rb_ref, tri_ref, x1_ref, xg_ref, meta_ref, segs_ref, *scrs, dils):
    os_ = [_token_order(r, scrs[2 * i], dil) for i, (r, dil) in enumerate(zip((o1, o2, o3), dils))]
    ls = [_token_order(r, scrs[2 * i + 1], dil) for i, (r, dil) in enumerate(zip((l1, l2, l3), dils))]
    mx = jnp.maximum(jnp.maximum(ls[0], ls[1]), ls[2])
    es = [jnp.exp2(l - mx) for l in ls]
    den = es[0] + es[1] + es[2]
    yb = (es[0] * os_[0] + es[1] * os_[1] + es[2] * os_[2]) / den
    za = lax.dot_general(ya_ref[...], wa_ref[...], (((0,), (0,)), ((), ())), preferred_element_type=F32)
    zb = jnp.dot(yb.astype(BF16), wb_ref[...], preferred_element_type=F32)
    merged = ga_ref[...].astype(F32) * za + gb_ref[...].astype(F32) * zb
    x1 = x_ref[...] + jnp.dot(merged.astype(BF16), wo_ref[...], preferred_element_type=F32)
    x1_ref[...] = x1
    h2 = _rms(x1, gf_ref[...])

    h_hi = h2.astype(BF16)
    h_lo = (h2 - h_hi.astype(F32)).astype(BF16)
    nt_dot = lambda a, b_: lax.dot_general(a, b_, (((1,), (1,)), ((), ())), preferred_element_type=F32)
    n_e = rwt_ref.shape[1]
    both = nt_dot(rwt_ref[...].reshape(2 * n_e, -1), h_hi)
    logits = (both[:n_e] + nt_dot(rwt_ref[0], h_lo) + both[n_e:]) + rb_ref[...]
    n_e, tm = logits.shape
    iota_e = lax.broadcasted_iota(I32, (n_e, tm), 0)
    work = logits
    vals, hots = [], []
    for _ in range(TOP_K):
        mk = jnp.max(work, axis=0, keepdims=True)
        ik = jnp.min(jnp.where(work == mk, iota_e, n_e), axis=0, keepdims=True)
        hot = iota_e == ik
        vals.append(mk)
        hots.append(hot)
        work = jnp.where(hot, -jnp.inf, work)
    exps = [jnp.exp(vk - vals[0]) for vk in vals]
    tot = exps[0] + exps[1] + exps[2] + exps[3]
    probs = [ek / tot for ek in exps]

    sel = jnp.zeros((n_e, tm), F32)
    for hot in hots:
        sel = sel + jnp.where(hot, 1.0, 0.0)
    cnt = jnp.sum(sel, axis=1, keepdims=True)
    pcnt = jnp.floor((cnt + (CHUNK - 1)) / CHUNK) * CHUNK
    r_i = lax.broadcasted_iota(I32, (n_e, n_e), 0)
    c_i = lax.broadcasted_iota(I32, (n_e, n_e), 1)
    pcnt_row = jnp.sum(jnp.where(r_i == c_i, pcnt, 0.0), axis=0, keepdims=True)
    seg_off = jnp.sum(jnp.where(c_i < r_i, pcnt_row, 0.0), axis=1, keepdims=True)
    before = jnp.dot(sel.astype(BF16), tri_ref[...], preferred_element_type=F32)
    slot_of = seg_off + before
    slots = [jnp.sum(jnp.where(hot, slot_of, 0.0), axis=0, keepdims=True) for hot in hots]

    iota_s = lax.broadcasted_iota(I32, (PERM_BLOCK, tm), 0)
    slots_i = [sk.astype(I32) for sk in slots]

    def one_hot_block(blk):
        perm = jnp.zeros((PERM_BLOCK, tm), F32)
        for sk in slots_i:
            perm = jnp.where(iota_s == sk - blk * PERM_BLOCK, 1.0, perm)
        return perm.astype(BF16)

    n_blk = SLOTS // PERM_BLOCK
    perm_next = one_hot_block(0)
    for blk in range(n_blk):
        perm = perm_next
        if blk + 1 < n_blk:
            perm_next = one_hot_block(blk + 1)
        xg_ref[blk * PERM_BLOCK:(blk + 1) * PERM_BLOCK, :] = jnp.dot(
            perm, h_hi, preferred_element_type=F32).astype(BF16)

    rows = slots + probs
    meta_t = jnp.concatenate(rows + [jnp.zeros((LANES - len(rows), tm), F32)], axis=0)
    meta_ref[...] = meta_t.T
    lane = lax.broadcasted_iota(I32, (n_e, LANES), 1)
    segs_ref[...] = jnp.where(lane == 0, pcnt, jnp.where(lane == 1, seg_off, 0.0))


def _post(x2d, ya, os_, ls_, dils, ga, gb, wa, wb, wo, gf, rwt, rb):
    t, d = x2d.shape
    bw = os_[0].shape[1] // dils[0]
    tm = TM_POST
    nt = t // tm
    tri = jnp.asarray(np.triu(np.ones((tm, tm), np.float32), k=1), dtype=BF16)
    row = lambda w, dil=1: pl.BlockSpec((tm // dil, dil * w), lambda i: (i, 0))
    full = lambda a: pl.BlockSpec(a.shape, lambda i: (0,) * a.ndim)
    return pl.pallas_call(
        functools.partial(_post_kernel, dils=dils),
        grid=(nt,),
        in_specs=[row(d), pl.BlockSpec((ya.shape[0], tm), lambda i: (0, i))]
                 + [row(bw, dil) for dil in dils] + [row(bw, dil) for dil in dils]
                 + [row(d), row(d), full(wa), full(wb), full(wo), full(gf), full(rwt), full(rb), full(tri)],
        scratch_shapes=[pltpu.VMEM((bw // LANES, tm, LANES), F32) for _ in range(2 * len(dils))],
        out_specs=[row(d), pl.BlockSpec((SLOTS, d), lambda i: (i, 0)), row(LANES),
                   pl.BlockSpec((N_EXPERTS, LANES), lambda i: (i, 0))],
        out_shape=[jax.ShapeDtypeStruct((t, d), F32), jax.ShapeDtypeStruct((nt * SLOTS, d), BF16),
                   jax.ShapeDtypeStruct((t, LANES), F32), jax.ShapeDtypeStruct((nt * N_EXPERTS, LANES), F32)],
        compiler_params=pltpu.CompilerParams(
            dimension_semantics=("parallel",), vmem_limit_bytes=V7X_VMEM_LIMIT_BYTES),
        name="post_attn",
    )(x2d, ya, *os_, *ls_, ga, gb, wa, wb, wo, gf, rwt, rb, tri)


def _expert_tiles(pcnt, seg_off, n_tiles):
    nt, n_e = pcnt.shape
    cpt = CHUNKS_PER_TILE
    nch = (pcnt // CHUNK).T
    cum = jnp.cumsum(nch, axis=1)
    total = cum[:, -1]
    tiles_e = (total + cpt - 1) // cpt
    tile_end = jnp.cumsum(tiles_e)
    n_active = tile_end[-1]
    i = jnp.arange(n_tiles, dtype=I32)
    last = jnp.minimum(i, n_active - 1)
    te = jnp.sum((last[:, None] >= tile_end[None, :]).astype(I32), axis=1)
    hot_e = (te[:, None] == jnp.arange(n_e, dtype=I32)[None, :])
    pick = lambda tab: jnp.sum(jnp.where(hot_e[:, :, None], tab[None], 0), axis=1)
    tile_start = jnp.sum(jnp.where(hot_e, (tile_end - tiles_e)[None, :], 0), axis=1)
    total_t = jnp.sum(jnp.where(hot_e, total[None, :], 0), axis=1)
    q = (last - tile_start)[:, None] * cpt + jnp.arange(cpt, dtype=I32)[None, :]
    valid = (q < total_t[:, None]) & (i < n_active)[:, None]
    cum_t, nch_t = pick(cum), pick(nch)
    chunk0_t = pick((seg_off.T + jnp.arange(nt, dtype=I32)[None, :] * SLOTS) // CHUNK)
    jj = jnp.sum((q[:, :, None] >= cum_t[:, None, :]).astype(I32), axis=2)
    hot_j = jj[:, :, None] == jnp.arange(nt, dtype=I32)[None, None, :]
    first = jnp.sum(jnp.where(hot_j, (cum_t - nch_t)[:, None, :], 0), axis=2)
    base = jnp.sum(jnp.where(hot_j, chunk0_t[:, None, :], 0), axis=2)
    src = jnp.where(valid, base + q - first, 0)
    trash = nt * SLOTS // CHUNK + (i % 2)[:, None] * cpt + jnp.arange(cpt, dtype=I32)[None, :]
    dst = jnp.where(valid, src, trash)
    group_end = jnp.sum(jnp.where(hot_e, tile_end[None, :], 0), axis=1)
    nxt = jnp.sum((group_end[:, None] >= tile_end[None, :]).astype(I32), axis=1)
    nxt = jnp.where(group_end < n_active, nxt, -1)
    return (te.astype(I32), nxt.astype(I32), n_active.astype(I32)[None],
            src.reshape(-1).astype(I32), dst.reshape(-1).astype(I32))


def _expert_kernel(te_ref, nx_ref, na_ref, cs_ref, cd_ref, xg_hbm, wg_hbm, bg_ref, wu_hbm, bu_ref, wd_hbm,
                   bd_ref, yg_hbm, xbuf, ybuf, wst, wbf, zbuf, sem_in, sem_out, sem_zero, sem_w, *,
                   n_token_tiles):
    i = pl.program_id(0)
    n_active = na_ref[0]
    slot = i % 2
    cpt = CHUNKS_PER_TILE
    w_hbm = (wg_hbm, wu_hbm, wd_hbm)

    def w_copy(n, e):
        return pltpu.make_async_copy(w_hbm[n].at[e], wst.at[n], sem_w.at[n])

    zero_starts = [j * SLOTS + TM_POST * TOP_K for j in range(n_token_tiles)]
    zero_starts += [n_token_tiles * SLOTS + j * SLACK_ROWS for j in range(2 * TM_EXPERT // SLACK_ROWS)]

    def zero_copy(n):
        return pltpu.make_async_copy(
            zbuf, yg_hbm.at[pl.ds(zero_starts[n] // CHUNK, SLACK_ROWS // CHUNK)], sem_zero.at[n])

    @pl.when(i == 0)
    def _():
        zbuf[...] = jnp.zeros_like(zbuf)
        for n in range(len(zero_starts)):
            zero_copy(n).start()

    def in_copy(c, sl, chunk):
        return pltpu.make_async_copy(xg_hbm.at[chunk], xbuf.at[sl, c], sem_in.at[sl])

    def out_copy(c, sl, chunk):
        return pltpu.make_async_copy(ybuf.at[sl, c], yg_hbm.at[chunk], sem_out.at[sl])

    def start_gather(tile, sl):
        for c in range(cpt):
            in_copy(c, sl, cs_ref[tile * cpt + c]).start()

    def wait_gather(sl):
        pltpu.make_async_copy(xg_hbm.at[pl.ds(0, cpt)], xbuf.at[sl], sem_in.at[sl]).wait()

    def wait_scatter(sl):
        pltpu.make_async_copy(ybuf.at[sl], yg_hbm.at[pl.ds(0, cpt)], sem_out.at[sl]).wait()

    @pl.when(i == 0)
    def _():
        start_gather(0, 0)
        for n in range(len(w_hbm)):
            w_copy(n, te_ref[0]).start()

    @pl.when(i + 1 < n_active)
    def _():
        start_gather(i + 1, 1 - slot)

    @pl.when(i < n_active)
    def _():
        wait_gather(slot)

        @pl.when((i == 0) | (te_ref[i] != te_ref[jnp.maximum(i - 1, 0)]))
        def _():
            for n in range(len(w_hbm)):
                w_copy(n, 0).wait()
                wbf[n] = wst[n].astype(BF16)

                @pl.when(nx_ref[i] >= 0)
                def _():
                    w_copy(n, nx_ref[i]).start()

        @pl.when(i >= 1)
        def _():
            for c in range(cpt):
                out_copy(c, 1 - slot, cd_ref[(i - 1) * cpt + c]).start()

        x = xbuf[slot].reshape(TM_EXPERT, -1)
        e = te_ref[i]
        gate = jnp.minimum(jnp.dot(x, wbf[0], preferred_element_type=F32) + bg_ref[e], SWIGLU_LIMIT)
        up = jnp.clip(jnp.dot(x, wbf[1], preferred_element_type=F32) + bu_ref[e], -SWIGLU_LIMIT, SWIGLU_LIMIT)
        act = gate * jax.nn.sigmoid(SWIGLU_ALPHA * gate) * (up + 1.0)
        y = jnp.dot(act.astype(BF16), wbf[2], preferred_element_type=F32) + bd_ref[e]

        @pl.when(i >= 2)
        def _():
            wait_scatter(slot)

        ybuf[slot] = y.astype(BF16).reshape(ybuf.shape[1:])

        @pl.when(i == 0)
        def _():
            for n in range(len(zero_starts)):
                zero_copy(n).wait()

        @pl.when(i == n_active - 1)
        def _():
            for c in range(cpt):
                out_copy(c, slot, cd_ref[i * cpt + c]).start()
            wait_scatter(slot)

            @pl.when(i >= 1)
            def _():
                wait_scatter(1 - slot)


def _experts(xg, te, nxt, n_active, src, dst, wg, bg, wu, bu, wd, bd):
    rows, d = xg.shape
    n_e, _, dff = wg.shape
    assert d == dff
    n_tiles = te.shape[0]
    n_token_tiles = rows // SLOTS
    n_zero = n_token_tiles + 2 * TM_EXPERT // SLACK_ROWS
    b_spec = pl.BlockSpec((n_e, 1, d), lambda i, *_: (0, 0, 0))
    hbm = pl.BlockSpec(memory_space=pl.ANY)
    tile_bufs = pltpu.VMEM((2, CHUNKS_PER_TILE, CHUNK, d), BF16)
    yg = pl.pallas_call(
        functools.partial(_expert_kernel, n_token_tiles=n_token_tiles),
        grid_spec=pltpu.PrefetchScalarGridSpec(
            num_scalar_prefetch=5,
            grid=(n_tiles,),
            in_specs=[hbm, hbm, b_spec, hbm, b_spec, hbm, b_spec],
            out_specs=hbm,
            scratch_shapes=[tile_bufs, tile_bufs,
                            pltpu.VMEM((3, d, dff), F32), pltpu.VMEM((3, d, dff), BF16),
                            pltpu.VMEM((SLACK_ROWS // CHUNK, CHUNK, d), BF16),
                            pltpu.SemaphoreType.DMA((2,)),
                            pltpu.SemaphoreType.DMA((2,)),
                            pltpu.SemaphoreType.DMA((n_zero,)),
                            pltpu.SemaphoreType.DMA((3,))]),
        out_shape=jax.ShapeDtypeStruct(((rows + 2 * TM_EXPERT) // CHUNK, CHUNK, d), BF16),
        compiler_params=pltpu.CompilerParams(
            dimension_semantics=("arbitrary",), vmem_limit_bytes=V7X_VMEM_LIMIT_BYTES),
        name="experts",
    )(te, nxt, n_active, src, dst, xg.reshape(rows // CHUNK, CHUNK, d), wg, bg, wu, bu, wd, bd)
    return yg.reshape(rows + 2 * TM_EXPERT, d)


def _combine_kernel(x1_ref, yg_ref, meta_ref, gn_ref, o_ref, *, final_norm):
    meta = meta_ref[...]
    tm = meta.shape[0]
    slots_i = [meta[:, k:k + 1].astype(I32) for k in range(TOP_K)]
    probs = [meta[:, TOP_K + k:TOP_K + k + 1] for k in range(TOP_K)]
    iota_s = lax.broadcasted_iota(I32, (tm, PERM_BLOCK), 1)
    xo = x1_ref[...]
    def weight_block(blk):
        wperm = jnp.zeros((tm, PERM_BLOCK), F32)
        for sk, pk in zip(slots_i, probs):
            wperm = jnp.where(iota_s == sk - blk * PERM_BLOCK, pk, wperm)
        return wperm.astype(BF16)

    n_blk = SLOTS // PERM_BLOCK
    w_next = weight_block(0)
    for blk in range(n_blk):
        w = w_next
        if blk + 1 < n_blk:
            w_next = weight_block(blk + 1)
        xo = xo + jnp.dot(w, yg_ref[blk * PERM_BLOCK:(blk + 1) * PERM_BLOCK, :], preferred_element_type=F32)
    o_ref[...] = _rms(xo, gn_ref[...]) if final_norm else xo


def _combine(x1, yg, meta, gn, final_norm):
    t, d = x1.shape
    tm = TM_POST
    return pl.pallas_call(
        functools.partial(_combine_kernel, final_norm=final_norm),
        grid=(t // tm,),
        in_specs=[pl.BlockSpec((tm, d), lambda i: (i, 0)),
                  pl.BlockSpec((SLOTS, d), lambda i: (i, 0)),
                  pl.BlockSpec((tm, LANES), lambda i: (i, 0)),
                  pl.BlockSpec((1, d), lambda i: (0, 0))],
        out_specs=pl.BlockSpec((tm, d), lambda i: (i, 0)),
        out_shape=jax.ShapeDtypeStruct((t, d), F32),
        compiler_params=pltpu.CompilerParams(
            dimension_semantics=("parallel",), vmem_limit_bytes=V7X_VMEM_LIMIT_BYTES),
        name="combine",
    )(x1, yg, meta, gn)


def kernel(x, norm_mix, w_in, b_in, sinks, rel_bias, w_branch_a, w_branch_b, w_out, norm_ffn,
           router_w, router_b, w_gate, b_gate, w_up, b_up, w_down, b_down, norm_final):
    b, s, d = x.shape
    t = b * s
    depth = w_in.shape[0]
    a_q_w = A_KV_HEADS * A_GROUP * HEAD_DIM
    a_kv_w = A_KV_HEADS * HEAD_DIM
    b_w = B_HEADS * HEAD_DIM
    n_grp = len(B_GROUPS)
    dils = tuple(dil for _, dil in B_GROUPS)
    segs, col = [], 0
    for kind, width in (("q", a_q_w), ("k", a_kv_w), ("vT", a_kv_w)):
        segs.append((col, width, kind, 1))
        col += width
    for kind in ("q", "k", "v"):
        for dil in dils:
            segs.append((col, b_w, kind, dil))
            col += b_w
    for _ in range(2):
        segs.append((col, d, "gate", 1))
        col += d
    segs = tuple(segs)
    n_a = A_KV_HEADS * A_GROUP
    nt = t // TM_POST
    max_chunks = nt * ((TM_POST * TOP_K + N_EXPERTS * (CHUNK - 1)) // CHUNK)
    n_tiles = -(-(max_chunks + N_EXPERTS * (CHUNKS_PER_TILE - 1)) // CHUNKS_PER_TILE)

    bias_a = _band_bias(rel_bias[:, :n_a], A_HALF_WINDOW, TQ_ATTN, 1)
    bias_b = [_band_bias(rel_bias[:, n_a + gi * B_HEADS:n_a + (gi + 1) * B_HEADS],
                         win // (2 * dil), TQ_ATTN, dil) for gi, (win, dil) in enumerate(B_GROUPS)]

    x2d = x.reshape(t, d)
    for layer in range(depth):
        proj = _inproj(x2d, norm_mix[layer][None], w_in[layer].astype(BF16), b_in[layer][None], segs)
        qa, ka, vat = proj[:3]
        qb, kb, vb = proj[3:3 + n_grp], proj[3 + n_grp:3 + 2 * n_grp], proj[3 + 2 * n_grp:3 + 3 * n_grp]
        ga, gb = proj[-2:]

        yat = _band_attn_gqa(qa.reshape(b, s, a_q_w), ka.reshape(b, s, a_kv_w), vat, bias_a, sinks[layer],
                             n_kv=A_KV_HEADS, grp=A_GROUP, hw=A_HALF_WINDOW, tq=TQ_ATTN, block=BLOCK_ATTN_A)
        os_, ls_ = [], []
        for gi, (win, dil) in enumerate(B_GROUPS):
            sub = lambda a: a.reshape(b, s // dil, dil * b_w)
            o, lse = _band_attn_heads(sub(qb[gi]), sub(kb[gi]), sub(vb[gi]), bias_b[gi], n_heads=B_HEADS,
                                      hw=win // (2 * dil), tq=TQ_ATTN, rows_per_step=BLOCK_ATTN_B, reps=dil)
            os_.append(o.reshape(t // dil, dil * b_w))
            ls_.append(lse.reshape(t // dil, dil * b_w))

        rwt = router_w[layer].T
        rwt_hi = rwt.astype(BF16)
        rwt_split = jnp.stack([rwt_hi, (rwt - rwt_hi.astype(F32)).astype(BF16)])
        x1, xg, meta, segs_out = _post(
            x2d, yat, os_, ls_, dils, ga, gb,
            w_branch_a[layer].astype(BF16), w_branch_b[layer].astype(BF16), w_out[layer].astype(BF16),
            norm_ffn[layer][None], rwt_split, router_b[layer][:, None])

        segs3 = segs_out.reshape(nt, N_EXPERTS, LANES)
        te, nxt, n_active, src, dst = _expert_tiles(segs3[:, :, 0].astype(I32), segs3[:, :, 1].astype(I32), n_tiles)
        yg = _experts(xg, te, nxt, n_active, src, dst, w_gate[layer], b_gate[layer][:, None], w_up[layer],
                      b_up[layer][:, None], w_down[layer], b_down[layer][:, None])
        x2d = _combine(x1, yg, meta, norm_final[None], layer == depth - 1)
    return x2d.reshape(b, s, d)
```

```python
import functools

import numpy as np
import jax
import jax.numpy as jnp
from jax import lax
from jax.experimental import pallas as pl
from jax.experimental.pallas import tpu as pltpu

F32 = jnp.float32
BF16 = jnp.bfloat16
I32 = jnp.int32

HEAD_DIM = 64
A_KV_HEADS = 4
A_GROUP = 4
A_HALF_WINDOW = 128
B_GROUPS = ((128, 1), (512, 4), (2048, 16))
B_HEADS = 4
N_BUCKETS = 32
MAX_DISTANCE = 1024
N_EXPERTS = 32
TOP_K = 4
SWIGLU_LIMIT = 7.0
SWIGLU_ALPHA = 1.702
EPS = 1e-5
NEG_INF = -1e30
LOG2E = 1.4426950408889634

V7X_VMEM_LIMIT_BYTES = 56 * 1024 * 1024
LANES = 128
BF16_SUBLANES = 16

TM_PROJ = 1024
TQ_ATTN = 128
BLOCK_ATTN_A = 2048
BLOCK_ATTN_B = 2048
TM_POST = 512
TM_EXPERT = 512

CHUNK = BF16_SUBLANES
CHUNKS_PER_TILE = TM_EXPERT // CHUNK
SLOTS = -(-(TM_POST * TOP_K + N_EXPERTS * (CHUNK - 1)) // LANES) * LANES
SLACK_ROWS = SLOTS - TM_POST * TOP_K
assert (2 * TM_EXPERT) % SLACK_ROWS == 0
PERM_BLOCK = 512
assert SLOTS % PERM_BLOCK == 0


def _t5_buckets(rel):
    half = N_BUCKETS // 2
    max_exact = half // 2
    ret = np.where(rel > 0, half, 0)
    n = np.abs(rel)
    large = max_exact + (np.log(np.maximum(n, 1) / max_exact)
                         / np.log(MAX_DISTANCE / max_exact) * (half - max_exact)).astype(np.int32)
    large = np.minimum(large, half - 1)
    return (ret + np.where(n < max_exact, n, large)).astype(np.int32)


def _rms(x, g):
    return x * lax.rsqrt(jnp.mean(x * x, axis=-1, keepdims=True) + EPS) * g


def _inproj_kernel(x_ref, g_ref, w_ref, b_ref, *refs, segs):
    out_refs, scr = refs[:-1], refs[-1]
    h = _rms(x_ref[...], g_ref[...]).astype(BF16)

    def project(n):
        c0, width = segs[n][:2]
        return jnp.dot(h, w_ref[:, c0:c0 + width], preferred_element_type=F32) + b_ref[:, c0:c0 + width]

    acc_next = project(0)
    for n, (ref, (c0, width, kind, dil)) in enumerate(zip(out_refs, segs)):
        acc = acc_next
        if n + 1 < len(segs):
            acc_next = project(n + 1)
        if kind == "q":
            acc = acc * (HEAD_DIM ** -0.5 * LOG2E)
        elif kind == "gate":
            acc = jax.nn.sigmoid(acc)
        if kind == "vT":
            ref[...] = acc.T.astype(ref.dtype)
        elif dil == 1:
            ref[...] = acc.astype(ref.dtype)
        else:
            n = acc.shape[0] // dil
            for c in range(width // LANES):
                scr[c] = acc[:, c * LANES:(c + 1) * LANES]
            for r in range(dil):
                for c in range(width // LANES):
                    col = r * width + c * LANES
                    ref[:, col:col + LANES] = scr[c, pl.ds(r, n, stride=dil), :].astype(ref.dtype)


def _inproj(x2d, g, w_bf16, b, segs):
    t, d = x2d.shape
    n = w_bf16.shape[1]
    max_w = max(w for _, w, _, dil in segs if dil > 1)
    out_specs, out_shape = [], []
    for _, w, kind, dil in segs:
        if kind == "vT":
            out_specs.append(pl.BlockSpec((w, TM_PROJ), lambda i: (0, i)))
            out_shape.append(jax.ShapeDtypeStruct((w, t), BF16))
        else:
            out_specs.append(pl.BlockSpec((TM_PROJ // dil, dil * w), lambda i: (i, 0)))
            out_shape.append(jax.ShapeDtypeStruct((t // dil, dil * w), BF16))
    return pl.pallas_call(
        functools.partial(_inproj_kernel, segs=segs),
        grid=(t // TM_PROJ,),
        in_specs=[
            pl.BlockSpec((TM_PROJ, d), lambda i: (i, 0)),
            pl.BlockSpec((1, d), lambda i: (0, 0)),
            pl.BlockSpec((d, n), lambda i: (0, 0), pipeline_mode=pl.Buffered(1)),
            pl.BlockSpec((1, n), lambda i: (0, 0)),
        ],
        out_specs=out_specs,
        out_shape=out_shape,
        scratch_shapes=[pltpu.VMEM((max_w // LANES, TM_PROJ, LANES), F32)],
        compiler_params=pltpu.CompilerParams(
            dimension_semantics=("parallel",), vmem_limit_bytes=V7X_VMEM_LIMIT_BYTES),
        name="inproj",
    )(x2d, g, w_bf16, b)


def _band_bias(table, hw, tq, dil):
    tk = tq + 2 * hw
    off = np.arange(tk)[None, :] - hw - np.arange(tq)[:, None]
    band = np.abs(off) <= hw
    col = np.arange(tk)[None, :]
    masks = np.stack([band & (col >= hw), band, band & (col < hw + tq)])
    onehot = (_t5_buckets(off * dil)[..., None] == np.arange(N_BUCKETS)).astype(np.float32)
    bias = jnp.einsum("qkn,nh->hqk", jnp.asarray(onehot), table.astype(F32),
                      precision=lax.Precision.HIGHEST)
    return jnp.where(masks[:, None], bias[None] * LOG2E, NEG_INF)


def _band_attn_gqa_kernel(q_ref, kp, kc, kn, vtp, vtc, vtn, bias_ref, sink_ref, o_ref, *, n_kv, grp, tq, hw):
    j, nt = pl.program_id(1), pl.num_programs(1)
    k = jnp.concatenate([kp[0], kc[0], kn[0]], axis=0)
    vt = jnp.concatenate([vtp[...], vtc[...], vtn[...]], axis=1)
    n_sub = q_ref.shape[1] // tq
    tk = tq + 2 * hw
    ones = jnp.ones((BF16_SUBLANES, tk), BF16)
    lane = lax.broadcasted_iota(I32, (1, grp * tq), 1)
    sinks = []
    for h in range(n_kv):
        sk = jnp.full((1, grp * tq), sink_ref[h * grp] * LOG2E, F32)
        for g in range(1, grp):
            sk = jnp.where(lane >= g * tq, sink_ref[h * grp + g] * LOG2E, sk)
        sinks.append(sk)
    units = []
    for sub in range(n_sub):
        var = 1
        if sub == 0:
            var = jnp.where(j == 0, 0, var)
        if sub == n_sub - 1:
            var = jnp.where(j == nt - 1, 2, var)
        units.extend((var, sub * tq, h) for h in range(n_kv))

    def scores(var, r0, h):
        k_h = k[r0:r0 + tk, h * HEAD_DIM:(h + 1) * HEAD_DIM]
        c0 = h * grp * HEAD_DIM
        q = jnp.concatenate([q_ref[0, r0:r0 + tq, c0 + g * HEAD_DIM:c0 + (g + 1) * HEAD_DIM]
                             for g in range(grp)], axis=0)
        st = lax.dot_general(k_h, q, (((1,), (1,)), ((), ())), preferred_element_type=F32)
        return st + bias_ref[var, h]

    st_next = scores(*units[0])
    for n, (var, r0, h) in enumerate(units):
        st = st_next
        if n + 1 < len(units):
            st_next = scores(*units[n + 1])
        sk = sinks[h]
        m = jnp.maximum(jnp.max(st, axis=0, keepdims=True), sk)
        pt = jnp.exp2((st - m).astype(BF16))
        vt_h = jnp.concatenate([vt[h * HEAD_DIM:(h + 1) * HEAD_DIM, r0:r0 + tk], ones], axis=0)
        ot = jnp.dot(vt_h, pt, preferred_element_type=F32)
        l = ot[HEAD_DIM:HEAD_DIM + 1] + jnp.exp2(sk - m)
        o = ot[:HEAD_DIM] / l
        for g in range(grp):
            c = (h * grp + g) * HEAD_DIM
            o_ref[c:c + HEAD_DIM, r0:r0 + tq] = o[:, g * tq:(g + 1) * tq].astype(o_ref.dtype)


def _band_attn_gqa(q, k, vt, bias3, sinks, *, n_kv, grp, hw, tq, block):
    b, s, _ = q.shape
    hq = n_kv * grp
    qw, kw = hq * HEAD_DIM, n_kv * HEAD_DIM
    nt = s // block
    ratio = block // hw
    nhw = s // hw
    tk = tq + 2 * hw
    assert nt * (block // tq) >= 2
    prev_j = lambda j: jnp.maximum(j * ratio - 1, 0)
    next_j = lambda j: jnp.minimum((j + 1) * ratio, nhw - 1)
    bias_t = jnp.swapaxes(bias3.reshape(3, n_kv, grp * tq, tk), 2, 3)
    return pl.pallas_call(
        functools.partial(_band_attn_gqa_kernel, n_kv=n_kv, grp=grp, tq=tq, hw=hw),
        grid=(b, nt),
        in_specs=[
            pl.BlockSpec((1, block, qw), lambda bi, j: (bi, j, 0)),
            pl.BlockSpec((1, hw, kw), lambda bi, j: (bi, prev_j(j), 0)),
            pl.BlockSpec((1, block, kw), lambda bi, j: (bi, j, 0)),
            pl.BlockSpec((1, hw, kw), lambda bi, j: (bi, next_j(j), 0)),
            pl.BlockSpec((kw, hw), lambda bi, j: (0, bi * nhw + prev_j(j))),
            pl.BlockSpec((kw, block), lambda bi, j: (0, bi * nt + j)),
            pl.BlockSpec((kw, hw), lambda bi, j: (0, bi * nhw + next_j(j))),
            pl.BlockSpec((3, n_kv, tk, grp * tq), lambda bi, j: (0, 0, 0, 0), pipeline_mode=pl.Buffered(1)),
            pl.BlockSpec(memory_space=pltpu.SMEM),
        ],
        out_specs=pl.BlockSpec((qw, block), lambda bi, j: (0, bi * nt + j)),
        out_shape=jax.ShapeDtypeStruct((qw, b * s), BF16),
        compiler_params=pltpu.CompilerParams(
            dimension_semantics=("parallel", "parallel"), vmem_limit_bytes=V7X_VMEM_LIMIT_BYTES),
        name="band_attn_a",
    )(q, k, k, k, vt, vt, vt, bias_t, sinks)


def _band_attn_heads_kernel(q_ref, kp, kc, kn, vp, vc, vn, bias_ref, o_ref, lse_ref, *, n_heads, tq, hw):
    j, nt = pl.program_id(2), pl.num_programs(2)
    k = jnp.concatenate([kp[0], kc[0], kn[0]], axis=0)
    v = jnp.concatenate([vp[0], vc[0], vn[0]], axis=0)
    n_sub = q_ref.shape[1] // tq
    tk = tq + 2 * hw
    width = n_heads * HEAD_DIM
    lane_head = lax.broadcasted_iota(I32, (tq, width), 1) // HEAD_DIM
    ones = jnp.ones((BF16_SUBLANES, tk), BF16)

    units = [(c0, sub) for c0 in range(0, q_ref.shape[2], width) for sub in range(n_sub)]

    def scores(c0, sub):
        var = 1
        if sub == 0:
            var = jnp.where(j == 0, 0, var)
        if sub == n_sub - 1:
            var = jnp.where(j == nt - 1, 2, var)
        r0 = sub * tq
        q = q_ref[0, r0:r0 + tq, c0:c0 + width]
        q_bd = jnp.concatenate([jnp.where(lane_head == h, q, jnp.zeros_like(q)) for h in range(n_heads)], axis=0)
        st = lax.dot_general(k[r0:r0 + tk, c0:c0 + width], q_bd, (((1,), (1,)), ((), ())),
                             preferred_element_type=F32)
        return st + bias_ref[var, 0]

    st_next = scores(*units[0])
    for n, (c0, sub) in enumerate(units):
        r0 = sub * tq
        st = st_next
        if n + 1 < len(units):
            st_next = scores(*units[n + 1])
        m = jnp.max(st, axis=0, keepdims=True)
        pt = jnp.exp2((st - m).astype(BF16))
        vt = jnp.concatenate([v[r0:r0 + tk, c0:c0 + width].T, ones], axis=0)
        ot = jnp.dot(vt, pt, preferred_element_type=F32)
        l = ot[width:width + 1]
        lse = m + jnp.log2(l)
        o_sel = jnp.concatenate([ot[h * HEAD_DIM:(h + 1) * HEAD_DIM, h * tq:(h + 1) * tq] / l[:, h * tq:(h + 1) * tq]
                                 for h in range(n_heads)], axis=0)
        lse_sel = jnp.concatenate([jnp.broadcast_to(lse[:, h * tq:(h + 1) * tq], (HEAD_DIM, tq))
                                   for h in range(n_heads)], axis=0)
        o_ref[0, r0:r0 + tq, c0:c0 + width] = o_sel.T.astype(o_ref.dtype)
        lse_ref[0, r0:r0 + tq, c0:c0 + width] = lse_sel.T


def _band_attn_heads(q, k, v, bias3, *, n_heads, hw, tq, rows_per_step, reps):
    b, l, _ = q.shape
    block = min(rows_per_step, l)
    nt = l // block
    ratio = block // hw
    nhw = l // hw
    tk = tq + 2 * hw
    assert nt * (block // tq) >= 2
    res_per_step = max(1, min(reps, rows_per_step // block))
    qw = kw = res_per_step * n_heads * HEAD_DIM

    prev = lambda bi, r, j: (bi, jnp.maximum(j * ratio - 1, 0), r)
    cur = lambda bi, r, j: (bi, j, r)
    nxt = lambda bi, r, j: (bi, jnp.minimum((j + 1) * ratio, nhw - 1), r)
    in_specs = [
        pl.BlockSpec((1, block, qw), cur),
        pl.BlockSpec((1, hw, kw), prev), pl.BlockSpec((1, block, kw), cur), pl.BlockSpec((1, hw, kw), nxt),
        pl.BlockSpec((1, hw, kw), prev), pl.BlockSpec((1, block, kw), cur), pl.BlockSpec((1, hw, kw), nxt),
        pl.BlockSpec((3, 1, tk, n_heads * tq), lambda bi, r, j: (0, 0, 0, 0), pipeline_mode=pl.Buffered(1)),
    ]
    return pl.pallas_call(
        functools.partial(_band_attn_heads_kernel, n_heads=n_heads, tq=tq, hw=hw),
        grid=(b, reps // res_per_step, nt),
        in_specs=in_specs,
        out_specs=[pl.BlockSpec((1, block, qw), cur), pl.BlockSpec((1, block, qw), cur)],
        out_shape=[jax.ShapeDtypeStruct(q.shape, BF16), jax.ShapeDtypeStruct(q.shape, F32)],
        compiler_params=pltpu.CompilerParams(
            dimension_semantics=("parallel", "parallel", "parallel"),
            vmem_limit_bytes=V7X_VMEM_LIMIT_BYTES),
        name="band_attn_b",
    )(q, k, k, k, v, v, v, jnp.swapaxes(bias3.reshape(3, 1, n_heads * tq, tk), 2, 3))


def _token_order(ref, scr, dil):
    if dil == 1:
        return ref[...].astype(F32)
    n = ref.shape[0]
    width = ref.shape[1] // dil
    for r in range(dil):
        for c in range(width // LANES):
            col = r * width + c * LANES
            scr[c, pl.ds(r, n, stride=dil), :] = ref[:, col:col + LANES].astype(F32)
    return jnp.concatenate([scr[c] for c in range(width // LANES)], axis=1)


def _post_kernel(x_ref, ya_ref, o1, o2, o3, l1, l2, l3, ga_ref, gb_ref, wa_ref, wb_ref, wo_ref,
                 gf_ref, rwt_ref, rb_ref, tri_ref, x1_ref, xg_ref, meta_ref, segs_ref, *scrs, dils):
    os_ = [_token_order(r, scrs[2 * i], dil) for i, (r, dil) in enumerate(zip((o1, o2, o3), dils))]
    ls = [_token_order(r, scrs[2 * i + 1], dil) for i, (r, dil) in enumerate(zip((l1, l2, l3), dils))]
    mx = jnp.maximum(jnp.maximum(ls[0], ls[1]), ls[2])
    es = [jnp.exp2(l - mx) for l in ls]
    den = es[0] + es[1] + es[2]
    yb = (es[0] * os_[0] + es[1] * os_[1] + es[2] * os_[2]) / den
    za = lax.dot_general(ya_ref[...], wa_ref[...], (((0,), (0,)), ((), ())), preferred_element_type=F32)
    zb = jnp.dot(yb.astype(BF16), wb_ref[...], preferred_element_type=F32)
    merged = ga_ref[...].astype(F32) * za + gb_ref[...].astype(F32) * zb
    x1 = x_ref[...] + jnp.dot(merged.astype(BF16), wo_ref[...], preferred_element_type=F32)
    x1_ref[...] = x1
    h2 = _rms(x1, gf_ref[...])

    h_hi = h2.astype(BF16)
    h_lo = (h2 - h_hi.astype(F32)).astype(BF16)
    nt_dot = lambda a, b_: lax.dot_general(a, b_, (((1,), (1,)), ((), ())), preferred_element_type=F32)
    n_e = rwt_ref.shape[1]
    both = nt_dot(rwt_ref[...].reshape(2 * n_e, -1), h_hi)
    logits = (both[:n_e] + nt_dot(rwt_ref[0], h_lo) + both[n_e:]) + rb_ref[...]
    n_e, tm = logits.shape
    iota_e = lax.broadcasted_iota(I32, (n_e, tm), 0)
    work = logits
    vals, hots = [], []
    for _ in range(TOP_K):
        mk = jnp.max(work, axis=0, keepdims=True)
        ik = jnp.min(jnp.where(work == mk, iota_e, n_e), axis=0, keepdims=True)
        hot = iota_e == ik
        vals.append(mk)
        hots.append(hot)
        work = jnp.where(hot, -jnp.inf, work)
    exps = [jnp.exp(vk - vals[0]) for vk in vals]
    tot = exps[0] + exps[1] + exps[2] + exps[3]
    probs = [ek / tot for ek in exps]

    sel = jnp.zeros((n_e, tm), F32)
    for hot in hots:
        sel = sel + jnp.where(hot, 1.0, 0.0)
    cnt = jnp.sum(sel, axis=1, keepdims=True)
    pcnt = jnp.floor((cnt + (CHUNK - 1)) / CHUNK) * CHUNK
    r_i = lax.broadcasted_iota(I32, (n_e, n_e), 0)
    c_i = lax.broadcasted_iota(I32, (n_e, n_e), 1)
    pcnt_row = jnp.sum(jnp.where(r_i == c_i, pcnt, 0.0), axis=0, keepdims=True)
    seg_off = jnp.sum(jnp.where(c_i < r_i, pcnt_row, 0.0), axis=1, keepdims=True)
    before = jnp.dot(sel.astype(BF16), tri_ref[...], preferred_element_type=F32)
    slot_of = seg_off + before
    slots = [jnp.sum(jnp.where(hot, slot_of, 0.0), axis=0, keepdims=True) for hot in hots]

    iota_s = lax.broadcasted_iota(I32, (PERM_BLOCK, tm), 0)
    slots_i = [sk.astype(I32) for sk in slots]

    def one_hot_block(blk):
        perm = jnp.zeros((PERM_BLOCK, tm), F32)
        for sk in slots_i:
            perm = jnp.where(iota_s == sk - blk * PERM_BLOCK, 1.0, perm)
        return perm.astype(BF16)

    n_blk = SLOTS // PERM_BLOCK
    perm_next = one_hot_block(0)
    for blk in range(n_blk):
        perm = perm_next
        if blk + 1 < n_blk:
            perm_next = one_hot_block(blk + 1)
        xg_ref[blk * PERM_BLOCK:(blk + 1) * PERM_BLOCK, :] = jnp.dot(
            perm, h_hi, preferred_element_type=F32).astype(BF16)

    rows = slots + probs
    meta_t = jnp.concatenate(rows + [jnp.zeros((LANES - len(rows), tm), F32)], axis=0)
    meta_ref[...] = meta_t.T
    lane = lax.broadcasted_iota(I32, (n_e, LANES), 1)
    segs_ref[...] = jnp.where(lane == 0, pcnt, jnp.where(lane == 1, seg_off, 0.0))


def _post(x2d, ya, os_, ls_, dils, ga, gb, wa, wb, wo, gf, rwt, rb):
    t, d = x2d.shape
    bw = os_[0].shape[1] // dils[0]
    tm = TM_POST
    nt = t // tm
    tri = jnp.asarray(np.triu(np.ones((tm, tm), np.float32), k=1), dtype=BF16)
    row = lambda w, dil=1: pl.BlockSpec((tm // dil, dil * w), lambda i: (i, 0))
    full = lambda a: pl.BlockSpec(a.shape, lambda i: (0,) * a.ndim)
    return pl.pallas_call(
        functools.partial(_post_kernel, dils=dils),
        grid=(nt,),
        in_specs=[row(d), pl.BlockSpec((ya.shape[0], tm), lambda i: (0, i))]
                 + [row(bw, dil) for dil in dils] + [row(bw, dil) for dil in dils]
                 + [row(d), row(d), full(wa), full(wb), full(wo), full(gf), full(rwt), full(rb), full(tri)],
        scratch_shapes=[pltpu.VMEM((bw // LANES, tm, LANES), F32) for _ in range(2 * len(dils))],
        out_specs=[row(d), pl.BlockSpec((SLOTS, d), lambda i: (i, 0)), row(LANES),
                   pl.BlockSpec((N_EXPERTS, LANES), lambda i: (i, 0))],
        out_shape=[jax.ShapeDtypeStruct((t, d), F32), jax.ShapeDtypeStruct((nt * SLOTS, d), BF16),
                   jax.ShapeDtypeStruct((t, LANES), F32), jax.ShapeDtypeStruct((nt * N_EXPERTS, LANES), F32)],
        compiler_params=pltpu.CompilerParams(
            dimension_semantics=("parallel",), vmem_limit_bytes=V7X_VMEM_LIMIT_BYTES),
        name="post_attn",
    )(x2d, ya, *os_, *ls_, ga, gb, wa, wb, wo, gf, rwt, rb, tri)


def _expert_tiles(pcnt, seg_off, n_tiles):
    nt, n_e = pcnt.shape
    cpt = CHUNKS_PER_TILE
    nch = (pcnt // CHUNK).T
    cum = jnp.cumsum(nch, axis=1)
    total = cum[:, -1]
    tiles_e = (total + cpt - 1) // cpt
    tile_end = jnp.cumsum(tiles_e)
    n_active = tile_end[-1]
    i = jnp.arange(n_tiles, dtype=I32)
    last = jnp.minimum(i, n_active - 1)
    te = jnp.sum((last[:, None] >= tile_end[None, :]).astype(I32), axis=1)
    hot_e = (te[:, None] == jnp.arange(n_e, dtype=I32)[None, :])
    pick = lambda tab: jnp.sum(jnp.where(hot_e[:, :, None], tab[None], 0), axis=1)
    tile_start = jnp.sum(jnp.where(hot_e, (tile_end - tiles_e)[None, :], 0), axis=1)
    total_t = jnp.sum(jnp.where(hot_e, total[None, :], 0), axis=1)
    q = (last - tile_start)[:, None] * cpt + jnp.arange(cpt, dtype=I32)[None, :]
    valid = (q < total_t[:, None]) & (i < n_active)[:, None]
    cum_t, nch_t = pick(cum), pick(nch)
    chunk0_t = pick((seg_off.T + jnp.arange(nt, dtype=I32)[None, :] * SLOTS) // CHUNK)
    jj = jnp.sum((q[:, :, None] >= cum_t[:, None, :]).astype(I32), axis=2)
    hot_j = jj[:, :, None] == jnp.arange(nt, dtype=I32)[None, None, :]
    first = jnp.sum(jnp.where(hot_j, (cum_t - nch_t)[:, None, :], 0), axis=2)
    base = jnp.sum(jnp.where(hot_j, chunk0_t[:, None, :], 0), axis=2)
    src = jnp.where(valid, base + q - first, 0)
    trash = nt * SLOTS // CHUNK + (i % 2)[:, None] * cpt + jnp.arange(cpt, dtype=I32)[None, :]
    dst = jnp.where(valid, src, trash)
    group_end = jnp.sum(jnp.where(hot_e, tile_end[None, :], 0), axis=1)
    nxt = jnp.sum((group_end[:, None] >= tile_end[None, :]).astype(I32), axis=1)
    nxt = jnp.where(group_end < n_active, nxt, -1)
    return (te.astype(I32), nxt.astype(I32), n_active.astype(I32)[None],
            src.reshape(-1).astype(I32), dst.reshape(-1).astype(I32))


def _expert_kernel(te_ref, nx_ref, na_ref, cs_ref, cd_ref, xg_hbm, wg_hbm, bg_ref, wu_hbm, bu_ref, wd_hbm,
                   bd_ref, yg_hbm, xbuf, ybuf, wst, wbf, zbuf, sem_in, sem_out, sem_zero, sem_w, *,
                   n_token_tiles):
    i = pl.program_id(0)
    n_active = na_ref[0]
    slot = i % 2
    cpt = CHUNKS_PER_TILE
    w_hbm = (wg_hbm, wu_hbm, wd_hbm)

    def w_copy(n, e):
        return pltpu.make_async_copy(w_hbm[n].at[e], wst.at[n], sem_w.at[n])

    zero_starts = [j * SLOTS + TM_POST * TOP_K for j in range(n_token_tiles)]
    zero_starts += [n_token_tiles * SLOTS + j * SLACK_ROWS for j in range(2 * TM_EXPERT // SLACK_ROWS)]

    def zero_copy(n):
        return pltpu.make_async_copy(
            zbuf, yg_hbm.at[pl.ds(zero_starts[n] // CHUNK, SLACK_ROWS // CHUNK)], sem_zero.at[n])

    @pl.when(i == 0)
    def _():
        zbuf[...] = jnp.zeros_like(zbuf)
        for n in range(len(zero_starts)):
            zero_copy(n).start()

    def in_copy(c, sl, chunk):
        return pltpu.make_async_copy(xg_hbm.at[chunk], xbuf.at[sl, c], sem_in.at[sl])

    def out_copy(c, sl, chunk):
        return pltpu.make_async_copy(ybuf.at[sl, c], yg_hbm.at[chunk], sem_out.at[sl])

    def start_gather(tile, sl):
        for c in range(cpt):
            in_copy(c, sl, cs_ref[tile * cpt + c]).start()

    def wait_gather(sl):
        pltpu.make_async_copy(xg_hbm.at[pl.ds(0, cpt)], xbuf.at[sl], sem_in.at[sl]).wait()

    def wait_scatter(sl):
        pltpu.make_async_copy(ybuf.at[sl], yg_hbm.at[pl.ds(0, cpt)], sem_out.at[sl]).wait()

    @pl.when(i == 0)
    def _():
        start_gather(0, 0)
        for n in range(len(w_hbm)):
            w_copy(n, te_ref[0]).start()

    @pl.when(i + 1 < n_active)
    def _():
        start_gather(i + 1, 1 - slot)

    @pl.when(i < n_active)
    def _():
        wait_gather(slot)

        @pl.when((i == 0) | (te_ref[i] != te_ref[jnp.maximum(i - 1, 0)]))
        def _():
            for n in range(len(w_hbm)):
                w_copy(n, 0).wait()
                wbf[n] = wst[n].astype(BF16)

                @pl.when(nx_ref[i] >= 0)
                def _():
                    w_copy(n, nx_ref[i]).start()

        @pl.when(i >= 1)
        def _():
            for c in range(cpt):
                out_copy(c, 1 - slot, cd_ref[(i - 1) * cpt + c]).start()

        x = xbuf[slot].reshape(TM_EXPERT, -1)
        e = te_ref[i]
        gate = jnp.minimum(jnp.dot(x, wbf[0], preferred_element_type=F32) + bg_ref[e], SWIGLU_LIMIT)
        up = jnp.clip(jnp.dot(x, wbf[1], preferred_element_type=F32) + bu_ref[e], -SWIGLU_LIMIT, SWIGLU_LIMIT)
        act = gate * jax.nn.sigmoid(SWIGLU_ALPHA * gate) * (up + 1.0)
        y = jnp.dot(act.astype(BF16), wbf[2], preferred_element_type=F32) + bd_ref[e]

        @pl.when(i >= 2)
        def _():
            wait_scatter(slot)

        ybuf[slot] = y.astype(BF16).reshape(ybuf.shape[1:])

        @pl.when(i == 0)
        def _():
            for n in range(len(zero_starts)):
                zero_copy(n).wait()

        @pl.when(i == n_active - 1)
        def _():
            for c in range(cpt):
                out_copy(c, slot, cd_ref[i * cpt + c]).start()
            wait_scatter(slot)

            @pl.when(i >= 1)
            def _():
                wait_scatter(1 - slot)


def _experts(xg, te, nxt, n_active, src, dst, wg, bg, wu, bu, wd, bd):
    rows, d = xg.shape
    n_e, _, dff = wg.shape
    assert d == dff
    n_tiles = te.shape[0]
    n_token_tiles = rows // SLOTS
    n_zero = n_token_tiles + 2 * TM_EXPERT // SLACK_ROWS
    b_spec = pl.BlockSpec((n_e, 1, d), lambda i, *_: (0, 0, 0))
    hbm = pl.BlockSpec(memory_space=pl.ANY)
    tile_bufs = pltpu.VMEM((2, CHUNKS_PER_TILE, CHUNK, d), BF16)
    yg = pl.pallas_call(
        functools.partial(_expert_kernel, n_token_tiles=n_token_tiles),
        grid_spec=pltpu.PrefetchScalarGridSpec(
            num_scalar_prefetch=5,
            grid=(n_tiles,),
            in_specs=[hbm, hbm, b_spec, hbm, b_spec, hbm, b_spec],
            out_specs=hbm,
            scratch_shapes=[tile_bufs, tile_bufs,
                            pltpu.VMEM((3, d, dff), F32), pltpu.VMEM((3, d, dff), BF16),
                            pltpu.VMEM((SLACK_ROWS // CHUNK, CHUNK, d), BF16),
                            pltpu.SemaphoreType.DMA((2,)),
                            pltpu.SemaphoreType.DMA((2,)),
                            pltpu.SemaphoreType.DMA((n_zero,)),
                            pltpu.SemaphoreType.DMA((3,))]),
        out_shape=jax.ShapeDtypeStruct(((rows + 2 * TM_EXPERT) // CHUNK, CHUNK, d), BF16),
        compiler_params=pltpu.CompilerParams(
            dimension_semantics=("arbitrary",), vmem_limit_bytes=V7X_VMEM_LIMIT_BYTES),
        name="experts",
    )(te, nxt, n_active, src, dst, xg.reshape(rows // CHUNK, CHUNK, d), wg, bg, wu, bu, wd, bd)
    return yg.reshape(rows + 2 * TM_EXPERT, d)


def _combine_kernel(x1_ref, yg_ref, meta_ref, gn_ref, o_ref, *, final_norm):
    meta = meta_ref[...]
    tm = meta.shape[0]
    slots_i = [meta[:, k:k + 1].astype(I32) for k in range(TOP_K)]
    probs = [meta[:, TOP_K + k:TOP_K + k + 1] for k in range(TOP_K)]
    iota_s = lax.broadcasted_iota(I32, (tm, PERM_BLOCK), 1)
    xo = x1_ref[...]
    def weight_block(blk):
        wperm = jnp.zeros((tm, PERM_BLOCK), F32)
        for sk, pk in zip(slots_i, probs):
            wperm = jnp.where(iota_s == sk - blk * PERM_BLOCK, pk, wperm)
        return wperm.astype(BF16)

    n_blk = SLOTS // PERM_BLOCK
    w_next = weight_block(0)
    for blk in range(n_blk):
        w = w_next
        if blk + 1 < n_blk:
            w_next = weight_block(blk + 1)
        xo = xo + jnp.dot(w, yg_ref[blk * PERM_BLOCK:(blk + 1) * PERM_BLOCK, :], preferred_element_type=F32)
    o_ref[...] = _rms(xo, gn_ref[...]) if final_norm else xo


def _combine(x1, yg, meta, gn, final_norm):
    t, d = x1.shape
    tm = TM_POST
    return pl.pallas_call(
        functools.partial(_combine_kernel, final_norm=final_norm),
        grid=(t // tm,),
        in_specs=[pl.BlockSpec((tm, d), lambda i: (i, 0)),
                  pl.BlockSpec((SLOTS, d), lambda i: (i, 0)),
                  pl.BlockSpec((tm, LANES), lambda i: (i, 0)),
                  pl.BlockSpec((1, d), lambda i: (0, 0))],
        out_specs=pl.BlockSpec((tm, d), lambda i: (i, 0)),
        out_shape=jax.ShapeDtypeStruct((t, d), F32),
        compiler_params=pltpu.CompilerParams(
            dimension_semantics=("parallel",), vmem_limit_bytes=V7X_VMEM_LIMIT_BYTES),
        name="combine",
    )(x1, yg, meta, gn)


def kernel(x, norm_mix, w_in, b_in, sinks, rel_bias, w_branch_a, w_branch_b, w_out, norm_ffn,
           router_w, router_b, w_gate, b_gate, w_up, b_up, w_down, b_down, norm_final):
    b, s, d = x.shape
    t = b * s
    depth = w_in.shape[0]
    a_q_w = A_KV_HEADS * A_GROUP * HEAD_DIM
    a_kv_w = A_KV_HEADS * HEAD_DIM
    b_w = B_HEADS * HEAD_DIM
    n_grp = len(B_GROUPS)
    dils = tuple(dil for _, dil in B_GROUPS)
    segs, col = [], 0
    for kind, width in (("q", a_q_w), ("k", a_kv_w), ("vT", a_kv_w)):
        segs.append((col, width, kind, 1))
        col += width
    for kind in ("q", "k", "v"):
        for dil in dils:
            segs.append((col, b_w, kind, dil))
            col += b_w
    for _ in range(2):
        segs.append((col, d, "gate", 1))
        col += d
    segs = tuple(segs)
    n_a = A_KV_HEADS * A_GROUP
    nt = t // TM_POST
    max_chunks = nt * ((TM_POST * TOP_K + N_EXPERTS * (CHUNK - 1)) // CHUNK)
    n_tiles = -(-(max_chunks + N_EXPERTS * (CHUNKS_PER_TILE - 1)) // CHUNKS_PER_TILE)

    bias_a = _band_bias(rel_bias[:, :n_a], A_HALF_WINDOW, TQ_ATTN, 1)
    bias_b = [_band_bias(rel_bias[:, n_a + gi * B_HEADS:n_a + (gi + 1) * B_HEADS],
                         win // (2 * dil), TQ_ATTN, dil) for gi, (win, dil) in enumerate(B_GROUPS)]

    x2d = x.reshape(t, d)
    for layer in range(depth):
        proj = _inproj(x2d, norm_mix[layer][None], w_in[layer].astype(BF16), b_in[layer][None], segs)
        qa, ka, vat = proj[:3]
        qb, kb, vb = proj[3:3 + n_grp], proj[3 + n_grp:3 + 2 * n_grp], proj[3 + 2 * n_grp:3 + 3 * n_grp]
        ga, gb = proj[-2:]

        yat = _band_attn_gqa(qa.reshape(b, s, a_q_w), ka.reshape(b, s, a_kv_w), vat, bias_a, sinks[layer],
                             n_kv=A_KV_HEADS, grp=A_GROUP, hw=A_HALF_WINDOW, tq=TQ_ATTN, block=BLOCK_ATTN_A)
        os_, ls_ = [], []
        for gi, (win, dil) in enumerate(B_GROUPS):
            sub = lambda a: a.reshape(b, s // dil, dil * b_w)
            o, lse = _band_attn_heads(sub(qb[gi]), sub(kb[gi]), sub(vb[gi]), bias_b[gi], n_heads=B_HEADS,
                                      hw=win // (2 * dil), tq=TQ_ATTN, rows_per_step=BLOCK_ATTN_B, reps=dil)
            os_.append(o.reshape(t // dil, dil * b_w))
            ls_.append(lse.reshape(t // dil, dil * b_w))

        rwt = router_w[layer].T
        rwt_hi = rwt.astype(BF16)
        rwt_split = jnp.stack([rwt_hi, (rwt - rwt_hi.astype(F32)).astype(BF16)])
        x1, xg, meta, segs_out = _post(
            x2d, yat, os_, ls_, dils, ga, gb,
            w_branch_a[layer].astype(BF16), w_branch_b[layer].astype(BF16), w_out[layer].astype(BF16),
            norm_ffn[layer][None], rwt_split, router_b[layer][:, None])

        segs3 = segs_out.reshape(nt, N_EXPERTS, LANES)
        te, nxt, n_active, src, dst = _expert_tiles(segs3[:, :, 0].astype(I32), segs3[:, :, 1].astype(I32), n_tiles)
        yg = _experts(xg, te, nxt, n_active, src, dst, w_gate[layer], b_gate[layer][:, None], w_up[layer],
                      b_up[layer][:, None], w_down[layer], b_down[layer][:, None])
        x2d = _combine(x1, yg, meta, norm_final[None], layer == depth - 1)
    return x2d.reshape(b, s, d)
```

```python
import functools

import numpy as np
import jax
import jax.numpy as jnp
from jax import lax
from jax.experimental import pallas as pl
from jax.experimental.pallas import tpu as pltpu

F32 = jnp.float32
BF16 = jnp.bfloat16
I32 = jnp.int32

HEAD_DIM = 64
A_KV_HEADS = 4
A_GROUP = 4
A_HALF_WINDOW = 128
B_GROUPS = ((128, 1), (512, 4), (2048, 16))
B_HEADS = 4
N_BUCKETS = 32
MAX_DISTANCE = 1024
N_EXPERTS = 32
TOP_K = 4
SWIGLU_LIMIT = 7.0
SWIGLU_ALPHA = 1.702
EPS = 1e-5
NEG_INF = -1e30
LOG2E = 1.4426950408889634

V7X_VMEM_LIMIT_BYTES = 56 * 1024 * 1024
LANES = 128
BF16_SUBLANES = 16

TM_PROJ = 1024
TQ_ATTN = 128
BLOCK_ATTN_A = 2048
BLOCK_ATTN_B = 2048
TM_POST = 512
TM_EXPERT = 512

CHUNK = BF16_SUBLANES
CHUNKS_PER_TILE = TM_EXPERT // CHUNK
SLOTS = -(-(TM_POST * TOP_K + N_EXPERTS * (CHUNK - 1)) // LANES) * LANES
SLACK_ROWS = SLOTS - TM_POST * TOP_K
assert (2 * TM_EXPERT) % SLACK_ROWS == 0
PERM_BLOCK = 512
assert SLOTS % PERM_BLOCK == 0


def _t5_buckets(rel):
    half = N_BUCKETS // 2
    max_exact = half // 2
    ret = np.where(rel > 0, half, 0)
    n = np.abs(rel)
    large = max_exact + (np.log(np.maximum(n, 1) / max_exact)
                         / np.log(MAX_DISTANCE / max_exact) * (half - max_exact)).astype(np.int32)
    large = np.minimum(large, half - 1)
    return (ret + np.where(n < max_exact, n, large)).astype(np.int32)


def _rms(x, g):
    return x * lax.rsqrt(jnp.mean(x * x, axis=-1, keepdims=True) + EPS) * g


def _inproj_kernel(x_ref, g_ref, w_ref, b_ref, *refs, segs):
    out_refs, scr = refs[:-1], refs[-1]
    h = _rms(x_ref[...], g_ref[...]).astype(BF16)

    def project(n):
        c0, width = segs[n][:2]
        return jnp.dot(h, w_ref[:, c0:c0 + width], preferred_element_type=F32) + b_ref[:, c0:c0 + width]

    acc_next = project(0)
    for n, (ref, (c0, width, kind, dil)) in enumerate(zip(out_refs, segs)):
        acc = acc_next
        if n + 1 < len(segs):
            acc_next = project(n + 1)
        if kind == "q":
            acc = acc * (HEAD_DIM ** -0.5 * LOG2E)
        elif kind == "gate":
            acc = jax.nn.sigmoid(acc)
        if kind == "vT":
            ref[...] = acc.T.astype(ref.dtype)
        elif dil == 1:
            ref[...] = acc.astype(ref.dtype)
        else:
            n = acc.shape[0] // dil
            for c in range(width // LANES):
                scr[c] = acc[:, c * LANES:(c + 1) * LANES]
            for r in range(dil):
                for c in range(width // LANES):
                    col = r * width + c * LANES
                    ref[:, col:col + LANES] = scr[c, pl.ds(r, n, stride=dil), :].astype(ref.dtype)


def _inproj(x2d, g, w_bf16, b, segs):
    t, d = x2d.shape
    n = w_bf16.shape[1]
    max_w = max(w for _, w, _, dil in segs if dil > 1)
    out_specs, out_shape = [], []
    for _, w, kind, dil in segs:
        if kind == "vT":
            out_specs.append(pl.BlockSpec((w, TM_PROJ), lambda i: (0, i)))
            out_shape.append(jax.ShapeDtypeStruct((w, t), BF16))
        else:
            out_specs.append(pl.BlockSpec((TM_PROJ // dil, dil * w), lambda i: (i, 0)))
            out_shape.append(jax.ShapeDtypeStruct((t // dil, dil * w), BF16))
    return pl.pallas_call(
        functools.partial(_inproj_kernel, segs=segs),
        grid=(t // TM_PROJ,),
        in_specs=[
            pl.BlockSpec((TM_PROJ, d), lambda i: (i, 0)),
            pl.BlockSpec((1, d), lambda i: (0, 0)),
            pl.BlockSpec((d, n), lambda i: (0, 0), pipeline_mode=pl.Buffered(1)),
            pl.BlockSpec((1, n), lambda i: (0, 0)),
        ],
        out_specs=out_specs,
        out_shape=out_shape,
        scratch_shapes=[pltpu.VMEM((max_w // LANES, TM_PROJ, LANES), F32)],
        compiler_params=pltpu.CompilerParams(
            dimension_semantics=("parallel",), vmem_limit_bytes=V7X_VMEM_LIMIT_BYTES),
        name="inproj",
    )(x2d, g, w_bf16, b)


def _band_bias(table, hw, tq, dil, n_kv, grp):
    tk = tq + 2 * hw
    off = np.arange(tk)[None, :] - hw - np.arange(tq)[:, None]
    band = np.abs(off) <= hw
    col = np.arange(tk)[None, :]
    masks = np.stack([band & (col >= hw), band, band & (col < hw + tq)])
    onehot = (_t5_buckets(off * dil)[..., None] == np.arange(N_BUCKETS)).astype(np.float32)
    bias = jnp.einsum("qkn,nvg->vkgq", jnp.asarray(onehot), table.astype(F32).reshape(N_BUCKETS, n_kv, grp),
                      precision=lax.Precision.HIGHEST)
    keep = np.transpose(masks, (0, 2, 1))[:, None, :, None, :]
    return jnp.where(keep, bias[None] * LOG2E, NEG_INF).reshape(3, n_kv, tk, grp * tq)


def _band_attn_gqa_kernel(q_ref, kp, kc, kn, vtp, vtc, vtn, bias_ref, sink_ref, o_ref, *, n_kv, grp, tq, hw):
    j, nt = pl.program_id(1), pl.num_programs(1)
    k = jnp.concatenate([kp[0], kc[0], kn[0]], axis=0)
    vt = jnp.concatenate([vtp[...], vtc[...], vtn[...]], axis=1)
    n_sub = q_ref.shape[1] // tq
    tk = tq + 2 * hw
    ones = jnp.ones((BF16_SUBLANES, tk), BF16)
    lane = lax.broadcasted_iota(I32, (1, grp * tq), 1)
    sinks = []
    for h in range(n_kv):
        sk = jnp.full((1, grp * tq), sink_ref[h * grp] * LOG2E, F32)
        for g in range(1, grp):
            sk = jnp.where(lane >= g * tq, sink_ref[h * grp + g] * LOG2E, sk)
        sinks.append(sk)
    units = []
    for sub in range(n_sub):
        var = 1
        if sub == 0:
            var = jnp.where(j == 0, 0, var)
        if sub == n_sub - 1:
            var = jnp.where(j == nt - 1, 2, var)
        units.extend((var, sub * tq, h) for h in range(n_kv))

    def scores(var, r0, h):
        k_h = k[r0:r0 + tk, h * HEAD_DIM:(h + 1) * HEAD_DIM]
        c0 = h * grp * HEAD_DIM
        q = jnp.concatenate([q_ref[0, r0:r0 + tq, c0 + g * HEAD_DIM:c0 + (g + 1) * HEAD_DIM]
                             for g in range(grp)], axis=0)
        st = lax.dot_general(k_h, q, (((1,), (1,)), ((), ())), preferred_element_type=F32)
        return st + bias_ref[var, h]

    st_next = scores(*units[0])
    for n, (var, r0, h) in enumerate(units):
        st = st_next
        if n + 1 < len(units):
            st_next = scores(*units[n + 1])
        sk = sinks[h]
        m = jnp.maximum(jnp.max(st, axis=0, keepdims=True), sk)
        pt = jnp.exp2((st - m).astype(BF16))
        vt_h = jnp.concatenate([vt[h * HEAD_DIM:(h + 1) * HEAD_DIM, r0:r0 + tk], ones], axis=0)
        ot = jnp.dot(vt_h, pt, preferred_element_type=F32)
        l = ot[HEAD_DIM:HEAD_DIM + 1] + jnp.exp2(sk - m)
        o = ot[:HEAD_DIM] / l
        for g in range(grp):
            c = (h * grp + g) * HEAD_DIM
            o_ref[c:c + HEAD_DIM, r0:r0 + tq] = o[:, g * tq:(g + 1) * tq].astype(o_ref.dtype)


def _band_attn_gqa(q, k, vt, bias3, sinks, *, n_kv, grp, hw, tq, block):
    b, s, _ = q.shape
    hq = n_kv * grp
    qw, kw = hq * HEAD_DIM, n_kv * HEAD_DIM
    nt = s // block
    ratio = block // hw
    nhw = s // hw
    tk = tq + 2 * hw
    assert nt * (block // tq) >= 2
    prev_j = lambda j: jnp.maximum(j * ratio - 1, 0)
    next_j = lambda j: jnp.minimum((j + 1) * ratio, nhw - 1)
    assert bias3.shape == (3, n_kv, tk, grp * tq)
    return pl.pallas_call(
        functools.partial(_band_attn_gqa_kernel, n_kv=n_kv, grp=grp, tq=tq, hw=hw),
        grid=(b, nt),
        in_specs=[
            pl.BlockSpec((1, block, qw), lambda bi, j: (bi, j, 0)),
            pl.BlockSpec((1, hw, kw), lambda bi, j: (bi, prev_j(j), 0)),
            pl.BlockSpec((1, block, kw), lambda bi, j: (bi, j, 0)),
            pl.BlockSpec((1, hw, kw), lambda bi, j: (bi, next_j(j), 0)),
            pl.BlockSpec((kw, hw), lambda bi, j: (0, bi * nhw + prev_j(j))),
            pl.BlockSpec((kw, block), lambda bi, j: (0, bi * nt + j)),
            pl.BlockSpec((kw, hw), lambda bi, j: (0, bi * nhw + next_j(j))),
            pl.BlockSpec((3, n_kv, tk, grp * tq), lambda bi, j: (0, 0, 0, 0), pipeline_mode=pl.Buffered(1)),
            pl.BlockSpec(memory_space=pltpu.SMEM),
        ],
        out_specs=pl.BlockSpec((qw, block), lambda bi, j: (0, bi * nt + j)),
        out_shape=jax.ShapeDtypeStruct((qw, b * s), BF16),
        compiler_params=pltpu.CompilerParams(
            dimension_semantics=("parallel", "parallel"), vmem_limit_bytes=V7X_VMEM_LIMIT_BYTES),
        name="band_attn_a",
    )(q, k, k, k, vt, vt, vt, bias3, sinks)


def _band_attn_heads_kernel(q_ref, kp, kc, kn, vp, vc, vn, bias_ref, o_ref, lse_ref, *, n_heads, tq, hw):
    j, nt = pl.program_id(2), pl.num_programs(2)
    k = jnp.concatenate([kp[0], kc[0], kn[0]], axis=0)
    v = jnp.concatenate([vp[0], vc[0], vn[0]], axis=0)
    n_sub = q_ref.shape[1] // tq
    tk = tq + 2 * hw
    width = n_heads * HEAD_DIM
    lane_head = lax.broadcasted_iota(I32, (tq, width), 1) // HEAD_DIM
    ones = jnp.ones((BF16_SUBLANES, tk), BF16)

    units = [(c0, sub) for c0 in range(0, q_ref.shape[2], width) for sub in range(n_sub)]

    def scores(c0, sub):
        var = 1
        if sub == 0:
            var = jnp.where(j == 0, 0, var)
        if sub == n_sub - 1:
            var = jnp.where(j == nt - 1, 2, var)
        r0 = sub * tq
        q = q_ref[0, r0:r0 + tq, c0:c0 + width]
        q_bd = jnp.concatenate([jnp.where(lane_head == h, q, jnp.zeros_like(q)) for h in range(n_heads)], axis=0)
        st = lax.dot_general(k[r0:r0 + tk, c0:c0 + width], q_bd, (((1,), (1,)), ((), ())),
                             preferred_element_type=F32)
        return st + bias_ref[var, 0]

    st_next = scores(*units[0])
    for n, (c0, sub) in enumerate(units):
        r0 = sub * tq
        st = st_next
        if n + 1 < len(units):
            st_next = scores(*units[n + 1])
        m = jnp.max(st, axis=0, keepdims=True)
        pt = jnp.exp2((st - m).astype(BF16))
        vt = jnp.concatenate([v[r0:r0 + tk, c0:c0 + width].T, ones], axis=0)
        ot = jnp.dot(vt, pt, preferred_element_type=F32)
        l = ot[width:width + 1]
        lse = m + jnp.log2(l)
        o_sel = jnp.concatenate([ot[h * HEAD_DIM:(h + 1) * HEAD_DIM, h * tq:(h + 1) * tq] / l[:, h * tq:(h + 1) * tq]
                                 for h in range(n_heads)], axis=0)
        lse_sel = jnp.concatenate([jnp.broadcast_to(lse[:, h * tq:(h + 1) * tq], (HEAD_DIM, tq))
                                   for h in range(n_heads)], axis=0)
        o_ref[0, r0:r0 + tq, c0:c0 + width] = o_sel.T.astype(o_ref.dtype)
        lse_ref[0, r0:r0 + tq, c0:c0 + width] = lse_sel.T


def _band_attn_heads(q, k, v, bias3, *, n_heads, hw, tq, rows_per_step, reps):
    b, l, _ = q.shape
    block = min(rows_per_step, l)
    nt = l // block
    ratio = block // hw
    nhw = l // hw
    tk = tq + 2 * hw
    assert nt * (block // tq) >= 2
    res_per_step = max(1, min(reps, rows_per_step // block))
    qw = kw = res_per_step * n_heads * HEAD_DIM

    prev = lambda bi, r, j: (bi, jnp.maximum(j * ratio - 1, 0), r)
    cur = lambda bi, r, j: (bi, j, r)
    nxt = lambda bi, r, j: (bi, jnp.minimum((j + 1) * ratio, nhw - 1), r)
    in_specs = [
        pl.BlockSpec((1, block, qw), cur),
        pl.BlockSpec((1, hw, kw), prev), pl.BlockSpec((1, block, kw), cur), pl.BlockSpec((1, hw, kw), nxt),
        pl.BlockSpec((1, hw, kw), prev), pl.BlockSpec((1, block, kw), cur), pl.BlockSpec((1, hw, kw), nxt),
        pl.BlockSpec((3, 1, tk, n_heads * tq), lambda bi, r, j: (0, 0, 0, 0), pipeline_mode=pl.Buffered(1)),
    ]
    return pl.pallas_call(
        functools.partial(_band_attn_heads_kernel, n_heads=n_heads, tq=tq, hw=hw),
        grid=(b, reps // res_per_step, nt),
        in_specs=in_specs,
        out_specs=[pl.BlockSpec((1, block, qw), cur), pl.BlockSpec((1, block, qw), cur)],
        out_shape=[jax.ShapeDtypeStruct(q.shape, BF16), jax.ShapeDtypeStruct(q.shape, F32)],
        compiler_params=pltpu.CompilerParams(
            dimension_semantics=("parallel", "parallel", "parallel"),
            vmem_limit_bytes=V7X_VMEM_LIMIT_BYTES),
        name="band_attn_b",
    )(q, k, k, k, v, v, v, bias3)


def _token_order(ref, scr, dil):
    if dil == 1:
        return ref[...].astype(F32)
    n = ref.shape[0]
    width = ref.shape[1] // dil
    for r in range(dil):
        for c in range(width // LANES):
            col = r * width + c * LANES
            scr[c, pl.ds(r, n, stride=dil), :] = ref[:, col:col + LANES].astype(F32)
    return jnp.concatenate([scr[c] for c in range(width // LANES)], axis=1)


def _post_kernel(x_ref, ya_ref, o1, o2, o3, l1, l2, l3, ga_ref, gb_ref, wa_ref, wb_ref, wo_ref,
                 gf_ref, rwt_ref, rb_ref, tri_ref, x1_ref, xg_ref, meta_ref, segs_ref, *scrs, dils):
    os_ = [_token_order(r, scrs[2 * i], dil) for i, (r, dil) in enumerate(zip((o1, o2, o3), dils))]
    ls = [_token_order(r, scrs[2 * i + 1], dil) for i, (r, dil) in enumerate(zip((l1, l2, l3), dils))]
    mx = jnp.maximum(jnp.maximum(ls[0], ls[1]), ls[2])
    es = [jnp.exp2(l - mx) for l in ls]
    den = es[0] + es[1] + es[2]
    yb = (es[0] * os_[0] + es[1] * os_[1] + es[2] * os_[2]) / den
    za = lax.dot_general(ya_ref[...], wa_ref[...], (((0,), (0,)), ((), ())), preferred_element_type=F32)
    zb = jnp.dot(yb.astype(BF16), wb_ref[...], preferred_element_type=F32)
    merged = ga_ref[...].astype(F32) * za + gb_ref[...].astype(F32) * zb
    x1 = x_ref[...] + jnp.dot(merged.astype(BF16), wo_ref[...], preferred_element_type=F32)
    x1_ref[...] = x1
    h2 = _rms(x1, gf_ref[...])

    h_hi = h2.astype(BF16)
    h_lo = (h2 - h_hi.astype(F32)).astype(BF16)
    nt_dot = lambda a, b_: lax.dot_general(a, b_, (((1,), (1,)), ((), ())), preferred_element_type=F32)
    n_e = rwt_ref.shape[1]
    both = nt_dot(rwt_ref[...].reshape(2 * n_e, -1), h_hi)
    logits = (both[:n_e] + nt_dot(rwt_ref[0], h_lo) + both[n_e:]) + rb_ref[...]
    n_e, tm = logits.shape
    iota_e = lax.broadcasted_iota(I32, (n_e, tm), 0)
    work = logits
    vals, hots = [], []
    for _ in range(TOP_K):
        mk = jnp.max(work, axis=0, keepdims=True)
        ik = jnp.min(jnp.where(work == mk, iota_e, n_e), axis=0, keepdims=True)
        hot = iota_e == ik
        vals.append(mk)
        hots.append(hot)
        work = jnp.where(hot, -jnp.inf, work)
    exps = [jnp.exp(vk - vals[0]) for vk in vals]
    tot = exps[0] + exps[1] + exps[2] + exps[3]
    probs = [ek / tot for ek in exps]

    sel = jnp.zeros((n_e, tm), F32)
    for hot in hots:
        sel = sel + jnp.where(hot, 1.0, 0.0)
    cnt = jnp.sum(sel, axis=1, keepdims=True)
    pcnt = jnp.floor((cnt + (CHUNK - 1)) / CHUNK) * CHUNK
    r_i = lax.broadcasted_iota(I32, (n_e, n_e), 0)
    c_i = lax.broadcasted_iota(I32, (n_e, n_e), 1)
    pcnt_row = jnp.sum(jnp.where(r_i == c_i, pcnt, 0.0), axis=0, keepdims=True)
    seg_off = jnp.sum(jnp.where(c_i < r_i, pcnt_row, 0.0), axis=1, keepdims=True)
    before = jnp.dot(sel.astype(BF16), tri_ref[...], preferred_element_type=F32)
    slot_of = seg_off + before
    slots = [jnp.sum(jnp.where(hot, slot_of, 0.0), axis=0, keepdims=True) for hot in hots]

    iota_s = lax.broadcasted_iota(I32, (PERM_BLOCK, tm), 0)
    slots_i = [sk.astype(I32) for sk in slots]

    def one_hot_block(blk):
        perm = jnp.zeros((PERM_BLOCK, tm), F32)
        for sk in slots_i:
            perm = jnp.where(iota_s == sk - blk * PERM_BLOCK, 1.0, perm)
        return perm.astype(BF16)

    n_blk = SLOTS // PERM_BLOCK
    perm_next = one_hot_block(0)
    for blk in range(n_blk):
        perm = perm_next
        if blk + 1 < n_blk:
            perm_next = one_hot_block(blk + 1)
        xg_ref[blk * PERM_BLOCK:(blk + 1) * PERM_BLOCK, :] = jnp.dot(
            perm, h_hi, preferred_element_type=F32).astype(BF16)

    rows = slots + probs
    meta_t = jnp.concatenate(rows + [jnp.zeros((LANES - len(rows), tm), F32)], axis=0)
    meta_ref[...] = meta_t.T
    lane = lax.broadcasted_iota(I32, (n_e, LANES), 1)
    segs_ref[...] = jnp.where(lane == 0, pcnt, jnp.where(lane == 1, seg_off, 0.0))


def _post(x2d, ya, os_, ls_, dils, ga, gb, wa, wb, wo, gf, rwt, rb):
    t, d = x2d.shape
    bw = os_[0].shape[1] // dils[0]
    tm = TM_POST
    nt = t // tm
    tri = jnp.asarray(np.triu(np.ones((tm, tm), np.float32), k=1), dtype=BF16)
    row = lambda w, dil=1: pl.BlockSpec((tm // dil, dil * w), lambda i: (i, 0))
    full = lambda a: pl.BlockSpec(a.shape, lambda i: (0,) * a.ndim)
    return pl.pallas_call(
        functools.partial(_post_kernel, dils=dils),
        grid=(nt,),
        in_specs=[row(d), pl.BlockSpec((ya.shape[0], tm), lambda i: (0, i))]
                 + [row(bw, dil) for dil in dils] + [row(bw, dil) for dil in dils]
                 + [row(d), row(d), full(wa), full(wb), full(wo), full(gf), full(rwt), full(rb), full(tri)],
        scratch_shapes=[pltpu.VMEM((bw // LANES, tm, LANES), F32) for _ in range(2 * len(dils))],
        out_specs=[row(d), pl.BlockSpec((SLOTS, d), lambda i: (i, 0)), row(LANES),
                   pl.BlockSpec((N_EXPERTS, LANES), lambda i: (i, 0))],
        out_shape=[jax.ShapeDtypeStruct((t, d), F32), jax.ShapeDtypeStruct((nt * SLOTS, d), BF16),
                   jax.ShapeDtypeStruct((t, LANES), F32), jax.ShapeDtypeStruct((nt * N_EXPERTS, LANES), F32)],
        compiler_params=pltpu.CompilerParams(
            dimension_semantics=("parallel",), vmem_limit_bytes=V7X_VMEM_LIMIT_BYTES),
        name="post_attn",
    )(x2d, ya, *os_, *ls_, ga, gb, wa, wb, wo, gf, rwt, rb, tri)


def _expert_tiles(pcnt, seg_off, n_tiles):
    nt, n_e = pcnt.shape
    cpt = CHUNKS_PER_TILE
    nch = (pcnt // CHUNK).T
    cum = jnp.cumsum(nch, axis=1)
    total = cum[:, -1]
    tiles_e = (total + cpt - 1) // cpt
    tile_end = jnp.cumsum(tiles_e)
    n_active = tile_end[-1]
    i = jnp.arange(n_tiles, dtype=I32)
    last = jnp.minimum(i, n_active - 1)
    te = jnp.sum((last[:, None] >= tile_end[None, :]).astype(I32), axis=1)
    hot_e = (te[:, None] == jnp.arange(n_e, dtype=I32)[None, :])
    pick = lambda tab: jnp.sum(jnp.where(hot_e[:, :, None], tab[None], 0), axis=1)
    tile_start = jnp.sum(jnp.where(hot_e, (tile_end - tiles_e)[None, :], 0), axis=1)
    total_t = jnp.sum(jnp.where(hot_e, total[None, :], 0), axis=1)
    q = (last - tile_start)[:, None] * cpt + jnp.arange(cpt, dtype=I32)[None, :]
    valid = (q < total_t[:, None]) & (i < n_active)[:, None]
    cum_t, nch_t = pick(cum), pick(nch)
    chunk0_t = pick((seg_off.T + jnp.arange(nt, dtype=I32)[None, :] * SLOTS) // CHUNK)
    jj = jnp.sum((q[:, :, None] >= cum_t[:, None, :]).astype(I32), axis=2)
    hot_j = jj[:, :, None] == jnp.arange(nt, dtype=I32)[None, None, :]
    first = jnp.sum(jnp.where(hot_j, (cum_t - nch_t)[:, None, :], 0), axis=2)
    base = jnp.sum(jnp.where(hot_j, chunk0_t[:, None, :], 0), axis=2)
    src = jnp.where(valid, base + q - first, 0)
    trash = nt * SLOTS // CHUNK + (i % 2)[:, None] * cpt + jnp.arange(cpt, dtype=I32)[None, :]
    dst = jnp.where(valid, src, trash)
    group_end = jnp.sum(jnp.where(hot_e, tile_end[None, :], 0), axis=1)
    nxt = jnp.sum((group_end[:, None] >= tile_end[None, :]).astype(I32), axis=1)
    nxt = jnp.where(group_end < n_active, nxt, -1)
    return (te.astype(I32), nxt.astype(I32), n_active.astype(I32)[None],
            src.reshape(-1).astype(I32), dst.reshape(-1).astype(I32))


def _expert_kernel(te_ref, nx_ref, na_ref, cs_ref, cd_ref, xg_hbm, wg_hbm, bg_ref, wu_hbm, bu_ref, wd_hbm,
                   bd_ref, yg_hbm, xbuf, ybuf, wst, wbf, zbuf, sem_in, sem_out, sem_zero, sem_w, *,
                   n_token_tiles):
    i = pl.program_id(0)
    n_active = na_ref[0]
    slot = i % 2
    cpt = CHUNKS_PER_TILE
    w_hbm = (wg_hbm, wu_hbm, wd_hbm)

    def w_copy(n, e):
        return pltpu.make_async_copy(w_hbm[n].at[e], wst.at[n], sem_w.at[n])

    zero_starts = [j * SLOTS + TM_POST * TOP_K for j in range(n_token_tiles)]
    zero_starts += [n_token_tiles * SLOTS + j * SLACK_ROWS for j in range(2 * TM_EXPERT // SLACK_ROWS)]

    def zero_copy(n):
        return pltpu.make_async_copy(
            zbuf, yg_hbm.at[pl.ds(zero_starts[n] // CHUNK, SLACK_ROWS // CHUNK)], sem_zero.at[n])

    @pl.when(i == 0)
    def _():
        zbuf[...] = jnp.zeros_like(zbuf)
        for n in range(len(zero_starts)):
            zero_copy(n).start()

    def in_copy(c, sl, chunk):
        return pltpu.make_async_copy(xg_hbm.at[chunk], xbuf.at[sl, c], sem_in.at[sl])

    def out_copy(c, sl, chunk):
        return pltpu.make_async_copy(ybuf.at[sl, c], yg_hbm.at[chunk], sem_out.at[sl])

    def start_gather(tile, sl):
        for c in range(cpt):
            in_copy(c, sl, cs_ref[tile * cpt + c]).start()

    def wait_gather(sl):
        pltpu.make_async_copy(xg_hbm.at[pl.ds(0, cpt)], xbuf.at[sl], sem_in.at[sl]).wait()

    def wait_scatter(sl):
        pltpu.make_async_copy(ybuf.at[sl], yg_hbm.at[pl.ds(0, cpt)], sem_out.at[sl]).wait()

    @pl.when(i == 0)
    def _():
        start_gather(0, 0)
        for n in range(len(w_hbm)):
            w_copy(n, te_ref[0]).start()

    @pl.when(i + 1 < n_active)
    def _():
        start_gather(i + 1, 1 - slot)

    @pl.when(i < n_active)
    def _():
        wait_gather(slot)

        @pl.when((i == 0) | (te_ref[i] != te_ref[jnp.maximum(i - 1, 0)]))
        def _():
            for n in range(len(w_hbm)):
                w_copy(n, 0).wait()
                wbf[n] = wst[n].astype(BF16)

                @pl.when(nx_ref[i] >= 0)
                def _():
                    w_copy(n, nx_ref[i]).start()

        @pl.when(i >= 1)
        def _():
            for c in range(cpt):
                out_copy(c, 1 - slot, cd_ref[(i - 1) * cpt + c]).start()

        x = xbuf[slot].reshape(TM_EXPERT, -1)
        e = te_ref[i]
        gate = jnp.minimum(jnp.dot(x, wbf[0], preferred_element_type=F32) + bg_ref[e], SWIGLU_LIMIT)
        up = jnp.clip(jnp.dot(x, wbf[1], preferred_element_type=F32) + bu_ref[e], -SWIGLU_LIMIT, SWIGLU_LIMIT)
        act = gate * jax.nn.sigmoid(SWIGLU_ALPHA * gate) * (up + 1.0)
        y = jnp.dot(act.astype(BF16), wbf[2], preferred_element_type=F32) + bd_ref[e]

        @pl.when(i >= 2)
        def _():
            wait_scatter(slot)

        ybuf[slot] = y.astype(BF16).reshape(ybuf.shape[1:])

        @pl.when(i == 0)
        def _():
            for n in range(len(zero_starts)):
                zero_copy(n).wait()

        @pl.when(i == n_active - 1)
        def _():
            for c in range(cpt):
                out_copy(c, slot, cd_ref[i * cpt + c]).start()
            wait_scatter(slot)

            @pl.when(i >= 1)
            def _():
                wait_scatter(1 - slot)


def _experts(xg, te, nxt, n_active, src, dst, wg, bg, wu, bu, wd, bd):
    rows, d = xg.shape
    n_e, _, dff = wg.shape
    assert d == dff
    n_tiles = te.shape[0]
    n_token_tiles = rows // SLOTS
    n_zero = n_token_tiles + 2 * TM_EXPERT // SLACK_ROWS
    b_spec = pl.BlockSpec((n_e, 1, d), lambda i, *_: (0, 0, 0))
    hbm = pl.BlockSpec(memory_space=pl.ANY)
    tile_bufs = pltpu.VMEM((2, CHUNKS_PER_TILE, CHUNK, d), BF16)
    yg = pl.pallas_call(
        functools.partial(_expert_kernel, n_token_tiles=n_token_tiles),
        grid_spec=pltpu.PrefetchScalarGridSpec(
            num_scalar_prefetch=5,
            grid=(n_tiles,),
            in_specs=[hbm, hbm, b_spec, hbm, b_spec, hbm, b_spec],
            out_specs=hbm,
            scratch_shapes=[tile_bufs, tile_bufs,
                            pltpu.VMEM((3, d, dff), F32), pltpu.VMEM((3, d, dff), BF16),
                            pltpu.VMEM((SLACK_ROWS // CHUNK, CHUNK, d), BF16),
                            pltpu.SemaphoreType.DMA((2,)),
                            pltpu.SemaphoreType.DMA((2,)),
                            pltpu.SemaphoreType.DMA((n_zero,)),
                            pltpu.SemaphoreType.DMA((3,))]),
        out_shape=jax.ShapeDtypeStruct(((rows + 2 * TM_EXPERT) // CHUNK, CHUNK, d), BF16),
        compiler_params=pltpu.CompilerParams(
            dimension_semantics=("arbitrary",), vmem_limit_bytes=V7X_VMEM_LIMIT_BYTES),
        name="experts",
    )(te, nxt, n_active, src, dst, xg.reshape(rows // CHUNK, CHUNK, d), wg, bg, wu, bu, wd, bd)
    return yg.reshape(rows + 2 * TM_EXPERT, d)


def _combine_kernel(x1_ref, yg_ref, meta_ref, gn_ref, o_ref, *, final_norm):
    meta = meta_ref[...]
    tm = meta.shape[0]
    slots_i = [meta[:, k:k + 1].astype(I32) for k in range(TOP_K)]
    probs = [meta[:, TOP_K + k:TOP_K + k + 1] for k in range(TOP_K)]
    iota_s = lax.broadcasted_iota(I32, (tm, PERM_BLOCK), 1)
    xo = x1_ref[...]
    def weight_block(blk):
        wperm = jnp.zeros((tm, PERM_BLOCK), F32)
        for sk, pk in zip(slots_i, probs):
            wperm = jnp.where(iota_s == sk - blk * PERM_BLOCK, pk, wperm)
        return wperm.astype(BF16)

    n_blk = SLOTS // PERM_BLOCK
    w_next = weight_block(0)
    for blk in range(n_blk):
        w = w_next
        if blk + 1 < n_blk:
            w_next = weight_block(blk + 1)
        xo = xo + jnp.dot(w, yg_ref[blk * PERM_BLOCK:(blk + 1) * PERM_BLOCK, :], preferred_element_type=F32)
    o_ref[...] = _rms(xo, gn_ref[...]) if final_norm else xo


def _combine(x1, yg, meta, gn, final_norm):
    t, d = x1.shape
    tm = TM_POST
    return pl.pallas_call(
        functools.partial(_combine_kernel, final_norm=final_norm),
        grid=(t // tm,),
        in_specs=[pl.BlockSpec((tm, d), lambda i: (i, 0)),
                  pl.BlockSpec((SLOTS, d), lambda i: (i, 0)),
                  pl.BlockSpec((tm, LANES), lambda i: (i, 0)),
                  pl.BlockSpec((1, d), lambda i: (0, 0))],
        out_specs=pl.BlockSpec((tm, d), lambda i: (i, 0)),
        out_shape=jax.ShapeDtypeStruct((t, d), F32),
        compiler_params=pltpu.CompilerParams(
            dimension_semantics=("parallel",), vmem_limit_bytes=V7X_VMEM_LIMIT_BYTES),
        name="combine",
    )(x1, yg, meta, gn)


def kernel(x, norm_mix, w_in, b_in, sinks, rel_bias, w_branch_a, w_branch_b, w_out, norm_ffn,
           router_w, router_b, w_gate, b_gate, w_up, b_up, w_down, b_down, norm_final):
    b, s, d = x.shape
    t = b * s
    depth = w_in.shape[0]
    a_q_w = A_KV_HEADS * A_GROUP * HEAD_DIM
    a_kv_w = A_KV_HEADS * HEAD_DIM
    b_w = B_HEADS * HEAD_DIM
    n_grp = len(B_GROUPS)
    dils = tuple(dil for _, dil in B_GROUPS)
    segs, col = [], 0
    for kind, width in (("q", a_q_w), ("k", a_kv_w), ("vT", a_kv_w)):
        segs.append((col, width, kind, 1))
        col += width
    for kind in ("q", "k", "v"):
        for dil in dils:
            segs.append((col, b_w, kind, dil))
            col += b_w
    for _ in range(2):
        segs.append((col, d, "gate", 1))
        col += d
    segs = tuple(segs)
    n_a = A_KV_HEADS * A_GROUP
    nt = t // TM_POST
    max_chunks = nt * ((TM_POST * TOP_K + N_EXPERTS * (CHUNK - 1)) // CHUNK)
    n_tiles = -(-(max_chunks + N_EXPERTS * (CHUNKS_PER_TILE - 1)) // CHUNKS_PER_TILE)

    bias_a = _band_bias(rel_bias[:, :n_a], A_HALF_WINDOW, TQ_ATTN, 1, A_KV_HEADS, A_GROUP)
    bias_b = [_band_bias(rel_bias[:, n_a + gi * B_HEADS:n_a + (gi + 1) * B_HEADS],
                         win // (2 * dil), TQ_ATTN, dil, 1, B_HEADS) for gi, (win, dil) in enumerate(B_GROUPS)]

    x2d = x.reshape(t, d)
    for layer in range(depth):
        proj = _inproj(x2d, norm_mix[layer][None], w_in[layer].astype(BF16), b_in[layer][None], segs)
        qa, ka, vat = proj[:3]
        qb, kb, vb = proj[3:3 + n_grp], proj[3 + n_grp:3 + 2 * n_grp], proj[3 + 2 * n_grp:3 + 3 * n_grp]
        ga, gb = proj[-2:]

        yat = _band_attn_gqa(qa.reshape(b, s, a_q_w), ka.reshape(b, s, a_kv_w), vat, bias_a, sinks[layer],
                             n_kv=A_KV_HEADS, grp=A_GROUP, hw=A_HALF_WINDOW, tq=TQ_ATTN, block=BLOCK_ATTN_A)
        os_, ls_ = [], []
        for gi, (win, dil) in enumerate(B_GROUPS):
            sub = lambda a: a.reshape(b, s // dil, dil * b_w)
            o, lse = _band_attn_heads(sub(qb[gi]), sub(kb[gi]), sub(vb[gi]), bias_b[gi], n_heads=B_HEADS,
                                      hw=win // (2 * dil), tq=TQ_ATTN, rows_per_step=BLOCK_ATTN_B, reps=dil)
            os_.append(o.reshape(t // dil, dil * b_w))
            ls_.append(lse.reshape(t // dil, dil * b_w))

        rwt = router_w[layer].T
        rwt_hi = rwt.astype(BF16)
        rwt_split = jnp.stack([rwt_hi, (rwt - rwt_hi.astype(F32)).astype(BF16)])
        x1, xg, meta, segs_out = _post(
            x2d, yat, os_, ls_, dils, ga, gb,
            w_branch_a[layer].astype(BF16), w_branch_b[layer].astype(BF16), w_out[layer].astype(BF16),
            norm_ffn[layer][None], rwt_split, router_b[layer][:, None])

        segs3 = segs_out.reshape(nt, N_EXPERTS, LANES)
        te, nxt, n_active, src, dst = _expert_tiles(segs3[:, :, 0].astype(I32), segs3[:, :, 1].astype(I32), n_tiles)
        yg = _experts(xg, te, nxt, n_active, src, dst, w_gate[layer], b_gate[layer][:, None], w_up[layer],
                      b_up[layer][:, None], w_down[layer], b_down[layer][:, None])
        x2d = _combine(x1, yg, meta, norm_final[None], layer == depth - 1)
    return x2d.reshape(b, s, d)
```

```python
import functools

import numpy as np
import jax
import jax.numpy as jnp
from jax import lax
from jax.experimental import pallas as pl
from jax.experimental.pallas import tpu as pltpu

F32 = jnp.float32
BF16 = jnp.bfloat16
I32 = jnp.int32

HEAD_DIM = 64
A_KV_HEADS = 4
A_GROUP = 4
A_HALF_WINDOW = 128
B_GROUPS = ((128, 1), (512, 4), (2048, 16))
B_HEADS = 4
N_BUCKETS = 32
MAX_DISTANCE = 1024
N_EXPERTS = 32
TOP_K = 4
SWIGLU_LIMIT = 7.0
SWIGLU_ALPHA = 1.702
EPS = 1e-5
NEG_INF = -1e30
LOG2E = 1.4426950408889634

V7X_VMEM_LIMIT_BYTES = 56 * 1024 * 1024
LANES = 128
BF16_SUBLANES = 16

TM_PROJ = 1024
TQ_ATTN = 128
BLOCK_ATTN_A = 2048
BLOCK_ATTN_B = 2048
TM_POST = 512
TM_EXPERT = 512

CHUNK = BF16_SUBLANES
CHUNKS_PER_TILE = TM_EXPERT // CHUNK
SLOTS = -(-(TM_POST * TOP_K + N_EXPERTS * (CHUNK - 1)) // LANES) * LANES
SLACK_ROWS = SLOTS - TM_POST * TOP_K
assert (2 * TM_EXPERT) % SLACK_ROWS == 0
PERM_BLOCK = 512
assert SLOTS % PERM_BLOCK == 0


def _t5_buckets(rel):
    half = N_BUCKETS // 2
    max_exact = half // 2
    ret = np.where(rel > 0, half, 0)
    n = np.abs(rel)
    large = max_exact + (np.log(np.maximum(n, 1) / max_exact)
                         / np.log(MAX_DISTANCE / max_exact) * (half - max_exact)).astype(np.int32)
    large = np.minimum(large, half - 1)
    return (ret + np.where(n < max_exact, n, large)).astype(np.int32)


def _rms(x, g):
    return x * lax.rsqrt(jnp.mean(x * x, axis=-1, keepdims=True) + EPS) * g


def _inproj_kernel(x_ref, g_ref, w_ref, b_ref, *refs, segs):
    out_refs, scr = refs[:-1], refs[-1]
    h = _rms(x_ref[...], g_ref[...]).astype(BF16)

    for ref, (c0, width, kind, dil) in zip(out_refs, segs):
        acc = jnp.dot(h, w_ref[:, c0:c0 + width], preferred_element_type=F32) + b_ref[:, c0:c0 + width]
        if kind == "q":
            acc = acc * (HEAD_DIM ** -0.5 * LOG2E)
        elif kind == "gate":
            acc = jax.nn.sigmoid(acc)
        if kind == "vT":
            ref[...] = acc.T.astype(ref.dtype)
        elif dil == 1:
            ref[...] = acc.astype(ref.dtype)
        else:
            rows = acc.shape[0] // dil
            for c in range(width // LANES):
                scr[c] = acc[:, c * LANES:(c + 1) * LANES]
            for r in range(dil):
                for c in range(width // LANES):
                    col = r * width + c * LANES
                    ref[:, col:col + LANES] = scr[c, pl.ds(r, rows, stride=dil), :].astype(ref.dtype)


def _inproj(x2d, g, w_bf16, b, segs):
    t, d = x2d.shape
    n = w_bf16.shape[1]
    max_w = max(w for _, w, _, dil in segs if dil > 1)
    out_specs, out_shape = [], []
    for _, w, kind, dil in segs:
        if kind == "vT":
            out_specs.append(pl.BlockSpec((w, TM_PROJ), lambda i: (0, i)))
            out_shape.append(jax.ShapeDtypeStruct((w, t), BF16))
        else:
            out_specs.append(pl.BlockSpec((TM_PROJ // dil, dil * w), lambda i: (i, 0)))
            out_shape.append(jax.ShapeDtypeStruct((t // dil, dil * w), BF16))
    return pl.pallas_call(
        functools.partial(_inproj_kernel, segs=segs),
        grid=(t // TM_PROJ,),
        in_specs=[
            pl.BlockSpec((TM_PROJ, d), lambda i: (i, 0)),
            pl.BlockSpec((1, d), lambda i: (0, 0)),
            pl.BlockSpec((d, n), lambda i: (0, 0), pipeline_mode=pl.Buffered(1)),
            pl.BlockSpec((1, n), lambda i: (0, 0)),
        ],
        out_specs=out_specs,
        out_shape=out_shape,
        scratch_shapes=[pltpu.VMEM((max_w // LANES, TM_PROJ, LANES), F32)],
        compiler_params=pltpu.CompilerParams(
            dimension_semantics=("parallel",), vmem_limit_bytes=V7X_VMEM_LIMIT_BYTES),
        name="inproj",
    )(x2d, g, w_bf16, b)


def _band_bias(table, hw, tq, dil, n_kv, grp):
    tk = tq + 2 * hw
    off = np.arange(tk)[None, :] - hw - np.arange(tq)[:, None]
    band = np.abs(off) <= hw
    col = np.arange(tk)[None, :]
    masks = np.stack([band & (col >= hw), band, band & (col < hw + tq)])
    onehot = (_t5_buckets(off * dil)[..., None] == np.arange(N_BUCKETS)).astype(np.float32)
    bias = jnp.einsum("qkn,nvg->vkgq", jnp.asarray(onehot), table.astype(F32).reshape(N_BUCKETS, n_kv, grp),
                      precision=lax.Precision.HIGHEST)
    keep = np.transpose(masks, (0, 2, 1))[:, None, :, None, :]
    return jnp.where(keep, bias[None] * LOG2E, NEG_INF).reshape(3, n_kv, tk, grp * tq)


def _band_attn_gqa_kernel(q_ref, kp, kc, kn, vtp, vtc, vtn, bias_ref, sink_ref, o_ref, *, n_kv, grp, tq, hw):
    j, nt = pl.program_id(1), pl.num_programs(1)
    k = jnp.concatenate([kp[0], kc[0], kn[0]], axis=0)
    vt = jnp.concatenate([vtp[...], vtc[...], vtn[...]], axis=1)
    n_sub = q_ref.shape[1] // tq
    tk = tq + 2 * hw
    ones = jnp.ones((BF16_SUBLANES, tk), BF16)
    lane = lax.broadcasted_iota(I32, (1, grp * tq), 1)
    sinks = []
    for h in range(n_kv):
        sk = jnp.full((1, grp * tq), sink_ref[h * grp] * LOG2E, F32)
        for g in range(1, grp):
            sk = jnp.where(lane >= g * tq, sink_ref[h * grp + g] * LOG2E, sk)
        sinks.append(sk)
    units = []
    for sub in range(n_sub):
        var = 1
        if sub == 0:
            var = jnp.where(j == 0, 0, var)
        if sub == n_sub - 1:
            var = jnp.where(j == nt - 1, 2, var)
        units.extend((var, sub * tq, h) for h in range(n_kv))

    def scores(var, r0, h):
        k_h = k[r0:r0 + tk, h * HEAD_DIM:(h + 1) * HEAD_DIM]
        c0 = h * grp * HEAD_DIM
        q = jnp.concatenate([q_ref[0, r0:r0 + tq, c0 + g * HEAD_DIM:c0 + (g + 1) * HEAD_DIM]
                             for g in range(grp)], axis=0)
        st = lax.dot_general(k_h, q, (((1,), (1,)), ((), ())), preferred_element_type=F32)
        return st + bias_ref[var, h]

    st_next = scores(*units[0])
    for n, (var, r0, h) in enumerate(units):
        st = st_next
        if n + 1 < len(units):
            st_next = scores(*units[n + 1])
        sk = sinks[h]
        m = jnp.maximum(jnp.max(st, axis=0, keepdims=True), sk)
        pt = jnp.exp2((st - m).astype(BF16))
        vt_h = jnp.concatenate([vt[h * HEAD_DIM:(h + 1) * HEAD_DIM, r0:r0 + tk], ones], axis=0)
        ot = jnp.dot(vt_h, pt, preferred_element_type=F32)
        l = ot[HEAD_DIM:HEAD_DIM + 1] + jnp.exp2(sk - m)
        o = ot[:HEAD_DIM] / l
        for g in range(grp):
            c = (h * grp + g) * HEAD_DIM
            o_ref[c:c + HEAD_DIM, r0:r0 + tq] = o[:, g * tq:(g + 1) * tq].astype(o_ref.dtype)


def _band_attn_gqa(q, k, vt, bias3, sinks, *, n_kv, grp, hw, tq, block):
    b, s, _ = q.shape
    hq = n_kv * grp
    qw, kw = hq * HEAD_DIM, n_kv * HEAD_DIM
    nt = s // block
    ratio = block // hw
    nhw = s // hw
    tk = tq + 2 * hw
    assert nt * (block // tq) >= 2
    prev_j = lambda j: jnp.maximum(j * ratio - 1, 0)
    next_j = lambda j: jnp.minimum((j + 1) * ratio, nhw - 1)
    assert bias3.shape == (3, n_kv, tk, grp * tq)
    return pl.pallas_call(
        functools.partial(_band_attn_gqa_kernel, n_kv=n_kv, grp=grp, tq=tq, hw=hw),
        grid=(b, nt),
        in_specs=[
            pl.BlockSpec((1, block, qw), lambda bi, j: (bi, j, 0)),
            pl.BlockSpec((1, hw, kw), lambda bi, j: (bi, prev_j(j), 0)),
            pl.BlockSpec((1, block, kw), lambda bi, j: (bi, j, 0)),
            pl.BlockSpec((1, hw, kw), lambda bi, j: (bi, next_j(j), 0)),
            pl.BlockSpec((kw, hw), lambda bi, j: (0, bi * nhw + prev_j(j))),
            pl.BlockSpec((kw, block), lambda bi, j: (0, bi * nt + j)),
            pl.BlockSpec((kw, hw), lambda bi, j: (0, bi * nhw + next_j(j))),
            pl.BlockSpec((3, n_kv, tk, grp * tq), lambda bi, j: (0, 0, 0, 0), pipeline_mode=pl.Buffered(1)),
            pl.BlockSpec(memory_space=pltpu.SMEM),
        ],
        out_specs=pl.BlockSpec((qw, block), lambda bi, j: (0, bi * nt + j)),
        out_shape=jax.ShapeDtypeStruct((qw, b * s), BF16),
        compiler_params=pltpu.CompilerParams(
            dimension_semantics=("parallel", "parallel"), vmem_limit_bytes=V7X_VMEM_LIMIT_BYTES),
        name="band_attn_a",
    )(q, k, k, k, vt, vt, vt, bias3, sinks)


def _band_attn_heads_kernel(q_ref, kp, kc, kn, vp, vc, vn, bias_ref, o_ref, lse_ref, *, n_heads, tq, hw):
    j, nt = pl.program_id(2), pl.num_programs(2)
    k = jnp.concatenate([kp[0], kc[0], kn[0]], axis=0)
    v = jnp.concatenate([vp[0], vc[0], vn[0]], axis=0)
    n_sub = q_ref.shape[1] // tq
    tk = tq + 2 * hw
    width = n_heads * HEAD_DIM
    lane_head = lax.broadcasted_iota(I32, (tq, width), 1) // HEAD_DIM
    ones = jnp.ones((BF16_SUBLANES, tk), BF16)

    units = [(c0, sub) for c0 in range(0, q_ref.shape[2], width) for sub in range(n_sub)]

    def scores(c0, sub):
        var = 1
        if sub == 0:
            var = jnp.where(j == 0, 0, var)
        if sub == n_sub - 1:
            var = jnp.where(j == nt - 1, 2, var)
        r0 = sub * tq
        q = q_ref[0, r0:r0 + tq, c0:c0 + width]
        q_bd = jnp.concatenate([jnp.where(lane_head == h, q, jnp.zeros_like(q)) for h in range(n_heads)], axis=0)
        st = lax.dot_general(k[r0:r0 + tk, c0:c0 + width], q_bd, (((1,), (1,)), ((), ())),
                             preferred_element_type=F32)
        return st + bias_ref[var, 0]

    st_next = scores(*units[0])
    for n, (c0, sub) in enumerate(units):
        r0 = sub * tq
        st = st_next
        if n + 1 < len(units):
            st_next = scores(*units[n + 1])
        m = jnp.max(st, axis=0, keepdims=True)
        pt = jnp.exp2((st - m).astype(BF16))
        vt = jnp.concatenate([v[r0:r0 + tk, c0:c0 + width].T, ones], axis=0)
        ot = jnp.dot(vt, pt, preferred_element_type=F32)
        l = ot[width:width + 1]
        lse = m + jnp.log2(l)
        o_sel = jnp.concatenate([ot[h * HEAD_DIM:(h + 1) * HEAD_DIM, h * tq:(h + 1) * tq] / l[:, h * tq:(h + 1) * tq]
                                 for h in range(n_heads)], axis=0)
        lse_sel = jnp.concatenate([jnp.broadcast_to(lse[:, h * tq:(h + 1) * tq], (HEAD_DIM, tq))
                                   for h in range(n_heads)], axis=0)
        o_ref[0, r0:r0 + tq, c0:c0 + width] = o_sel.T.astype(o_ref.dtype)
        lse_ref[0, r0:r0 + tq, c0:c0 + width] = lse_sel.T


def _band_attn_heads(q, k, v, bias3, *, n_heads, hw, tq, rows_per_step, reps):
    b, l, _ = q.shape
    block = min(rows_per_step, l)
    nt = l // block
    ratio = block // hw
    nhw = l // hw
    tk = tq + 2 * hw
    assert nt * (block // tq) >= 2
    res_per_step = max(1, min(reps, rows_per_step // block))
    qw = kw = res_per_step * n_heads * HEAD_DIM

    prev = lambda bi, r, j: (bi, jnp.maximum(j * ratio - 1, 0), r)
    cur = lambda bi, r, j: (bi, j, r)
    nxt = lambda bi, r, j: (bi, jnp.minimum((j + 1) * ratio, nhw - 1), r)
    in_specs = [
        pl.BlockSpec((1, block, qw), cur),
        pl.BlockSpec((1, hw, kw), prev), pl.BlockSpec((1, block, kw), cur), pl.BlockSpec((1, hw, kw), nxt),
        pl.BlockSpec((1, hw, kw), prev), pl.BlockSpec((1, block, kw), cur), pl.BlockSpec((1, hw, kw), nxt),
        pl.BlockSpec((3, 1, tk, n_heads * tq), lambda bi, r, j: (0, 0, 0, 0), pipeline_mode=pl.Buffered(1)),
    ]
    return pl.pallas_call(
        functools.partial(_band_attn_heads_kernel, n_heads=n_heads, tq=tq, hw=hw),
        grid=(b, reps // res_per_step, nt),
        in_specs=in_specs,
        out_specs=[pl.BlockSpec((1, block, qw), cur), pl.BlockSpec((1, block, qw), cur)],
        out_shape=[jax.ShapeDtypeStruct(q.shape, BF16), jax.ShapeDtypeStruct(q.shape, F32)],
        compiler_params=pltpu.CompilerParams(
            dimension_semantics=("parallel", "parallel", "parallel"),
            vmem_limit_bytes=V7X_VMEM_LIMIT_BYTES),
        name="band_attn_b",
    )(q, k, k, k, v, v, v, bias3)


def _token_order(ref, scr, dil):
    if dil == 1:
        return ref[...].astype(F32)
    n = ref.shape[0]
    width = ref.shape[1] // dil
    for r in range(dil):
        for c in range(width // LANES):
            col = r * width + c * LANES
            scr[c, pl.ds(r, n, stride=dil), :] = ref[:, col:col + LANES].astype(F32)
    return jnp.concatenate([scr[c] for c in range(width // LANES)], axis=1)


def _post_kernel(x_ref, ya_ref, o1, o2, o3, l1, l2, l3, ga_ref, gb_ref, wa_ref, wb_ref, wo_ref,
                 gf_ref, rwt_ref, rb_ref, tri_ref, x1_ref, xg_ref, meta_ref, segs_ref, *scrs, dils):
    os_ = [_token_order(r, scrs[2 * i], dil) for i, (r, dil) in enumerate(zip((o1, o2, o3), dils))]
    ls = [_token_order(r, scrs[2 * i + 1], dil) for i, (r, dil) in enumerate(zip((l1, l2, l3), dils))]
    mx = jnp.maximum(jnp.maximum(ls[0], ls[1]), ls[2])
    es = [jnp.exp2(l - mx) for l in ls]
    den = es[0] + es[1] + es[2]
    yb = (es[0] * os_[0] + es[1] * os_[1] + es[2] * os_[2]) / den
    za = lax.dot_general(ya_ref[...], wa_ref[...], (((0,), (0,)), ((), ())), preferred_element_type=F32)
    zb = jnp.dot(yb.astype(BF16), wb_ref[...], preferred_element_type=F32)
    merged = ga_ref[...].astype(F32) * za + gb_ref[...].astype(F32) * zb
    x1 = x_ref[...] + jnp.dot(merged.astype(BF16), wo_ref[...], preferred_element_type=F32)
    x1_ref[...] = x1
    h2 = _rms(x1, gf_ref[...])

    h_hi = h2.astype(BF16)
    h_lo = (h2 - h_hi.astype(F32)).astype(BF16)
    nt_dot = lambda a, b_: lax.dot_general(a, b_, (((1,), (1,)), ((), ())), preferred_element_type=F32)
    n_e = rwt_ref.shape[1]
    both = nt_dot(rwt_ref[...].reshape(2 * n_e, -1), h_hi)
    logits = (both[:n_e] + nt_dot(rwt_ref[0], h_lo) + both[n_e:]) + rb_ref[...]
    n_e, tm = logits.shape
    iota_e = lax.broadcasted_iota(I32, (n_e, tm), 0)
    work = logits
    vals, hots = [], []
    for _ in range(TOP_K):
        mk = jnp.max(work, axis=0, keepdims=True)
        ik = jnp.min(jnp.where(work == mk, iota_e, n_e), axis=0, keepdims=True)
        hot = iota_e == ik
        vals.append(mk)
        hots.append(hot)
        work = jnp.where(hot, -jnp.inf, work)
    exps = [jnp.exp(vk - vals[0]) for vk in vals]
    tot = exps[0] + exps[1] + exps[2] + exps[3]
    probs = [ek / tot for ek in exps]

    sel = jnp.zeros((n_e, tm), F32)
    for hot in hots:
        sel = sel + jnp.where(hot, 1.0, 0.0)
    cnt = jnp.sum(sel, axis=1, keepdims=True)
    pcnt = jnp.floor((cnt + (CHUNK - 1)) / CHUNK) * CHUNK
    r_i = lax.broadcasted_iota(I32, (n_e, n_e), 0)
    c_i = lax.broadcasted_iota(I32, (n_e, n_e), 1)
    pcnt_row = jnp.sum(jnp.where(r_i == c_i, pcnt, 0.0), axis=0, keepdims=True)
    seg_off = jnp.sum(jnp.where(c_i < r_i, pcnt_row, 0.0), axis=1, keepdims=True)
    before = jnp.dot(sel.astype(BF16), tri_ref[...], preferred_element_type=F32)
    slot_of = seg_off + before
    slots = [jnp.sum(jnp.where(hot, slot_of, 0.0), axis=0, keepdims=True) for hot in hots]

    iota_s = lax.broadcasted_iota(I32, (PERM_BLOCK, tm), 0)
    slots_i = [sk.astype(I32) for sk in slots]
    for blk in range(SLOTS // PERM_BLOCK):
        perm = jnp.zeros((PERM_BLOCK, tm), F32)
        for sk in slots_i:
            perm = jnp.where(iota_s == sk - blk * PERM_BLOCK, 1.0, perm)
        xg_ref[blk * PERM_BLOCK:(blk + 1) * PERM_BLOCK, :] = jnp.dot(
            perm.astype(BF16), h_hi, preferred_element_type=F32).astype(BF16)

    rows = slots + probs
    meta_t = jnp.concatenate(rows + [jnp.zeros((LANES - len(rows), tm), F32)], axis=0)
    meta_ref[...] = meta_t.T
    lane = lax.broadcasted_iota(I32, (n_e, LANES), 1)
    segs_ref[...] = jnp.where(lane == 0, pcnt, jnp.where(lane == 1, seg_off, 0.0))


def _post(x2d, ya, os_, ls_, dils, ga, gb, wa, wb, wo, gf, rwt, rb):
    t, d = x2d.shape
    bw = os_[0].shape[1] // dils[0]
    tm = TM_POST
    nt = t // tm
    tri = jnp.asarray(np.triu(np.ones((tm, tm), np.float32), k=1), dtype=BF16)
    row = lambda w, dil=1: pl.BlockSpec((tm // dil, dil * w), lambda i: (i, 0))
    full = lambda a: pl.BlockSpec(a.shape, lambda i: (0,) * a.ndim)
    return pl.pallas_call(
        functools.partial(_post_kernel, dils=dils),
        grid=(nt,),
        in_specs=[row(d), pl.BlockSpec((ya.shape[0], tm), lambda i: (0, i))]
                 + [row(bw, dil) for dil in dils] + [row(bw, dil) for dil in dils]
                 + [row(d), row(d), full(wa), full(wb), full(wo), full(gf), full(rwt), full(rb), full(tri)],
        scratch_shapes=[pltpu.VMEM((bw // LANES, tm, LANES), F32) for _ in range(2 * len(dils))],
        out_specs=[row(d), pl.BlockSpec((SLOTS, d), lambda i: (i, 0)), row(LANES),
                   pl.BlockSpec((N_EXPERTS, LANES), lambda i: (i, 0))],
        out_shape=[jax.ShapeDtypeStruct((t, d), F32), jax.ShapeDtypeStruct((nt * SLOTS, d), BF16),
                   jax.ShapeDtypeStruct((t, LANES), F32), jax.ShapeDtypeStruct((nt * N_EXPERTS, LANES), F32)],
        compiler_params=pltpu.CompilerParams(
            dimension_semantics=("parallel",), vmem_limit_bytes=V7X_VMEM_LIMIT_BYTES),
        name="post_attn",
    )(x2d, ya, *os_, *ls_, ga, gb, wa, wb, wo, gf, rwt, rb, tri)


def _expert_tiles(pcnt, seg_off, n_tiles):
    nt, n_e = pcnt.shape
    cpt = CHUNKS_PER_TILE
    nch = (pcnt // CHUNK).T
    cum = jnp.cumsum(nch, axis=1)
    total = cum[:, -1]
    tiles_e = (total + cpt - 1) // cpt
    tile_end = jnp.cumsum(tiles_e)
    n_active = tile_end[-1]
    i = jnp.arange(n_tiles, dtype=I32)
    last = jnp.minimum(i, n_active - 1)
    te = jnp.sum((last[:, None] >= tile_end[None, :]).astype(I32), axis=1)
    hot_e = (te[:, None] == jnp.arange(n_e, dtype=I32)[None, :])
    pick = lambda tab: jnp.sum(jnp.where(hot_e[:, :, None], tab[None], 0), axis=1)
    tile_start = jnp.sum(jnp.where(hot_e, (tile_end - tiles_e)[None, :], 0), axis=1)
    total_t = jnp.sum(jnp.where(hot_e, total[None, :], 0), axis=1)
    q = (last - tile_start)[:, None] * cpt + jnp.arange(cpt, dtype=I32)[None, :]
    valid = (q < total_t[:, None]) & (i < n_active)[:, None]
    cum_t, nch_t = pick(cum), pick(nch)
    chunk0_t = pick((seg_off.T + jnp.arange(nt, dtype=I32)[None, :] * SLOTS) // CHUNK)
    jj = jnp.sum((q[:, :, None] >= cum_t[:, None, :]).astype(I32), axis=2)
    hot_j = jj[:, :, None] == jnp.arange(nt, dtype=I32)[None, None, :]
    first = jnp.sum(jnp.where(hot_j, (cum_t - nch_t)[:, None, :], 0), axis=2)
    base = jnp.sum(jnp.where(hot_j, chunk0_t[:, None, :], 0), axis=2)
    src = jnp.where(valid, base + q - first, 0)
    trash = nt * SLOTS // CHUNK + (i % 2)[:, None] * cpt + jnp.arange(cpt, dtype=I32)[None, :]
    dst = jnp.where(valid, src, trash)
    group_end = jnp.sum(jnp.where(hot_e, tile_end[None, :], 0), axis=1)
    nxt = jnp.sum((group_end[:, None] >= tile_end[None, :]).astype(I32), axis=1)
    nxt = jnp.where(group_end < n_active, nxt, -1)
    return (te.astype(I32), nxt.astype(I32), n_active.astype(I32)[None],
            src.reshape(-1).astype(I32), dst.reshape(-1).astype(I32))


def _expert_kernel(te_ref, nx_ref, na_ref, cs_ref, cd_ref, xg_hbm, wg_hbm, bg_ref, wu_hbm, bu_ref, wd_hbm,
                   bd_ref, yg_hbm, xbuf, ybuf, wst, wbf, zbuf, sem_in, sem_out, sem_zero, sem_w, *,
                   n_token_tiles):
    i = pl.program_id(0)
    n_active = na_ref[0]
    slot = i % 2
    cpt = CHUNKS_PER_TILE
    w_hbm = (wg_hbm, wu_hbm, wd_hbm)

    def w_copy(n, e):
        return pltpu.make_async_copy(w_hbm[n].at[e], wst.at[n], sem_w.at[n])

    zero_starts = [j * SLOTS + TM_POST * TOP_K for j in range(n_token_tiles)]
    zero_starts += [n_token_tiles * SLOTS + j * SLACK_ROWS for j in range(2 * TM_EXPERT // SLACK_ROWS)]

    def zero_copy(n):
        return pltpu.make_async_copy(
            zbuf, yg_hbm.at[pl.ds(zero_starts[n] // CHUNK, SLACK_ROWS // CHUNK)], sem_zero.at[n])

    @pl.when(i == 0)
    def _():
        zbuf[...] = jnp.zeros_like(zbuf)
        for n in range(len(zero_starts)):
            zero_copy(n).start()

    def in_copy(c, sl, chunk):
        return pltpu.make_async_copy(xg_hbm.at[chunk], xbuf.at[sl, c], sem_in.at[sl])

    def out_copy(c, sl, chunk):
        return pltpu.make_async_copy(ybuf.at[sl, c], yg_hbm.at[chunk], sem_out.at[sl])

    def start_gather(tile, sl):
        for c in range(cpt):
            in_copy(c, sl, cs_ref[tile * cpt + c]).start()

    def wait_gather(sl):
        pltpu.make_async_copy(xg_hbm.at[pl.ds(0, cpt)], xbuf.at[sl], sem_in.at[sl]).wait()

    def wait_scatter(sl):
        pltpu.make_async_copy(ybuf.at[sl], yg_hbm.at[pl.ds(0, cpt)], sem_out.at[sl]).wait()

    @pl.when(i == 0)
    def _():
        start_gather(0, 0)
        for n in range(len(w_hbm)):
            w_copy(n, te_ref[0]).start()

    @pl.when(i + 1 < n_active)
    def _():
        start_gather(i + 1, 1 - slot)

    @pl.when(i < n_active)
    def _():
        wait_gather(slot)

        @pl.when((i == 0) | (te_ref[i] != te_ref[jnp.maximum(i - 1, 0)]))
        def _():
            for n in range(len(w_hbm)):
                w_copy(n, 0).wait()
                wbf[n] = wst[n].astype(BF16)

                @pl.when(nx_ref[i] >= 0)
                def _():
                    w_copy(n, nx_ref[i]).start()

        @pl.when(i >= 1)
        def _():
            for c in range(cpt):
                out_copy(c, 1 - slot, cd_ref[(i - 1) * cpt + c]).start()

        x = xbuf[slot].reshape(TM_EXPERT, -1)
        e = te_ref[i]
        gate = jnp.minimum(jnp.dot(x, wbf[0], preferred_element_type=F32) + bg_ref[e], SWIGLU_LIMIT)
        up = jnp.clip(jnp.dot(x, wbf[1], preferred_element_type=F32) + bu_ref[e], -SWIGLU_LIMIT, SWIGLU_LIMIT)
        act = gate * jax.nn.sigmoid(SWIGLU_ALPHA * gate) * (up + 1.0)
        y = jnp.dot(act.astype(BF16), wbf[2], preferred_element_type=F32) + bd_ref[e]

        @pl.when(i >= 2)
        def _():
            wait_scatter(slot)

        ybuf[slot] = y.astype(BF16).reshape(ybuf.shape[1:])

        @pl.when(i == 0)
        def _():
            for n in range(len(zero_starts)):
                zero_copy(n).wait()

        @pl.when(i == n_active - 1)
        def _():
            for c in range(cpt):
                out_copy(c, slot, cd_ref[i * cpt + c]).start()
            wait_scatter(slot)

            @pl.when(i >= 1)
            def _():
                wait_scatter(1 - slot)


def _experts(xg, te, nxt, n_active, src, dst, wg, bg, wu, bu, wd, bd):
    rows, d = xg.shape
    n_e, _, dff = wg.shape
    assert d == dff
    n_tiles = te.shape[0]
    n_token_tiles = rows // SLOTS
    n_zero = n_token_tiles + 2 * TM_EXPERT // SLACK_ROWS
    b_spec = pl.BlockSpec((n_e, 1, d), lambda i, *_: (0, 0, 0))
    hbm = pl.BlockSpec(memory_space=pl.ANY)
    tile_bufs = pltpu.VMEM((2, CHUNKS_PER_TILE, CHUNK, d), BF16)
    yg = pl.pallas_call(
        functools.partial(_expert_kernel, n_token_tiles=n_token_tiles),
        grid_spec=pltpu.PrefetchScalarGridSpec(
            num_scalar_prefetch=5,
            grid=(n_tiles,),
            in_specs=[hbm, hbm, b_spec, hbm, b_spec, hbm, b_spec],
            out_specs=hbm,
            scratch_shapes=[tile_bufs, tile_bufs,
                            pltpu.VMEM((3, d, dff), F32), pltpu.VMEM((3, d, dff), BF16),
                            pltpu.VMEM((SLACK_ROWS // CHUNK, CHUNK, d), BF16),
                            pltpu.SemaphoreType.DMA((2,)),
                            pltpu.SemaphoreType.DMA((2,)),
                            pltpu.SemaphoreType.DMA((n_zero,)),
                            pltpu.SemaphoreType.DMA((3,))]),
        out_shape=jax.ShapeDtypeStruct(((rows + 2 * TM_EXPERT) // CHUNK, CHUNK, d), BF16),
        compiler_params=pltpu.CompilerParams(
            dimension_semantics=("arbitrary",), vmem_limit_bytes=V7X_VMEM_LIMIT_BYTES),
        name="experts",
    )(te, nxt, n_active, src, dst, xg.reshape(rows // CHUNK, CHUNK, d), wg, bg, wu, bu, wd, bd)
    return yg.reshape(rows + 2 * TM_EXPERT, d)


def _combine_kernel(x1_ref, yg_ref, meta_ref, gn_ref, o_ref, *, final_norm):
    meta = meta_ref[...]
    tm = meta.shape[0]
    slots_i = [meta[:, k:k + 1].astype(I32) for k in range(TOP_K)]
    probs = [meta[:, TOP_K + k:TOP_K + k + 1] for k in range(TOP_K)]
    iota_s = lax.broadcasted_iota(I32, (tm, PERM_BLOCK), 1)
    xo = x1_ref[...]
    for blk in range(SLOTS // PERM_BLOCK):
        base = blk * PERM_BLOCK
        wperm = jnp.zeros((tm, PERM_BLOCK), F32)
        for sk, pk in zip(slots_i, probs):
            wperm = jnp.where(iota_s == sk - base, pk, wperm)
        xo = xo + jnp.dot(wperm.astype(BF16), yg_ref[base:base + PERM_BLOCK, :], preferred_element_type=F32)
    o_ref[...] = _rms(xo, gn_ref[...]) if final_norm else xo


def _combine(x1, yg, meta, gn, final_norm):
    t, d = x1.shape
    tm = TM_POST
    return pl.pallas_call(
        functools.partial(_combine_kernel, final_norm=final_norm),
        grid=(t // tm,),
        in_specs=[pl.BlockSpec((tm, d), lambda i: (i, 0)),
                  pl.BlockSpec((SLOTS, d), lambda i: (i, 0)),
                  pl.BlockSpec((tm, LANES), lambda i: (i, 0)),
                  pl.BlockSpec((1, d), lambda i: (0, 0))],
        out_specs=pl.BlockSpec((tm, d), lambda i: (i, 0)),
        out_shape=jax.ShapeDtypeStruct((t, d), F32),
        compiler_params=pltpu.CompilerParams(
            dimension_semantics=("parallel",), vmem_limit_bytes=V7X_VMEM_LIMIT_BYTES),
        name="combine",
    )(x1, yg, meta, gn)


def kernel(x, norm_mix, w_in, b_in, sinks, rel_bias, w_branch_a, w_branch_b, w_out, norm_ffn,
           router_w, router_b, w_gate, b_gate, w_up, b_up, w_down, b_down, norm_final):
    b, s, d = x.shape
    t = b * s
    depth = w_in.shape[0]
    a_q_w = A_KV_HEADS * A_GROUP * HEAD_DIM
    a_kv_w = A_KV_HEADS * HEAD_DIM
    b_w = B_HEADS * HEAD_DIM
    n_grp = len(B_GROUPS)
    dils = tuple(dil for _, dil in B_GROUPS)
    segs, col = [], 0
    for kind, width in (("q", a_q_w), ("k", a_kv_w), ("vT", a_kv_w)):
        segs.append((col, width, kind, 1))
        col += width
    for kind in ("q", "k", "v"):
        for dil in dils:
            segs.append((col, b_w, kind, dil))
            col += b_w
    for _ in range(2):
        segs.append((col, d, "gate", 1))
        col += d
    segs = tuple(segs)
    n_a = A_KV_HEADS * A_GROUP
    nt = t // TM_POST
    max_chunks = nt * ((TM_POST * TOP_K + N_EXPERTS * (CHUNK - 1)) // CHUNK)
    n_tiles = -(-(max_chunks + N_EXPERTS * (CHUNKS_PER_TILE - 1)) // CHUNKS_PER_TILE)

    bias_a = _band_bias(rel_bias[:, :n_a], A_HALF_WINDOW, TQ_ATTN, 1, A_KV_HEADS, A_GROUP)
    bias_b = [_band_bias(rel_bias[:, n_a + gi * B_HEADS:n_a + (gi + 1) * B_HEADS],
                         win // (2 * dil), TQ_ATTN, dil, 1, B_HEADS) for gi, (win, dil) in enumerate(B_GROUPS)]

    x2d = x.reshape(t, d)
    for layer in range(depth):
        proj = _inproj(x2d, norm_mix[layer][None], w_in[layer].astype(BF16), b_in[layer][None], segs)
        qa, ka, vat = proj[:3]
        qb, kb, vb = proj[3:3 + n_grp], proj[3 + n_grp:3 + 2 * n_grp], proj[3 + 2 * n_grp:3 + 3 * n_grp]
        ga, gb = proj[-2:]

        yat = _band_attn_gqa(qa.reshape(b, s, a_q_w), ka.reshape(b, s, a_kv_w), vat, bias_a, sinks[layer],
                             n_kv=A_KV_HEADS, grp=A_GROUP, hw=A_HALF_WINDOW, tq=TQ_ATTN, block=BLOCK_ATTN_A)
        os_, ls_ = [], []
        for gi, (win, dil) in enumerate(B_GROUPS):
            sub = lambda a: a.reshape(b, s // dil, dil * b_w)
            o, lse = _band_attn_heads(sub(qb[gi]), sub(kb[gi]), sub(vb[gi]), bias_b[gi], n_heads=B_HEADS,
                                      hw=win // (2 * dil), tq=TQ_ATTN, rows_per_step=BLOCK_ATTN_B, reps=dil)
            os_.append(o.reshape(t // dil, dil * b_w))
            ls_.append(lse.reshape(t // dil, dil * b_w))

        rwt = router_w[layer].T
        rwt_hi = rwt.astype(BF16)
        rwt_split = jnp.stack([rwt_hi, (rwt - rwt_hi.astype(F32)).astype(BF16)])
        x1, xg, meta, segs_out = _post(
            x2d, yat, os_, ls_, dils, ga, gb,
            w_branch_a[layer].astype(BF16), w_branch_b[layer].astype(BF16), w_out[layer].astype(BF16),
            norm_ffn[layer][None], rwt_split, router_b[layer][:, None])

        segs3 = segs_out.reshape(nt, N_EXPERTS, LANES)
        te, nxt, n_active, src, dst = _expert_tiles(segs3[:, :, 0].astype(I32), segs3[:, :, 1].astype(I32), n_tiles)
        yg = _experts(xg, te, nxt, n_active, src, dst, w_gate[layer], b_gate[layer][:, None], w_up[layer],
                      b_up[layer][:, None], w_down[layer], b_down[layer][:, None])
        x2d = _combine(x1, yg, meta, norm_final[None], layer == depth - 1)
    return x2d.reshape(b, s, d)
```

```python
import functools

import numpy as np
import jax
import jax.numpy as jnp
from jax import lax
from jax.experimental import pallas as pl
from jax.experimental.pallas import tpu as pltpu

F32 = jnp.float32
BF16 = jnp.bfloat16
I32 = jnp.int32

HEAD_DIM = 64
A_KV_HEADS = 4
A_GROUP = 4
A_HALF_WINDOW = 128
B_GROUPS = ((128, 1), (512, 4), (2048, 16))
B_HEADS = 4
N_BUCKETS = 32
MAX_DISTANCE = 1024
N_EXPERTS = 32
TOP_K = 4
SWIGLU_LIMIT = 7.0
SWIGLU_ALPHA = 1.702
EPS = 1e-5
NEG_INF = -1e30
LOG2E = 1.4426950408889634

V7X_VMEM_LIMIT_BYTES = 56 * 1024 * 1024
LANES = 128
BF16_SUBLANES = 16

TM_PROJ = 1024
TQ_ATTN = 128
BLOCK_ATTN_A = 2048
BLOCK_ATTN_B = 4096
TM_POST = 512
TM_EXPERT = 512

CHUNK = BF16_SUBLANES
CHUNKS_PER_TILE = TM_EXPERT // CHUNK
SLOTS = -(-(TM_POST * TOP_K + N_EXPERTS * (CHUNK - 1)) // LANES) * LANES
SLACK_ROWS = SLOTS - TM_POST * TOP_K
assert (2 * TM_EXPERT) % SLACK_ROWS == 0
PERM_BLOCK = 512
assert SLOTS % PERM_BLOCK == 0


def _t5_buckets(rel):
    half = N_BUCKETS // 2
    max_exact = half // 2
    ret = np.where(rel > 0, half, 0)
    n = np.abs(rel)
    large = max_exact + (np.log(np.maximum(n, 1) / max_exact)
                         / np.log(MAX_DISTANCE / max_exact) * (half - max_exact)).astype(np.int32)
    large = np.minimum(large, half - 1)
    return (ret + np.where(n < max_exact, n, large)).astype(np.int32)


def _rms(x, g):
    return x * lax.rsqrt(jnp.mean(x * x, axis=-1, keepdims=True) + EPS) * g


def _inproj_kernel(x_ref, g_ref, w_ref, b_ref, *refs, segs):
    out_refs, scr = refs[:-1], refs[-1]
    h = _rms(x_ref[...], g_ref[...]).astype(BF16)

    for ref, (c0, width, kind, dil) in zip(out_refs, segs):
        acc = jnp.dot(h, w_ref[:, c0:c0 + width], preferred_element_type=F32) + b_ref[:, c0:c0 + width]
        if kind == "q":
            acc = acc * (HEAD_DIM ** -0.5 * LOG2E)
        elif kind == "gate":
            acc = jax.nn.sigmoid(acc)
        if kind == "vT":
            ref[...] = acc.T.astype(ref.dtype)
        elif dil == 1:
            ref[...] = acc.astype(ref.dtype)
        else:
            rows = acc.shape[0] // dil
            for c in range(width // LANES):
                scr[c] = acc[:, c * LANES:(c + 1) * LANES]
            for r in range(dil):
                for c in range(width // LANES):
                    col = r * width + c * LANES
                    ref[:, col:col + LANES] = scr[c, pl.ds(r, rows, stride=dil), :].astype(ref.dtype)


def _inproj(x2d, g, w_bf16, b, segs):
    t, d = x2d.shape
    n = w_bf16.shape[1]
    max_w = max(w for _, w, _, dil in segs if dil > 1)
    out_specs, out_shape = [], []
    for _, w, kind, dil in segs:
        if kind == "vT":
            out_specs.append(pl.BlockSpec((w, TM_PROJ), lambda i: (0, i)))
            out_shape.append(jax.ShapeDtypeStruct((w, t), BF16))
        else:
            out_specs.append(pl.BlockSpec((TM_PROJ // dil, dil * w), lambda i: (i, 0)))
            out_shape.append(jax.ShapeDtypeStruct((t // dil, dil * w), BF16))
    return pl.pallas_call(
        functools.partial(_inproj_kernel, segs=segs),
        grid=(t // TM_PROJ,),
        in_specs=[
            pl.BlockSpec((TM_PROJ, d), lambda i: (i, 0)),
            pl.BlockSpec((1, d), lambda i: (0, 0)),
            pl.BlockSpec((d, n), lambda i: (0, 0), pipeline_mode=pl.Buffered(1)),
            pl.BlockSpec((1, n), lambda i: (0, 0)),
        ],
        out_specs=out_specs,
        out_shape=out_shape,
        scratch_shapes=[pltpu.VMEM((max_w // LANES, TM_PROJ, LANES), F32)],
        compiler_params=pltpu.CompilerParams(
            dimension_semantics=("parallel",), vmem_limit_bytes=V7X_VMEM_LIMIT_BYTES),
        name="inproj",
    )(x2d, g, w_bf16, b)


def _band_bias(table, hw, tq, dil, n_kv, grp):
    tk = tq + 2 * hw
    off = np.arange(tk)[None, :] - hw - np.arange(tq)[:, None]
    band = np.abs(off) <= hw
    col = np.arange(tk)[None, :]
    masks = np.stack([band & (col >= hw), band, band & (col < hw + tq)])
    onehot = (_t5_buckets(off * dil)[..., None] == np.arange(N_BUCKETS)).astype(np.float32)
    bias = jnp.einsum("qkn,nvg->vkgq", jnp.asarray(onehot), table.astype(F32).reshape(N_BUCKETS, n_kv, grp),
                      precision=lax.Precision.HIGHEST)
    keep = np.transpose(masks, (0, 2, 1))[:, None, :, None, :]
    return jnp.where(keep, bias[None] * LOG2E, NEG_INF).reshape(3, n_kv, tk, grp * tq)


def _band_attn_gqa_kernel(q_ref, kp, kc, kn, vtp, vtc, vtn, bias_ref, sink_ref, o_ref, *, n_kv, grp, tq, hw):
    j, nt = pl.program_id(1), pl.num_programs(1)
    k = jnp.concatenate([kp[0], kc[0], kn[0]], axis=0)
    vt = jnp.concatenate([vtp[...], vtc[...], vtn[...]], axis=1)
    n_sub = q_ref.shape[1] // tq
    tk = tq + 2 * hw
    ones = jnp.ones((BF16_SUBLANES, tk), BF16)
    lane = lax.broadcasted_iota(I32, (1, grp * tq), 1)
    sinks = []
    for h in range(n_kv):
        sk = jnp.full((1, grp * tq), sink_ref[h * grp] * LOG2E, F32)
        for g in range(1, grp):
            sk = jnp.where(lane >= g * tq, sink_ref[h * grp + g] * LOG2E, sk)
        sinks.append(sk)
    units = []
    for sub in range(n_sub):
        var = 1
        if sub == 0:
            var = jnp.where(j == 0, 0, var)
        if sub == n_sub - 1:
            var = jnp.where(j == nt - 1, 2, var)
        units.extend((var, sub * tq, h) for h in range(n_kv))

    def scores(var, r0, h):
        k_h = k[r0:r0 + tk, h * HEAD_DIM:(h + 1) * HEAD_DIM]
        c0 = h * grp * HEAD_DIM
        q = jnp.concatenate([q_ref[0, r0:r0 + tq, c0 + g * HEAD_DIM:c0 + (g + 1) * HEAD_DIM]
                             for g in range(grp)], axis=0)
        st = lax.dot_general(k_h, q, (((1,), (1,)), ((), ())), preferred_element_type=F32)
        return st + bias_ref[var, h]

    st_next = scores(*units[0])
    for n, (var, r0, h) in enumerate(units):
        st = st_next
        if n + 1 < len(units):
            st_next = scores(*units[n + 1])
        sk = sinks[h]
        m = jnp.maximum(jnp.max(st, axis=0, keepdims=True), sk)
        pt = jnp.exp2((st - m).astype(BF16))
        vt_h = jnp.concatenate([vt[h * HEAD_DIM:(h + 1) * HEAD_DIM, r0:r0 + tk], ones], axis=0)
        ot = jnp.dot(vt_h, pt, preferred_element_type=F32)
        l = ot[HEAD_DIM:HEAD_DIM + 1] + jnp.exp2(sk - m)
        o = ot[:HEAD_DIM] / l
        for g in range(grp):
            c = (h * grp + g) * HEAD_DIM
            o_ref[c:c + HEAD_DIM, r0:r0 + tq] = o[:, g * tq:(g + 1) * tq].astype(o_ref.dtype)


def _band_attn_gqa(q, k, vt, bias3, sinks, *, n_kv, grp, hw, tq, block):
    b, s, _ = q.shape
    hq = n_kv * grp
    qw, kw = hq * HEAD_DIM, n_kv * HEAD_DIM
    nt = s // block
    ratio = block // hw
    nhw = s // hw
    tk = tq + 2 * hw
    assert nt * (block // tq) >= 2
    prev_j = lambda j: jnp.maximum(j * ratio - 1, 0)
    next_j = lambda j: jnp.minimum((j + 1) * ratio, nhw - 1)
    assert bias3.shape == (3, n_kv, tk, grp * tq)
    return pl.pallas_call(
        functools.partial(_band_attn_gqa_kernel, n_kv=n_kv, grp=grp, tq=tq, hw=hw),
        grid=(b, nt),
        in_specs=[
            pl.BlockSpec((1, block, qw), lambda bi, j: (bi, j, 0)),
            pl.BlockSpec((1, hw, kw), lambda bi, j: (bi, prev_j(j), 0)),
            pl.BlockSpec((1, block, kw), lambda bi, j: (bi, j, 0)),
            pl.BlockSpec((1, hw, kw), lambda bi, j: (bi, next_j(j), 0)),
            pl.BlockSpec((kw, hw), lambda bi, j: (0, bi * nhw + prev_j(j))),
            pl.BlockSpec((kw, block), lambda bi, j: (0, bi * nt + j)),
            pl.BlockSpec((kw, hw), lambda bi, j: (0, bi * nhw + next_j(j))),
            pl.BlockSpec((3, n_kv, tk, grp * tq), lambda bi, j: (0, 0, 0, 0), pipeline_mode=pl.Buffered(1)),
            pl.BlockSpec(memory_space=pltpu.SMEM),
        ],
        out_specs=pl.BlockSpec((qw, block), lambda bi, j: (0, bi * nt + j)),
        out_shape=jax.ShapeDtypeStruct((qw, b * s), BF16),
        compiler_params=pltpu.CompilerParams(
            dimension_semantics=("parallel", "parallel"), vmem_limit_bytes=V7X_VMEM_LIMIT_BYTES),
        name="band_attn_a",
    )(q, k, k, k, vt, vt, vt, bias3, sinks)


def _band_attn_heads_kernel(q_ref, kp, kc, kn, vp, vc, vn, bias_ref, o_ref, lse_ref, *, n_heads, tq, hw):
    j, nt = pl.program_id(2), pl.num_programs(2)
    k = jnp.concatenate([kp[0], kc[0], kn[0]], axis=0)
    v = jnp.concatenate([vp[0], vc[0], vn[0]], axis=0)
    n_sub = q_ref.shape[1] // tq
    tk = tq + 2 * hw
    width = n_heads * HEAD_DIM
    lane_head = lax.broadcasted_iota(I32, (tq, width), 1) // HEAD_DIM
    ones = jnp.ones((BF16_SUBLANES, tk), BF16)

    units = [(c0, sub) for c0 in range(0, q_ref.shape[2], width) for sub in range(n_sub)]

    def scores(c0, sub):
        var = 1
        if sub == 0:
            var = jnp.where(j == 0, 0, var)
        if sub == n_sub - 1:
            var = jnp.where(j == nt - 1, 2, var)
        r0 = sub * tq
        q = q_ref[0, r0:r0 + tq, c0:c0 + width]
        q_bd = jnp.concatenate([jnp.where(lane_head == h, q, jnp.zeros_like(q)) for h in range(n_heads)], axis=0)
        st = lax.dot_general(k[r0:r0 + tk, c0:c0 + width], q_bd, (((1,), (1,)), ((), ())),
                             preferred_element_type=F32)
        return st + bias_ref[var, 0]

    st_next = scores(*units[0])
    for n, (c0, sub) in enumerate(units):
        r0 = sub * tq
        st = st_next
        if n + 1 < len(units):
            st_next = scores(*units[n + 1])
        m = jnp.max(st, axis=0, keepdims=True)
        pt = jnp.exp2((st - m).astype(BF16))
        vt = jnp.concatenate([v[r0:r0 + tk, c0:c0 + width].T, ones], axis=0)
        ot = jnp.dot(vt, pt, preferred_element_type=F32)
        l = ot[width:width + 1]
        lse = m + jnp.log2(l)
        o_sel = jnp.concatenate([ot[h * HEAD_DIM:(h + 1) * HEAD_DIM, h * tq:(h + 1) * tq] / l[:, h * tq:(h + 1) * tq]
                                 for h in range(n_heads)], axis=0)
        lse_sel = jnp.concatenate([jnp.broadcast_to(lse[:, h * tq:(h + 1) * tq], (HEAD_DIM, tq))
                                   for h in range(n_heads)], axis=0)
        o_ref[0, r0:r0 + tq, c0:c0 + width] = o_sel.T.astype(o_ref.dtype)
        lse_ref[0, r0:r0 + tq, c0:c0 + width] = lse_sel.T


def _band_attn_heads(q, k, v, bias3, *, n_heads, hw, tq, rows_per_step, reps):
    b, l, _ = q.shape
    block = min(rows_per_step, l)
    nt = l // block
    ratio = block // hw
    nhw = l // hw
    tk = tq + 2 * hw
    assert nt * (block // tq) >= 2
    res_per_step = max(1, min(reps, rows_per_step // block))
    qw = kw = res_per_step * n_heads * HEAD_DIM

    prev = lambda bi, r, j: (bi, jnp.maximum(j * ratio - 1, 0), r)
    cur = lambda bi, r, j: (bi, j, r)
    nxt = lambda bi, r, j: (bi, jnp.minimum((j + 1) * ratio, nhw - 1), r)
    in_specs = [
        pl.BlockSpec((1, block, qw), cur),
        pl.BlockSpec((1, hw, kw), prev), pl.BlockSpec((1, block, kw), cur), pl.BlockSpec((1, hw, kw), nxt),
        pl.BlockSpec((1, hw, kw), prev), pl.BlockSpec((1, block, kw), cur), pl.BlockSpec((1, hw, kw), nxt),
        pl.BlockSpec((3, 1, tk, n_heads * tq), lambda bi, r, j: (0, 0, 0, 0), pipeline_mode=pl.Buffered(1)),
    ]
    return pl.pallas_call(
        functools.partial(_band_attn_heads_kernel, n_heads=n_heads, tq=tq, hw=hw),
        grid=(b, reps // res_per_step, nt),
        in_specs=in_specs,
        out_specs=[pl.BlockSpec((1, block, qw), cur), pl.BlockSpec((1, block, qw), cur)],
        out_shape=[jax.ShapeDtypeStruct(q.shape, BF16), jax.ShapeDtypeStruct(q.shape, F32)],
        compiler_params=pltpu.CompilerParams(
            dimension_semantics=("parallel", "parallel", "parallel"),
            vmem_limit_bytes=V7X_VMEM_LIMIT_BYTES),
        name="band_attn_b",
    )(q, k, k, k, v, v, v, bias3)


def _token_order(ref, scr, dil):
    if dil == 1:
        return ref[...].astype(F32)
    n = ref.shape[0]
    width = ref.shape[1] // dil
    for r in range(dil):
        for c in range(width // LANES):
            col = r * width + c * LANES
            scr[c, pl.ds(r, n, stride=dil), :] = ref[:, col:col + LANES].astype(F32)
    return jnp.concatenate([scr[c] for c in range(width // LANES)], axis=1)


def _post_kernel(x_ref, ya_ref, o1, o2, o3, l1, l2, l3, ga_ref, gb_ref, wa_ref, wb_ref, wo_ref,
                 gf_ref, rwt_ref, rb_ref, tri_ref, x1_ref, xg_ref, meta_ref, segs_ref, *scrs, dils):
    os_ = [_token_order(r, scrs[2 * i], dil) for i, (r, dil) in enumerate(zip((o1, o2, o3), dils))]
    ls = [_token_order(r, scrs[2 * i + 1], dil) for i, (r, dil) in enumerate(zip((l1, l2, l3), dils))]
    mx = jnp.maximum(jnp.maximum(ls[0], ls[1]), ls[2])
    es = [jnp.exp2(l - mx) for l in ls]
    den = es[0] + es[1] + es[2]
    yb = (es[0] * os_[0] + es[1] * os_[1] + es[2] * os_[2]) / den
    za = lax.dot_general(ya_ref[...], wa_ref[...], (((0,), (0,)), ((), ())), preferred_element_type=F32)
    zb = jnp.dot(yb.astype(BF16), wb_ref[...], preferred_element_type=F32)
    merged = ga_ref[...].astype(F32) * za + gb_ref[...].astype(F32) * zb
    x1 = x_ref[...] + jnp.dot(merged.astype(BF16), wo_ref[...], preferred_element_type=F32)
    x1_ref[...] = x1
    h2 = _rms(x1, gf_ref[...])

    h_hi = h2.astype(BF16)
    h_lo = (h2 - h_hi.astype(F32)).astype(BF16)
    nt_dot = lambda a, b_: lax.dot_general(a, b_, (((1,), (1,)), ((), ())), preferred_element_type=F32)
    n_e = rwt_ref.shape[1]
    both = nt_dot(rwt_ref[...].reshape(2 * n_e, -1), h_hi)
    logits = (both[:n_e] + nt_dot(rwt_ref[0], h_lo) + both[n_e:]) + rb_ref[...]
    n_e, tm = logits.shape
    iota_e = lax.broadcasted_iota(I32, (n_e, tm), 0)
    work = logits
    vals, hots = [], []
    for _ in range(TOP_K):
        mk = jnp.max(work, axis=0, keepdims=True)
        ik = jnp.min(jnp.where(work == mk, iota_e, n_e), axis=0, keepdims=True)
        hot = iota_e == ik
        vals.append(mk)
        hots.append(hot)
        work = jnp.where(hot, -jnp.inf, work)
    exps = [jnp.exp(vk - vals[0]) for vk in vals]
    tot = exps[0] + exps[1] + exps[2] + exps[3]
    probs = [ek / tot for ek in exps]

    sel = jnp.zeros((n_e, tm), F32)
    for hot in hots:
        sel = sel + jnp.where(hot, 1.0, 0.0)
    cnt = jnp.sum(sel, axis=1, keepdims=True)
    pcnt = jnp.floor((cnt + (CHUNK - 1)) / CHUNK) * CHUNK
    r_i = lax.broadcasted_iota(I32, (n_e, n_e), 0)
    c_i = lax.broadcasted_iota(I32, (n_e, n_e), 1)
    pcnt_row = jnp.sum(jnp.where(r_i == c_i, pcnt, 0.0), axis=0, keepdims=True)
    seg_off = jnp.sum(jnp.where(c_i < r_i, pcnt_row, 0.0), axis=1, keepdims=True)
    before = jnp.dot(sel.astype(BF16), tri_ref[...], preferred_element_type=F32)
    slot_of = seg_off + before
    slots = [jnp.sum(jnp.where(hot, slot_of, 0.0), axis=0, keepdims=True) for hot in hots]

    iota_s = lax.broadcasted_iota(I32, (PERM_BLOCK, tm), 0)
    slots_i = [sk.astype(I32) for sk in slots]
    for blk in range(SLOTS // PERM_BLOCK):
        perm = jnp.zeros((PERM_BLOCK, tm), F32)
        for sk in slots_i:
            perm = jnp.where(iota_s == sk - blk * PERM_BLOCK, 1.0, perm)
        xg_ref[blk * PERM_BLOCK:(blk + 1) * PERM_BLOCK, :] = jnp.dot(
            perm.astype(BF16), h_hi, preferred_element_type=F32).astype(BF16)

    rows = slots + probs
    meta_t = jnp.concatenate(rows + [jnp.zeros((LANES - len(rows), tm), F32)], axis=0)
    meta_ref[...] = meta_t.T
    lane = lax.broadcasted_iota(I32, (n_e, LANES), 1)
    segs_ref[...] = jnp.where(lane == 0, pcnt, jnp.where(lane == 1, seg_off, 0.0))


def _post(x2d, ya, os_, ls_, dils, ga, gb, wa, wb, wo, gf, rwt, rb):
    t, d = x2d.shape
    bw = os_[0].shape[1] // dils[0]
    tm = TM_POST
    nt = t // tm
    tri = jnp.asarray(np.triu(np.ones((tm, tm), np.float32), k=1), dtype=BF16)
    row = lambda w, dil=1: pl.BlockSpec((tm // dil, dil * w), lambda i: (i, 0))
    full = lambda a: pl.BlockSpec(a.shape, lambda i: (0,) * a.ndim)
    return pl.pallas_call(
        functools.partial(_post_kernel, dils=dils),
        grid=(nt,),
        in_specs=[row(d), pl.BlockSpec((ya.shape[0], tm), lambda i: (0, i))]
                 + [row(bw, dil) for dil in dils] + [row(bw, dil) for dil in dils]
                 + [row(d), row(d), full(wa), full(wb), full(wo), full(gf), full(rwt), full(rb), full(tri)],
        scratch_shapes=[pltpu.VMEM((bw // LANES, tm, LANES), F32) for _ in range(2 * len(dils))],
        out_specs=[row(d), pl.BlockSpec((SLOTS, d), lambda i: (i, 0)), row(LANES),
                   pl.BlockSpec((N_EXPERTS, LANES), lambda i: (i, 0))],
        out_shape=[jax.ShapeDtypeStruct((t, d), F32), jax.ShapeDtypeStruct((nt * SLOTS, d), BF16),
                   jax.ShapeDtypeStruct((t, LANES), F32), jax.ShapeDtypeStruct((nt * N_EXPERTS, LANES), F32)],
        compiler_params=pltpu.CompilerParams(
            dimension_semantics=("parallel",), vmem_limit_bytes=V7X_VMEM_LIMIT_BYTES),
        name="post_attn",
    )(x2d, ya, *os_, *ls_, ga, gb, wa, wb, wo, gf, rwt, rb, tri)


def _expert_tiles(pcnt, seg_off, n_tiles):
    nt, n_e = pcnt.shape
    cpt = CHUNKS_PER_TILE
    nch = (pcnt // CHUNK).T
    cum = jnp.cumsum(nch, axis=1)
    total = cum[:, -1]
    tiles_e = (total + cpt - 1) // cpt
    tile_end = jnp.cumsum(tiles_e)
    n_active = tile_end[-1]
    i = jnp.arange(n_tiles, dtype=I32)
    last = jnp.minimum(i, n_active - 1)
    te = jnp.sum((last[:, None] >= tile_end[None, :]).astype(I32), axis=1)
    hot_e = (te[:, None] == jnp.arange(n_e, dtype=I32)[None, :])
    pick = lambda tab: jnp.sum(jnp.where(hot_e[:, :, None], tab[None], 0), axis=1)
    tile_start = jnp.sum(jnp.where(hot_e, (tile_end - tiles_e)[None, :], 0), axis=1)
    total_t = jnp.sum(jnp.where(hot_e, total[None, :], 0), axis=1)
    q = (last - tile_start)[:, None] * cpt + jnp.arange(cpt, dtype=I32)[None, :]
    valid = (q < total_t[:, None]) & (i < n_active)[:, None]
    cum_t, nch_t = pick(cum), pick(nch)
    chunk0_t = pick((seg_off.T + jnp.arange(nt, dtype=I32)[None, :] * SLOTS) // CHUNK)
    jj = jnp.sum((q[:, :, None] >= cum_t[:, None, :]).astype(I32), axis=2)
    hot_j = jj[:, :, None] == jnp.arange(nt, dtype=I32)[None, None, :]
    first = jnp.sum(jnp.where(hot_j, (cum_t - nch_t)[:, None, :], 0), axis=2)
    base = jnp.sum(jnp.where(hot_j, chunk0_t[:, None, :], 0), axis=2)
    src = jnp.where(valid, base + q - first, 0)
    trash = nt * SLOTS // CHUNK + (i % 2)[:, None] * cpt + jnp.arange(cpt, dtype=I32)[None, :]
    dst = jnp.where(valid, src, trash)
    group_end = jnp.sum(jnp.where(hot_e, tile_end[None, :], 0), axis=1)
    nxt = jnp.sum((group_end[:, None] >= tile_end[None, :]).astype(I32), axis=1)
    nxt = jnp.where(group_end < n_active, nxt, -1)
    return (te.astype(I32), nxt.astype(I32), n_active.astype(I32)[None],
            src.reshape(-1).astype(I32), dst.reshape(-1).astype(I32))


def _expert_kernel(te_ref, nx_ref, na_ref, cs_ref, cd_ref, xg_hbm, wg_hbm, bg_ref, wu_hbm, bu_ref, wd_hbm,
                   bd_ref, yg_hbm, xbuf, ybuf, wst, wbf, zbuf, sem_in, sem_out, sem_zero, sem_w, *,
                   n_token_tiles):
    i = pl.program_id(0)
    n_active = na_ref[0]
    slot = i % 2
    cpt = CHUNKS_PER_TILE
    w_hbm = (wg_hbm, wu_hbm, wd_hbm)

    def w_copy(n, e):
        return pltpu.make_async_copy(w_hbm[n].at[e], wst.at[n], sem_w.at[n])

    zero_starts = [j * SLOTS + TM_POST * TOP_K for j in range(n_token_tiles)]
    zero_starts += [n_token_tiles * SLOTS + j * SLACK_ROWS for j in range(2 * TM_EXPERT // SLACK_ROWS)]

    def zero_copy(n):
        return pltpu.make_async_copy(
            zbuf, yg_hbm.at[pl.ds(zero_starts[n] // CHUNK, SLACK_ROWS // CHUNK)], sem_zero.at[n])

    @pl.when(i == 0)
    def _():
        zbuf[...] = jnp.zeros_like(zbuf)
        for n in range(len(zero_starts)):
            zero_copy(n).start()

    def in_copy(c, sl, chunk):
        return pltpu.make_async_copy(xg_hbm.at[chunk], xbuf.at[sl, c], sem_in.at[sl])

    def out_copy(c, sl, chunk):
        return pltpu.make_async_copy(ybuf.at[sl, c], yg_hbm.at[chunk], sem_out.at[sl])

    def start_gather(tile, sl):
        for c in range(cpt):
            in_copy(c, sl, cs_ref[tile * cpt + c]).start()

    def wait_gather(sl):
        pltpu.make_async_copy(xg_hbm.at[pl.ds(0, cpt)], xbuf.at[sl], sem_in.at[sl]).wait()

    def wait_scatter(sl):
        pltpu.make_async_copy(ybuf.at[sl], yg_hbm.at[pl.ds(0, cpt)], sem_out.at[sl]).wait()

    @pl.when(i == 0)
    def _():
        start_gather(0, 0)
        for n in range(len(w_hbm)):
            w_copy(n, te_ref[0]).start()

    @pl.when(i + 1 < n_active)
    def _():
        start_gather(i + 1, 1 - slot)

    @pl.when(i < n_active)
    def _():
        wait_gather(slot)

        @pl.when((i == 0) | (te_ref[i] != te_ref[jnp.maximum(i - 1, 0)]))
        def _():
            for n in range(len(w_hbm)):
                w_copy(n, 0).wait()
                wbf[n] = wst[n].astype(BF16)

                @pl.when(nx_ref[i] >= 0)
                def _():
                    w_copy(n, nx_ref[i]).start()

        @pl.when(i >= 1)
        def _():
            for c in range(cpt):
                out_copy(c, 1 - slot, cd_ref[(i - 1) * cpt + c]).start()

        x = xbuf[slot].reshape(TM_EXPERT, -1)
        e = te_ref[i]
        gate = jnp.minimum(jnp.dot(x, wbf[0], preferred_element_type=F32) + bg_ref[e], SWIGLU_LIMIT)
        up = jnp.clip(jnp.dot(x, wbf[1], preferred_element_type=F32) + bu_ref[e], -SWIGLU_LIMIT, SWIGLU_LIMIT)
        act = gate * jax.nn.sigmoid(SWIGLU_ALPHA * gate) * (up + 1.0)
        y = jnp.dot(act.astype(BF16), wbf[2], preferred_element_type=F32) + bd_ref[e]

        @pl.when(i >= 2)
        def _():
            wait_scatter(slot)

        ybuf[slot] = y.astype(BF16).reshape(ybuf.shape[1:])

        @pl.when(i == 0)
        def _():
            for n in range(len(zero_starts)):
                zero_copy(n).wait()

        @pl.when(i == n_active - 1)
        def _():
            for c in range(cpt):
                out_copy(c, slot, cd_ref[i * cpt + c]).start()
            wait_scatter(slot)

            @pl.when(i >= 1)
            def _():
                wait_scatter(1 - slot)


def _experts(xg, te, nxt, n_active, src, dst, wg, bg, wu, bu, wd, bd):
    rows, d = xg.shape
    n_e, _, dff = wg.shape
    assert d == dff
    n_tiles = te.shape[0]
    n_token_tiles = rows // SLOTS
    n_zero = n_token_tiles + 2 * TM_EXPERT // SLACK_ROWS
    b_spec = pl.BlockSpec((n_e, 1, d), lambda i, *_: (0, 0, 0))
    hbm = pl.BlockSpec(memory_space=pl.ANY)
    tile_bufs = pltpu.VMEM((2, CHUNKS_PER_TILE, CHUNK, d), BF16)
    yg = pl.pallas_call(
        functools.partial(_expert_kernel, n_token_tiles=n_token_tiles),
        grid_spec=pltpu.PrefetchScalarGridSpec(
            num_scalar_prefetch=5,
            grid=(n_tiles,),
            in_specs=[hbm, hbm, b_spec, hbm, b_spec, hbm, b_spec],
            out_specs=hbm,
            scratch_shapes=[tile_bufs, tile_bufs,
                            pltpu.VMEM((3, d, dff), F32), pltpu.VMEM((3, d, dff), BF16),
                            pltpu.VMEM((SLACK_ROWS // CHUNK, CHUNK, d), BF16),
                            pltpu.SemaphoreType.DMA((2,)),
                            pltpu.SemaphoreType.DMA((2,)),
                            pltpu.SemaphoreType.DMA((n_zero,)),
                            pltpu.SemaphoreType.DMA((3,))]),
        out_shape=jax.ShapeDtypeStruct(((rows + 2 * TM_EXPERT) // CHUNK, CHUNK, d), BF16),
        compiler_params=pltpu.CompilerParams(
            dimension_semantics=("arbitrary",), vmem_limit_bytes=V7X_VMEM_LIMIT_BYTES),
        name="experts",
    )(te, nxt, n_active, src, dst, xg.reshape(rows // CHUNK, CHUNK, d), wg, bg, wu, bu, wd, bd)
    return yg.reshape(rows + 2 * TM_EXPERT, d)


def _combine_kernel(x1_ref, yg_ref, meta_ref, gn_ref, o_ref, *, final_norm):
    meta = meta_ref[...]
    tm = meta.shape[0]
    slots_i = [meta[:, k:k + 1].astype(I32) for k in range(TOP_K)]
    probs = [meta[:, TOP_K + k:TOP_K + k + 1] for k in range(TOP_K)]
    iota_s = lax.broadcasted_iota(I32, (tm, PERM_BLOCK), 1)
    xo = x1_ref[...]
    for blk in range(SLOTS // PERM_BLOCK):
        base = blk * PERM_BLOCK
        wperm = jnp.zeros((tm, PERM_BLOCK), F32)
        for sk, pk in zip(slots_i, probs):
            wperm = jnp.where(iota_s == sk - base, pk, wperm)
        xo = xo + jnp.dot(wperm.astype(BF16), yg_ref[base:base + PERM_BLOCK, :], preferred_element_type=F32)
    o_ref[...] = _rms(xo, gn_ref[...]) if final_norm else xo


def _combine(x1, yg, meta, gn, final_norm):
    t, d = x1.shape
    tm = TM_POST
    return pl.pallas_call(
        functools.partial(_combine_kernel, final_norm=final_norm),
        grid=(t // tm,),
        in_specs=[pl.BlockSpec((tm, d), lambda i: (i, 0)),
                  pl.BlockSpec((SLOTS, d), lambda i: (i, 0)),
                  pl.BlockSpec((tm, LANES), lambda i: (i, 0)),
                  pl.BlockSpec((1, d), lambda i: (0, 0))],
        out_specs=pl.BlockSpec((tm, d), lambda i: (i, 0)),
        out_shape=jax.ShapeDtypeStruct((t, d), F32),
        compiler_params=pltpu.CompilerParams(
            dimension_semantics=("parallel",), vmem_limit_bytes=V7X_VMEM_LIMIT_BYTES),
        name="combine",
    )(x1, yg, meta, gn)


def kernel(x, norm_mix, w_in, b_in, sinks, rel_bias, w_branch_a, w_branch_b, w_out, norm_ffn,
           router_w, router_b, w_gate, b_gate, w_up, b_up, w_down, b_down, norm_final):
    b, s, d = x.shape
    t = b * s
    depth = w_in.shape[0]
    a_q_w = A_KV_HEADS * A_GROUP * HEAD_DIM
    a_kv_w = A_KV_HEADS * HEAD_DIM
    b_w = B_HEADS * HEAD_DIM
    n_grp = len(B_GROUPS)
    dils = tuple(dil for _, dil in B_GROUPS)
    segs, col = [], 0
    for kind, width in (("q", a_q_w), ("k", a_kv_w), ("vT", a_kv_w)):
        segs.append((col, width, kind, 1))
        col += width
    for kind in ("q", "k", "v"):
        for dil in dils:
            segs.append((col, b_w, kind, dil))
            col += b_w
    for _ in range(2):
        segs.append((col, d, "gate", 1))
        col += d
    segs = tuple(segs)
    n_a = A_KV_HEADS * A_GROUP
    nt = t // TM_POST
    max_chunks = nt * ((TM_POST * TOP_K + N_EXPERTS * (CHUNK - 1)) // CHUNK)
    n_tiles = -(-(max_chunks + N_EXPERTS * (CHUNKS_PER_TILE - 1)) // CHUNKS_PER_TILE)

    bias_a = _band_bias(rel_bias[:, :n_a], A_HALF_WINDOW, TQ_ATTN, 1, A_KV_HEADS, A_GROUP)
    bias_b = [_band_bias(rel_bias[:, n_a + gi * B_HEADS:n_a + (gi + 1) * B_HEADS],
                         win // (2 * dil), TQ_ATTN, dil, 1, B_HEADS) for gi, (win, dil) in enumerate(B_GROUPS)]

    x2d = x.reshape(t, d)
    for layer in range(depth):
        proj = _inproj(x2d, norm_mix[layer][None], w_in[layer].astype(BF16), b_in[layer][None], segs)
        qa, ka, vat = proj[:3]
        qb, kb, vb = proj[3:3 + n_grp], proj[3 + n_grp:3 + 2 * n_grp], proj[3 + 2 * n_grp:3 + 3 * n_grp]
        ga, gb = proj[-2:]

        yat = _band_attn_gqa(qa.reshape(b, s, a_q_w), ka.reshape(b, s, a_kv_w), vat, bias_a, sinks[layer],
                             n_kv=A_KV_HEADS, grp=A_GROUP, hw=A_HALF_WINDOW, tq=TQ_ATTN, block=BLOCK_ATTN_A)
        os_, ls_ = [], []
        for gi, (win, dil) in enumerate(B_GROUPS):
            sub = lambda a: a.reshape(b, s // dil, dil * b_w)
            o, lse = _band_attn_heads(sub(qb[gi]), sub(kb[gi]), sub(vb[gi]), bias_b[gi], n_heads=B_HEADS,
                                      hw=win // (2 * dil), tq=TQ_ATTN, rows_per_step=BLOCK_ATTN_B, reps=dil)
            os_.append(o.reshape(t // dil, dil * b_w))
            ls_.append(lse.reshape(t // dil, dil * b_w))

        rwt = router_w[layer].T
        rwt_hi = rwt.astype(BF16)
        rwt_split = jnp.stack([rwt_hi, (rwt - rwt_hi.astype(F32)).astype(BF16)])
        x1, xg, meta, segs_out = _post(
            x2d, yat, os_, ls_, dils, ga, gb,
            w_branch_a[layer].astype(BF16), w_branch_b[layer].astype(BF16), w_out[layer].astype(BF16),
            norm_ffn[layer][None], rwt_split, router_b[layer][:, None])

        segs3 = segs_out.reshape(nt, N_EXPERTS, LANES)
        te, nxt, n_active, src, dst = _expert_tiles(segs3[:, :, 0].astype(I32), segs3[:, :, 1].astype(I32), n_tiles)
        yg = _experts(xg, te, nxt, n_active, src, dst, w_gate[layer], b_gate[layer][:, None], w_up[layer],
                      b_up[layer][:, None], w_down[layer], b_down[layer][:, None])
        x2d = _combine(x1, yg, meta, norm_final[None], layer == depth - 1)
    return x2d.reshape(b, s, d)
```

```python
import functools

import numpy as np
import jax
import jax.numpy as jnp
from jax import lax
from jax.experimental import pallas as pl
from jax.experimental.pallas import tpu as pltpu

F32 = jnp.float32
BF16 = jnp.bfloat16
I32 = jnp.int32

HEAD_DIM = 64
A_KV_HEADS = 4
A_GROUP = 4
A_HALF_WINDOW = 128
B_GROUPS = ((128, 1), (512, 4), (2048, 16))
B_HEADS = 4
N_BUCKETS = 32
MAX_DISTANCE = 1024
N_EXPERTS = 32
TOP_K = 4
SWIGLU_LIMIT = 7.0
SWIGLU_ALPHA = 1.702
EPS = 1e-5
NEG_INF = -1e30
LOG2E = 1.4426950408889634

V7X_VMEM_LIMIT_BYTES = 56 * 1024 * 1024
LANES = 128
BF16_SUBLANES = 16

TM_PROJ = 1024
TQ_ATTN = 128
BLOCK_ATTN_A = 2048
BLOCK_ATTN_B = 4096
TM_POST = 512
TM_EXPERT = 512

CHUNK = BF16_SUBLANES
CHUNKS_PER_TILE = TM_EXPERT // CHUNK
SLOTS = -(-(TM_POST * TOP_K + N_EXPERTS * (CHUNK - 1)) // LANES) * LANES
SLACK_ROWS = SLOTS - TM_POST * TOP_K
assert (2 * TM_EXPERT) % SLACK_ROWS == 0
WEIGHT_DMA_PRIORITY = 1
PERM_BLOCK = 512
assert SLOTS % PERM_BLOCK == 0


def _t5_buckets(rel):
    half = N_BUCKETS // 2
    max_exact = half // 2
    ret = np.where(rel > 0, half, 0)
    n = np.abs(rel)
    large = max_exact + (np.log(np.maximum(n, 1) / max_exact)
                         / np.log(MAX_DISTANCE / max_exact) * (half - max_exact)).astype(np.int32)
    large = np.minimum(large, half - 1)
    return (ret + np.where(n < max_exact, n, large)).astype(np.int32)


def _rms(x, g):
    return x * lax.rsqrt(jnp.mean(x * x, axis=-1, keepdims=True) + EPS) * g


def _inproj_kernel(x_ref, g_ref, w_ref, b_ref, *refs, segs):
    out_refs, scr = refs[:-1], refs[-1]
    h = _rms(x_ref[...], g_ref[...]).astype(BF16)

    for ref, (c0, width, kind, dil) in zip(out_refs, segs):
        acc = jnp.dot(h, w_ref[:, c0:c0 + width], preferred_element_type=F32) + b_ref[:, c0:c0 + width]
        if kind == "q":
            acc = acc * (HEAD_DIM ** -0.5 * LOG2E)
        elif kind == "gate":
            acc = jax.nn.sigmoid(acc)
        if kind == "vT":
            ref[...] = acc.T.astype(ref.dtype)
        elif dil == 1:
            ref[...] = acc.astype(ref.dtype)
        else:
            rows = acc.shape[0] // dil
            for c in range(width // LANES):
                scr[c] = acc[:, c * LANES:(c + 1) * LANES]
            for r in range(dil):
                for c in range(width // LANES):
                    col = r * width + c * LANES
                    ref[:, col:col + LANES] = scr[c, pl.ds(r, rows, stride=dil), :].astype(ref.dtype)


def _inproj(x2d, g, w_bf16, b, segs):
    t, d = x2d.shape
    n = w_bf16.shape[1]
    max_w = max(w for _, w, _, dil in segs if dil > 1)
    out_specs, out_shape = [], []
    for _, w, kind, dil in segs:
        if kind == "vT":
            out_specs.append(pl.BlockSpec((w, TM_PROJ), lambda i: (0, i)))
            out_shape.append(jax.ShapeDtypeStruct((w, t), BF16))
        else:
            out_specs.append(pl.BlockSpec((TM_PROJ // dil, dil * w), lambda i: (i, 0)))
            out_shape.append(jax.ShapeDtypeStruct((t // dil, dil * w), BF16))
    return pl.pallas_call(
        functools.partial(_inproj_kernel, segs=segs),
        grid=(t // TM_PROJ,),
        in_specs=[
            pl.BlockSpec((TM_PROJ, d), lambda i: (i, 0)),
            pl.BlockSpec((1, d), lambda i: (0, 0)),
            pl.BlockSpec((d, n), lambda i: (0, 0), pipeline_mode=pl.Buffered(1)),
            pl.BlockSpec((1, n), lambda i: (0, 0)),
        ],
        out_specs=out_specs,
        out_shape=out_shape,
        scratch_shapes=[pltpu.VMEM((max_w // LANES, TM_PROJ, LANES), F32)],
        compiler_params=pltpu.CompilerParams(
            dimension_semantics=("parallel",), vmem_limit_bytes=V7X_VMEM_LIMIT_BYTES),
        name="inproj",
    )(x2d, g, w_bf16, b)


def _band_bias(table, hw, tq, dil, n_kv, grp):
    tk = tq + 2 * hw
    off = np.arange(tk)[None, :] - hw - np.arange(tq)[:, None]
    band = np.abs(off) <= hw
    col = np.arange(tk)[None, :]
    masks = np.stack([band & (col >= hw), band, band & (col < hw + tq)])
    onehot = (_t5_buckets(off * dil)[..., None] == np.arange(N_BUCKETS)).astype(np.float32)
    bias = jnp.einsum("qkn,nvg->vkgq", jnp.asarray(onehot), table.astype(F32).reshape(N_BUCKETS, n_kv, grp),
                      precision=lax.Precision.HIGHEST)
    keep = np.transpose(masks, (0, 2, 1))[:, None, :, None, :]
    return jnp.where(keep, bias[None] * LOG2E, NEG_INF).reshape(3, n_kv, tk, grp * tq)


def _band_attn_gqa_kernel(q_ref, kp, kc, kn, vtp, vtc, vtn, bias_ref, sink_ref, o_ref, *, n_kv, grp, tq, hw):
    j, nt = pl.program_id(1), pl.num_programs(1)
    k = jnp.concatenate([kp[0], kc[0], kn[0]], axis=0)
    vt = jnp.concatenate([vtp[...], vtc[...], vtn[...]], axis=1)
    n_sub = q_ref.shape[1] // tq
    tk = tq + 2 * hw
    ones = jnp.ones((BF16_SUBLANES, tk), BF16)
    lane = lax.broadcasted_iota(I32, (1, grp * tq), 1)
    sinks = []
    for h in range(n_kv):
        sk = jnp.full((1, grp * tq), sink_ref[h * grp] * LOG2E, F32)
        for g in range(1, grp):
            sk = jnp.where(lane >= g * tq, sink_ref[h * grp + g] * LOG2E, sk)
        sinks.append(sk)
    units = []
    for sub in range(n_sub):
        var = 1
        if sub == 0:
            var = jnp.where(j == 0, 0, var)
        if sub == n_sub - 1:
            var = jnp.where(j == nt - 1, 2, var)
        units.extend((var, sub * tq, h) for h in range(n_kv))

    def scores(var, r0, h):
        k_h = k[r0:r0 + tk, h * HEAD_DIM:(h + 1) * HEAD_DIM]
        c0 = h * grp * HEAD_DIM
        q = jnp.concatenate([q_ref[0, r0:r0 + tq, c0 + g * HEAD_DIM:c0 + (g + 1) * HEAD_DIM]
                             for g in range(grp)], axis=0)
        st = lax.dot_general(k_h, q, (((1,), (1,)), ((), ())), preferred_element_type=F32)
        return st + bias_ref[var, h]

    st_next = scores(*units[0])
    for n, (var, r0, h) in enumerate(units):
        st = st_next
        if n + 1 < len(units):
            st_next = scores(*units[n + 1])
        sk = sinks[h]
        m = jnp.maximum(jnp.max(st, axis=0, keepdims=True), sk)
        pt = jnp.exp2((st - m).astype(BF16))
        vt_h = jnp.concatenate([vt[h * HEAD_DIM:(h + 1) * HEAD_DIM, r0:r0 + tk], ones], axis=0)
        ot = jnp.dot(vt_h, pt, preferred_element_type=F32)
        l = ot[HEAD_DIM:HEAD_DIM + 1] + jnp.exp2(sk - m)
        o = ot[:HEAD_DIM] / l
        for g in range(grp):
            c = (h * grp + g) * HEAD_DIM
            o_ref[c:c + HEAD_DIM, r0:r0 + tq] = o[:, g * tq:(g + 1) * tq].astype(o_ref.dtype)


def _band_attn_gqa(q, k, vt, bias3, sinks, *, n_kv, grp, hw, tq, block):
    b, s, _ = q.shape
    hq = n_kv * grp
    qw, kw = hq * HEAD_DIM, n_kv * HEAD_DIM
    nt = s // block
    ratio = block // hw
    nhw = s // hw
    tk = tq + 2 * hw
    assert nt * (block // tq) >= 2
    prev_j = lambda j: jnp.maximum(j * ratio - 1, 0)
    next_j = lambda j: jnp.minimum((j + 1) * ratio, nhw - 1)
    assert bias3.shape == (3, n_kv, tk, grp * tq)
    return pl.pallas_call(
        functools.partial(_band_attn_gqa_kernel, n_kv=n_kv, grp=grp, tq=tq, hw=hw),
        grid=(b, nt),
        in_specs=[
            pl.BlockSpec((1, block, qw), lambda bi, j: (bi, j, 0)),
            pl.BlockSpec((1, hw, kw), lambda bi, j: (bi, prev_j(j), 0)),
            pl.BlockSpec((1, block, kw), lambda bi, j: (bi, j, 0)),
            pl.BlockSpec((1, hw, kw), lambda bi, j: (bi, next_j(j), 0)),
            pl.BlockSpec((kw, hw), lambda bi, j: (0, bi * nhw + prev_j(j))),
            pl.BlockSpec((kw, block), lambda bi, j: (0, bi * nt + j)),
            pl.BlockSpec((kw, hw), lambda bi, j: (0, bi * nhw + next_j(j))),
            pl.BlockSpec((3, n_kv, tk, grp * tq), lambda bi, j: (0, 0, 0, 0), pipeline_mode=pl.Buffered(1)),
            pl.BlockSpec(memory_space=pltpu.SMEM),
        ],
        out_specs=pl.BlockSpec((qw, block), lambda bi, j: (0, bi * nt + j)),
        out_shape=jax.ShapeDtypeStruct((qw, b * s), BF16),
        compiler_params=pltpu.CompilerParams(
            dimension_semantics=("parallel", "parallel"), vmem_limit_bytes=V7X_VMEM_LIMIT_BYTES),
        name="band_attn_a",
    )(q, k, k, k, vt, vt, vt, bias3, sinks)


def _band_attn_heads_kernel(q_ref, kp, kc, kn, vp, vc, vn, bias_ref, o_ref, lse_ref, *, n_heads, tq, hw):
    j, nt = pl.program_id(2), pl.num_programs(2)
    k = jnp.concatenate([kp[0], kc[0], kn[0]], axis=0)
    v = jnp.concatenate([vp[0], vc[0], vn[0]], axis=0)
    n_sub = q_ref.shape[1] // tq
    tk = tq + 2 * hw
    width = n_heads * HEAD_DIM
    lane_head = lax.broadcasted_iota(I32, (tq, width), 1) // HEAD_DIM
    ones = jnp.ones((BF16_SUBLANES, tk), BF16)

    units = [(c0, sub) for c0 in range(0, q_ref.shape[2], width) for sub in range(n_sub)]

    def scores(c0, sub):
        var = 1
        if sub == 0:
            var = jnp.where(j == 0, 0, var)
        if sub == n_sub - 1:
            var = jnp.where(j == nt - 1, 2, var)
        r0 = sub * tq
        q = q_ref[0, r0:r0 + tq, c0:c0 + width]
        q_bd = jnp.concatenate([jnp.where(lane_head == h, q, jnp.zeros_like(q)) for h in range(n_heads)], axis=0)
        st = lax.dot_general(k[r0:r0 + tk, c0:c0 + width], q_bd, (((1,), (1,)), ((), ())),
                             preferred_element_type=F32)
        return st + bias_ref[var, 0]

    st_next = scores(*units[0])
    for n, (c0, sub) in enumerate(units):
        r0 = sub * tq
        st = st_next
        if n + 1 < len(units):
            st_next = scores(*units[n + 1])
        m = jnp.max(st, axis=0, keepdims=True)
        pt = jnp.exp2((st - m).astype(BF16))
        vt = jnp.concatenate([v[r0:r0 + tk, c0:c0 + width].T, ones], axis=0)
        ot = jnp.dot(vt, pt, preferred_element_type=F32)
        l = ot[width:width + 1]
        lse = m + jnp.log2(l)
        o_sel = jnp.concatenate([ot[h * HEAD_DIM:(h + 1) * HEAD_DIM, h * tq:(h + 1) * tq] / l[:, h * tq:(h + 1) * tq]
                                 for h in range(n_heads)], axis=0)
        lse_sel = jnp.concatenate([jnp.broadcast_to(lse[:, h * tq:(h + 1) * tq], (HEAD_DIM, tq))
                                   for h in range(n_heads)], axis=0)
        o_ref[0, r0:r0 + tq, c0:c0 + width] = o_sel.T.astype(o_ref.dtype)
        lse_ref[0, r0:r0 + tq, c0:c0 + width] = lse_sel.T


def _band_attn_heads(q, k, v, bias3, *, n_heads, hw, tq, rows_per_step, reps):
    b, l, _ = q.shape
    block = min(rows_per_step, l)
    nt = l // block
    ratio = block // hw
    nhw = l // hw
    tk = tq + 2 * hw
    assert nt * (block // tq) >= 2
    res_per_step = max(1, min(reps, rows_per_step // block))
    qw = kw = res_per_step * n_heads * HEAD_DIM

    prev = lambda bi, r, j: (bi, jnp.maximum(j * ratio - 1, 0), r)
    cur = lambda bi, r, j: (bi, j, r)
    nxt = lambda bi, r, j: (bi, jnp.minimum((j + 1) * ratio, nhw - 1), r)
    in_specs = [
        pl.BlockSpec((1, block, qw), cur),
        pl.BlockSpec((1, hw, kw), prev), pl.BlockSpec((1, block, kw), cur), pl.BlockSpec((1, hw, kw), nxt),
        pl.BlockSpec((1, hw, kw), prev), pl.BlockSpec((1, block, kw), cur), pl.BlockSpec((1, hw, kw), nxt),
        pl.BlockSpec((3, 1, tk, n_heads * tq), lambda bi, r, j: (0, 0, 0, 0), pipeline_mode=pl.Buffered(1)),
    ]
    return pl.pallas_call(
        functools.partial(_band_attn_heads_kernel, n_heads=n_heads, tq=tq, hw=hw),
        grid=(b, reps // res_per_step, nt),
        in_specs=in_specs,
        out_specs=[pl.BlockSpec((1, block, qw), cur), pl.BlockSpec((1, block, qw), cur)],
        out_shape=[jax.ShapeDtypeStruct(q.shape, BF16), jax.ShapeDtypeStruct(q.shape, F32)],
        compiler_params=pltpu.CompilerParams(
            dimension_semantics=("parallel", "parallel", "parallel"),
            vmem_limit_bytes=V7X_VMEM_LIMIT_BYTES),
        name="band_attn_b",
    )(q, k, k, k, v, v, v, bias3)


def _token_order(ref, scr, dil):
    if dil == 1:
        return ref[...].astype(F32)
    n = ref.shape[0]
    width = ref.shape[1] // dil
    for r in range(dil):
        for c in range(width // LANES):
            col = r * width + c * LANES
            scr[c, pl.ds(r, n, stride=dil), :] = ref[:, col:col + LANES].astype(F32)
    return jnp.concatenate([scr[c] for c in range(width // LANES)], axis=1)


def _post_kernel(x_ref, ya_ref, o1, o2, o3, l1, l2, l3, ga_ref, gb_ref, wa_ref, wb_ref, wo_ref,
                 gf_ref, rwt_ref, rb_ref, tri_ref, x1_ref, xg_ref, meta_ref, segs_ref, *scrs, dils):
    os_ = [_token_order(r, scrs[2 * i], dil) for i, (r, dil) in enumerate(zip((o1, o2, o3), dils))]
    ls = [_token_order(r, scrs[2 * i + 1], dil) for i, (r, dil) in enumerate(zip((l1, l2, l3), dils))]
    mx = jnp.maximum(jnp.maximum(ls[0], ls[1]), ls[2])
    es = [jnp.exp2(l - mx) for l in ls]
    den = es[0] + es[1] + es[2]
    yb = (es[0] * os_[0] + es[1] * os_[1] + es[2] * os_[2]) / den
    za = lax.dot_general(ya_ref[...], wa_ref[...], (((0,), (0,)), ((), ())), preferred_element_type=F32)
    zb = jnp.dot(yb.astype(BF16), wb_ref[...], preferred_element_type=F32)
    merged = ga_ref[...].astype(F32) * za + gb_ref[...].astype(F32) * zb
    x1 = x_ref[...] + jnp.dot(merged.astype(BF16), wo_ref[...], preferred_element_type=F32)
    x1_ref[...] = x1
    h2 = _rms(x1, gf_ref[...])

    h_hi = h2.astype(BF16)
    h_lo = (h2 - h_hi.astype(F32)).astype(BF16)
    nt_dot = lambda a, b_: lax.dot_general(a, b_, (((1,), (1,)), ((), ())), preferred_element_type=F32)
    n_e = rwt_ref.shape[1]
    both = nt_dot(rwt_ref[...].reshape(2 * n_e, -1), h_hi)
    logits = (both[:n_e] + nt_dot(rwt_ref[0], h_lo) + both[n_e:]) + rb_ref[...]
    n_e, tm = logits.shape
    iota_e = lax.broadcasted_iota(I32, (n_e, tm), 0)
    work = logits
    vals, hots = [], []
    for _ in range(TOP_K):
        mk = jnp.max(work, axis=0, keepdims=True)
        ik = jnp.min(jnp.where(work == mk, iota_e, n_e), axis=0, keepdims=True)
        hot = iota_e == ik
        vals.append(mk)
        hots.append(hot)
        work = jnp.where(hot, -jnp.inf, work)
    exps = [jnp.exp(vk - vals[0]) for vk in vals]
    tot = exps[0] + exps[1] + exps[2] + exps[3]
    probs = [ek / tot for ek in exps]

    sel = jnp.zeros((n_e, tm), F32)
    for hot in hots:
        sel = sel + jnp.where(hot, 1.0, 0.0)
    cnt = jnp.sum(sel, axis=1, keepdims=True)
    pcnt = jnp.floor((cnt + (CHUNK - 1)) / CHUNK) * CHUNK
    r_i = lax.broadcasted_iota(I32, (n_e, n_e), 0)
    c_i = lax.broadcasted_iota(I32, (n_e, n_e), 1)
    pcnt_row = jnp.sum(jnp.where(r_i == c_i, pcnt, 0.0), axis=0, keepdims=True)
    seg_off = jnp.sum(jnp.where(c_i < r_i, pcnt_row, 0.0), axis=1, keepdims=True)
    before = jnp.dot(sel.astype(BF16), tri_ref[...], preferred_element_type=F32)
    slot_of = seg_off + before
    slots = [jnp.sum(jnp.where(hot, slot_of, 0.0), axis=0, keepdims=True) for hot in hots]

    iota_s = lax.broadcasted_iota(I32, (PERM_BLOCK, tm), 0)
    slots_i = [sk.astype(I32) for sk in slots]
    for blk in range(SLOTS // PERM_BLOCK):
        perm = jnp.zeros((PERM_BLOCK, tm), F32)
        for sk in slots_i:
            perm = jnp.where(iota_s == sk - blk * PERM_BLOCK, 1.0, perm)
        xg_ref[blk * PERM_BLOCK:(blk + 1) * PERM_BLOCK, :] = jnp.dot(
            perm.astype(BF16), h_hi, preferred_element_type=F32).astype(BF16)

    rows = slots + probs
    meta_t = jnp.concatenate(rows + [jnp.zeros((LANES - len(rows), tm), F32)], axis=0)
    meta_ref[...] = meta_t.T
    lane = lax.broadcasted_iota(I32, (n_e, LANES), 1)
    segs_ref[...] = jnp.where(lane == 0, pcnt, jnp.where(lane == 1, seg_off, 0.0))


def _post(x2d, ya, os_, ls_, dils, ga, gb, wa, wb, wo, gf, rwt, rb):
    t, d = x2d.shape
    bw = os_[0].shape[1] // dils[0]
    tm = TM_POST
    nt = t // tm
    tri = jnp.asarray(np.triu(np.ones((tm, tm), np.float32), k=1), dtype=BF16)
    row = lambda w, dil=1: pl.BlockSpec((tm // dil, dil * w), lambda i: (i, 0))
    full = lambda a: pl.BlockSpec(a.shape, lambda i: (0,) * a.ndim)
    return pl.pallas_call(
        functools.partial(_post_kernel, dils=dils),
        grid=(nt,),
        in_specs=[row(d), pl.BlockSpec((ya.shape[0], tm), lambda i: (0, i))]
                 + [row(bw, dil) for dil in dils] + [row(bw, dil) for dil in dils]
                 + [row(d), row(d), full(wa), full(wb), full(wo), full(gf), full(rwt), full(rb), full(tri)],
        scratch_shapes=[pltpu.VMEM((bw // LANES, tm, LANES), F32) for _ in range(2 * len(dils))],
        out_specs=[row(d), pl.BlockSpec((SLOTS, d), lambda i: (i, 0)), row(LANES),
                   pl.BlockSpec((N_EXPERTS, LANES), lambda i: (i, 0))],
        out_shape=[jax.ShapeDtypeStruct((t, d), F32), jax.ShapeDtypeStruct((nt * SLOTS, d), BF16),
                   jax.ShapeDtypeStruct((t, LANES), F32), jax.ShapeDtypeStruct((nt * N_EXPERTS, LANES), F32)],
        compiler_params=pltpu.CompilerParams(
            dimension_semantics=("parallel",), vmem_limit_bytes=V7X_VMEM_LIMIT_BYTES),
        name="post_attn",
    )(x2d, ya, *os_, *ls_, ga, gb, wa, wb, wo, gf, rwt, rb, tri)


def _expert_tiles(pcnt, seg_off, n_tiles):
    nt, n_e = pcnt.shape
    cpt = CHUNKS_PER_TILE
    nch = (pcnt // CHUNK).T
    cum = jnp.cumsum(nch, axis=1)
    total = cum[:, -1]
    tiles_e = (total + cpt - 1) // cpt
    tile_end = jnp.cumsum(tiles_e)
    n_active = tile_end[-1]
    i = jnp.arange(n_tiles, dtype=I32)
    last = jnp.minimum(i, n_active - 1)
    te = jnp.sum((last[:, None] >= tile_end[None, :]).astype(I32), axis=1)
    hot_e = (te[:, None] == jnp.arange(n_e, dtype=I32)[None, :])
    pick = lambda tab: jnp.sum(jnp.where(hot_e[:, :, None], tab[None], 0), axis=1)
    tile_start = jnp.sum(jnp.where(hot_e, (tile_end - tiles_e)[None, :], 0), axis=1)
    total_t = jnp.sum(jnp.where(hot_e, total[None, :], 0), axis=1)
    q = (last - tile_start)[:, None] * cpt + jnp.arange(cpt, dtype=I32)[None, :]
    valid = (q < total_t[:, None]) & (i < n_active)[:, None]
    cum_t, nch_t = pick(cum), pick(nch)
    chunk0_t = pick((seg_off.T + jnp.arange(nt, dtype=I32)[None, :] * SLOTS) // CHUNK)
    jj = jnp.sum((q[:, :, None] >= cum_t[:, None, :]).astype(I32), axis=2)
    hot_j = jj[:, :, None] == jnp.arange(nt, dtype=I32)[None, None, :]
    first = jnp.sum(jnp.where(hot_j, (cum_t - nch_t)[:, None, :], 0), axis=2)
    base = jnp.sum(jnp.where(hot_j, chunk0_t[:, None, :], 0), axis=2)
    src = jnp.where(valid, base + q - first, 0)
    trash = nt * SLOTS // CHUNK + (i % 2)[:, None] * cpt + jnp.arange(cpt, dtype=I32)[None, :]
    dst = jnp.where(valid, src, trash)
    group_end = jnp.sum(jnp.where(hot_e, tile_end[None, :], 0), axis=1)
    nxt = jnp.sum((group_end[:, None] >= tile_end[None, :]).astype(I32), axis=1)
    nxt = jnp.where(group_end < n_active, nxt, -1)
    return (te.astype(I32), nxt.astype(I32), n_active.astype(I32)[None],
            src.reshape(-1).astype(I32), dst.reshape(-1).astype(I32))


def _expert_kernel(te_ref, nx_ref, na_ref, cs_ref, cd_ref, xg_hbm, wg_hbm, bg_ref, wu_hbm, bu_ref, wd_hbm,
                   bd_ref, yg_hbm, xbuf, ybuf, wst, wbf, zbuf, sem_in, sem_out, sem_zero, sem_w, *,
                   n_token_tiles):
    i = pl.program_id(0)
    n_active = na_ref[0]
    slot = i % 2
    cpt = CHUNKS_PER_TILE
    w_hbm = (wg_hbm, wu_hbm, wd_hbm)

    def w_copy(n, e):
        return pltpu.make_async_copy(w_hbm[n].at[e], wst.at[n], sem_w.at[n])

    zero_starts = [j * SLOTS + TM_POST * TOP_K for j in range(n_token_tiles)]
    zero_starts += [n_token_tiles * SLOTS + j * SLACK_ROWS for j in range(2 * TM_EXPERT // SLACK_ROWS)]

    def zero_copy(n):
        return pltpu.make_async_copy(
            zbuf, yg_hbm.at[pl.ds(zero_starts[n] // CHUNK, SLACK_ROWS // CHUNK)], sem_zero.at[n])

    @pl.when(i == 0)
    def _():
        zbuf[...] = jnp.zeros_like(zbuf)
        for n in range(len(zero_starts)):
            zero_copy(n).start()

    def in_copy(c, sl, chunk):
        return pltpu.make_async_copy(xg_hbm.at[chunk], xbuf.at[sl, c], sem_in.at[sl])

    def out_copy(c, sl, chunk):
        return pltpu.make_async_copy(ybuf.at[sl, c], yg_hbm.at[chunk], sem_out.at[sl])

    def start_gather(tile, sl):
        for c in range(cpt):
            in_copy(c, sl, cs_ref[tile * cpt + c]).start()

    def wait_gather(sl):
        pltpu.make_async_copy(xg_hbm.at[pl.ds(0, cpt)], xbuf.at[sl], sem_in.at[sl]).wait()

    def wait_scatter(sl):
        pltpu.make_async_copy(ybuf.at[sl], yg_hbm.at[pl.ds(0, cpt)], sem_out.at[sl]).wait()

    @pl.when(i == 0)
    def _():
        start_gather(0, 0)
        for n in range(len(w_hbm)):
            w_copy(n, te_ref[0]).start(priority=WEIGHT_DMA_PRIORITY)

    @pl.when(i + 1 < n_active)
    def _():
        start_gather(i + 1, 1 - slot)

    @pl.when(i < n_active)
    def _():
        wait_gather(slot)

        @pl.when((i == 0) | (te_ref[i] != te_ref[jnp.maximum(i - 1, 0)]))
        def _():
            for n in range(len(w_hbm)):
                w_copy(n, 0).wait()
                wbf[n] = wst[n].astype(BF16)

                @pl.when(nx_ref[i] >= 0)
                def _():
                    w_copy(n, nx_ref[i]).start(priority=WEIGHT_DMA_PRIORITY)

        @pl.when(i >= 1)
        def _():
            for c in range(cpt):
                out_copy(c, 1 - slot, cd_ref[(i - 1) * cpt + c]).start()

        x = xbuf[slot].reshape(TM_EXPERT, -1)
        e = te_ref[i]
        gate = jnp.minimum(jnp.dot(x, wbf[0], preferred_element_type=F32) + bg_ref[e], SWIGLU_LIMIT)
        up = jnp.clip(jnp.dot(x, wbf[1], preferred_element_type=F32) + bu_ref[e], -SWIGLU_LIMIT, SWIGLU_LIMIT)
        act = gate * jax.nn.sigmoid(SWIGLU_ALPHA * gate) * (up + 1.0)
        y = jnp.dot(act.astype(BF16), wbf[2], preferred_element_type=F32) + bd_ref[e]

        @pl.when(i >= 2)
        def _():
            wait_scatter(slot)

        ybuf[slot] = y.astype(BF16).reshape(ybuf.shape[1:])

        @pl.when(i == 0)
        def _():
            for n in range(len(zero_starts)):
                zero_copy(n).wait()

        @pl.when(i == n_active - 1)
        def _():
            for c in range(cpt):
                out_copy(c, slot, cd_ref[i * cpt + c]).start()
            wait_scatter(slot)

            @pl.when(i >= 1)
            def _():
                wait_scatter(1 - slot)


def _experts(xg, te, nxt, n_active, src, dst, wg, bg, wu, bu, wd, bd):
    rows, d = xg.shape
    n_e, _, dff = wg.shape
    assert d == dff
    n_tiles = te.shape[0]
    n_token_tiles = rows // SLOTS
    n_zero = n_token_tiles + 2 * TM_EXPERT // SLACK_ROWS
    b_spec = pl.BlockSpec((n_e, 1, d), lambda i, *_: (0, 0, 0))
    hbm = pl.BlockSpec(memory_space=pl.ANY)
    tile_bufs = pltpu.VMEM((2, CHUNKS_PER_TILE, CHUNK, d), BF16)
    yg = pl.pallas_call(
        functools.partial(_expert_kernel, n_token_tiles=n_token_tiles),
        grid_spec=pltpu.PrefetchScalarGridSpec(
            num_scalar_prefetch=5,
            grid=(n_tiles,),
            in_specs=[hbm, hbm, b_spec, hbm, b_spec, hbm, b_spec],
            out_specs=hbm,
            scratch_shapes=[tile_bufs, tile_bufs,
                            pltpu.VMEM((3, d, dff), F32), pltpu.VMEM((3, d, dff), BF16),
                            pltpu.VMEM((SLACK_ROWS // CHUNK, CHUNK, d), BF16),
                            pltpu.SemaphoreType.DMA((2,)),
                            pltpu.SemaphoreType.DMA((2,)),
                            pltpu.SemaphoreType.DMA((n_zero,)),
                            pltpu.SemaphoreType.DMA((3,))]),
        out_shape=jax.ShapeDtypeStruct(((rows + 2 * TM_EXPERT) // CHUNK, CHUNK, d), BF16),
        compiler_params=pltpu.CompilerParams(
            dimension_semantics=("arbitrary",), vmem_limit_bytes=V7X_VMEM_LIMIT_BYTES),
        name="experts",
    )(te, nxt, n_active, src, dst, xg.reshape(rows // CHUNK, CHUNK, d), wg, bg, wu, bu, wd, bd)
    return yg.reshape(rows + 2 * TM_EXPERT, d)


def _combine_kernel(x1_ref, yg_ref, meta_ref, gn_ref, o_ref, *, final_norm):
    meta = meta_ref[...]
    tm = meta.shape[0]
    slots_i = [meta[:, k:k + 1].astype(I32) for k in range(TOP_K)]
    probs = [meta[:, TOP_K + k:TOP_K + k + 1] for k in range(TOP_K)]
    iota_s = lax.broadcasted_iota(I32, (tm, PERM_BLOCK), 1)
    xo = x1_ref[...]
    for blk in range(SLOTS // PERM_BLOCK):
        base = blk * PERM_BLOCK
        wperm = jnp.zeros((tm, PERM_BLOCK), F32)
        for sk, pk in zip(slots_i, probs):
            wperm = jnp.where(iota_s == sk - base, pk, wperm)
        xo = xo + jnp.dot(wperm.astype(BF16), yg_ref[base:base + PERM_BLOCK, :], preferred_element_type=F32)
    o_ref[...] = _rms(xo, gn_ref[...]) if final_norm else xo


def _combine(x1, yg, meta, gn, final_norm):
    t, d = x1.shape
    tm = TM_POST
    return pl.pallas_call(
        functools.partial(_combine_kernel, final_norm=final_norm),
        grid=(t // tm,),
        in_specs=[pl.BlockSpec((tm, d), lambda i: (i, 0)),
                  pl.BlockSpec((SLOTS, d), lambda i: (i, 0)),
                  pl.BlockSpec((tm, LANES), lambda i: (i, 0)),
                  pl.BlockSpec((1, d), lambda i: (0, 0))],
        out_specs=pl.BlockSpec((tm, d), lambda i: (i, 0)),
        out_shape=jax.ShapeDtypeStruct((t, d), F32),
        compiler_params=pltpu.CompilerParams(
            dimension_semantics=("parallel",), vmem_limit_bytes=V7X_VMEM_LIMIT_BYTES),
        name="combine",
    )(x1, yg, meta, gn)


def kernel(x, norm_mix, w_in, b_in, sinks, rel_bias, w_branch_a, w_branch_b, w_out, norm_ffn,
           router_w, router_b, w_gate, b_gate, w_up, b_up, w_down, b_down, norm_final):
    b, s, d = x.shape
    t = b * s
    depth = w_in.shape[0]
    a_q_w = A_KV_HEADS * A_GROUP * HEAD_DIM
    a_kv_w = A_KV_HEADS * HEAD_DIM
    b_w = B_HEADS * HEAD_DIM
    n_grp = len(B_GROUPS)
    dils = tuple(dil for _, dil in B_GROUPS)
    segs, col = [], 0
    for kind, width in (("q", a_q_w), ("k", a_kv_w), ("vT", a_kv_w)):
        segs.append((col, width, kind, 1))
        col += width
    for kind in ("q", "k", "v"):
        for dil in dils:
            segs.append((col, b_w, kind, dil))
            col += b_w
    for _ in range(2):
        segs.append((col, d, "gate", 1))
        col += d
    segs = tuple(segs)
    n_a = A_KV_HEADS * A_GROUP
    nt = t // TM_POST
    max_chunks = nt * ((TM_POST * TOP_K + N_EXPERTS * (CHUNK - 1)) // CHUNK)
    n_tiles = -(-(max_chunks + N_EXPERTS * (CHUNKS_PER_TILE - 1)) // CHUNKS_PER_TILE)

    bias_a = _band_bias(rel_bias[:, :n_a], A_HALF_WINDOW, TQ_ATTN, 1, A_KV_HEADS, A_GROUP)
    bias_b = [_band_bias(rel_bias[:, n_a + gi * B_HEADS:n_a + (gi + 1) * B_HEADS],
                         win // (2 * dil), TQ_ATTN, dil, 1, B_HEADS) for gi, (win, dil) in enumerate(B_GROUPS)]

    x2d = x.reshape(t, d)
    for layer in range(depth):
        proj = _inproj(x2d, norm_mix[layer][None], w_in[layer].astype(BF16), b_in[layer][None], segs)
        qa, ka, vat = proj[:3]
        qb, kb, vb = proj[3:3 + n_grp], proj[3 + n_grp:3 + 2 * n_grp], proj[3 + 2 * n_grp:3 + 3 * n_grp]
        ga, gb = proj[-2:]

        yat = _band_attn_gqa(qa.reshape(b, s, a_q_w), ka.reshape(b, s, a_kv_w), vat, bias_a, sinks[layer],
                             n_kv=A_KV_HEADS, grp=A_GROUP, hw=A_HALF_WINDOW, tq=TQ_ATTN, block=BLOCK_ATTN_A)
        os_, ls_ = [], []
        for gi, (win, dil) in enumerate(B_GROUPS):
            sub = lambda a: a.reshape(b, s // dil, dil * b_w)
            o, lse = _band_attn_heads(sub(qb[gi]), sub(kb[gi]), sub(vb[gi]), bias_b[gi], n_heads=B_HEADS,
                                      hw=win // (2 * dil), tq=TQ_ATTN, rows_per_step=BLOCK_ATTN_B, reps=dil)
            os_.append(o.reshape(t // dil, dil * b_w))
            ls_.append(lse.reshape(t // dil, dil * b_w))

        rwt = router_w[layer].T
        rwt_hi = rwt.astype(BF16)
        rwt_split = jnp.stack([rwt_hi, (rwt - rwt_hi.astype(F32)).astype(BF16)])
        x1, xg, meta, segs_out = _post(
            x2d, yat, os_, ls_, dils, ga, gb,
            w_branch_a[layer].astype(BF16), w_branch_b[layer].astype(BF16), w_out[layer].astype(BF16),
            norm_ffn[layer][None], rwt_split, router_b[layer][:, None])

        segs3 = segs_out.reshape(nt, N_EXPERTS, LANES)
        te, nxt, n_active, src, dst = _expert_tiles(segs3[:, :, 0].astype(I32), segs3[:, :, 1].astype(I32), n_tiles)
        yg = _experts(xg, te, nxt, n_active, src, dst, w_gate[layer], b_gate[layer][:, None], w_up[layer],
                      b_up[layer][:, None], w_down[layer], b_down[layer][:, None])
        x2d = _combine(x1, yg, meta, norm_final[None], layer == depth - 1)
    return x2d.reshape(b, s, d)
```

```python
import functools

import numpy as np
import jax
import jax.numpy as jnp
from jax import lax
from jax.experimental import pallas as pl
from jax.experimental.pallas import tpu as pltpu

F32 = jnp.float32
BF16 = jnp.bfloat16
I32 = jnp.int32

HEAD_DIM = 64
A_KV_HEADS = 4
A_GROUP = 4
A_HALF_WINDOW = 128
B_GROUPS = ((128, 1), (512, 4), (2048, 16))
B_HEADS = 4
N_BUCKETS = 32
MAX_DISTANCE = 1024
N_EXPERTS = 32
TOP_K = 4
SWIGLU_LIMIT = 7.0
SWIGLU_ALPHA = 1.702
EPS = 1e-5
NEG_INF = -1e30
LOG2E = 1.4426950408889634

V7X_VMEM_LIMIT_BYTES = 56 * 1024 * 1024
LANES = 128
BF16_SUBLANES = 16

TM_PROJ = 1024
TQ_ATTN = 128
BLOCK_ATTN_A = 2048
BLOCK_ATTN_B = 4096
TM_POST = 512
TM_EXPERT = 512

CHUNK = BF16_SUBLANES
CHUNKS_PER_TILE = TM_EXPERT // CHUNK
SLOTS = -(-(TM_POST * TOP_K + N_EXPERTS * (CHUNK - 1)) // LANES) * LANES
SLACK_ROWS = SLOTS - TM_POST * TOP_K
assert (2 * TM_EXPERT) % SLACK_ROWS == 0
PERM_BLOCK = 512
assert SLOTS % PERM_BLOCK == 0


def _t5_buckets(rel):
    half = N_BUCKETS // 2
    max_exact = half // 2
    ret = np.where(rel > 0, half, 0)
    n = np.abs(rel)
    large = max_exact + (np.log(np.maximum(n, 1) / max_exact)
                         / np.log(MAX_DISTANCE / max_exact) * (half - max_exact)).astype(np.int32)
    large = np.minimum(large, half - 1)
    return (ret + np.where(n < max_exact, n, large)).astype(np.int32)


def _rms(x, g):
    return x * lax.rsqrt(jnp.mean(x * x, axis=-1, keepdims=True) + EPS) * g


def _inproj_kernel(x_ref, g_ref, w_ref, b_ref, *refs, segs):
    out_refs, scr = refs[:-1], refs[-1]
    h = _rms(x_ref[...], g_ref[...]).astype(BF16)

    for ref, (c0, width, kind, dil) in zip(out_refs, segs):
        acc = jnp.dot(h, w_ref[:, c0:c0 + width], preferred_element_type=F32) + b_ref[:, c0:c0 + width]
        if kind == "q":
            acc = acc * (HEAD_DIM ** -0.5 * LOG2E)
        elif kind == "gate":
            acc = jax.nn.sigmoid(acc)
        if kind == "vT":
            ref[...] = acc.T.astype(ref.dtype)
        elif dil == 1:
            ref[...] = acc.astype(ref.dtype)
        else:
            rows = acc.shape[0] // dil
            for c in range(width // LANES):
                scr[c] = acc[:, c * LANES:(c + 1) * LANES]
            for r in range(dil):
                for c in range(width // LANES):
                    col = r * width + c * LANES
                    ref[:, col:col + LANES] = scr[c, pl.ds(r, rows, stride=dil), :].astype(ref.dtype)


def _inproj(x2d, g, w_bf16, b, segs):
    t, d = x2d.shape
    n = w_bf16.shape[1]
    max_w = max(w for _, w, _, dil in segs if dil > 1)
    out_specs, out_shape = [], []
    for _, w, kind, dil in segs:
        if kind == "vT":
            out_specs.append(pl.BlockSpec((w, TM_PROJ), lambda i: (0, i)))
            out_shape.append(jax.ShapeDtypeStruct((w, t), BF16))
        else:
            out_specs.append(pl.BlockSpec((TM_PROJ // dil, dil * w), lambda i: (i, 0)))
            out_shape.append(jax.ShapeDtypeStruct((t // dil, dil * w), BF16))
    return pl.pallas_call(
        functools.partial(_inproj_kernel, segs=segs),
        grid=(t // TM_PROJ,),
        in_specs=[
            pl.BlockSpec((TM_PROJ, d), lambda i: (i, 0)),
            pl.BlockSpec((1, d), lambda i: (0, 0)),
            pl.BlockSpec((d, n), lambda i: (0, 0), pipeline_mode=pl.Buffered(1)),
            pl.BlockSpec((1, n), lambda i: (0, 0)),
        ],
        out_specs=out_specs,
        out_shape=out_shape,
        scratch_shapes=[pltpu.VMEM((max_w // LANES, TM_PROJ, LANES), F32)],
        compiler_params=pltpu.CompilerParams(
            dimension_semantics=("parallel",), vmem_limit_bytes=V7X_VMEM_LIMIT_BYTES),
        name="inproj",
    )(x2d, g, w_bf16, b)


def _band_bias(table, hw, tq, dil, n_kv, grp):
    tk = tq + 2 * hw
    off = np.arange(tk)[None, :] - hw - np.arange(tq)[:, None]
    band = np.abs(off) <= hw
    col = np.arange(tk)[None, :]
    masks = np.stack([band & (col >= hw), band, band & (col < hw + tq)])
    onehot = (_t5_buckets(off * dil)[..., None] == np.arange(N_BUCKETS)).astype(np.float32)
    bias = jnp.einsum("qkn,nvg->vkgq", jnp.asarray(onehot), table.astype(F32).reshape(N_BUCKETS, n_kv, grp),
                      precision=lax.Precision.HIGHEST)
    keep = np.transpose(masks, (0, 2, 1))[:, None, :, None, :]
    return jnp.where(keep, bias[None] * LOG2E, NEG_INF).reshape(3, n_kv, tk, grp * tq)


def _band_attn_gqa_kernel(q_ref, kp, kc, kn, vtp, vtc, vtn, bias_ref, sink_ref, o_ref, *, n_kv, grp, tq, hw):
    j, nt = pl.program_id(1), pl.num_programs(1)
    k = jnp.concatenate([kp[0], kc[0], kn[0]], axis=0)
    vt = jnp.concatenate([vtp[...], vtc[...], vtn[...]], axis=1)
    n_sub = q_ref.shape[1] // tq
    tk = tq + 2 * hw
    ones = jnp.ones((BF16_SUBLANES, tk), BF16)
    lane = lax.broadcasted_iota(I32, (1, grp * tq), 1)
    sinks = []
    for h in range(n_kv):
        sk = jnp.full((1, grp * tq), sink_ref[h * grp] * LOG2E, F32)
        for g in range(1, grp):
            sk = jnp.where(lane >= g * tq, sink_ref[h * grp + g] * LOG2E, sk)
        sinks.append(sk)
    units = []
    for sub in range(n_sub):
        var = 1
        if sub == 0:
            var = jnp.where(j == 0, 0, var)
        if sub == n_sub - 1:
            var = jnp.where(j == nt - 1, 2, var)
        units.extend((var, sub * tq, h) for h in range(n_kv))

    def scores(var, r0, h):
        k_h = k[r0:r0 + tk, h * HEAD_DIM:(h + 1) * HEAD_DIM]
        c0 = h * grp * HEAD_DIM
        q = jnp.concatenate([q_ref[0, r0:r0 + tq, c0 + g * HEAD_DIM:c0 + (g + 1) * HEAD_DIM]
                             for g in range(grp)], axis=0)
        st = lax.dot_general(k_h, q, (((1,), (1,)), ((), ())), preferred_element_type=F32)
        return st + bias_ref[var, h]

    st_next = scores(*units[0])
    for n, (var, r0, h) in enumerate(units):
        st = st_next
        if n + 1 < len(units):
            st_next = scores(*units[n + 1])
        sk = sinks[h]
        m = jnp.maximum(jnp.max(st, axis=0, keepdims=True), sk)
        pt = jnp.exp2((st - m).astype(BF16))
        vt_h = jnp.concatenate([vt[h * HEAD_DIM:(h + 1) * HEAD_DIM, r0:r0 + tk], ones], axis=0)
        ot = jnp.dot(vt_h, pt, preferred_element_type=F32)
        l = ot[HEAD_DIM:HEAD_DIM + 1] + jnp.exp2(sk - m)
        o = ot[:HEAD_DIM] / l
        for g in range(grp):
            c = (h * grp + g) * HEAD_DIM
            o_ref[c:c + HEAD_DIM, r0:r0 + tq] = o[:, g * tq:(g + 1) * tq].astype(o_ref.dtype)


def _band_attn_gqa(q, k, vt, bias3, sinks, *, n_kv, grp, hw, tq, block):
    b, s, _ = q.shape
    hq = n_kv * grp
    qw, kw = hq * HEAD_DIM, n_kv * HEAD_DIM
    nt = s // block
    ratio = block // hw
    nhw = s // hw
    tk = tq + 2 * hw
    assert nt * (block // tq) >= 2
    prev_j = lambda j: jnp.maximum(j * ratio - 1, 0)
    next_j = lambda j: jnp.minimum((j + 1) * ratio, nhw - 1)
    assert bias3.shape == (3, n_kv, tk, grp * tq)
    return pl.pallas_call(
        functools.partial(_band_attn_gqa_kernel, n_kv=n_kv, grp=grp, tq=tq, hw=hw),
        grid=(b, nt),
        in_specs=[
            pl.BlockSpec((1, block, qw), lambda bi, j: (bi, j, 0)),
            pl.BlockSpec((1, hw, kw), lambda bi, j: (bi, prev_j(j), 0)),
            pl.BlockSpec((1, block, kw), lambda bi, j: (bi, j, 0)),
            pl.BlockSpec((1, hw, kw), lambda bi, j: (bi, next_j(j), 0)),
            pl.BlockSpec((kw, hw), lambda bi, j: (0, bi * nhw + prev_j(j))),
            pl.BlockSpec((kw, block), lambda bi, j: (0, bi * nt + j)),
            pl.BlockSpec((kw, hw), lambda bi, j: (0, bi * nhw + next_j(j))),
            pl.BlockSpec((3, n_kv, tk, grp * tq), lambda bi, j: (0, 0, 0, 0), pipeline_mode=pl.Buffered(1)),
            pl.BlockSpec(memory_space=pltpu.SMEM),
        ],
        out_specs=pl.BlockSpec((qw, block), lambda bi, j: (0, bi * nt + j)),
        out_shape=jax.ShapeDtypeStruct((qw, b * s), BF16),
        compiler_params=pltpu.CompilerParams(
            dimension_semantics=("parallel", "parallel"), vmem_limit_bytes=V7X_VMEM_LIMIT_BYTES),
        name="band_attn_a",
    )(q, k, k, k, vt, vt, vt, bias3, sinks)


def _band_attn_heads_kernel(q_ref, kp, kc, kn, vp, vc, vn, bias_ref, o_ref, lse_ref, *, n_heads, tq, hw):
    j, nt = pl.program_id(2), pl.num_programs(2)
    k = jnp.concatenate([kp[0], kc[0], kn[0]], axis=0)
    v = jnp.concatenate([vp[0], vc[0], vn[0]], axis=0)
    n_sub = q_ref.shape[1] // tq
    tk = tq + 2 * hw
    width = n_heads * HEAD_DIM
    lane_head = lax.broadcasted_iota(I32, (tq, width), 1) // HEAD_DIM
    ones = jnp.ones((BF16_SUBLANES, tk), BF16)

    units = [(c0, sub) for c0 in range(0, q_ref.shape[2], width) for sub in range(n_sub)]

    def scores(c0, sub):
        var = 1
        if sub == 0:
            var = jnp.where(j == 0, 0, var)
        if sub == n_sub - 1:
            var = jnp.where(j == nt - 1, 2, var)
        r0 = sub * tq
        q = q_ref[0, r0:r0 + tq, c0:c0 + width]
        q_bd = jnp.concatenate([jnp.where(lane_head == h, q, jnp.zeros_like(q)) for h in range(n_heads)], axis=0)
        st = lax.dot_general(k[r0:r0 + tk, c0:c0 + width], q_bd, (((1,), (1,)), ((), ())),
                             preferred_element_type=F32)
        return st + bias_ref[var, 0]

    st_next = scores(*units[0])
    for n, (c0, sub) in enumerate(units):
        r0 = sub * tq
        st = st_next
        if n + 1 < len(units):
            st_next = scores(*units[n + 1])
        m = jnp.max(st, axis=0, keepdims=True)
        pt = jnp.exp2((st - m).astype(BF16))
        vt = jnp.concatenate([v[r0:r0 + tk, c0:c0 + width].T, ones], axis=0)
        ot = jnp.dot(vt, pt, preferred_element_type=F32)
        l = ot[width:width + 1]
        lse = m + jnp.log2(l)
        o_sel = jnp.concatenate([ot[h * HEAD_DIM:(h + 1) * HEAD_DIM, h * tq:(h + 1) * tq] / l[:, h * tq:(h + 1) * tq]
                                 for h in range(n_heads)], axis=0)
        lse_sel = jnp.concatenate([jnp.broadcast_to(lse[:, h * tq:(h + 1) * tq], (HEAD_DIM, tq))
                                   for h in range(n_heads)], axis=0)
        o_ref[0, r0:r0 + tq, c0:c0 + width] = o_sel.T.astype(o_ref.dtype)
        lse_ref[0, r0:r0 + tq, c0:c0 + width] = lse_sel.T


def _band_attn_heads(q, k, v, bias3, *, n_heads, hw, tq, rows_per_step, reps):
    b, l, _ = q.shape
    block = min(rows_per_step, l)
    nt = l // block
    ratio = block // hw
    nhw = l // hw
    tk = tq + 2 * hw
    assert nt * (block // tq) >= 2
    res_per_step = max(1, min(reps, rows_per_step // block))
    qw = kw = res_per_step * n_heads * HEAD_DIM

    prev = lambda bi, r, j: (bi, jnp.maximum(j * ratio - 1, 0), r)
    cur = lambda bi, r, j: (bi, j, r)
    nxt = lambda bi, r, j: (bi, jnp.minimum((j + 1) * ratio, nhw - 1), r)
    in_specs = [
        pl.BlockSpec((1, block, qw), cur),
        pl.BlockSpec((1, hw, kw), prev), pl.BlockSpec((1, block, kw), cur), pl.BlockSpec((1, hw, kw), nxt),
        pl.BlockSpec((1, hw, kw), prev), pl.BlockSpec((1, block, kw), cur), pl.BlockSpec((1, hw, kw), nxt),
        pl.BlockSpec((3, 1, tk, n_heads * tq), lambda bi, r, j: (0, 0, 0, 0), pipeline_mode=pl.Buffered(1)),
    ]
    return pl.pallas_call(
        functools.partial(_band_attn_heads_kernel, n_heads=n_heads, tq=tq, hw=hw),
        grid=(b, reps // res_per_step, nt),
        in_specs=in_specs,
        out_specs=[pl.BlockSpec((1, block, qw), cur), pl.BlockSpec((1, block, qw), cur)],
        out_shape=[jax.ShapeDtypeStruct(q.shape, BF16), jax.ShapeDtypeStruct(q.shape, F32)],
        compiler_params=pltpu.CompilerParams(
            dimension_semantics=("parallel", "parallel", "parallel"),
            vmem_limit_bytes=V7X_VMEM_LIMIT_BYTES),
        name="band_attn_b",
    )(q, k, k, k, v, v, v, bias3)


def _token_order(ref, scr, dil):
    if dil == 1:
        return ref[...].astype(F32)
    n = ref.shape[0]
    width = ref.shape[1] // dil
    for r in range(dil):
        for c in range(width // LANES):
            col = r * width + c * LANES
            scr[c, pl.ds(r, n, stride=dil), :] = ref[:, col:col + LANES].astype(F32)
    return jnp.concatenate([scr[c] for c in range(width // LANES)], axis=1)


def _post_kernel(x_ref, ya_ref, o1, o2, o3, l1, l2, l3, ga_ref, gb_ref, wa_ref, wb_ref, wo_ref,
                 gf_ref, rwt_ref, rb_ref, tri_ref, x1_ref, xg_ref, meta_ref, segs_ref, *scrs, dils):
    os_ = [_token_order(r, scrs[2 * i], dil) for i, (r, dil) in enumerate(zip((o1, o2, o3), dils))]
    ls = [_token_order(r, scrs[2 * i + 1], dil) for i, (r, dil) in enumerate(zip((l1, l2, l3), dils))]
    mx = jnp.maximum(jnp.maximum(ls[0], ls[1]), ls[2])
    es = [jnp.exp2(l - mx) for l in ls]
    den = es[0] + es[1] + es[2]
    yb = (es[0] * os_[0] + es[1] * os_[1] + es[2] * os_[2]) / den
    za = lax.dot_general(ya_ref[...], wa_ref[...], (((0,), (0,)), ((), ())), preferred_element_type=F32)
    zb = jnp.dot(yb.astype(BF16), wb_ref[...], preferred_element_type=F32)
    merged = ga_ref[...].astype(F32) * za + gb_ref[...].astype(F32) * zb
    x1 = x_ref[...] + jnp.dot(merged.astype(BF16), wo_ref[...], preferred_element_type=F32)
    x1_ref[...] = x1
    h2 = _rms(x1, gf_ref[...])

    h_hi = h2.astype(BF16)
    h_lo = (h2 - h_hi.astype(F32)).astype(BF16)
    nt_dot = lambda a, b_: lax.dot_general(a, b_, (((1,), (1,)), ((), ())), preferred_element_type=F32)
    n_e = rwt_ref.shape[1]
    both = nt_dot(rwt_ref[...].reshape(2 * n_e, -1), h_hi)
    logits = (both[:n_e] + nt_dot(rwt_ref[0], h_lo) + both[n_e:]) + rb_ref[...]
    n_e, tm = logits.shape
    iota_e = lax.broadcasted_iota(I32, (n_e, tm), 0)
    work = logits
    vals, hots = [], []
    for _ in range(TOP_K):
        mk = jnp.max(work, axis=0, keepdims=True)
        ik = jnp.min(jnp.where(work == mk, iota_e, n_e), axis=0, keepdims=True)
        hot = iota_e == ik
        vals.append(mk)
        hots.append(hot)
        work = jnp.where(hot, -jnp.inf, work)
    exps = [jnp.exp(vk - vals[0]) for vk in vals]
    tot = exps[0] + exps[1] + exps[2] + exps[3]
    probs = [ek / tot for ek in exps]

    sel = jnp.zeros((n_e, tm), F32)
    for hot in hots:
        sel = sel + jnp.where(hot, 1.0, 0.0)
    cnt = jnp.sum(sel, axis=1, keepdims=True)
    pcnt = jnp.floor((cnt + (CHUNK - 1)) / CHUNK) * CHUNK
    r_i = lax.broadcasted_iota(I32, (n_e, n_e), 0)
    c_i = lax.broadcasted_iota(I32, (n_e, n_e), 1)
    pcnt_row = jnp.sum(jnp.where(r_i == c_i, pcnt, 0.0), axis=0, keepdims=True)
    seg_off = jnp.sum(jnp.where(c_i < r_i, pcnt_row, 0.0), axis=1, keepdims=True)
    before = jnp.dot(sel.astype(BF16), tri_ref[...], preferred_element_type=F32)
    slot_of = seg_off + before
    slots = [jnp.sum(jnp.where(hot, slot_of, 0.0), axis=0, keepdims=True) for hot in hots]

    iota_s = lax.broadcasted_iota(I32, (PERM_BLOCK, tm), 0)
    slots_i = [sk.astype(I32) for sk in slots]
    for blk in range(SLOTS // PERM_BLOCK):
        perm = jnp.zeros((PERM_BLOCK, tm), F32)
        for sk in slots_i:
            perm = jnp.where(iota_s == sk - blk * PERM_BLOCK, 1.0, perm)
        xg_ref[blk * PERM_BLOCK:(blk + 1) * PERM_BLOCK, :] = jnp.dot(
            perm.astype(BF16), h_hi, preferred_element_type=F32).astype(BF16)

    rows = slots + probs
    meta_t = jnp.concatenate(rows + [jnp.zeros((LANES - len(rows), tm), F32)], axis=0)
    meta_ref[...] = meta_t.T
    lane = lax.broadcasted_iota(I32, (n_e, LANES), 1)
    segs_ref[...] = jnp.where(lane == 0, pcnt, jnp.where(lane == 1, seg_off, 0.0))


def _post(x2d, ya, os_, ls_, dils, ga, gb, wa, wb, wo, gf, rwt, rb):
    t, d = x2d.shape
    bw = os_[0].shape[1] // dils[0]
    tm = TM_POST
    nt = t // tm
    tri = jnp.asarray(np.triu(np.ones((tm, tm), np.float32), k=1), dtype=BF16)
    row = lambda w, dil=1: pl.BlockSpec((tm // dil, dil * w), lambda i: (i, 0))
    full = lambda a: pl.BlockSpec(a.shape, lambda i: (0,) * a.ndim)
    return pl.pallas_call(
        functools.partial(_post_kernel, dils=dils),
        grid=(nt,),
        in_specs=[row(d), pl.BlockSpec((ya.shape[0], tm), lambda i: (0, i))]
                 + [row(bw, dil) for dil in dils] + [row(bw, dil) for dil in dils]
                 + [row(d), row(d), full(wa), full(wb), full(wo), full(gf), full(rwt), full(rb), full(tri)],
        scratch_shapes=[pltpu.VMEM((bw // LANES, tm, LANES), F32) for _ in range(2 * len(dils))],
        out_specs=[row(d), pl.BlockSpec((SLOTS, d), lambda i: (i, 0)), row(LANES),
                   pl.BlockSpec((N_EXPERTS, LANES), lambda i: (i, 0))],
        out_shape=[jax.ShapeDtypeStruct((t, d), F32), jax.ShapeDtypeStruct((nt * SLOTS, d), BF16),
                   jax.ShapeDtypeStruct((t, LANES), F32), jax.ShapeDtypeStruct((nt * N_EXPERTS, LANES), F32)],
        compiler_params=pltpu.CompilerParams(
            dimension_semantics=("parallel",), vmem_limit_bytes=V7X_VMEM_LIMIT_BYTES),
        name="post_attn",
    )(x2d, ya, *os_, *ls_, ga, gb, wa, wb, wo, gf, rwt, rb, tri)


def _expert_tiles(pcnt, seg_off, n_tiles):
    nt, n_e = pcnt.shape
    cpt = CHUNKS_PER_TILE
    nch = (pcnt // CHUNK).T
    cum = jnp.cumsum(nch, axis=1)
    total = cum[:, -1]
    tiles_e = (total + cpt - 1) // cpt
    tile_end = jnp.cumsum(tiles_e)
    n_active = tile_end[-1]
    i = jnp.arange(n_tiles, dtype=I32)
    last = jnp.minimum(i, n_active - 1)
    te = jnp.sum((last[:, None] >= tile_end[None, :]).astype(I32), axis=1)
    hot_e = (te[:, None] == jnp.arange(n_e, dtype=I32)[None, :])
    pick = lambda tab: jnp.sum(jnp.where(hot_e[:, :, None], tab[None], 0), axis=1)
    tile_start = jnp.sum(jnp.where(hot_e, (tile_end - tiles_e)[None, :], 0), axis=1)
    total_t = jnp.sum(jnp.where(hot_e, total[None, :], 0), axis=1)
    q = (last - tile_start)[:, None] * cpt + jnp.arange(cpt, dtype=I32)[None, :]
    valid = (q < total_t[:, None]) & (i < n_active)[:, None]
    cum_t, nch_t = pick(cum), pick(nch)
    chunk0_t = pick((seg_off.T + jnp.arange(nt, dtype=I32)[None, :] * SLOTS) // CHUNK)
    jj = jnp.sum((q[:, :, None] >= cum_t[:, None, :]).astype(I32), axis=2)
    hot_j = jj[:, :, None] == jnp.arange(nt, dtype=I32)[None, None, :]
    first = jnp.sum(jnp.where(hot_j, (cum_t - nch_t)[:, None, :], 0), axis=2)
    base = jnp.sum(jnp.where(hot_j, chunk0_t[:, None, :], 0), axis=2)
    src = jnp.where(valid, base + q - first, 0)
    trash = nt * SLOTS // CHUNK + (i % 2)[:, None] * cpt + jnp.arange(cpt, dtype=I32)[None, :]
    dst = jnp.where(valid, src, trash)
    group_end = jnp.sum(jnp.where(hot_e, tile_end[None, :], 0), axis=1)
    nxt = jnp.sum((group_end[:, None] >= tile_end[None, :]).astype(I32), axis=1)
    nxt = jnp.where(group_end < n_active, nxt, -1)
    return (te.astype(I32), nxt.astype(I32), n_active.astype(I32)[None],
            src.reshape(-1).astype(I32), dst.reshape(-1).astype(I32))


def _expert_kernel(te_ref, nx_ref, na_ref, cs_ref, cd_ref, xg_hbm, wg_hbm, bg_ref, wu_hbm, bu_ref, wd_hbm,
                   bd_ref, yg_hbm, xbuf, ybuf, wst, wbf, zbuf, sem_in, sem_out, sem_zero, sem_w, *,
                   n_token_tiles):
    i = pl.program_id(0)
    n_active = na_ref[0]
    slot = i % 2
    cpt = CHUNKS_PER_TILE
    w_hbm = (wg_hbm, wu_hbm, wd_hbm)

    def w_copy(n, e):
        return pltpu.make_async_copy(w_hbm[n].at[e], wst.at[n], sem_w.at[n])

    zero_starts = [j * SLOTS + TM_POST * TOP_K for j in range(n_token_tiles)]
    zero_starts += [n_token_tiles * SLOTS + j * SLACK_ROWS for j in range(2 * TM_EXPERT // SLACK_ROWS)]

    def zero_copy(n):
        return pltpu.make_async_copy(
            zbuf, yg_hbm.at[pl.ds(zero_starts[n] // CHUNK, SLACK_ROWS // CHUNK)], sem_zero.at[n])

    @pl.when(i == 0)
    def _():
        zbuf[...] = jnp.zeros_like(zbuf)
        for n in range(len(zero_starts)):
            zero_copy(n).start()

    def in_copy(c, sl, chunk):
        return pltpu.make_async_copy(xg_hbm.at[chunk], xbuf.at[sl, c], sem_in.at[sl])

    def out_copy(c, sl, chunk):
        return pltpu.make_async_copy(ybuf.at[sl, c], yg_hbm.at[chunk], sem_out.at[sl])

    def start_gather(tile, sl):
        for c in range(cpt):
            in_copy(c, sl, cs_ref[tile * cpt + c]).start(priority=c % 2)

    def wait_gather(sl):
        pltpu.make_async_copy(xg_hbm.at[pl.ds(0, cpt)], xbuf.at[sl], sem_in.at[sl]).wait()

    def wait_scatter(sl):
        pltpu.make_async_copy(ybuf.at[sl], yg_hbm.at[pl.ds(0, cpt)], sem_out.at[sl]).wait()

    @pl.when(i == 0)
    def _():
        start_gather(0, 0)
        for n in range(len(w_hbm)):
            w_copy(n, te_ref[0]).start()

    @pl.when(i + 1 < n_active)
    def _():
        start_gather(i + 1, 1 - slot)

    @pl.when(i < n_active)
    def _():
        wait_gather(slot)

        @pl.when((i == 0) | (te_ref[i] != te_ref[jnp.maximum(i - 1, 0)]))
        def _():
            for n in range(len(w_hbm)):
                w_copy(n, 0).wait()
                wbf[n] = wst[n].astype(BF16)

                @pl.when(nx_ref[i] >= 0)
                def _():
                    w_copy(n, nx_ref[i]).start()

        @pl.when(i >= 1)
        def _():
            for c in range(cpt):
                out_copy(c, 1 - slot, cd_ref[(i - 1) * cpt + c]).start(priority=c % 2)

        x = xbuf[slot].reshape(TM_EXPERT, -1)
        e = te_ref[i]
        gate = jnp.minimum(jnp.dot(x, wbf[0], preferred_element_type=F32) + bg_ref[e], SWIGLU_LIMIT)
        up = jnp.clip(jnp.dot(x, wbf[1], preferred_element_type=F32) + bu_ref[e], -SWIGLU_LIMIT, SWIGLU_LIMIT)
        act = gate * jax.nn.sigmoid(SWIGLU_ALPHA * gate) * (up + 1.0)
        y = jnp.dot(act.astype(BF16), wbf[2], preferred_element_type=F32) + bd_ref[e]

        @pl.when(i >= 2)
        def _():
            wait_scatter(slot)

        ybuf[slot] = y.astype(BF16).reshape(ybuf.shape[1:])

        @pl.when(i == 0)
        def _():
            for n in range(len(zero_starts)):
                zero_copy(n).wait()

        @pl.when(i == n_active - 1)
        def _():
            for c in range(cpt):
                out_copy(c, slot, cd_ref[i * cpt + c]).start()
            wait_scatter(slot)

            @pl.when(i >= 1)
            def _():
                wait_scatter(1 - slot)


def _experts(xg, te, nxt, n_active, src, dst, wg, bg, wu, bu, wd, bd):
    rows, d = xg.shape
    n_e, _, dff = wg.shape
    assert d == dff
    n_tiles = te.shape[0]
    n_token_tiles = rows // SLOTS
    n_zero = n_token_tiles + 2 * TM_EXPERT // SLACK_ROWS
    b_spec = pl.BlockSpec((n_e, 1, d), lambda i, *_: (0, 0, 0))
    hbm = pl.BlockSpec(memory_space=pl.ANY)
    tile_bufs = pltpu.VMEM((2, CHUNKS_PER_TILE, CHUNK, d), BF16)
    yg = pl.pallas_call(
        functools.partial(_expert_kernel, n_token_tiles=n_token_tiles),
        grid_spec=pltpu.PrefetchScalarGridSpec(
            num_scalar_prefetch=5,
            grid=(n_tiles,),
            in_specs=[hbm, hbm, b_spec, hbm, b_spec, hbm, b_spec],
            out_specs=hbm,
            scratch_shapes=[tile_bufs, tile_bufs,
                            pltpu.VMEM((3, d, dff), F32), pltpu.VMEM((3, d, dff), BF16),
                            pltpu.VMEM((SLACK_ROWS // CHUNK, CHUNK, d), BF16),
                            pltpu.SemaphoreType.DMA((2,)),
                            pltpu.SemaphoreType.DMA((2,)),
                            pltpu.SemaphoreType.DMA((n_zero,)),
                            pltpu.SemaphoreType.DMA((3,))]),
        out_shape=jax.ShapeDtypeStruct(((rows + 2 * TM_EXPERT) // CHUNK, CHUNK, d), BF16),
        compiler_params=pltpu.CompilerParams(
            dimension_semantics=("arbitrary",), vmem_limit_bytes=V7X_VMEM_LIMIT_BYTES),
        name="experts",
    )(te, nxt, n_active, src, dst, xg.reshape(rows // CHUNK, CHUNK, d), wg, bg, wu, bu, wd, bd)
    return yg.reshape(rows + 2 * TM_EXPERT, d)


def _combine_kernel(x1_ref, yg_ref, meta_ref, gn_ref, o_ref, *, final_norm):
    meta = meta_ref[...]
    tm = meta.shape[0]
    slots_i = [meta[:, k:k + 1].astype(I32) for k in range(TOP_K)]
    probs = [meta[:, TOP_K + k:TOP_K + k + 1] for k in range(TOP_K)]
    iota_s = lax.broadcasted_iota(I32, (tm, PERM_BLOCK), 1)
    xo = x1_ref[...]
    for blk in range(SLOTS // PERM_BLOCK):
        base = blk * PERM_BLOCK
        wperm = jnp.zeros((tm, PERM_BLOCK), F32)
        for sk, pk in zip(slots_i, probs):
            wperm = jnp.where(iota_s == sk - base, pk, wperm)
        xo = xo + jnp.dot(wperm.astype(BF16), yg_ref[base:base + PERM_BLOCK, :], preferred_element_type=F32)
    o_ref[...] = _rms(xo, gn_ref[...]) if final_norm else xo


def _combine(x1, yg, meta, gn, final_norm):
    t, d = x1.shape
    tm = TM_POST
    return pl.pallas_call(
        functools.partial(_combine_kernel, final_norm=final_norm),
        grid=(t // tm,),
        in_specs=[pl.BlockSpec((tm, d), lambda i: (i, 0)),
                  pl.BlockSpec((SLOTS, d), lambda i: (i, 0)),
                  pl.BlockSpec((tm, LANES), lambda i: (i, 0)),
                  pl.BlockSpec((1, d), lambda i: (0, 0))],
        out_specs=pl.BlockSpec((tm, d), lambda i: (i, 0)),
        out_shape=jax.ShapeDtypeStruct((t, d), F32),
        compiler_params=pltpu.CompilerParams(
            dimension_semantics=("parallel",), vmem_limit_bytes=V7X_VMEM_LIMIT_BYTES),
        name="combine",
    )(x1, yg, meta, gn)


def kernel(x, norm_mix, w_in, b_in, sinks, rel_bias, w_branch_a, w_branch_b, w_out, norm_ffn,
           router_w, router_b, w_gate, b_gate, w_up, b_up, w_down, b_down, norm_final):
    b, s, d = x.shape
    t = b * s
    depth = w_in.shape[0]
    a_q_w = A_KV_HEADS * A_GROUP * HEAD_DIM
    a_kv_w = A_KV_HEADS * HEAD_DIM
    b_w = B_HEADS * HEAD_DIM
    n_grp = len(B_GROUPS)
    dils = tuple(dil for _, dil in B_GROUPS)
    segs, col = [], 0
    for kind, width in (("q", a_q_w), ("k", a_kv_w), ("vT", a_kv_w)):
        segs.append((col, width, kind, 1))
        col += width
    for kind in ("q", "k", "v"):
        for dil in dils:
            segs.append((col, b_w, kind, dil))
            col += b_w
    for _ in range(2):
        segs.append((col, d, "gate", 1))
        col += d
    segs = tuple(segs)
    n_a = A_KV_HEADS * A_GROUP
    nt = t // TM_POST
    max_chunks = nt * ((TM_POST * TOP_K + N_EXPERTS * (CHUNK - 1)) // CHUNK)
    n_tiles = -(-(max_chunks + N_EXPERTS * (CHUNKS_PER_TILE - 1)) // CHUNKS_PER_TILE)

    bias_a = _band_bias(rel_bias[:, :n_a], A_HALF_WINDOW, TQ_ATTN, 1, A_KV_HEADS, A_GROUP)
    bias_b = [_band_bias(rel_bias[:, n_a + gi * B_HEADS:n_a + (gi + 1) * B_HEADS],
                         win // (2 * dil), TQ_ATTN, dil, 1, B_HEADS) for gi, (win, dil) in enumerate(B_GROUPS)]

    x2d = x.reshape(t, d)
    for layer in range(depth):
        proj = _inproj(x2d, norm_mix[layer][None], w_in[layer].astype(BF16), b_in[layer][None], segs)
        qa, ka, vat = proj[:3]
        qb, kb, vb = proj[3:3 + n_grp], proj[3 + n_grp:3 + 2 * n_grp], proj[3 + 2 * n_grp:3 + 3 * n_grp]
        ga, gb = proj[-2:]

        yat = _band_attn_gqa(qa.reshape(b, s, a_q_w), ka.reshape(b, s, a_kv_w), vat, bias_a, sinks[layer],
                             n_kv=A_KV_HEADS, grp=A_GROUP, hw=A_HALF_WINDOW, tq=TQ_ATTN, block=BLOCK_ATTN_A)
        os_, ls_ = [], []
        for gi, (win, dil) in enumerate(B_GROUPS):
            sub = lambda a: a.reshape(b, s // dil, dil * b_w)
            o, lse = _band_attn_heads(sub(qb[gi]), sub(kb[gi]), sub(vb[gi]), bias_b[gi], n_heads=B_HEADS,
                                      hw=win // (2 * dil), tq=TQ_ATTN, rows_per_step=BLOCK_ATTN_B, reps=dil)
            os_.append(o.reshape(t // dil, dil * b_w))
            ls_.append(lse.reshape(t // dil, dil * b_w))

        rwt = router_w[layer].T
        rwt_hi = rwt.astype(BF16)
        rwt_split = jnp.stack([rwt_hi, (rwt - rwt_hi.astype(F32)).astype(BF16)])
        x1, xg, meta, segs_out = _post(
            x2d, yat, os_, ls_, dils, ga, gb,
            w_branch_a[layer].astype(BF16), w_branch_b[layer].astype(BF16), w_out[layer].astype(BF16),
            norm_ffn[layer][None], rwt_split, router_b[layer][:, None])

        segs3 = segs_out.reshape(nt, N_EXPERTS, LANES)
        te, nxt, n_active, src, dst = _expert_tiles(segs3[:, :, 0].astype(I32), segs3[:, :, 1].astype(I32), n_tiles)
        yg = _experts(xg, te, nxt, n_active, src, dst, w_gate[layer], b_gate[layer][:, None], w_up[layer],
                      b_up[layer][:, None], w_down[layer], b_down[layer][:, None])
        x2d = _combine(x1, yg, meta, norm_final[None], layer == depth - 1)
    return x2d.reshape(b, s, d)
```
